```python
import math
import jax
import jax.numpy as jnp
from jax import lax
import numpy as np

D_MODEL = 1024
BATCH = 8
SEQ = 2048
DEPTH = 2
DEC_BATCH = 32
DEC_SEQ = 1
PAST_LEN = 16384
PAGE_SIZE = 128

HEAD_DIM = 64
D_MIX = D_MODEL
H_A = D_MIX // 2 // HEAD_DIM
KV_A = H_A // 2
G_A = H_A // KV_A
H_B = (D_MIX - H_A * HEAD_DIM) // HEAD_DIM
KV_B = H_B // 2
G_B = H_B // KV_B
H_IDX = 8
D_IDX = 64
TOPK_MAX = 256
D_FF = 4 * D_MODEL
N_BUCKETS = 32
MAX_DISTANCE = 128
Q_BLOCK = 128
FORGET_BIAS = 3.0
LN_EPS = 1e-5
ALPHA = (2 * DEPTH) ** 0.25
BETA = (8 * DEPTH) ** -0.25
SPLITS = (H_A * HEAD_DIM, KV_A * HEAD_DIM, KV_A * HEAD_DIM, H_A,
          H_B * HEAD_DIM, KV_B * HEAD_DIM, KV_B * HEAD_DIM,
          H_IDX * D_IDX, D_IDX, H_IDX)
SPLIT_OFFSETS = tuple(int(o) for o in np.cumsum(SPLITS)[:-1])
D_IN = int(sum(SPLITS))

kernel_name = 'fox_dsa_hymba_deepnorm_step'


def _layer_norm(x, g, b):
    xf = x.astype(jnp.float32)
    mu = xf.mean(-1, keepdims=True)
    var = jnp.square(xf - mu).mean(-1, keepdims=True)
    return ((xf - mu) * lax.rsqrt(var + LN_EPS) * g + b).astype(x.dtype)


def _rel_bucket(dist):
    max_exact = N_BUCKETS // 2
    d = jnp.maximum(dist, 1).astype(jnp.float32)
    large = max_exact + (jnp.log(d / max_exact) / math.log(MAX_DISTANCE / max_exact)
                         * (N_BUCKETS - max_exact)).astype(jnp.int32)
    large = jnp.minimum(large, N_BUCKETS - 1)
    return jnp.where(dist < max_exact, dist, large)


def _project(x, w_in, b_f):
    B, T = x.shape[:2]
    z = jnp.einsum('btd,de->bte', x, w_in)
    qa, ka, va, fa, qb, kb, vb, qi, ki, wi = jnp.split(z, SPLIT_OFFSETS, axis=-1)
    heads = lambda a, h, d: a.reshape(B, T, h, d)
    logf = jax.nn.log_sigmoid((fa + b_f).astype(jnp.float32))
    return (heads(qa, H_A, HEAD_DIM), heads(ka, KV_A, HEAD_DIM), heads(va, KV_A, HEAD_DIM), logf,
            heads(qb, H_B, HEAD_DIM), heads(kb, KV_B, HEAD_DIM), heads(vb, KV_B, HEAD_DIM),
            heads(qi, H_IDX, D_IDX), ki, wi * (H_IDX ** -0.5))


def _fox_attend(q, qpos, cq, k, v, ck):
    B, Tq = q.shape[:2]
    L = k.shape[1]
    qg = q.reshape(B, Tq, KV_A, G_A, HEAD_DIM)
    decay = (cq.reshape(B, Tq, KV_A, G_A)[..., None]
             - ck.reshape(B, L, KV_A, G_A).transpose(0, 2, 3, 1)[:, None])
    s = jnp.einsum('bqkgd,bskd->bqkgs', qg, k).astype(jnp.float32) * (HEAD_DIM ** -0.5) + decay
    causal = (qpos[:, None] >= jnp.arange(L)[None, :])[None, :, None, None, :]
    s = jnp.where(causal, s, -jnp.inf)
    p = jax.nn.softmax(s, axis=-1).astype(v.dtype)
    o = jnp.einsum('bqkgs,bskd->bqkgd', p, v)
    return o.reshape(B, Tq, H_A * HEAD_DIM)


def _dsa_attend(q, qi, wi, qpos, k, v, ki, rel_bias):
    B, Tq = q.shape[:2]
    L = k.shape[1]
    topk = min(TOPK_MAX, L // 4)
    idx_logits = jnp.einsum('bqhd,bsd->bqhs', qi, ki).astype(jnp.float32) * (D_IDX ** -0.5)
    score = jnp.einsum('bqh,bqhs->bqs', wi.astype(jnp.float32), jax.nn.relu(idx_logits))
    causal = qpos[:, None] >= jnp.arange(L)[None, :]
    score = jnp.where(causal[None], score, -jnp.inf)
    _, sel = lax.top_k(score, topk)
    take = jax.vmap(lambda a, i: a[i])
    k_sel = take(k, sel)
    v_sel = take(v, sel)
    dist = qpos[None, :, None] - sel
    valid = dist >= 0
    bias = rel_bias[_rel_bucket(jnp.maximum(dist, 0))]
    bias = bias.reshape(B, Tq, topk, KV_B, G_B).transpose(0, 1, 3, 4, 2).astype(jnp.float32)
    qg = q.reshape(B, Tq, KV_B, G_B, HEAD_DIM)
    s = jnp.einsum('bqkgd,bqskd->bqkgs', qg, k_sel).astype(jnp.float32) * (HEAD_DIM ** -0.5) + bias
    s = jnp.where(valid[:, :, None, None, :], s, -jnp.inf)
    p = jax.nn.softmax(s, axis=-1).astype(v.dtype)
    o = jnp.einsum('bqkgs,bqskd->bqkgd', p, v_sel)
    return o.reshape(B, Tq, H_B * HEAD_DIM)


def _mixer_prompt(x, w_in_l, b_f_l, rel_bias):
    B, T = x.shape[:2]
    qa, ka, va, logf, qb, kb, vb, qi, ki, wi = _project(x, w_in_l, b_f_l)
    c = jnp.cumsum(logf, axis=1)

    def block(i):
        q0 = i * Q_BLOCK
        sl = lambda a: lax.dynamic_slice_in_dim(a, q0, Q_BLOCK, axis=1)
        qpos = q0 + jnp.arange(Q_BLOCK)
        oa = _fox_attend(sl(qa), qpos, sl(c), ka, va, c)
        ob = _dsa_attend(sl(qb), sl(qi), sl(wi), qpos, kb, vb, ki, rel_bias)
        return jnp.concatenate([oa, ob], axis=-1)

    o = lax.map(block, jnp.arange(T // Q_BLOCK))
    o = o.transpose(1, 0, 2, 3).reshape(B, T, D_MIX)
    return o, (ka, va, logf, kb, vb, ki)


def _mixer_sample(x, ca_k, ca_v, ca_f, cb_k, cb_v, ci_k, page_table, w_in_l, b_f_l, rel_bias):
    B, T = x.shape[:2]
    P = page_table.shape[1] * PAGE_SIZE
    qa, ka, va, logf, qb, kb, vb, qi, ki, wi = _project(x, w_in_l, b_f_l)

    def with_past(cache, new):
        g = cache[page_table]
        g = g.reshape((B, P) + g.shape[3:])
        return jnp.concatenate([g, new.astype(g.dtype)], axis=1)

    k_a, v_a = with_past(ca_k, ka), with_past(ca_v, va)
    k_b, v_b, k_i = with_past(cb_k, kb), with_past(cb_v, vb), with_past(ci_k, ki)
    logf_all = jnp.concatenate(
        [ca_f[page_table].reshape(B, P, H_A).astype(jnp.float32), logf], axis=1)
    suffix = lax.cumsum(logf_all, axis=1, reverse=True)
    cneg = -jnp.concatenate([suffix[:, 1:], jnp.zeros_like(suffix[:, :1])], axis=1)
    qpos = P + jnp.arange(T)
    oa = _fox_attend(qa, qpos, cneg[:, P:], k_a, v_a, cneg)
    ob = _dsa_attend(qb, qi, wi, qpos, k_b, v_b, k_i, rel_bias)
    return jnp.concatenate([oa, ob], axis=-1), (ka, va, logf, kb, vb, ki)


def _post_block(x, o, w_out_l, ln1_g_l, ln1_b_l, w_up_l, w_down_l, ln2_g_l, ln2_b_l):
    x = _layer_norm(ALPHA * x + jnp.einsum('btm,md->btd', o, w_out_l), ln1_g_l, ln1_b_l)
    h = jnp.square(jax.nn.relu(jnp.einsum('btd,df->btf', x, w_up_l)))
    h = jnp.einsum('btf,fd->btd', h, w_down_l)
    return _layer_norm(ALPHA * x + h, ln2_g_l, ln2_b_l)


def setup_inputs(seed: int = 0) -> dict:
    key = jax.random.key(seed)
    ks = jax.random.split(key, 24)
    f32 = jnp.float32
    n_pages = PAST_LEN // PAGE_SIZE
    n_used = DEC_BATCH * n_pages
    n_pool = n_used + n_used // 4
    pool = (DEPTH, n_pool, PAGE_SIZE)
    nrm = lambda k, s: jax.random.normal(k, s, f32)
    x_prompt = nrm(ks[0], (BATCH, SEQ, D_MODEL))
    x_sample = nrm(ks[1], (DEC_BATCH, DEC_SEQ, D_MODEL))
    cache_a_k = nrm(ks[2], pool + (KV_A, HEAD_DIM))
    cache_a_v = BETA * nrm(ks[3], pool + (KV_A, HEAD_DIM))
    cache_a_logf = jax.nn.log_sigmoid(FORGET_BIAS + nrm(ks[4], pool + (H_A,)))
    cache_b_k = nrm(ks[5], pool + (KV_B, HEAD_DIM))
    cache_b_v = BETA * nrm(ks[6], pool + (KV_B, HEAD_DIM))
    cache_idx_k = nrm(ks[7], pool + (D_IDX,))
    page_table = jax.random.permutation(ks[8], n_pool)[:n_used].reshape(DEC_BATCH, n_pages).astype(jnp.int32)
    col_scale = np.ones((D_IN,), np.float32)
    o_va = SPLITS[0] + SPLITS[1]
    col_scale[o_va:o_va + SPLITS[2]] = BETA
    o_vb = sum(SPLITS[:6])
    col_scale[o_vb:o_vb + SPLITS[6]] = BETA
    w_in = nrm(ks[9], (DEPTH, D_MODEL, D_IN)) * (D_MODEL ** -0.5) * jnp.asarray(col_scale)
    b_f = FORGET_BIAS + 0.1 * nrm(ks[10], (DEPTH, H_A))
    w_out = nrm(ks[11], (DEPTH, D_MIX, D_MODEL)) * (D_MIX ** -0.5) * BETA
    ln1_g = 1.0 + 0.02 * nrm(ks[12], (DEPTH, D_MODEL))
    ln1_b = 0.02 * nrm(ks[13], (DEPTH, D_MODEL))
    w_up = nrm(ks[14], (DEPTH, D_MODEL, D_FF)) * (D_MODEL ** -0.5) * BETA
    w_down = nrm(ks[15], (DEPTH, D_FF, D_MODEL)) * (D_FF ** -0.5) * BETA
    ln2_g = 1.0 + 0.02 * nrm(ks[16], (DEPTH, D_MODEL))
    ln2_b = 0.02 * nrm(ks[17], (DEPTH, D_MODEL))
    rel_bias = 0.3 * nrm(ks[18], (N_BUCKETS, H_B))
    return {'x_prompt': x_prompt, 'x_sample': x_sample,
            'cache_a_k': cache_a_k, 'cache_a_v': cache_a_v, 'cache_a_logf': cache_a_logf,
            'cache_b_k': cache_b_k, 'cache_b_v': cache_b_v, 'cache_idx_k': cache_idx_k,
            'page_table': page_table, 'w_in': w_in, 'b_f': b_f, 'w_out': w_out,
            'ln1_g': ln1_g, 'ln1_b': ln1_b, 'w_up': w_up, 'w_down': w_down,
            'ln2_g': ln2_g, 'ln2_b': ln2_b, 'rel_bias': rel_bias}


def reference(x_prompt, x_sample, cache_a_k, cache_a_v, cache_a_logf, cache_b_k, cache_b_v,
              cache_idx_k, page_table, w_in, b_f, w_out, ln1_g, ln1_b, w_up, w_down,
              ln2_g, ln2_b, rel_bias):
    xp, xs = x_prompt, x_sample
    p_rows, s_rows = [], []
    for l in range(DEPTH):
        o, rows = _mixer_prompt(xp, w_in[l], b_f[l], rel_bias)
        xp = _post_block(xp, o, w_out[l], ln1_g[l], ln1_b[l], w_up[l], w_down[l], ln2_g[l], ln2_b[l])
        p_rows.append(rows)
        o, rows = _mixer_sample(xs, cache_a_k[l], cache_a_v[l], cache_a_logf[l], cache_b_k[l],
                                cache_b_v[l], cache_idx_k[l], page_table, w_in[l], b_f[l], rel_bias)
        xs = _post_block(xs, o, w_out[l], ln1_g[l], ln1_b[l], w_up[l], w_down[l], ln2_g[l], ln2_b[l])
        s_rows.append(rows)
    st = lambda rows, j: jnp.stack([r[j] for r in rows])
    return (xp, xs,
            st(p_rows, 0), st(p_rows, 1), st(p_rows, 2), st(p_rows, 3), st(p_rows, 4), st(p_rows, 5),
            st(s_rows, 0), st(s_rows, 1), st(s_rows, 2), st(s_rows, 3), st(s_rows, 4), st(s_rows, 5))
```

```python
import functools
import math

import numpy as np
import jax
import jax.numpy as jnp
from jax import lax
from jax.experimental import pallas as pl
from jax.experimental.pallas import tpu as pltpu

HEAD_DIM = 64
H_A = 8
KV_A = 4
H_B = 8
KV_B = 4
GROUP = 2
H_IDX = 8
D_IDX = 64
TOPK_MAX = 256
N_BUCKETS = 32
MAX_DISTANCE = 128
PAGE_SIZE = 128
FORGET_BIAS = 3.0
LN_EPS = 1e-5

LANES = 128
VMEM_LIMIT_BYTES = 56 * 1024 * 1024

NEG_BIG = -1e30
INT_MIN = -(2 ** 31)

BF16 = jnp.bfloat16
F32 = jnp.float32

HEAD_PERM = (0, 2, 1, 3, 4, 6, 5, 7)


def _cparams(semantics):
    return pltpu.CompilerParams(dimension_semantics=semantics,
                                vmem_limit_bytes=VMEM_LIMIT_BYTES)


def _dot(a, b):
    return jnp.dot(a, b, preferred_element_type=F32)


def _dot_nt(a, b):
    return lax.dot_general(a, b, (((1,), (1,)), ((), ())), preferred_element_type=F32)


def _split2(x):
    hi = x.astype(BF16)
    lo = (x - hi.astype(F32)).astype(BF16)
    return hi, lo


def _split3(x):
    h1 = x.astype(BF16)
    r1 = x - h1.astype(F32)
    h2 = r1.astype(BF16)
    h3 = (r1 - h2.astype(F32)).astype(BF16)
    return h1, h2, h3


def _dot3(xh, xl, wh, wl):
    return _dot(xh, wh) + _dot(xl, wh) + _dot(xh, wl)


def _log_sigmoid(x):
    return jnp.minimum(x, 0.0) - jnp.log1p(jnp.exp(-jnp.abs(x)))


def _layer_norm(x, g, b):
    mu = jnp.mean(x, axis=-1, keepdims=True)
    xc = x - mu
    var = jnp.mean(xc * xc, axis=-1, keepdims=True)
    return xc * lax.rsqrt(var + LN_EPS) * g + b


def _rel_bucket(dist):
    max_exact = N_BUCKETS // 2
    d = jnp.maximum(dist, 1).astype(F32)
    large = max_exact + (jnp.log(d / max_exact) / math.log(MAX_DISTANCE / max_exact)
                         * (N_BUCKETS - max_exact)).astype(jnp.int32)
    large = jnp.minimum(large, N_BUCKETS - 1)
    return jnp.where(dist < max_exact, dist, large)


def _order_key(score):
    bits = pltpu.bitcast(score, jnp.int32)
    key = bits ^ (lax.shift_right_arithmetic(bits, 31) & jnp.int32(0x7FFFFFFF))
    return jnp.where(bits == jnp.int32(INT_MIN), 0, key)


def _half_masked(x, hi_half):
    lane = lax.broadcasted_iota(jnp.int32, x.shape, 1)
    keep = (lane >= HEAD_DIM) if hi_half else (lane < HEAD_DIM)
    return jnp.where(keep, x.astype(F32), 0.0).astype(BF16)


def _proj_kernel(*refs, with_cumsum, rows_per_seq_tiles):
    if with_cumsum:
        (x_ref, wbig_ref, wih_ref, wil_ref, wfh_ref, wfl_ref, wwh_ref, wwl_ref, bf_ref, tri_ref,
         qa_ref, qb_ref, ka_ref, va_ref, kb_ref, vb_ref, qih_ref, qil_ref, ki_ref, kih_ref, kil_ref,
         wi_ref, logf_ref, c_ref, carry_ref) = refs
    else:
        (x_ref, wbig_ref, wih_ref, wil_ref, wfh_ref, wfl_ref, wwh_ref, wwl_ref, bf_ref,
         qa_ref, qb_ref, ka_ref, va_ref, kb_ref, vb_ref, qih_ref, qil_ref, ki_ref, kih_ref, kil_ref,
         wi_ref, logf_ref) = refs
    x = x_ref[...]
    xh, xl = _split2(x)
    nq = H_A * HEAD_DIM
    nk = KV_A * HEAD_DIM
    qa_ref[...] = _dot(xh, wbig_ref[:, 0:nq]).astype(BF16)
    qb_ref[...] = _dot(xh, wbig_ref[:, nq:2 * nq]).astype(BF16)
    o = 2 * nq
    ka_ref[...] = _dot(xh, wbig_ref[:, o:o + nk])
    va_ref[...] = _dot(xh, wbig_ref[:, o + nk:o + 2 * nk])
    kb_ref[...] = _dot(xh, wbig_ref[:, o + 2 * nk:o + 3 * nk])
    vb_ref[...] = _dot(xh, wbig_ref[:, o + 3 * nk:o + 4 * nk])

    ni = H_IDX * D_IDX
    qi = _dot3(xh, xl, wih_ref[:, 0:ni], wil_ref[:, 0:ni])
    qh, ql = _split2(qi)
    qih_ref[...] = qh
    qil_ref[...] = ql
    ki2 = _dot3(xh, xl, wih_ref[:, ni:ni + LANES], wil_ref[:, ni:ni + LANES])
    ki_ref[...] = ki2[:, 0:D_IDX]
    kh, kl = _split2(ki2)
    kih_ref[...] = kh
    kil_ref[...] = kl

    wi = _dot3(xh, xl, wwh_ref[...], wwl_ref[...])
    wi_ref[...] = (wi * (H_IDX ** -0.5))[:, 0:H_IDX]
    fa = _dot3(xh, xl, wfh_ref[...], wfl_ref[...])
    logf = _log_sigmoid(fa + bf_ref[...])
    logf_ref[...] = logf[:, 0:H_A]

    if with_cumsum:
        @pl.when(pl.program_id(0) % rows_per_seq_tiles == 0)
        def _():
            carry_ref[...] = jnp.zeros_like(carry_ref)

        h1, h2, h3 = _split3(logf)
        tri = tri_ref[...]
        c = (_dot(tri, h1) + _dot(tri, h2)) + _dot(tri, h3) + carry_ref[...]
        c_ref[...] = c[:, 0:H_A]
        carry_ref[...] = c[c.shape[0] - 1:c.shape[0], :]


def _project(x2d, wts, bf_pad, *, rows_per_seq, with_cumsum):
    m, d = x2d.shape
    tm = min(512, rows_per_seq) if with_cumsum else min(512, m)
    assert m % tm == 0 and (not with_cumsum or rows_per_seq % tm == 0)
    nq, nk, ni = H_A * HEAD_DIM, KV_A * HEAD_DIM, H_IDX * D_IDX
    full = lambda a: pl.BlockSpec(a.shape, lambda i: (0,) * a.ndim)
    row = lambda n: pl.BlockSpec((tm, n), lambda i: (i, 0))
    inputs = [x2d, wts['big'], wts['idx_hi'], wts['idx_lo'], wts['f_hi'], wts['f_lo'],
              wts['w_hi'], wts['w_lo'], bf_pad]
    in_specs = [row(d)] + [full(a) for a in inputs[1:]]
    out_shapes = [((m, nq), BF16), ((m, nq), BF16), ((m, nk), F32), ((m, nk), F32), ((m, nk), F32),
                  ((m, nk), F32), ((m, ni), BF16), ((m, ni), BF16), ((m, D_IDX), F32),
                  ((m, LANES), BF16), ((m, LANES), BF16), ((m, H_IDX), F32), ((m, H_A), F32)]
    scratch = []
    if with_cumsum:
        tri = jnp.tril(jnp.ones((tm, tm), F32)).astype(BF16)
        inputs.append(tri)
        in_specs.append(full(tri))
        out_shapes.append(((m, H_A), F32))
        scratch = [pltpu.VMEM((1, LANES), F32)]
    outs = pl.pallas_call(
        functools.partial(_proj_kernel, with_cumsum=with_cumsum, rows_per_seq_tiles=rows_per_seq // tm),
        grid=(m // tm,),
        in_specs=in_specs,
        out_specs=[row(s[0][1]) for s in out_shapes],
        out_shape=[jax.ShapeDtypeStruct(*s) for s in out_shapes],
        scratch_shapes=scratch,
        compiler_params=_cparams(("arbitrary",)),
        name="proj_cumsum" if with_cumsum else "proj",
    )(*inputs)
    names = ['qa', 'qb', 'ka', 'va', 'kb', 'vb', 'qi_hi', 'qi_lo', 'ki', 'ki2_hi', 'ki2_lo', 'wi', 'logf']
    if with_cumsum:
        names.append('c')
    return dict(zip(names, outs))


def _post_kernel(x_ref, oa_ref, ob_ref, woa_ref, wob_ref, g1_ref, b1_ref, wup_ref, wdn_ref,
                 g2_ref, b2_ref, y_ref, acc_ref, *, alpha, n_chunks):
    x = x_ref[...]
    att = _dot(oa_ref[...], woa_ref[...]) + _dot(ob_ref[...], wob_ref[...])
    x1 = _layer_norm(alpha * x + att, g1_ref[...], b1_ref[...])
    x1b = x1.astype(BF16)
    acc_ref[...] = jnp.zeros_like(acc_ref)

    def body(c, carry):
        u = jnp.maximum(_dot(x1b, wup_ref[c]), 0.0)
        acc_ref[...] += _dot((u * u).astype(BF16), wdn_ref[c])
        return carry

    lax.fori_loop(0, n_chunks, body, 0)
    y_ref[...] = _layer_norm(alpha * x1 + acc_ref[...], g2_ref[...], b2_ref[...])


def _post_block(x2d, oa, ob, wts, alpha):
    m, d = x2d.shape
    tm = min(256, m)
    n_chunks = wts['up'].shape[0]
    full = lambda a: pl.BlockSpec(a.shape, lambda i: (0,) * a.ndim)
    row = lambda n: pl.BlockSpec((tm, n), lambda i: (i, 0))
    weights = [wts['out_a'], wts['out_b'], wts['ln1_g'], wts['ln1_b'], wts['up'], wts['down'],
               wts['ln2_g'], wts['ln2_b']]
    return pl.pallas_call(
        functools.partial(_post_kernel, alpha=alpha, n_chunks=n_chunks),
        grid=(m // tm,),
        in_specs=[row(d), row(oa.shape[1]), row(ob.shape[1])] + [full(w) for w in weights],
        out_specs=row(d),
        out_shape=jax.ShapeDtypeStruct((m, d), F32),
        scratch_shapes=[pltpu.VMEM((tm, d), F32)],
        compiler_params=_cparams(("arbitrary",)),
        name="post_block",
    )(x2d, oa, ob, *weights)


def _stack_variants(q2):
    g0, g1 = q2[:, 0:LANES], q2[:, LANES:2 * LANES]
    return jnp.concatenate([_half_masked(g0, False), _half_masked(g0, True),
                            _half_masked(g1, False), _half_masked(g1, True)], axis=0)


def _flash_init(m_ref, l_ref, acc_ref):
    m_ref[...] = jnp.full_like(m_ref, NEG_BIG)
    l_ref[...] = jnp.zeros_like(l_ref)
    acc_ref[...] = jnp.zeros_like(acc_ref)


def _flash_step(qs, kblk, vblk, biases, keep, m_ref, l_ref, acc_ref, tq):
    s_all = _dot_nt(qs, kblk)
    for v in range(2 * GROUP):
        rows = slice(v * tq, (v + 1) * tq)
        s = s_all[rows] + biases[v]
        if keep is not None:
            s = jnp.where(keep, s, NEG_BIG)
        m_old = m_ref[rows]
        m_new = jnp.maximum(m_old, jnp.max(s, axis=1, keepdims=True))
        a = jnp.exp(m_old - m_new)
        p = jnp.exp(s - m_new)
        l_ref[rows] = a * l_ref[rows] + jnp.sum(p, axis=1, keepdims=True)
        acc_ref[rows] = a * acc_ref[rows] + _dot(p.astype(BF16), vblk)
        m_ref[rows] = m_new


def _flash_finish(l_ref, acc_ref, tq):
    o = [acc_ref[v * tq:(v + 1) * tq] / l_ref[v * tq:(v + 1) * tq] for v in range(2 * GROUP)]
    lane = lax.broadcasted_iota(jnp.int32, o[0].shape, 1)
    lo = lane < HEAD_DIM
    return jnp.concatenate([jnp.where(lo, o[0], o[1]), jnp.where(lo, o[2], o[3])],
                           axis=1).astype(BF16)


def _causal_keep(tq):
    r = lax.broadcasted_iota(jnp.int32, (tq, tq), 0)
    c = lax.broadcasted_iota(jnp.int32, (tq, tq), 1)
    return c <= r


def _fox_kernel(q_ref, k_ref, v_ref, ck_ref, o_ref, m_ref, l_ref, acc_ref, *, tq):
    i = pl.program_id(2)
    qs = _stack_variants(q_ref[...])
    _flash_init(m_ref, l_ref, acc_ref)

    def step(j, keep):
        start = pl.multiple_of(j * tq, tq)
        kblk = k_ref[pl.ds(start, tq), :].astype(BF16)
        vblk = v_ref[pl.ds(start, tq), :].astype(BF16)
        ck = ck_ref[j]
        biases = [-ck[v:v + 1, :] for v in range(2 * GROUP)]
        _flash_step(qs, kblk, vblk, biases, keep, m_ref, l_ref, acc_ref, tq)

    def body(j, carry):
        step(j, None)
        return carry

    lax.fori_loop(0, i, body, 0)
    step(i, _causal_keep(tq))
    o_ref[...] = _flash_finish(l_ref, acc_ref, tq)


def _fox_attention(q, k, v, ck, batch, seq, tq):
    nq = seq // tq
    return pl.pallas_call(
        functools.partial(_fox_kernel, tq=tq),
        grid=(batch, KV_A // 2, nq),
        in_specs=[
            pl.BlockSpec((tq, 2 * LANES), lambda b, m, i: (b * nq + i, m)),
            pl.BlockSpec((seq, LANES), lambda b, m, i: (b, m)),
            pl.BlockSpec((seq, LANES), lambda b, m, i: (b, m)),
            pl.BlockSpec((None, None, nq, 2 * GROUP, tq), lambda b, m, i: (b, m, 0, 0, 0)),
        ],
        out_specs=pl.BlockSpec((tq, 2 * LANES), lambda b, m, i: (b * nq + i, m)),
        out_shape=jax.ShapeDtypeStruct(q.shape, BF16),
        scratch_shapes=[pltpu.VMEM((4 * tq, 1), F32), pltpu.VMEM((4 * tq, 1), F32),
                        pltpu.VMEM((4 * tq, LANES), F32)],
        compiler_params=_cparams(("arbitrary", "arbitrary", "arbitrary")),
        name="fox_attention",
    )(q, k, v, ck)


def _relb_kernel(rb_ref, o_ref, *, tq):
    delta = pl.program_id(0)
    h = pl.program_id(1)
    r = lax.broadcasted_iota(jnp.int32, (tq, tq), 0)
    c = lax.broadcasted_iota(jnp.int32, (tq, tq), 1)
    bucket = _rel_bucket(jnp.maximum(delta * tq + r - c, 0))
    acc = jnp.zeros((tq, tq), F32)
    for n in range(N_BUCKETS):
        acc = jnp.where(bucket == n, rb_ref[n, h], acc)
    o_ref[...] = acc


def _rel_bias_tiles(rb_perm, tq):
    return pl.pallas_call(
        functools.partial(_relb_kernel, tq=tq),
        grid=(2, H_B),
        in_specs=[pl.BlockSpec(memory_space=pltpu.SMEM)],
        out_specs=pl.BlockSpec((None, None, tq, tq), lambda d, h: (d, h, 0, 0)),
        out_shape=jax.ShapeDtypeStruct((2, H_B, tq, tq), F32),
        compiler_params=_cparams(("arbitrary", "arbitrary")),
        name="rel_bias_tiles",
    )(rb_perm)


def _strict_upper(n):
    r = lax.broadcasted_iota(jnp.int32, (n, n), 0)
    c = lax.broadcasted_iota(jnp.int32, (n, n), 1)
    return jnp.where(r < c, 1.0, 0.0).astype(BF16)


def _dsa_kernel(qih_ref, qil_ref, kih_ref, kil_ref, wi_ref, qb_ref, k_ref, v_ref, relb_ref, rbfar_ref,
                o_ref, keys_ref, mb_ref, m_ref, l_ref, acc_ref, *, tq, topk):
    i = pl.program_id(1)
    nblk = i + 1

    qh, ql = qih_ref[...], qil_ref[...]
    rows = []
    for h in range(H_IDX):
        blk = slice((h // 2) * LANES, (h // 2 + 1) * LANES)
        mh = _half_masked(qh[:, blk], h % 2 == 1)
        ml = _half_masked(ql[:, blk], h % 2 == 1)
        rows.append(jnp.concatenate([mh, ml, mh], axis=1))
    qi = jnp.concatenate(rows, axis=0)
    wi = wi_ref[...]
    keep_diag = _causal_keep(tq)

    def score_block(j, diag):
        start = pl.multiple_of(j * tq, tq)
        kh = kih_ref[pl.ds(start, tq), :]
        kl = kil_ref[pl.ds(start, tq), :]
        logits = _dot_nt(qi, jnp.concatenate([kh, kh, kl], axis=1))
        sc = jnp.zeros((tq, tq), F32)
        for h in range(H_IDX):
            sc = sc + wi[:, h:h + 1] * jnp.maximum(logits[h * tq:(h + 1) * tq], 0.0)
        key = _order_key(sc)
        if diag:
            key = jnp.where(keep_diag, key, INT_MIN)
        keys_ref[j] = key

    def score_body(j, carry):
        score_block(j, False)
        return carry

    lax.fori_loop(0, i, score_body, 0)
    score_block(i, True)

    needs_select = (i + 1) * tq > topk

    @pl.when(jnp.logical_not(needs_select))
    def _():
        def zero_body(j, carry):
            mb_ref[j] = jnp.zeros((tq, tq), F32)
            return carry
        lax.fori_loop(0, nblk, zero_body, 0)

    @pl.when(needs_select)
    def _():
        def count(pred_fn):
            def body(j, acc):
                hit = jnp.where(pred_fn(keys_ref[j]), 1.0, 0.0)
                for c in range(tq // LANES):
                    acc = acc + hit[:, c * LANES:(c + 1) * LANES]
                return acc
            acc = lax.fori_loop(0, nblk, body, jnp.zeros((tq, LANES), F32))
            return jnp.sum(acc, axis=1, keepdims=True)

        def bit_body(b, t):
            cand = t + lax.shift_left(jnp.int32(1), 31 - b)
            return jnp.where(count(lambda kblk: kblk >= cand) >= topk, cand, t)

        thr = lax.fori_loop(0, 32, bit_body, jnp.full((tq, 1), INT_MIN, jnp.int32))
        need = topk - count(lambda kblk: kblk > thr)
        sut = _strict_upper(tq)

        def mask_body(j, seen):
            kblk = keys_ref[j]
            eq = kblk == thr
            eqf = jnp.where(eq, 1.0, 0.0)
            rank = _dot(eqf.astype(BF16), sut) + seen
            mb_ref[j] = jnp.where(kblk > thr, 0.0,
                                  jnp.where(eq, jnp.where(rank < need, 0.0, NEG_BIG), NEG_BIG))
            return seen + jnp.sum(eqf, axis=1, keepdims=True)

        lax.fori_loop(0, nblk, mask_body, jnp.zeros((tq, 1), F32))

    for m in range(KV_B // 2):
        qs = _stack_variants(qb_ref[:, m * 2 * LANES:(m + 1) * 2 * LANES])
        _flash_init(m_ref, l_ref, acc_ref)
        lanes = slice(m * LANES, (m + 1) * LANES)

        def step(j, rel, keep):
            start = pl.multiple_of(j * tq, tq)
            kblk = k_ref[pl.ds(start, tq), lanes].astype(BF16)
            vblk = v_ref[pl.ds(start, tq), lanes].astype(BF16)
            mb = mb_ref[j]
            biases = [mb + rel(v) for v in range(2 * GROUP)]
            _flash_step(qs, kblk, vblk, biases, keep, m_ref, l_ref, acc_ref, tq)

        def far_body(j, carry):
            step(j, lambda v: rbfar_ref[4 * m + v], None)
            return carry

        lax.fori_loop(0, jnp.maximum(i - 1, 0), far_body, 0)

        @pl.when(i >= 1)
        def _():
            step(i - 1, lambda v: relb_ref[1, 4 * m + v], None)

        step(i, lambda v: relb_ref[0, 4 * m + v], keep_diag)
        o_ref[:, m * 2 * LANES:(m + 1) * 2 * LANES] = _flash_finish(l_ref, acc_ref, tq)


def _dsa_attention(p, relb, rb_far, batch, seq, tq, topk):
    nq = seq // tq
    nqc = H_B * HEAD_DIM
    qrow = lambda n: pl.BlockSpec((tq, n), lambda b, i: (b * nq + i, 0))
    seqblk = lambda n: pl.BlockSpec((seq, n), lambda b, i: (b, 0))
    return pl.pallas_call(
        functools.partial(_dsa_kernel, tq=tq, topk=topk),
        grid=(batch, nq),
        in_specs=[qrow(H_IDX * D_IDX), qrow(H_IDX * D_IDX), seqblk(LANES), seqblk(LANES), qrow(H_IDX),
                  qrow(nqc), seqblk(KV_B * HEAD_DIM), seqblk(KV_B * HEAD_DIM),
                  pl.BlockSpec(relb.shape, lambda b, i: (0, 0, 0, 0)),
                  pl.BlockSpec(memory_space=pltpu.SMEM)],
        out_specs=qrow(nqc),
        out_shape=jax.ShapeDtypeStruct((batch * seq, nqc), BF16),
        scratch_shapes=[pltpu.VMEM((nq, tq, tq), jnp.int32), pltpu.VMEM((nq, tq, tq), F32),
                        pltpu.VMEM((4 * tq, 1), F32), pltpu.VMEM((4 * tq, 1), F32),
                        pltpu.VMEM((4 * tq, LANES), F32)],
        compiler_params=_cparams(("arbitrary", "arbitrary")),
        name="dsa_attention",
    )(p['qi_hi'], p['qi_lo'], p['ki2_hi'], p['ki2_lo'], p['wi'], p['qb'], p['kb'], p['vb'], relb, rb_far)


def _page_specs(n, block, layer, pages_per_step):
    def make(r):
        return pl.BlockSpec((None, None) + block,
                            lambda b, g, pt: (layer, pt[b, g * pages_per_step + r]) + (0,) * len(block))
    return [make(r) for r in range(n)]


def _sscore_kernel(pt_ref, q16_ref, wi_ref, *refs, pps):
    page_refs, out_ref = refs[:pps], refs[pps]
    q16 = q16_ref[...]
    qh = q16[0:H_IDX]
    wi = wi_ref[...]
    for r in range(pps):
        kh, kl = _split2(page_refs[r][...])
        a = _dot(q16, kh)
        logits = (a[0:H_IDX] + a[H_IDX:2 * H_IDX]) + _dot(qh, kl)
        out_ref[:, r * PAGE_SIZE:(r + 1) * PAGE_SIZE] = jnp.sum(
            wi * jnp.maximum(logits, 0.0), axis=0, keepdims=True)


def _sample_scores(page_table, q16, wi_col, cache_idx_k, layer, pps):
    db, n_pages = page_table.shape
    grid_spec = pltpu.PrefetchScalarGridSpec(
        num_scalar_prefetch=1,
        grid=(db, n_pages // pps),
        in_specs=[pl.BlockSpec((None, 2 * H_IDX, D_IDX), lambda b, g, pt: (b, 0, 0)),
                  pl.BlockSpec((None, H_IDX, 1), lambda b, g, pt: (b, 0, 0))]
                 + _page_specs(pps, (D_IDX, PAGE_SIZE), layer, pps),
        out_specs=pl.BlockSpec((None, 1, pps * PAGE_SIZE), lambda b, g, pt: (b, 0, g)),
    )
    return pl.pallas_call(
        functools.partial(_sscore_kernel, pps=pps),
        grid_spec=grid_spec,
        out_shape=jax.ShapeDtypeStruct((db, 1, n_pages * PAGE_SIZE), F32),
        compiler_params=_cparams(("arbitrary", "arbitrary")),
        name="sample_scores",
    )(page_table, q16, wi_col, *([cache_idx_k] * pps))


def _sdecay_kernel(pt_ref, lnew_ref, *refs, pps):
    page_refs, out_ref, carry_ref = refs[:pps], refs[pps], refs[pps + 1]

    @pl.when(pl.program_id(1) == 0)
    def _():
        carry_ref[...] = jnp.broadcast_to(lnew_ref[...], carry_ref.shape)

    r_i = lax.broadcasted_iota(jnp.int32, (PAGE_SIZE, PAGE_SIZE), 0)
    c_i = lax.broadcasted_iota(jnp.int32, (PAGE_SIZE, PAGE_SIZE), 1)
    later = jnp.where(r_i > c_i, 1.0, 0.0).astype(BF16)
    for r in reversed(range(pps)):
        x = page_refs[r][...]
        h1, h2, h3 = _split3(x)
        within = (_dot(h1, later) + _dot(h2, later)) + _dot(h3, later)
        carry = carry_ref[...]
        out_ref[:, r * PAGE_SIZE:(r + 1) * PAGE_SIZE] = within + carry[:, 0:1]
        carry_ref[...] = carry + jnp.sum(x, axis=1, keepdims=True)


def _sample_decay(page_table, logf_new_col, logf_t_cache, layer, pps):
    db, n_pages = page_table.shape
    nsteps = n_pages // pps

    def page_spec(r):
        return pl.BlockSpec((None, None, H_A, PAGE_SIZE),
                            lambda b, g, pt: (layer, pt[b, (nsteps - 1 - g) * pps + r], 0, 0))

    grid_spec = pltpu.PrefetchScalarGridSpec(
        num_scalar_prefetch=1,
        grid=(db, nsteps),
        in_specs=[pl.BlockSpec((None, H_A, 1), lambda b, g, pt: (b, 0, 0))]
                 + [page_spec(r) for r in range(pps)],
        out_specs=pl.BlockSpec((None, H_A, pps * PAGE_SIZE), lambda b, g, pt: (b, 0, nsteps - 1 - g)),
        scratch_shapes=[pltpu.VMEM((H_A, LANES), F32)],
    )
    return pl.pallas_call(
        functools.partial(_sdecay_kernel, pps=pps),
        grid_spec=grid_spec,
        out_shape=jax.ShapeDtypeStruct((db, H_A, n_pages * PAGE_SIZE), F32),
        compiler_params=_cparams(("arbitrary", "arbitrary")),
        name="sample_decay",
    )(page_table, logf_new_col, *([logf_t_cache] * pps))


def _sselect_kernel(sc_ref, q_ref, kt_ref, wi_ref, mb_ref, mbn_ref, keys_ref, *, topk, chunk):
    db, past = sc_ref.shape
    prod = q_ref[...] * kt_ref[...]
    lane = lax.broadcasted_iota(jnp.int32, prod.shape, 1)
    wi = wi_ref[...]
    sc_new = jnp.zeros((db, 1), F32)
    for h in range(H_IDX):
        seg = (lane >= h * D_IDX) & (lane < (h + 1) * D_IDX)
        logit = jnp.sum(jnp.where(seg, prod, 0.0), axis=1, keepdims=True)
        sc_new = sc_new + wi[:, h:h + 1] * jnp.maximum(logit, 0.0)
    key_new = _order_key(sc_new)
    keys_ref[...] = _order_key(sc_ref[...])

    def count(pred_fn):
        n = jnp.sum(jnp.where(pred_fn(keys_ref[...]), 1.0, 0.0), axis=1, keepdims=True)
        return n + jnp.where(pred_fn(key_new), 1.0, 0.0)

    def bit_body(b, t):
        cand = t + lax.shift_left(jnp.int32(1), 31 - b)
        return jnp.where(count(lambda k: k >= cand) >= topk, cand, t)

    thr = lax.fori_loop(0, 32, bit_body, jnp.full((db, 1), INT_MIN, jnp.int32))
    need = topk - count(lambda k: k > thr)
    sut = _strict_upper(chunk)
    seen = jnp.zeros((db, 1), F32)
    for c in range(past // chunk):
        cols = slice(c * chunk, (c + 1) * chunk)
        kblk = keys_ref[:, cols]
        eq = kblk == thr
        eqf = jnp.where(eq, 1.0, 0.0)
        rank = _dot(eqf.astype(BF16), sut) + seen
        mb_ref[:, cols] = jnp.where(kblk > thr, 0.0,
                                    jnp.where(eq, jnp.where(rank < need, 0.0, NEG_BIG), NEG_BIG))
        seen = seen + jnp.sum(eqf, axis=1, keepdims=True)
    sel_new = jnp.where(key_new > thr, 0.0,
                        jnp.where(key_new == thr, jnp.where(seen < need, 0.0, NEG_BIG), NEG_BIG))
    mbn_ref[...] = jnp.broadcast_to(sel_new, mbn_ref.shape)


def _sample_select(scores, q_f32, ki_tiled, wi, topk):
    db, past = scores.shape
    return pl.pallas_call(
        functools.partial(_sselect_kernel, topk=topk, chunk=2 * LANES),
        out_shape=[jax.ShapeDtypeStruct((db, past), F32), jax.ShapeDtypeStruct((db, LANES), F32)],
        scratch_shapes=[pltpu.VMEM((db, past), jnp.int32)],
        compiler_params=pltpu.CompilerParams(vmem_limit_bytes=VMEM_LIMIT_BYTES),
        name="sample_select",
    )(scores, q_f32, ki_tiled, wi)


def _sattn_kernel(pt_ref, qa_ref, qb_ref, dec_ref, mb_ref, rbt_ref, kan_ref, van_ref, kbn_ref, vbn_ref,
                  mbn_ref, *refs, pps, past):
    ak, av = refs[0:pps], refs[pps:2 * pps]
    bk, bv = refs[2 * pps:3 * pps], refs[3 * pps:4 * pps]
    oa_ref, ob_ref, ma_ref, la_ref, acca_ref, mb_m_ref, lb_ref, accb_ref = refs[4 * pps:]
    g = pl.program_id(1)
    width = pps * PAGE_SIZE

    @pl.when(g == 0)
    def _():
        _flash_init(ma_ref, la_ref, acca_ref)
        _flash_init(mb_m_ref, lb_ref, accb_ref)

    def update(s, v_refs, m_ref, l_ref, acc_ref):
        m_old = m_ref[...]
        m_new = jnp.maximum(m_old, jnp.max(s, axis=1, keepdims=True))
        a = jnp.exp(m_old - m_new)
        p = jnp.exp(s - m_new)
        l_ref[...] = a * l_ref[...] + jnp.sum(p, axis=1, keepdims=True)
        pv = jnp.zeros(acc_ref.shape, F32)
        for r in range(pps):
            pv = pv + _dot_nt(p[:, r * PAGE_SIZE:(r + 1) * PAGE_SIZE].astype(BF16),
                              v_refs[r][...].astype(BF16))
        acc_ref[...] = a * acc_ref[...] + pv
        m_ref[...] = m_new

    qa, qb = qa_ref[...], qb_ref[...]
    s_a = jnp.concatenate([_dot(qa, ak[r][...].astype(BF16)) for r in range(pps)], axis=1)
    update(s_a + dec_ref[...], av, ma_ref, la_ref, acca_ref)

    pos = g * width + lax.broadcasted_iota(jnp.int32, (1, width), 1)
    bucket = _rel_bucket(past - pos)
    rbt = rbt_ref[...]
    rel = jnp.zeros((H_B, width), F32)
    for n in range(N_BUCKETS):
        rel = jnp.where(bucket == n, rbt[:, n:n + 1], rel)
    s_b = jnp.concatenate([_dot(qb, bk[r][...].astype(BF16)) for r in range(pps)], axis=1)
    update(s_b + rel + mb_ref[...], bv, mb_m_ref, lb_ref, accb_ref)

    @pl.when(g == pl.num_programs(1) - 1)
    def _():
        def finish(q, kn, vn, extra, m_ref, l_ref, acc_ref, o_ref):
            s = jnp.sum(q.astype(F32) * kn, axis=1, keepdims=True) + extra
            m_old = m_ref[...]
            m_new = jnp.maximum(m_old, s)
            a = jnp.exp(m_old - m_new)
            p = jnp.exp(s - m_new)
            l = a * l_ref[...] + p
            o_ref[...] = (a * acc_ref[...] + p * vn) / l

        finish(qa, kan_ref[...], van_ref[...], 0.0, ma_ref, la_ref, acca_ref, oa_ref)
        finish(qb, kbn_ref[...], vbn_ref[...], rbt[:, 0:1] + mbn_ref[:, 0:1],
               mb_m_ref, lb_ref, accb_ref, ob_ref)


def _sample_attention(page_table, qa_blk, qb_blk, dec, mb, rbt, new, caches, layer, pps):
    db, n_pages = page_table.shape
    past = n_pages * PAGE_SIZE
    width = KV_A * HEAD_DIM
    per_b = lambda shape: pl.BlockSpec((None,) + shape, lambda b, g, pt: (b,) + (0,) * len(shape))
    chunked = lambda rows: pl.BlockSpec((None, rows, pps * PAGE_SIZE), lambda b, g, pt: (b, 0, g))
    in_specs = [per_b((H_A, width)), per_b((H_B, width)), chunked(H_A), chunked(1),
                pl.BlockSpec(rbt.shape, lambda b, g, pt: (0, 0)),
                per_b((1, width)), per_b((1, width)), per_b((1, width)), per_b((1, width)),
                per_b((1, LANES))]
    for _ in range(4):
        in_specs += _page_specs(pps, (width, PAGE_SIZE), layer, pps)
    grid_spec = pltpu.PrefetchScalarGridSpec(
        num_scalar_prefetch=1,
        grid=(db, n_pages // pps),
        in_specs=in_specs,
        out_specs=[per_b((H_A, width)), per_b((H_B, width))],
        scratch_shapes=[pltpu.VMEM((H_A, 1), F32), pltpu.VMEM((H_A, 1), F32), pltpu.VMEM((H_A, width), F32),
                        pltpu.VMEM((H_B, 1), F32), pltpu.VMEM((H_B, 1), F32), pltpu.VMEM((H_B, width), F32)],
    )
    pages = []
    for c in caches:
        pages += [c] * pps
    return pl.pallas_call(
        functools.partial(_sattn_kernel, pps=pps, past=past),
        grid_spec=grid_spec,
        out_shape=[jax.ShapeDtypeStruct((db, H_A, width), F32), jax.ShapeDtypeStruct((db, H_B, width), F32)],
        compiler_params=_cparams(("arbitrary", "arbitrary")),
        name="sample_attention",
    )(page_table, qa_blk, qb_blk, dec, mb, rbt, new['ka'], new['va'], new['kb'], new['vb'], new['mbn'], *pages)


def _head_cols(w, perm):
    d = w.shape[0]
    return w.reshape(d, len(perm), HEAD_DIM)[:, list(perm), :].reshape(d, -1)


def _pad_cols(w, n):
    return jnp.pad(w, ((0, 0), (0, n - w.shape[1])))


def _prep_proj_weights(w_in_l, perm):
    sizes = (H_A * HEAD_DIM, KV_A * HEAD_DIM, KV_A * HEAD_DIM, H_A, H_B * HEAD_DIM, KV_B * HEAD_DIM,
             KV_B * HEAD_DIM, H_IDX * D_IDX, D_IDX, H_IDX)
    offs = np.cumsum((0,) + sizes)
    qa, ka, va, fa, qb, kb, vb, qi, ki, wi = [w_in_l[:, offs[n]:offs[n + 1]] for n in range(len(sizes))]
    scale = HEAD_DIM ** -0.5
    big = jnp.concatenate([_head_cols(qa, perm) * scale, _head_cols(qb, perm) * scale, ka, va, kb, vb],
                          axis=1).astype(BF16)
    idx = jnp.concatenate([qi * (D_IDX ** -0.5), ki, ki], axis=1)
    idx_hi, idx_lo = _split2(idx)
    f_hi, f_lo = _split2(_pad_cols(fa, LANES))
    w_hi, w_lo = _split2(_pad_cols(wi, LANES))
    return dict(big=big, idx_hi=idx_hi, idx_lo=idx_lo, f_hi=f_hi, f_lo=f_lo, w_hi=w_hi, w_lo=w_lo)


def _prep_post_weights(w_out_l, perm_a, perm_b, ln1_g, ln1_b, w_up_l, w_down_l, ln2_g, ln2_b, ff_chunk):
    d_mix, d = w_out_l.shape
    na = H_A * HEAD_DIM
    rows = lambda w, perm: w.reshape(len(perm), HEAD_DIM, d)[np.asarray(perm)].reshape(-1, d)
    d_ff = w_up_l.shape[1]
    nc = d_ff // ff_chunk
    row = lambda a: a.reshape(1, -1)
    return dict(out_a=rows(w_out_l[:na], perm_a).astype(BF16), out_b=rows(w_out_l[na:], perm_b).astype(BF16),
                ln1_g=row(ln1_g), ln1_b=row(ln1_b), ln2_g=row(ln2_g), ln2_b=row(ln2_b),
                up=w_up_l.reshape(d, nc, ff_chunk).transpose(1, 0, 2).astype(BF16),
                down=w_down_l.reshape(nc, ff_chunk, d).astype(BF16))


ATTN_BLOCK = 256
FF_CHUNK = 512
SCORE_PAGES_PER_STEP = 16
DECAY_PAGES_PER_STEP = 16
ATTN_PAGES_PER_STEP = 8


def _prompt_layer(x2d, batch, seq, wproj, wpost, bf_pad, relb, rb_far, alpha):
    tq = min(ATTN_BLOCK, seq)
    p = _project(x2d, wproj, bf_pad, rows_per_seq=seq, with_cumsum=True)
    nkv = seq // tq
    perm = list(HEAD_PERM)
    ck = p['c'].reshape(batch, seq, H_A)[:, :, perm].reshape(batch, nkv, tq, KV_A // 2, 2 * GROUP)
    ck = ck.transpose(0, 3, 1, 4, 2)
    oa = _fox_attention(p['qa'], p['ka'], p['va'], ck, batch, seq, tq)
    ob = _dsa_attention(p, relb, rb_far, batch, seq, tq, min(TOPK_MAX, seq // 4))
    y = _post_block(x2d, oa, ob, wpost, alpha)
    return y, (p['ka'], p['va'], p['logf'], p['kb'], p['vb'], p['ki'])


def _block_diag_q(q):
    db = q.shape[0]
    qh = q.reshape(db, H_A, 1, HEAD_DIM)
    kv_of_head = jnp.arange(H_A) // GROUP
    onehot = (kv_of_head[:, None] == jnp.arange(KV_A)[None, :])[None, :, :, None]
    return jnp.where(onehot, qh, jnp.zeros((), q.dtype)).reshape(db, H_A, KV_A * HEAD_DIM)


def _diag_heads(o_wide):
    db = o_wide.shape[0]
    o = o_wide.reshape(db, H_A, KV_A, HEAD_DIM)
    idx = (jnp.arange(H_A) // GROUP)[None, :, None, None]
    return jnp.take_along_axis(o, jnp.broadcast_to(idx, (db, H_A, 1, HEAD_DIM)), axis=2).reshape(db, -1)


def _sample_layer(x2d, page_table, caches, layer, wproj, wpost, bf_pad, rbt, alpha):
    db = x2d.shape[0]
    n_pages = page_table.shape[1]
    past = n_pages * PAGE_SIZE
    p = _project(x2d, wproj, bf_pad, rows_per_seq=1, with_cumsum=False)
    q16 = jnp.concatenate([p['qi_hi'].reshape(db, H_IDX, D_IDX), p['qi_lo'].reshape(db, H_IDX, D_IDX)], axis=1)
    scores = _sample_scores(page_table, q16, p['wi'].reshape(db, H_IDX, 1), caches['idx_k'], layer,
                            min(SCORE_PAGES_PER_STEP, n_pages))
    dec = _sample_decay(page_table, p['logf'].reshape(db, H_A, 1), caches['logf_t'], layer,
                        min(DECAY_PAGES_PER_STEP, n_pages))
    q_f32 = p['qi_hi'].astype(F32) + p['qi_lo'].astype(F32)
    mb, mbn = _sample_select(scores.reshape(db, past), q_f32, jnp.tile(p['ki'], (1, H_IDX)), p['wi'],
                             min(TOPK_MAX, (past + 1) // 4))
    new = dict(ka=p['ka'].reshape(db, 1, -1), va=p['va'].reshape(db, 1, -1), kb=p['kb'].reshape(db, 1, -1),
               vb=p['vb'].reshape(db, 1, -1), mbn=mbn.reshape(db, 1, LANES))
    oa_w, ob_w = _sample_attention(page_table, _block_diag_q(p['qa']), _block_diag_q(p['qb']), dec,
                                   mb.reshape(db, 1, past), rbt, new,
                                   (caches['a_k'], caches['a_v'], caches['b_k'], caches['b_v']), layer,
                                   min(ATTN_PAGES_PER_STEP, n_pages))
    y = _post_block(x2d, _diag_heads(oa_w).astype(BF16), _diag_heads(ob_w).astype(BF16), wpost, alpha)
    return y, (p['ka'], p['va'], p['logf'], p['kb'], p['vb'], p['ki'])


def kernel(x_prompt, x_sample, cache_a_k, cache_a_v, cache_a_logf, cache_b_k, cache_b_v, cache_idx_k,
           page_table, w_in, b_f, w_out, ln1_g, ln1_b, w_up, w_down, ln2_g, ln2_b, rel_bias):
    depth = w_in.shape[0]
    batch, seq, d_model = x_prompt.shape
    db, dec_seq, _ = x_sample.shape
    assert dec_seq == 1
    alpha = (2 * depth) ** 0.25
    natural = tuple(range(H_A))
    tq = min(ATTN_BLOCK, seq)

    kv_t = lambda c: c.transpose(0, 1, 3, 4, 2).reshape(c.shape[:2] + (-1, PAGE_SIZE))
    caches = dict(a_k=kv_t(cache_a_k), a_v=kv_t(cache_a_v), b_k=kv_t(cache_b_k), b_v=kv_t(cache_b_v),
                  idx_k=cache_idx_k.transpose(0, 1, 3, 2),
                  logf_t=cache_a_logf.astype(F32).transpose(0, 1, 3, 2))
    relb = _rel_bias_tiles(rel_bias[:, list(HEAD_PERM)], tq)
    rb_far = rel_bias[N_BUCKETS - 1, list(HEAD_PERM)]
    rbt = rel_bias.T

    xp = x_prompt.reshape(batch * seq, d_model)
    xs = x_sample.reshape(db, d_model)
    p_rows, s_rows = [], []
    for l in range(depth):
        bf_pad = _pad_cols(b_f[l].reshape(1, -1), LANES)
        post = lambda perm: _prep_post_weights(w_out[l], perm, perm, ln1_g[l], ln1_b[l], w_up[l], w_down[l],
                                               ln2_g[l], ln2_b[l], FF_CHUNK)
        xp, rows = _prompt_layer(xp, batch, seq, _prep_proj_weights(w_in[l], HEAD_PERM), post(HEAD_PERM),
                                 bf_pad, relb, rb_far, alpha)
        p_rows.append(rows)
        xs, rows = _sample_layer(xs, page_table, caches, l, _prep_proj_weights(w_in[l], natural),
                                 post(natural), bf_pad, rbt, alpha)
        s_rows.append(rows)

    def stack(rows, j, lead, tail):
        return jnp.stack([r[j] for r in rows]).reshape((depth,) + lead + tail)

    pl_, sl_ = (batch, seq), (db, dec_seq)
    kv = (KV_A, HEAD_DIM)
    return (xp.reshape(batch, seq, d_model), xs.reshape(db, dec_seq, d_model),
            stack(p_rows, 0, pl_, kv), stack(p_rows, 1, pl_, kv), stack(p_rows, 2, pl_, (H_A,)),
            stack(p_rows, 3, pl_, kv), stack(p_rows, 4, pl_, kv), stack(p_rows, 5, pl_, (D_IDX,)),
            stack(s_rows, 0, sl_, kv), stack(s_rows, 1, sl_, kv), stack(s_rows, 2, sl_, (H_A,)),
            stack(s_rows, 3, sl_, kv), stack(s_rows, 4, sl_, kv), stack(s_rows, 5, sl_, (D_IDX,)))
```

```python
import functools
import math

import numpy as np
import jax
import jax.numpy as jnp
from jax import lax
from jax.experimental import pallas as pl
from jax.experimental.pallas import tpu as pltpu

HEAD_DIM = 64
H_A = 8
KV_A = 4
H_B = 8
KV_B = 4
GROUP = 2
H_IDX = 8
D_IDX = 64
TOPK_MAX = 256
N_BUCKETS = 32
MAX_DISTANCE = 128
PAGE_SIZE = 128
FORGET_BIAS = 3.0
LN_EPS = 1e-5

LANES = 128
VMEM_LIMIT_BYTES = 56 * 1024 * 1024

NEG_BIG = -1e30
INT_MIN = -(2 ** 31)

BF16 = jnp.bfloat16
F32 = jnp.float32

HEAD_PERM = (0, 2, 1, 3, 4, 6, 5, 7)


def _cparams(semantics):
    return pltpu.CompilerParams(dimension_semantics=semantics,
                                vmem_limit_bytes=VMEM_LIMIT_BYTES)


def _dot(a, b):
    return jnp.dot(a, b, preferred_element_type=F32)


def _dot_nt(a, b):
    return lax.dot_general(a, b, (((1,), (1,)), ((), ())), preferred_element_type=F32)


def _split2(x):
    hi = x.astype(BF16)
    lo = (x - hi.astype(F32)).astype(BF16)
    return hi, lo


def _split3(x):
    h1 = x.astype(BF16)
    r1 = x - h1.astype(F32)
    h2 = r1.astype(BF16)
    h3 = (r1 - h2.astype(F32)).astype(BF16)
    return h1, h2, h3


def _dot3(xh, xl, wh, wl):
    return _dot(xh, wh) + _dot(xl, wh) + _dot(xh, wl)


def _log_sigmoid(x):
    return jnp.minimum(x, 0.0) - jnp.log1p(jnp.exp(-jnp.abs(x)))


def _layer_norm(x, g, b):
    mu = jnp.mean(x, axis=-1, keepdims=True)
    xc = x - mu
    var = jnp.mean(xc * xc, axis=-1, keepdims=True)
    return xc * lax.rsqrt(var + LN_EPS) * g + b


def _rel_bucket(dist):
    max_exact = N_BUCKETS // 2
    d = jnp.maximum(dist, 1).astype(F32)
    large = max_exact + (jnp.log(d / max_exact) / math.log(MAX_DISTANCE / max_exact)
                         * (N_BUCKETS - max_exact)).astype(jnp.int32)
    large = jnp.minimum(large, N_BUCKETS - 1)
    return jnp.where(dist < max_exact, dist, large)


def _order_key(score):
    bits = pltpu.bitcast(score, jnp.int32)
    key = bits ^ (lax.shift_right_arithmetic(bits, 31) & jnp.int32(0x7FFFFFFF))
    return jnp.where(bits == jnp.int32(INT_MIN), 0, key)


def _half_masked(x, hi_half):
    lane = lax.broadcasted_iota(jnp.int32, x.shape, 1)
    keep = (lane >= HEAD_DIM) if hi_half else (lane < HEAD_DIM)
    return jnp.where(keep, x.astype(F32), 0.0).astype(BF16)


def _proj_kernel(*refs, with_cumsum, rows_per_seq_tiles):
    if with_cumsum:
        (x_ref, wbig_ref, wih_ref, wil_ref, bf_ref, tri_ref,
         qa_ref, qb_ref, ka_ref, va_ref, kb_ref, vb_ref, qih_ref, qil_ref, ki_ref, kih_ref, kil_ref,
         wi_ref, logf_ref, c_ref, carry_ref) = refs
    else:
        (x_ref, wbig_ref, wih_ref, wil_ref, bf_ref,
         qa_ref, qb_ref, ka_ref, va_ref, kb_ref, vb_ref, qih_ref, qil_ref, ki_ref, kih_ref, kil_ref,
         wi_ref, logf_ref) = refs
    x = x_ref[...]
    xh, xl = _split2(x)
    nq = H_A * HEAD_DIM
    nk = KV_A * HEAD_DIM
    qa_ref[...] = _dot(xh, wbig_ref[:, 0:nq]).astype(BF16)
    qb_ref[...] = _dot(xh, wbig_ref[:, nq:2 * nq]).astype(BF16)
    o = 2 * nq
    ka_ref[...] = _dot(xh, wbig_ref[:, o:o + nk])
    va_ref[...] = _dot(xh, wbig_ref[:, o + nk:o + 2 * nk])
    kb_ref[...] = _dot(xh, wbig_ref[:, o + 2 * nk:o + 3 * nk])
    vb_ref[...] = _dot(xh, wbig_ref[:, o + 3 * nk:o + 4 * nk])

    ni = H_IDX * D_IDX
    qi = _dot3(xh, xl, wih_ref[:, 0:ni], wil_ref[:, 0:ni])
    qh, ql = _split2(qi)
    qih_ref[...] = qh
    qil_ref[...] = ql
    rest = _dot3(xh, xl, wih_ref[:, ni:ni + 2 * LANES], wil_ref[:, ni:ni + 2 * LANES])
    ki2 = rest[:, 0:LANES]
    ki_ref[...] = ki2[:, 0:D_IDX]
    kh, kl = _split2(ki2)
    kih_ref[...] = kh
    kil_ref[...] = kl

    small = rest[:, LANES:2 * LANES]
    wi_ref[...] = small[:, H_A:H_A + H_IDX] * (H_IDX ** -0.5)
    logf = _log_sigmoid(small + bf_ref[...])
    logf_ref[...] = logf[:, 0:H_A]

    if with_cumsum:
        @pl.when(pl.program_id(0) % rows_per_seq_tiles == 0)
        def _():
            carry_ref[...] = jnp.zeros_like(carry_ref)

        h1, h2, h3 = _split3(logf)
        tri = tri_ref[...]
        c = (_dot(tri, h1) + _dot(tri, h2)) + _dot(tri, h3) + carry_ref[...]
        c_ref[...] = c[:, 0:H_A]
        carry_ref[...] = c[c.shape[0] - 1:c.shape[0], :]


def _project(x2d, wts, bf_pad, *, rows_per_seq, with_cumsum):
    m, d = x2d.shape
    tm = min(512, rows_per_seq) if with_cumsum else min(512, m)
    assert m % tm == 0 and (not with_cumsum or rows_per_seq % tm == 0)
    nq, nk, ni = H_A * HEAD_DIM, KV_A * HEAD_DIM, H_IDX * D_IDX
    full = lambda a: pl.BlockSpec(a.shape, lambda i: (0,) * a.ndim)
    row = lambda n: pl.BlockSpec((tm, n), lambda i: (i, 0))
    inputs = [x2d, wts['big'], wts['idx_hi'], wts['idx_lo'], bf_pad]
    in_specs = [row(d)] + [full(a) for a in inputs[1:]]
    out_shapes = [((m, nq), BF16), ((m, nq), BF16), ((m, nk), F32), ((m, nk), F32), ((m, nk), F32),
                  ((m, nk), F32), ((m, ni), BF16), ((m, ni), BF16), ((m, D_IDX), F32),
                  ((m, LANES), BF16), ((m, LANES), BF16), ((m, H_IDX), F32), ((m, H_A), F32)]
    scratch = []
    if with_cumsum:
        tri = jnp.tril(jnp.ones((tm, tm), F32)).astype(BF16)
        inputs.append(tri)
        in_specs.append(full(tri))
        out_shapes.append(((m, H_A), F32))
        scratch = [pltpu.VMEM((1, LANES), F32)]
    outs = pl.pallas_call(
        functools.partial(_proj_kernel, with_cumsum=with_cumsum, rows_per_seq_tiles=rows_per_seq // tm),
        grid=(m // tm,),
        in_specs=in_specs,
        out_specs=[row(s[0][1]) for s in out_shapes],
        out_shape=[jax.ShapeDtypeStruct(*s) for s in out_shapes],
        scratch_shapes=scratch,
        compiler_params=_cparams(("arbitrary",)),
        name="proj_cumsum" if with_cumsum else "proj",
    )(*inputs)
    names = ['qa', 'qb', 'ka', 'va', 'kb', 'vb', 'qi_hi', 'qi_lo', 'ki', 'ki2_hi', 'ki2_lo', 'wi', 'logf']
    if with_cumsum:
        names.append('c')
    return dict(zip(names, outs))


def _post_kernel(x_ref, oa_ref, ob_ref, woa_ref, wob_ref, g1_ref, b1_ref, wup_ref, wdn_ref,
                 g2_ref, b2_ref, y_ref, acc_ref, *, alpha, n_chunks):
    x = x_ref[...]
    att = _dot(oa_ref[...], woa_ref[...]) + _dot(ob_ref[...], wob_ref[...])
    x1 = _layer_norm(alpha * x + att, g1_ref[...], b1_ref[...])
    x1b = x1.astype(BF16)
    acc_ref[...] = jnp.zeros_like(acc_ref)

    def body(c, carry):
        u = jnp.maximum(_dot(x1b, wup_ref[c]), 0.0)
        acc_ref[...] += _dot((u * u).astype(BF16), wdn_ref[c])
        return carry

    lax.fori_loop(0, n_chunks, body, 0)
    y_ref[...] = _layer_norm(alpha * x1 + acc_ref[...], g2_ref[...], b2_ref[...])


def _post_block(x2d, oa, ob, wts, alpha):
    m, d = x2d.shape
    tm = min(POST_ROWS, m)
    n_chunks = wts['up'].shape[0]
    full = lambda a: pl.BlockSpec(a.shape, lambda i: (0,) * a.ndim, pipeline_mode=pl.Buffered(1))
    row = lambda n: pl.BlockSpec((tm, n), lambda i: (i, 0))
    weights = [wts['out_a'], wts['out_b'], wts['ln1_g'], wts['ln1_b'], wts['up'], wts['down'],
               wts['ln2_g'], wts['ln2_b']]
    return pl.pallas_call(
        functools.partial(_post_kernel, alpha=alpha, n_chunks=n_chunks),
        grid=(m // tm,),
        in_specs=[row(d), row(oa.shape[1]), row(ob.shape[1])] + [full(w) for w in weights],
        out_specs=row(d),
        out_shape=jax.ShapeDtypeStruct((m, d), F32),
        scratch_shapes=[pltpu.VMEM((tm, d), F32)],
        compiler_params=_cparams(("arbitrary",)),
        name="post_block",
    )(x2d, oa, ob, *weights)


ROW_CHUNK = 32


def _stack_variants(q2, qv_ref):
    tq = q2.shape[0]
    g0, g1 = q2[:, 0:LANES], q2[:, LANES:2 * LANES]
    for v, (g, hi) in enumerate(((g0, False), (g0, True), (g1, False), (g1, True))):
        qv_ref[v * tq:(v + 1) * tq, :] = _half_masked(g, hi)


def _flash_init(m_ref, acc_ref):
    m_ref[...] = jnp.full_like(m_ref, NEG_BIG)
    acc_ref[...] = jnp.zeros_like(acc_ref)


def _value_blocks(vblk_f32):
    lane = lax.broadcasted_iota(jnp.int32, vblk_f32.shape, 1)
    lo = lane < HEAD_DIM
    return (jnp.where(lo, vblk_f32, 1.0).astype(BF16), jnp.where(lo, 1.0, vblk_f32).astype(BF16))


def _flash_step(state, kblk, vblks, bias_fn, causal, tq):
    qv_ref, s_ref, p_ref, m_ref, acc_ref = state
    tk = kblk.shape[0]
    for v in range(2 * GROUP):
        base = v * tq
        s_ref[v] = _dot_nt(qv_ref[base:base + tq, :], kblk)
        for r in range(tq // ROW_CHUNK):
            rows = slice(r * ROW_CHUNK, (r + 1) * ROW_CHUNK)
            grows = slice(base + r * ROW_CHUNK, base + (r + 1) * ROW_CHUNK)
            s = s_ref[v, rows, :] + bias_fn(v, rows)
            if causal:
                ri = lax.broadcasted_iota(jnp.int32, (ROW_CHUNK, tk), 0) + r * ROW_CHUNK
                ci = lax.broadcasted_iota(jnp.int32, (ROW_CHUNK, tk), 1)
                s = jnp.where(ci <= ri, s, NEG_BIG)
            m_old = m_ref[grows, :]
            m_new = jnp.maximum(m_old, jnp.max(s, axis=1, keepdims=True))
            m_ref[grows, :] = m_new
            p = jnp.exp(s - jnp.concatenate([m_new] * (tk // LANES), axis=1))
            p_ref[v, rows, :] = p.astype(BF16)
            acc_ref[grows, :] = acc_ref[grows, :] * jnp.exp(m_old - m_new)
        acc_ref[base:base + tq, :] += _dot(p_ref[v], vblks[v % 2])


def _flash_finish(acc_ref, tq):
    o = []
    for v in range(2 * GROUP):
        acc = acc_ref[v * tq:(v + 1) * tq, :]
        l_lane = 0 if v % 2 else HEAD_DIM
        o.append(acc / acc[:, l_lane:l_lane + 1])
    lane = lax.broadcasted_iota(jnp.int32, o[0].shape, 1)
    lo = lane < HEAD_DIM
    return jnp.concatenate([jnp.where(lo, o[0], o[1]), jnp.where(lo, o[2], o[3])],
                           axis=1).astype(BF16)


def _flash_scratch(tq):
    return [pltpu.VMEM((4 * tq, LANES), BF16), pltpu.VMEM((2 * GROUP, tq, tq), F32),
            pltpu.VMEM((2 * GROUP, tq, tq), BF16), pltpu.VMEM((4 * tq, LANES), F32),
            pltpu.VMEM((4 * tq, LANES), F32)]


def _causal_keep(tq):
    r = lax.broadcasted_iota(jnp.int32, (tq, tq), 0)
    c = lax.broadcasted_iota(jnp.int32, (tq, tq), 1)
    return c <= r


def _fox_kernel(q_ref, k_ref, v_ref, ck_ref, o_ref, *state, tq):
    i = pl.program_id(2)
    _stack_variants(q_ref[...], state[0])
    _flash_init(state[3], state[4])

    def step(j, causal):
        start = pl.multiple_of(j * tq, tq)
        kblk = k_ref[pl.ds(start, tq), :].astype(BF16)
        vblks = _value_blocks(v_ref[pl.ds(start, tq), :])
        _flash_step(state, kblk, vblks, lambda v, rows: -ck_ref[j, v:v + 1, :], causal, tq)

    def body(j, carry):
        step(j, False)
        return carry

    lax.fori_loop(0, i, body, 0)
    step(i, True)
    o_ref[...] = _flash_finish(state[4], tq)


def _fox_attention(q, k, v, ck, batch, seq, tq):
    nq = seq // tq
    return pl.pallas_call(
        functools.partial(_fox_kernel, tq=tq),
        grid=(batch, KV_A // 2, nq),
        in_specs=[
            pl.BlockSpec((tq, 2 * LANES), lambda b, m, i: (b * nq + i, m)),
            pl.BlockSpec((seq, LANES), lambda b, m, i: (b, m)),
            pl.BlockSpec((seq, LANES), lambda b, m, i: (b, m)),
            pl.BlockSpec((None, None, nq, 2 * GROUP, tq), lambda b, m, i: (b, m, 0, 0, 0)),
        ],
        out_specs=pl.BlockSpec((tq, 2 * LANES), lambda b, m, i: (b * nq + i, m)),
        out_shape=jax.ShapeDtypeStruct(q.shape, BF16),
        scratch_shapes=_flash_scratch(tq),
        compiler_params=_cparams(("arbitrary", "arbitrary", "arbitrary")),
        name="fox_attention",
    )(q, k, v, ck)


def _relb_kernel(rb_ref, o_ref, *, tq):
    delta = pl.program_id(0)
    h = pl.program_id(1)
    r = lax.broadcasted_iota(jnp.int32, (tq, tq), 0)
    c = lax.broadcasted_iota(jnp.int32, (tq, tq), 1)
    bucket = _rel_bucket(jnp.maximum(delta * tq + r - c, 0))
    acc = jnp.zeros((tq, tq), F32)
    for n in range(N_BUCKETS):
        acc = jnp.where(bucket == n, rb_ref[n, h], acc)
    o_ref[...] = acc


def _rel_bias_tiles(rb_perm, tq):
    return pl.pallas_call(
        functools.partial(_relb_kernel, tq=tq),
        grid=(2, H_B),
        in_specs=[pl.BlockSpec(memory_space=pltpu.SMEM)],
        out_specs=pl.BlockSpec((None, None, tq, tq), lambda d, h: (d, h, 0, 0)),
        out_shape=jax.ShapeDtypeStruct((2, H_B, tq, tq), F32),
        compiler_params=_cparams(("arbitrary", "arbitrary")),
        name="rel_bias_tiles",
    )(rb_perm)


def _strict_upper(n):
    r = lax.broadcasted_iota(jnp.int32, (n, n), 0)
    c = lax.broadcasted_iota(jnp.int32, (n, n), 1)
    return jnp.where(r < c, 1.0, 0.0).astype(BF16)


def _dsa_kernel(qih_ref, qil_ref, kih_ref, kil_ref, wi_ref, qb_ref, k_ref, v_ref, relb_ref, rbfar_ref,
                o_ref, keys_ref, mb_ref, *state, tq, topk):
    i = pl.program_id(1)
    nblk = i + 1

    qh, ql = qih_ref[...], qil_ref[...]
    rows = []
    for h in range(H_IDX):
        blk = slice((h // 2) * LANES, (h // 2 + 1) * LANES)
        mh = _half_masked(qh[:, blk], h % 2 == 1)
        ml = _half_masked(ql[:, blk], h % 2 == 1)
        rows.append(jnp.concatenate([mh, ml, mh], axis=1))
    qi = jnp.concatenate(rows, axis=0)
    wi = wi_ref[...]
    keep_diag = _causal_keep(tq)

    def score_block(j, diag):
        start = pl.multiple_of(j * tq, tq)
        kh = kih_ref[pl.ds(start, tq), :]
        kl = kil_ref[pl.ds(start, tq), :]
        logits = _dot_nt(qi, jnp.concatenate([kh, kh, kl], axis=1))
        sc = jnp.zeros((tq, tq), F32)
        for h in range(H_IDX):
            sc = sc + wi[:, h:h + 1] * jnp.maximum(logits[h * tq:(h + 1) * tq], 0.0)
        key = _order_key(sc)
        if diag:
            key = jnp.where(keep_diag, key, INT_MIN)
        keys_ref[j] = key

    def score_body(j, carry):
        score_block(j, False)
        return carry

    lax.fori_loop(0, i, score_body, 0)
    score_block(i, True)

    needs_select = (i + 1) * tq > topk

    @pl.when(jnp.logical_not(needs_select))
    def _():
        def zero_body(j, carry):
            mb_ref[j] = jnp.zeros((tq, tq), F32)
            return carry
        lax.fori_loop(0, nblk, zero_body, 0)

    @pl.when(needs_select)
    def _():
        def count(pred_fn):
            def body(j, acc):
                hit = jnp.where(pred_fn(keys_ref[j]), 1.0, 0.0)
                for c in range(tq // LANES):
                    acc = acc + hit[:, c * LANES:(c + 1) * LANES]
                return acc
            acc = lax.fori_loop(0, nblk, body, jnp.zeros((tq, LANES), F32))
            return jnp.sum(acc, axis=1, keepdims=True)

        def bit_body(b, t):
            cand = t + lax.shift_left(jnp.int32(1), 31 - b)
            return jnp.where(count(lambda kblk: kblk >= cand) >= topk, cand, t)

        thr = lax.fori_loop(0, 32, bit_body, jnp.full((tq, 1), INT_MIN, jnp.int32))
        need = topk - count(lambda kblk: kblk > thr)
        sut = _strict_upper(tq)

        def mask_body(j, seen):
            kblk = keys_ref[j]
            eq = kblk == thr
            eqf = jnp.where(eq, 1.0, 0.0)
            rank = _dot(eqf.astype(BF16), sut) + seen
            mb_ref[j] = jnp.where(kblk > thr, 0.0,
                                  jnp.where(eq, jnp.where(rank < need, 0.0, NEG_BIG), NEG_BIG))
            return seen + jnp.sum(eqf, axis=1, keepdims=True)

        lax.fori_loop(0, nblk, mask_body, jnp.zeros((tq, 1), F32))

    for m in range(KV_B // 2):
        _stack_variants(qb_ref[:, m * 2 * LANES:(m + 1) * 2 * LANES], state[0])
        _flash_init(state[3], state[4])
        lanes = slice(m * LANES, (m + 1) * LANES)

        def step(j, rel, causal):
            start = pl.multiple_of(j * tq, tq)
            kblk = k_ref[pl.ds(start, tq), lanes].astype(BF16)
            vblks = _value_blocks(v_ref[pl.ds(start, tq), lanes])
            _flash_step(state, kblk, vblks, lambda v, rows: mb_ref[j, rows, :] + rel(v, rows), causal, tq)

        def far_body(j, carry):
            step(j, lambda v, rows: rbfar_ref[4 * m + v], False)
            return carry

        lax.fori_loop(0, jnp.maximum(i - 1, 0), far_body, 0)

        @pl.when(i >= 1)
        def _():
            step(i - 1, lambda v, rows: relb_ref[1, 4 * m + v, rows, :], False)

        step(i, lambda v, rows: relb_ref[0, 4 * m + v, rows, :], True)
        o_ref[:, m * 2 * LANES:(m + 1) * 2 * LANES] = _flash_finish(state[4], tq)


def _dsa_attention(p, relb, rb_far, batch, seq, tq, topk):
    nq = seq // tq
    nqc = H_B * HEAD_DIM
    qrow = lambda n: pl.BlockSpec((tq, n), lambda b, i: (b * nq + i, 0))
    seqblk = lambda n: pl.BlockSpec((seq, n), lambda b, i: (b, 0))
    return pl.pallas_call(
        functools.partial(_dsa_kernel, tq=tq, topk=topk),
        grid=(batch, nq),
        in_specs=[qrow(H_IDX * D_IDX), qrow(H_IDX * D_IDX), seqblk(LANES), seqblk(LANES), qrow(H_IDX),
                  qrow(nqc), seqblk(KV_B * HEAD_DIM), seqblk(KV_B * HEAD_DIM),
                  pl.BlockSpec(relb.shape, lambda b, i: (0, 0, 0, 0)),
                  pl.BlockSpec(memory_space=pltpu.SMEM)],
        out_specs=qrow(nqc),
        out_shape=jax.ShapeDtypeStruct((batch * seq, nqc), BF16),
        scratch_shapes=[pltpu.VMEM((nq, tq, tq), jnp.int32), pltpu.VMEM((nq, tq, tq), F32)]
                       + _flash_scratch(tq),
        compiler_params=_cparams(("arbitrary", "arbitrary")),
        name="dsa_attention",
    )(p['qi_hi'], p['qi_lo'], p['ki2_hi'], p['ki2_lo'], p['wi'], p['qb'], p['kb'], p['vb'], relb, rb_far)


def _page_specs(n, block, layer, pages_per_step):
    def make(r):
        return pl.BlockSpec((None, None) + block,
                            lambda b, g, pt: (layer, pt[b, g * pages_per_step + r]) + (0,) * len(block))
    return [make(r) for r in range(n)]


def _sscore_kernel(pt_ref, q16_ref, wi_ref, *refs, pps):
    page_refs, out_ref = refs[:pps], refs[pps]
    q16 = q16_ref[...]
    qh = q16[0:H_IDX]
    wi = wi_ref[...]
    for r in range(pps):
        kh, kl = _split2(page_refs[r][...])
        a = _dot(q16, kh)
        logits = (a[0:H_IDX] + a[H_IDX:2 * H_IDX]) + _dot(qh, kl)
        out_ref[:, r * PAGE_SIZE:(r + 1) * PAGE_SIZE] = jnp.sum(
            wi * jnp.maximum(logits, 0.0), axis=0, keepdims=True)


def _sample_scores(page_table, q16, wi_col, cache_idx_k, layer, pps):
    db, n_pages = page_table.shape
    grid_spec = pltpu.PrefetchScalarGridSpec(
        num_scalar_prefetch=1,
        grid=(db, n_pages // pps),
        in_specs=[pl.BlockSpec((None, 2 * H_IDX, D_IDX), lambda b, g, pt: (b, 0, 0)),
                  pl.BlockSpec((None, H_IDX, 1), lambda b, g, pt: (b, 0, 0))]
                 + _page_specs(pps, (D_IDX, PAGE_SIZE), layer, pps),
        out_specs=pl.BlockSpec((None, 1, pps * PAGE_SIZE), lambda b, g, pt: (b, 0, g)),
    )
    return pl.pallas_call(
        functools.partial(_sscore_kernel, pps=pps),
        grid_spec=grid_spec,
        out_shape=jax.ShapeDtypeStruct((db, 1, n_pages * PAGE_SIZE), F32),
        compiler_params=_cparams(("arbitrary", "arbitrary")),
        name="sample_scores",
    )(page_table, q16, wi_col, *([cache_idx_k] * pps))


def _sdecay_kernel(pt_ref, lnew_ref, *refs, pps):
    page_refs, out_ref, carry_ref = refs[:pps], refs[pps], refs[pps + 1]

    @pl.when(pl.program_id(1) == 0)
    def _():
        carry_ref[...] = jnp.broadcast_to(lnew_ref[...], carry_ref.shape)

    r_i = lax.broadcasted_iota(jnp.int32, (PAGE_SIZE, PAGE_SIZE), 0)
    c_i = lax.broadcasted_iota(jnp.int32, (PAGE_SIZE, PAGE_SIZE), 1)
    later = jnp.where(r_i > c_i, 1.0, 0.0).astype(BF16)
    for r in reversed(range(pps)):
        x = page_refs[r][...]
        h1, h2, h3 = _split3(x)
        within = (_dot(h1, later) + _dot(h2, later)) + _dot(h3, later)
        carry = carry_ref[...]
        out_ref[:, r * PAGE_SIZE:(r + 1) * PAGE_SIZE] = within + carry[:, 0:1]
        carry_ref[...] = carry + jnp.sum(x, axis=1, keepdims=True)


def _sample_decay(page_table, logf_new_col, logf_t_cache, layer, pps):
    db, n_pages = page_table.shape
    nsteps = n_pages // pps

    def page_spec(r):
        return pl.BlockSpec((None, None, H_A, PAGE_SIZE),
                            lambda b, g, pt: (layer, pt[b, (nsteps - 1 - g) * pps + r], 0, 0))

    grid_spec = pltpu.PrefetchScalarGridSpec(
        num_scalar_prefetch=1,
        grid=(db, nsteps),
        in_specs=[pl.BlockSpec((None, H_A, 1), lambda b, g, pt: (b, 0, 0))]
                 + [page_spec(r) for r in range(pps)],
        out_specs=pl.BlockSpec((None, H_A, pps * PAGE_SIZE), lambda b, g, pt: (b, 0, nsteps - 1 - g)),
        scratch_shapes=[pltpu.VMEM((H_A, LANES), F32)],
    )
    return pl.pallas_call(
        functools.partial(_sdecay_kernel, pps=pps),
        grid_spec=grid_spec,
        out_shape=jax.ShapeDtypeStruct((db, H_A, n_pages * PAGE_SIZE), F32),
        compiler_params=_cparams(("arbitrary", "arbitrary")),
        name="sample_decay",
    )(page_table, logf_new_col, *([logf_t_cache] * pps))


def _sselect_kernel(sc_ref, q_ref, kt_ref, wi_ref, mb_ref, mbn_ref, keys_ref, *, topk, chunk):
    db, past = sc_ref.shape
    prod = q_ref[...] * kt_ref[...]
    lane = lax.broadcasted_iota(jnp.int32, prod.shape, 1)
    wi = wi_ref[...]
    sc_new = jnp.zeros((db, 1), F32)
    for h in range(H_IDX):
        seg = (lane >= h * D_IDX) & (lane < (h + 1) * D_IDX)
        logit = jnp.sum(jnp.where(seg, prod, 0.0), axis=1, keepdims=True)
        sc_new = sc_new + wi[:, h:h + 1] * jnp.maximum(logit, 0.0)
    key_new = _order_key(sc_new)
    keys_ref[...] = _order_key(sc_ref[...])

    def count(pred_fn):
        n = jnp.sum(jnp.where(pred_fn(keys_ref[...]), 1.0, 0.0), axis=1, keepdims=True)
        return n + jnp.where(pred_fn(key_new), 1.0, 0.0)

    def bit_body(b, t):
        cand = t + lax.shift_left(jnp.int32(1), 31 - b)
        return jnp.where(count(lambda k: k >= cand) >= topk, cand, t)

    thr = lax.fori_loop(0, 32, bit_body, jnp.full((db, 1), INT_MIN, jnp.int32))
    need = topk - count(lambda k: k > thr)
    sut = _strict_upper(chunk)
    seen = jnp.zeros((db, 1), F32)
    for c in range(past // chunk):
        cols = slice(c * chunk, (c + 1) * chunk)
        kblk = keys_ref[:, cols]
        eq = kblk == thr
        eqf = jnp.where(eq, 1.0, 0.0)
        rank = _dot(eqf.astype(BF16), sut) + seen
        mb_ref[:, cols] = jnp.where(kblk > thr, 0.0,
                                    jnp.where(eq, jnp.where(rank < need, 0.0, NEG_BIG), NEG_BIG))
        seen = seen + jnp.sum(eqf, axis=1, keepdims=True)
    sel_new = jnp.where(key_new > thr, 0.0,
                        jnp.where(key_new == thr, jnp.where(seen < need, 0.0, NEG_BIG), NEG_BIG))
    mbn_ref[...] = jnp.broadcast_to(sel_new, mbn_ref.shape)


def _sample_select(scores, q_f32, ki_tiled, wi, topk):
    db, past = scores.shape
    return pl.pallas_call(
        functools.partial(_sselect_kernel, topk=topk, chunk=2 * LANES),
        out_shape=[jax.ShapeDtypeStruct((db, past), F32), jax.ShapeDtypeStruct((db, LANES), F32)],
        scratch_shapes=[pltpu.VMEM((db, past), jnp.int32)],
        compiler_params=pltpu.CompilerParams(vmem_limit_bytes=VMEM_LIMIT_BYTES),
        name="sample_select",
    )(scores, q_f32, ki_tiled, wi)


def _sattn_kernel(pt_ref, qa_ref, qb_ref, dec_ref, mb_ref, rbt_ref, kan_ref, van_ref, kbn_ref, vbn_ref,
                  mbn_ref, *refs, pps, past):
    ak, av = refs[0:pps], refs[pps:2 * pps]
    bk, bv = refs[2 * pps:3 * pps], refs[3 * pps:4 * pps]
    oa_ref, ob_ref, ma_ref, la_ref, acca_ref, mb_m_ref, lb_ref, accb_ref = refs[4 * pps:]
    g = pl.program_id(1)
    width = pps * PAGE_SIZE

    @pl.when(g == 0)
    def _():
        for m_ref, l_ref, acc_ref in ((ma_ref, la_ref, acca_ref), (mb_m_ref, lb_ref, accb_ref)):
            m_ref[...] = jnp.full_like(m_ref, NEG_BIG)
            l_ref[...] = jnp.zeros_like(l_ref)
            acc_ref[...] = jnp.zeros_like(acc_ref)

    def update(s, v_refs, m_ref, l_ref, acc_ref):
        m_old = m_ref[...]
        m_new = jnp.maximum(m_old, jnp.max(s, axis=1, keepdims=True))
        a = jnp.exp(m_old - m_new)
        p = jnp.exp(s - m_new)
        l_ref[...] = a * l_ref[...] + jnp.sum(p, axis=1, keepdims=True)
        pv = jnp.zeros(acc_ref.shape, F32)
        for r in range(pps):
            pv = pv + _dot_nt(p[:, r * PAGE_SIZE:(r + 1) * PAGE_SIZE].astype(BF16),
                              v_refs[r][...].astype(BF16))
        acc_ref[...] = a * acc_ref[...] + pv
        m_ref[...] = m_new

    qa, qb = qa_ref[...], qb_ref[...]
    s_a = jnp.concatenate([_dot(qa, ak[r][...].astype(BF16)) for r in range(pps)], axis=1)
    update(s_a + dec_ref[...], av, ma_ref, la_ref, acca_ref)

    pos = g * width + lax.broadcasted_iota(jnp.int32, (1, width), 1)
    bucket = _rel_bucket(past - pos)
    rbt = rbt_ref[...]
    rel = jnp.zeros((H_B, width), F32)
    for n in range(N_BUCKETS):
        rel = jnp.where(bucket == n, rbt[:, n:n + 1], rel)
    s_b = jnp.concatenate([_dot(qb, bk[r][...].astype(BF16)) for r in range(pps)], axis=1)
    update(s_b + rel + mb_ref[...], bv, mb_m_ref, lb_ref, accb_ref)

    @pl.when(g == pl.num_programs(1) - 1)
    def _():
        def finish(q, kn, vn, extra, m_ref, l_ref, acc_ref, o_ref):
            s = jnp.sum(q.astype(F32) * kn, axis=1, keepdims=True) + extra
            m_old = m_ref[...]
            m_new = jnp.maximum(m_old, s)
            a = jnp.exp(m_old - m_new)
            p = jnp.exp(s - m_new)
            l = a * l_ref[...] + p
            o_ref[...] = (a * acc_ref[...] + p * vn) / l

        finish(qa, kan_ref[...], van_ref[...], 0.0, ma_ref, la_ref, acca_ref, oa_ref)
        finish(qb, kbn_ref[...], vbn_ref[...], rbt[:, 0:1] + mbn_ref[:, 0:1],
               mb_m_ref, lb_ref, accb_ref, ob_ref)


def _sample_attention(page_table, qa_blk, qb_blk, dec, mb, rbt, new, caches, layer, pps):
    db, n_pages = page_table.shape
    past = n_pages * PAGE_SIZE
    width = KV_A * HEAD_DIM
    per_b = lambda shape: pl.BlockSpec((None,) + shape, lambda b, g, pt: (b,) + (0,) * len(shape))
    chunked = lambda rows: pl.BlockSpec((None, rows, pps * PAGE_SIZE), lambda b, g, pt: (b, 0, g))
    in_specs = [per_b((H_A, width)), per_b((H_B, width)), chunked(H_A), chunked(1),
                pl.BlockSpec(rbt.shape, lambda b, g, pt: (0, 0)),
                per_b((1, width)), per_b((1, width)), per_b((1, width)), per_b((1, width)),
                per_b((1, LANES))]
    for _ in range(4):
        in_specs += _page_specs(pps, (width, PAGE_SIZE), layer, pps)
    grid_spec = pltpu.PrefetchScalarGridSpec(
        num_scalar_prefetch=1,
        grid=(db, n_pages // pps),
        in_specs=in_specs,
        out_specs=[per_b((H_A, width)), per_b((H_B, width))],
        scratch_shapes=[pltpu.VMEM((H_A, 1), F32), pltpu.VMEM((H_A, 1), F32), pltpu.VMEM((H_A, width), F32),
                        pltpu.VMEM((H_B, 1), F32), pltpu.VMEM((H_B, 1), F32), pltpu.VMEM((H_B, width), F32)],
    )
    pages = []
    for c in caches:
        pages += [c] * pps
    return pl.pallas_call(
        functools.partial(_sattn_kernel, pps=pps, past=past),
        grid_spec=grid_spec,
        out_shape=[jax.ShapeDtypeStruct((db, H_A, width), F32), jax.ShapeDtypeStruct((db, H_B, width), F32)],
        compiler_params=_cparams(("arbitrary", "arbitrary")),
        name="sample_attention",
    )(page_table, qa_blk, qb_blk, dec, mb, rbt, new['ka'], new['va'], new['kb'], new['vb'], new['mbn'], *pages)


def _head_cols(w, perm):
    d = w.shape[0]
    return w.reshape(d, len(perm), HEAD_DIM)[:, list(perm), :].reshape(d, -1)


def _pad_cols(w, n):
    return jnp.pad(w, ((0, 0), (0, n - w.shape[1])))


def _prep_proj_weights(w_in_l, perm):
    sizes = (H_A * HEAD_DIM, KV_A * HEAD_DIM, KV_A * HEAD_DIM, H_A, H_B * HEAD_DIM, KV_B * HEAD_DIM,
             KV_B * HEAD_DIM, H_IDX * D_IDX, D_IDX, H_IDX)
    offs = np.cumsum((0,) + sizes)
    qa, ka, va, fa, qb, kb, vb, qi, ki, wi = [w_in_l[:, offs[n]:offs[n + 1]] for n in range(len(sizes))]
    scale = HEAD_DIM ** -0.5
    big = jnp.concatenate([_head_cols(qa, perm) * scale, _head_cols(qb, perm) * scale, ka, va, kb, vb],
                          axis=1).astype(BF16)
    small = _pad_cols(jnp.concatenate([fa, wi], axis=1), LANES)
    idx = jnp.concatenate([qi * (D_IDX ** -0.5), ki, ki, small], axis=1)
    idx_hi, idx_lo = _split2(idx)
    return dict(big=big, idx_hi=idx_hi, idx_lo=idx_lo)


def _prep_post_weights(w_out_l, perm_a, perm_b, ln1_g, ln1_b, w_up_l, w_down_l, ln2_g, ln2_b, ff_chunk):
    d_mix, d = w_out_l.shape
    na = H_A * HEAD_DIM
    rows = lambda w, perm: w.reshape(len(perm), HEAD_DIM, d)[np.asarray(perm)].reshape(-1, d)
    d_ff = w_up_l.shape[1]
    nc = d_ff // ff_chunk
    row = lambda a: a.reshape(1, -1)
    return dict(out_a=rows(w_out_l[:na], perm_a).astype(BF16), out_b=rows(w_out_l[na:], perm_b).astype(BF16),
                ln1_g=row(ln1_g), ln1_b=row(ln1_b), ln2_g=row(ln2_g), ln2_b=row(ln2_b),
                up=w_up_l.reshape(d, nc, ff_chunk).transpose(1, 0, 2).astype(BF16),
                down=w_down_l.reshape(nc, ff_chunk, d).astype(BF16))


ATTN_BLOCK = 256
FF_CHUNK = 512
POST_ROWS = 512
SCORE_PAGES_PER_STEP = 16
DECAY_PAGES_PER_STEP = 16
ATTN_PAGES_PER_STEP = 8


def _prompt_layer(x2d, batch, seq, wproj, wpost, bf_pad, relb, rb_far, alpha):
    tq = min(ATTN_BLOCK, seq)
    p = _project(x2d, wproj, bf_pad, rows_per_seq=seq, with_cumsum=True)
    nkv = seq // tq
    perm = list(HEAD_PERM)
    ck = p['c'].reshape(batch, seq, H_A)[:, :, perm].reshape(batch, nkv, tq, KV_A // 2, 2 * GROUP)
    ck = ck.transpose(0, 3, 1, 4, 2)
    oa = _fox_attention(p['qa'], p['ka'], p['va'], ck, batch, seq, tq)
    ob = _dsa_attention(p, relb, rb_far, batch, seq, tq, min(TOPK_MAX, seq // 4))
    y = _post_block(x2d, oa, ob, wpost, alpha)
    return y, (p['ka'], p['va'], p['logf'], p['kb'], p['vb'], p['ki'])


def _block_diag_q(q):
    db = q.shape[0]
    qh = q.reshape(db, H_A, 1, HEAD_DIM)
    kv_of_head = jnp.arange(H_A) // GROUP
    onehot = (kv_of_head[:, None] == jnp.arange(KV_A)[None, :])[None, :, :, None]
    return jnp.where(onehot, qh, jnp.zeros((), q.dtype)).reshape(db, H_A, KV_A * HEAD_DIM)


def _diag_heads(o_wide):
    db = o_wide.shape[0]
    o = o_wide.reshape(db, H_A, KV_A, HEAD_DIM)
    idx = (jnp.arange(H_A) // GROUP)[None, :, None, None]
    return jnp.take_along_axis(o, jnp.broadcast_to(idx, (db, H_A, 1, HEAD_DIM)), axis=2).reshape(db, -1)


def _sample_layer(x2d, page_table, caches, layer, wproj, wpost, bf_pad, rbt, alpha):
    db = x2d.shape[0]
    n_pages = page_table.shape[1]
    past = n_pages * PAGE_SIZE
    p = _project(x2d, wproj, bf_pad, rows_per_seq=1, with_cumsum=False)
    q16 = jnp.concatenate([p['qi_hi'].reshape(db, H_IDX, D_IDX), p['qi_lo'].reshape(db, H_IDX, D_IDX)], axis=1)
    scores = _sample_scores(page_table, q16, p['wi'].reshape(db, H_IDX, 1), caches['idx_k'], layer,
                            min(SCORE_PAGES_PER_STEP, n_pages))
    dec = _sample_decay(page_table, p['logf'].reshape(db, H_A, 1), caches['logf_t'], layer,
                        min(DECAY_PAGES_PER_STEP, n_pages))
    q_f32 = p['qi_hi'].astype(F32) + p['qi_lo'].astype(F32)
    mb, mbn = _sample_select(scores.reshape(db, past), q_f32, jnp.tile(p['ki'], (1, H_IDX)), p['wi'],
                             min(TOPK_MAX, (past + 1) // 4))
    new = dict(ka=p['ka'].reshape(db, 1, -1), va=p['va'].reshape(db, 1, -1), kb=p['kb'].reshape(db, 1, -1),
               vb=p['vb'].reshape(db, 1, -1), mbn=mbn.reshape(db, 1, LANES))
    oa_w, ob_w = _sample_attention(page_table, _block_diag_q(p['qa']), _block_diag_q(p['qb']), dec,
                                   mb.reshape(db, 1, past), rbt, new,
                                   (caches['a_k'], caches['a_v'], caches['b_k'], caches['b_v']), layer,
                                   min(ATTN_PAGES_PER_STEP, n_pages))
    y = _post_block(x2d, _diag_heads(oa_w).astype(BF16), _diag_heads(ob_w).astype(BF16), wpost, alpha)
    return y, (p['ka'], p['va'], p['logf'], p['kb'], p['vb'], p['ki'])


def kernel(x_prompt, x_sample, cache_a_k, cache_a_v, cache_a_logf, cache_b_k, cache_b_v, cache_idx_k,
           page_table, w_in, b_f, w_out, ln1_g, ln1_b, w_up, w_down, ln2_g, ln2_b, rel_bias):
    depth = w_in.shape[0]
    batch, seq, d_model = x_prompt.shape
    db, dec_seq, _ = x_sample.shape
    assert dec_seq == 1
    alpha = (2 * depth) ** 0.25
    natural = tuple(range(H_A))
    tq = min(ATTN_BLOCK, seq)

    kv_t = lambda c: c.transpose(0, 1, 3, 4, 2).reshape(c.shape[:2] + (-1, PAGE_SIZE))
    caches = dict(a_k=kv_t(cache_a_k), a_v=kv_t(cache_a_v), b_k=kv_t(cache_b_k), b_v=kv_t(cache_b_v),
                  idx_k=cache_idx_k.transpose(0, 1, 3, 2),
                  logf_t=cache_a_logf.astype(F32).transpose(0, 1, 3, 2))
    relb = _rel_bias_tiles(rel_bias[:, list(HEAD_PERM)], tq)
    rb_far = rel_bias[N_BUCKETS - 1, list(HEAD_PERM)]
    rbt = rel_bias.T

    xp = x_prompt.reshape(batch * seq, d_model)
    xs = x_sample.reshape(db, d_model)
    p_rows, s_rows = [], []
    for l in range(depth):
        bf_pad = _pad_cols(b_f[l].reshape(1, -1), LANES)
        post = lambda perm: _prep_post_weights(w_out[l], perm, perm, ln1_g[l], ln1_b[l], w_up[l], w_down[l],
                                               ln2_g[l], ln2_b[l], FF_CHUNK)
        xp, rows = _prompt_layer(xp, batch, seq, _prep_proj_weights(w_in[l], HEAD_PERM), post(HEAD_PERM),
                                 bf_pad, relb, rb_far, alpha)
        p_rows.append(rows)
        xs, rows = _sample_layer(xs, page_table, caches, l, _prep_proj_weights(w_in[l], natural),
                                 post(natural), bf_pad, rbt, alpha)
        s_rows.append(rows)

    def stack(rows, j, lead, tail):
        return jnp.stack([r[j] for r in rows]).reshape((depth,) + lead + tail)

    pl_, sl_ = (batch, seq), (db, dec_seq)
    kv = (KV_A, HEAD_DIM)
    return (xp.reshape(batch, seq, d_model), xs.reshape(db, dec_seq, d_model),
            stack(p_rows, 0, pl_, kv), stack(p_rows, 1, pl_, kv), stack(p_rows, 2, pl_, (H_A,)),
            stack(p_rows, 3, pl_, kv), stack(p_rows, 4, pl_, kv), stack(p_rows, 5, pl_, (D_IDX,)),
            stack(s_rows, 0, sl_, kv), stack(s_rows, 1, sl_, kv), stack(s_rows, 2, sl_, (H_A,)),
            stack(s_rows, 3, sl_, kv), stack(s_rows, 4, sl_, kv), stack(s_rows, 5, sl_, (D_IDX,)))
```

```python
import functools
import math

import numpy as np
import jax
import jax.numpy as jnp
from jax import lax
from jax.experimental import pallas as pl
from jax.experimental.pallas import tpu as pltpu

HEAD_DIM = 64
H_A = 8
KV_A = 4
H_B = 8
KV_B = 4
GROUP = 2
H_IDX = 8
D_IDX = 64
TOPK_MAX = 256
N_BUCKETS = 32
MAX_DISTANCE = 128
PAGE_SIZE = 128
LN_EPS = 1e-5

LANES = 128
VMEM_LIMIT_BYTES = 56 * 1024 * 1024

NEG_BIG = -1e30
INT_MIN = -(2 ** 31)

BF16 = jnp.bfloat16
F32 = jnp.float32

HEAD_PERM = (0, 2, 1, 3, 4, 6, 5, 7)


def _cparams(semantics):
    return pltpu.CompilerParams(dimension_semantics=semantics,
                                vmem_limit_bytes=VMEM_LIMIT_BYTES)


def _dot(a, b):
    return jnp.dot(a, b, preferred_element_type=F32)


def _dot_nt(a, b):
    return lax.dot_general(a, b, (((1,), (1,)), ((), ())), preferred_element_type=F32)


def _split2(x):
    hi = x.astype(BF16)
    lo = (x - hi.astype(F32)).astype(BF16)
    return hi, lo


def _split3(x):
    h1 = x.astype(BF16)
    r1 = x - h1.astype(F32)
    h2 = r1.astype(BF16)
    h3 = (r1 - h2.astype(F32)).astype(BF16)
    return h1, h2, h3


def _dot3(xh, xl, wh, wl):
    return _dot(xh, wh) + _dot(xl, wh) + _dot(xh, wl)


def _log_sigmoid(x):
    return jnp.minimum(x, 0.0) - jnp.log1p(jnp.exp(-jnp.abs(x)))


def _layer_norm(x, g, b):
    mu = jnp.mean(x, axis=-1, keepdims=True)
    xc = x - mu
    var = jnp.mean(xc * xc, axis=-1, keepdims=True)
    return xc * lax.rsqrt(var + LN_EPS) * g + b


def _rel_bucket(dist):
    max_exact = N_BUCKETS // 2
    d = jnp.maximum(dist, 1).astype(F32)
    large = max_exact + (jnp.log(d / max_exact) / math.log(MAX_DISTANCE / max_exact)
                         * (N_BUCKETS - max_exact)).astype(jnp.int32)
    large = jnp.minimum(large, N_BUCKETS - 1)
    return jnp.where(dist < max_exact, dist, large)


def _order_key(score):
    bits = pltpu.bitcast(score, jnp.int32)
    key = bits ^ (lax.shift_right_arithmetic(bits, 31) & jnp.int32(0x7FFFFFFF))
    return jnp.where(bits == jnp.int32(INT_MIN), 0, key)


def _proj_kernel(x_ref, wbig_ref, wih_ref, wil_ref, bf_ref,
                 qa_ref, qb_ref, ka_ref, va_ref, kb_ref, vb_ref, qih_ref, qil_ref, ki_ref, kih_ref, kil_ref,
                 wi_ref, logf_ref):
    x = x_ref[...]
    xh, xl = _split2(x)
    nq = H_A * HEAD_DIM
    nk = KV_A * HEAD_DIM
    qa_ref[...] = _dot(xh, wbig_ref[:, 0:nq]).astype(BF16)
    qb_ref[...] = _dot(xh, wbig_ref[:, nq:2 * nq]).astype(BF16)
    o = 2 * nq
    ka_ref[...] = _dot(xh, wbig_ref[:, o:o + nk])
    va_ref[...] = _dot(xh, wbig_ref[:, o + nk:o + 2 * nk])
    kb_ref[...] = _dot(xh, wbig_ref[:, o + 2 * nk:o + 3 * nk])
    vb_ref[...] = _dot(xh, wbig_ref[:, o + 3 * nk:o + 4 * nk])

    ni = H_IDX * D_IDX
    qi = _dot3(xh, xl, wih_ref[:, 0:ni], wil_ref[:, 0:ni])
    qh, ql = _split2(qi)
    qih_ref[...] = qh
    qil_ref[...] = ql
    rest = _dot3(xh, xl, wih_ref[:, ni:ni + 2 * LANES], wil_ref[:, ni:ni + 2 * LANES])
    ki2 = rest[:, 0:LANES]
    ki_ref[...] = ki2[:, 0:D_IDX]
    kh, kl = _split2(ki2)
    kih_ref[...] = kh
    kil_ref[...] = kl

    small = rest[:, LANES:2 * LANES]
    wi_ref[...] = small[:, H_A:H_A + H_IDX] * (H_IDX ** -0.5)
    logf = _log_sigmoid(small + bf_ref[...])
    logf_ref[...] = logf[:, 0:H_A]


def _project(x2d, wts, bf_pad):
    m, d = x2d.shape
    tm = min(512, m)
    assert m % tm == 0
    nq, nk, ni = H_A * HEAD_DIM, KV_A * HEAD_DIM, H_IDX * D_IDX
    full = lambda a: pl.BlockSpec(a.shape, lambda i: (0,) * a.ndim)
    row = lambda n: pl.BlockSpec((tm, n), lambda i: (i, 0))
    inputs = [x2d, wts['big'], wts['idx_hi'], wts['idx_lo'], bf_pad]
    out_shapes = [((m, nq), BF16), ((m, nq), BF16), ((m, nk), F32), ((m, nk), F32), ((m, nk), F32),
                  ((m, nk), F32), ((m, ni), BF16), ((m, ni), BF16), ((m, D_IDX), F32),
                  ((m, LANES), BF16), ((m, LANES), BF16), ((m, H_IDX), F32), ((m, H_A), F32)]
    outs = pl.pallas_call(
        _proj_kernel,
        grid=(m // tm,),
        in_specs=[row(d)] + [full(a) for a in inputs[1:]],
        out_specs=[row(s[0][1]) for s in out_shapes],
        out_shape=[jax.ShapeDtypeStruct(*s) for s in out_shapes],
        compiler_params=_cparams(("arbitrary",)),
        name="proj",
    )(*inputs)
    names = ['qa', 'qb', 'ka', 'va', 'kb', 'vb', 'qi_hi', 'qi_lo', 'ki', 'ki2_hi', 'ki2_lo', 'wi', 'logf']
    return dict(zip(names, outs))


def _proj_t_kernel(x_ref, wt_ref, wn_ref, wnh_ref, wnl_ref, wth_ref, wtl_ref, bf_ref, tri_ref,
                   p1_ref, p2_ref, p3_ref,
                   qat_ref, qbt_ref, vat_ref, vbt_ref, kat_ref, kbt_ref, vax_ref, vbx_ref,
                   kaug_ref, kbn_ref, qith_ref, qitl_ref, ki_ref, kih_ref, kil_ref, wi_ref, logf_ref,
                   carry_ref, *, tiles_per_seq, tk):
    x = x_ref[...]
    xh, xl = _split2(x)
    tm = x.shape[0]
    nq = H_A * HEAD_DIM
    nk = KV_A * HEAD_DIM

    qat_ref[...] = _dot_nt(wt_ref[0:nq, :], xh).astype(BF16)
    qbt_ref[...] = _dot_nt(wt_ref[nq:2 * nq, :], xh).astype(BF16)
    o = 2 * nq
    row = lax.broadcasted_iota(jnp.int32, (2 * HEAD_DIM, tm), 0)
    lo_rows = row < HEAD_DIM
    for t_ref, x_out_ref, r0 in ((vat_ref, vax_ref, o), (vbt_ref, vbx_ref, o + nk)):
        vt = _dot_nt(wt_ref[r0:r0 + nk, :], xh)
        t_ref[...] = vt
        for m in range(KV_A // 2):
            blk = vt[m * 2 * HEAD_DIM:(m + 1) * 2 * HEAD_DIM, :]
            lo = jnp.where(lo_rows, blk, 1.0).astype(BF16)
            hi = jnp.where(lo_rows, 1.0, blk).astype(BF16)
            for t in range(tm // tk):
                cols = slice(t * tk, (t + 1) * tk)
                x_out_ref[t, m * 4 * HEAD_DIM:m * 4 * HEAD_DIM + 2 * HEAD_DIM, :] = lo[:, cols]
                x_out_ref[t, m * 4 * HEAD_DIM + 2 * HEAD_DIM:(m + 1) * 4 * HEAD_DIM, :] = hi[:, cols]
    kat_ref[...] = _dot_nt(wt_ref[o + 2 * nk:o + 3 * nk, :], xh)
    kbt_ref[...] = _dot_nt(wt_ref[o + 3 * nk:o + 4 * nk, :], xh)

    kn = _dot(xh, wn_ref[...])
    kbn_ref[...] = kn[:, 2 * nk:3 * nk].astype(BF16)

    qit = (_dot_nt(wth_ref[...], xh) + _dot_nt(wth_ref[...], xl)) + _dot_nt(wtl_ref[...], xh)
    qh, ql = _split2(qit)
    qith_ref[...] = qh
    qitl_ref[...] = ql

    rest = _dot3(xh, xl, wnh_ref[...], wnl_ref[...])
    ki2 = rest[:, 0:LANES]
    ki_ref[...] = ki2[:, 0:D_IDX]
    kh, kl = _split2(ki2)
    kih_ref[...] = kh
    kil_ref[...] = kl
    small = rest[:, LANES:2 * LANES]
    wi_ref[...] = small[:, H_A:H_A + H_IDX] * (H_IDX ** -0.5)
    logf = _log_sigmoid(small + bf_ref[...])
    logf_ref[...] = logf[:, 0:H_A]

    @pl.when(pl.program_id(0) % tiles_per_seq == 0)
    def _():
        carry_ref[...] = jnp.zeros_like(carry_ref)

    h1, h2, h3 = _split3(logf)
    tri = tri_ref[...]
    c = (_dot(tri, h1) + _dot(tri, h2)) + _dot(tri, h3) + carry_ref[...]
    carry_ref[...] = c[tm - 1:tm, :]
    c1, c2, c3 = _split3(c)
    aug = (_dot(c1, p1_ref[...]) + _dot(c2, p2_ref[...])) + _dot(c3, p3_ref[...])
    kaug_ref[...] = (kn[:, 0:2 * nk] + aug).astype(BF16)


def _decay_placement():
    mats = np.zeros((3, LANES, 2 * KV_A * HEAD_DIM), np.float32)
    for m in range(KV_A // 2):
        for v in range(2 * GROUP):
            head = HEAD_PERM[4 * m + v]
            hi_variant, g = v % 2, v // 2
            base = m * 4 * HEAD_DIM + (2 * HEAD_DIM if hi_variant else HEAD_DIM)
            for t in range(3):
                mats[t, head, base + 3 * g + t] = -1.0
    return [jnp.asarray(mats[t], BF16) for t in range(3)]


def _project_t(x2d, wts, bf_pad, *, batch, seq, tk):
    m, d = x2d.shape
    tm = min(512, seq)
    assert seq % tm == 0 and tm % tk == 0
    tps = seq // tm
    nq, nk, ni = H_A * HEAD_DIM, KV_A * HEAD_DIM, H_IDX * D_IDX
    nkv = seq // tk
    full = lambda a: pl.BlockSpec(a.shape, lambda i: (0,) * a.ndim)
    row = lambda n: pl.BlockSpec((tm, n), lambda i: (i, 0))
    feat = lambda r: pl.BlockSpec((None, r, tm), lambda i: (i // tps, 0, i % tps))
    vx = pl.BlockSpec((None, tm // tk, 2 * nk, tk), lambda i: (i // tps, i % tps, 0, 0))
    tri = jnp.tril(jnp.ones((tm, tm), F32)).astype(BF16)
    inputs = [x2d, wts['t_big'], wts['n_keys'], wts['n_idx_hi'], wts['n_idx_lo'], wts['t_idx_hi'],
              wts['t_idx_lo'], bf_pad, tri] + _decay_placement()
    fshape = lambda r, dt: ((batch, r, seq), dt)
    outs = [
        (fshape(nq, BF16), feat(nq)), (fshape(nq, BF16), feat(nq)),
        (fshape(nk, F32), feat(nk)), (fshape(nk, F32), feat(nk)),
        (fshape(nk, F32), feat(nk)), (fshape(nk, F32), feat(nk)),
        (((batch, nkv, 2 * nk, tk), BF16), vx), (((batch, nkv, 2 * nk, tk), BF16), vx),
        (((m, 2 * nk), BF16), row(2 * nk)), (((m, nk), BF16), row(nk)),
        (fshape(ni, BF16), feat(ni)), (fshape(ni, BF16), feat(ni)),
        (((m, D_IDX), F32), row(D_IDX)), (((m, LANES), BF16), row(LANES)), (((m, LANES), BF16), row(LANES)),
        (((m, H_IDX), F32), row(H_IDX)), (((m, H_A), F32), row(H_A)),
    ]
    res = pl.pallas_call(
        functools.partial(_proj_t_kernel, tiles_per_seq=tps, tk=tk),
        grid=(m // tm,),
        in_specs=[row(d)] + [full(a) for a in inputs[1:]],
        out_specs=[o[1] for o in outs],
        out_shape=[jax.ShapeDtypeStruct(*o[0]) for o in outs],
        scratch_shapes=[pltpu.VMEM((1, LANES), F32)],
        compiler_params=_cparams(("arbitrary",)),
        name="proj_prompt",
    )(*inputs)
    names = ['qat', 'qbt', 'vat', 'vbt', 'kat', 'kbt', 'vax', 'vbx', 'kaug', 'kbn', 'qit_hi', 'qit_lo',
             'ki', 'ki2_hi', 'ki2_lo', 'wi', 'logf']
    return dict(zip(names, res))


def _post_kernel(x_ref, oa_ref, ob_ref, woa_ref, wob_ref, g1_ref, b1_ref, wup_ref, wdn_ref,
                 g2_ref, b2_ref, y_ref, acc_ref, *, alpha, n_chunks, feature_major):
    x = x_ref[...]
    if feature_major:
        tn = lambda a, w: lax.dot_general(a, w, (((0,), (0,)), ((), ())), preferred_element_type=F32)
        att = tn(oa_ref[...], woa_ref[...]) + tn(ob_ref[...], wob_ref[...])
    else:
        att = _dot(oa_ref[...], woa_ref[...]) + _dot(ob_ref[...], wob_ref[...])
    x1 = _layer_norm(alpha * x + att, g1_ref[...], b1_ref[...])
    x1b = x1.astype(BF16)
    acc_ref[...] = jnp.zeros_like(acc_ref)

    def body(c, carry):
        u = jnp.maximum(_dot(x1b, wup_ref[c]), 0.0)
        acc_ref[...] += _dot((u * u).astype(BF16), wdn_ref[c])
        return carry

    lax.fori_loop(0, n_chunks, body, 0)
    y_ref[...] = _layer_norm(alpha * x1 + acc_ref[...], g2_ref[...], b2_ref[...])


def _post_block(x2d, oa, ob, wts, alpha, seq=None):
    m, d = x2d.shape
    tm = min(POST_ROWS, m if seq is None else seq)
    n_chunks = wts['up'].shape[0]
    full = lambda a: pl.BlockSpec(a.shape, lambda i: (0,) * a.ndim, pipeline_mode=pl.Buffered(1))
    row = lambda n: pl.BlockSpec((tm, n), lambda i: (i, 0))
    if seq is None:
        o_spec = lambda a: row(a.shape[1])
    else:
        tps = seq // tm
        o_spec = lambda a: pl.BlockSpec((None, a.shape[1], tm), lambda i: (i // tps, 0, i % tps))
    weights = [wts['out_a'], wts['out_b'], wts['ln1_g'], wts['ln1_b'], wts['up'], wts['down'],
               wts['ln2_g'], wts['ln2_b']]
    return pl.pallas_call(
        functools.partial(_post_kernel, alpha=alpha, n_chunks=n_chunks, feature_major=seq is not None),
        grid=(m // tm,),
        in_specs=[row(d), o_spec(oa), o_spec(ob)] + [full(w) for w in weights],
        out_specs=row(d),
        out_shape=jax.ShapeDtypeStruct((m, d), F32),
        scratch_shapes=[pltpu.VMEM((tm, d), F32)],
        compiler_params=_cparams(("arbitrary",)),
        name="post_block",
    )(x2d, oa, ob, *weights)


def _variant_rows(g, hi_half, ones_from=None):
    row = lax.broadcasted_iota(jnp.int32, g.shape, 0)
    keep = (row >= HEAD_DIM) if hi_half else (row < HEAD_DIM)
    fill = 0.0
    if ones_from is not None:
        fill = jnp.where((row >= ones_from) & (row < ones_from + 3), 1.0, 0.0)
    return jnp.where(keep, g, fill).astype(BF16)


def _softmax_chunk(s, causal_q0, m_ref, p_ref, acc_ref, cols, acols):
    if causal_q0 is not None:
        key = lax.broadcasted_iota(jnp.int32, s.shape, 0)
        qry = lax.broadcasted_iota(jnp.int32, s.shape, 1) + causal_q0
        s = jnp.where(key <= qry, s, NEG_BIG)
    m_old = m_ref[:, cols]
    m_new = jnp.maximum(m_old, jnp.max(s, axis=0, keepdims=True))
    m_ref[:, cols] = m_new
    p_ref[:, cols] = jnp.exp(s - m_new[0:1, :]).astype(BF16)
    acc_ref[:, acols] = acc_ref[:, acols] * jnp.exp(m_old - m_new)[0:1, :]


def _finish_pair(acc_lo, acc_hi, tq):
    row = lax.broadcasted_iota(jnp.int32, (2 * HEAD_DIM, tq), 0)
    lo_rows = row < HEAD_DIM
    out = []
    for g in range(GROUP):
        cols = slice(g * tq, (g + 1) * tq)
        a_lo, a_hi = acc_lo[:, cols], acc_hi[:, cols]
        o_lo = a_lo / a_lo[HEAD_DIM:HEAD_DIM + 1, :]
        o_hi = a_hi / a_hi[0:1, :]
        out.append(jnp.where(lo_rows, o_lo, o_hi))
    return jnp.concatenate(out, axis=0).astype(BF16)


def _fox_kernel(q_ref, k_ref, v_ref, o_ref, qt_ref, s_ref, p_ref, m_ref, acc_ref, *, tq):
    i = pl.program_id(2)
    q = q_ref[...].astype(F32)
    g0, g1 = q[0:2 * HEAD_DIM], q[2 * HEAD_DIM:4 * HEAD_DIM]
    qt_ref[0, :, 0:tq] = _variant_rows(g0, False, HEAD_DIM)
    qt_ref[0, :, tq:2 * tq] = _variant_rows(g1, False, HEAD_DIM + 3)
    qt_ref[1, :, 0:tq] = _variant_rows(g0, True, 0)
    qt_ref[1, :, tq:2 * tq] = _variant_rows(g1, True, 3)
    m_ref[...] = jnp.full_like(m_ref, NEG_BIG)
    acc_ref[...] = jnp.zeros_like(acc_ref)

    def process(blocks):
        for slot, (j, _) in enumerate(blocks):
            start = pl.multiple_of(j * tq, tq)
            for d in range(2):
                s_ref[slot, d] = _dot(k_ref[pl.ds(start, tq), d * LANES:(d + 1) * LANES], qt_ref[d])
        for slot, (j, causal) in enumerate(blocks):
            for d in range(2):
                for c in range(2 * tq // LANES):
                    cols = slice(c * LANES, (c + 1) * LANES)
                    q0 = (c * LANES) % tq if causal else None
                    _softmax_chunk(s_ref[slot, d, :, cols], q0, m_ref.at[d], p_ref.at[slot, d],
                                   acc_ref.at[d], cols, cols)
                acc_ref[d] += _dot(v_ref[j, d * LANES:(d + 1) * LANES, :], p_ref[slot, d])

    def pair_body(t, carry):
        process(((2 * t, False), (2 * t + 1, False)))
        return carry

    lax.fori_loop(0, i // 2, pair_body, 0)

    @pl.when(i % 2 == 1)
    def _():
        process(((i - 1, False), (i, True)))

    @pl.when(i % 2 == 0)
    def _():
        process(((i, True),))

    o_ref[...] = _finish_pair(acc_ref[0], acc_ref[1], tq)


def _fox_attention(qt, kaug, vx, batch, seq, tq):
    nq = seq // tq
    pair = 4 * HEAD_DIM
    return pl.pallas_call(
        functools.partial(_fox_kernel, tq=tq),
        grid=(batch, KV_A // 2, nq),
        in_specs=[
            pl.BlockSpec((None, pair, tq), lambda b, m, i: (b, m, i)),
            pl.BlockSpec((seq, pair), lambda b, m, i: (b, m)),
            pl.BlockSpec((None, nq, pair, tq), lambda b, m, i: (b, 0, m, 0)),
        ],
        out_specs=pl.BlockSpec((None, pair, tq), lambda b, m, i: (b, m, i)),
        out_shape=jax.ShapeDtypeStruct(qt.shape, BF16),
        scratch_shapes=[pltpu.VMEM((2, LANES, 2 * tq), BF16), pltpu.VMEM((2, 2, tq, 2 * tq), F32),
                        pltpu.VMEM((2, 2, tq, 2 * tq), BF16), pltpu.VMEM((2, 8, 2 * tq), F32),
                        pltpu.VMEM((2, LANES, 2 * tq), F32)],
        compiler_params=_cparams(("arbitrary", "arbitrary", "arbitrary")),
        name="fox_attention",
    )(qt, kaug, vx)


def _relb_kernel(rb_ref, o_ref, *, tq):
    delta = pl.program_id(0)
    h = pl.program_id(1)
    key = lax.broadcasted_iota(jnp.int32, (tq, tq), 0)
    qry = lax.broadcasted_iota(jnp.int32, (tq, tq), 1)
    bucket = _rel_bucket(jnp.maximum(delta * tq + qry - key, 0))
    acc = jnp.zeros((tq, tq), F32)
    for n in range(N_BUCKETS):
        acc = jnp.where(bucket == n, rb_ref[n, h], acc)
    o_ref[...] = acc


def _rel_bias_tiles(rb_perm, tq):
    return pl.pallas_call(
        functools.partial(_relb_kernel, tq=tq),
        grid=(2, H_B),
        in_specs=[pl.BlockSpec(memory_space=pltpu.SMEM)],
        out_specs=pl.BlockSpec((None, None, tq, tq), lambda d, h: (d, h, 0, 0)),
        out_shape=jax.ShapeDtypeStruct((2, H_B, tq, tq), F32),
        compiler_params=_cparams(("arbitrary", "arbitrary")),
        name="rel_bias_tiles",
    )(rb_perm)


def _strict_upper(n):
    r = lax.broadcasted_iota(jnp.int32, (n, n), 0)
    c = lax.broadcasted_iota(jnp.int32, (n, n), 1)
    return jnp.where(r < c, 1.0, 0.0).astype(BF16)


def _strict_lower(n):
    r = lax.broadcasted_iota(jnp.int32, (n, n), 0)
    c = lax.broadcasted_iota(jnp.int32, (n, n), 1)
    return jnp.where(c < r, 1.0, 0.0).astype(BF16)


KEY_ROWS = 64

DSA_VARIANT_ORDER = (0, 2, 1, 3)


def _dsa_kernel(qih_ref, qil_ref, kih_ref, kil_ref, wi_ref, qb_ref, k_ref, v_ref, relb_ref, rbfar_ref,
                o_ref, keys_ref, mb_ref, qi_ref, lg_ref, qt_ref, p_ref, m_ref, acc_ref, *, tq, topk):
    i = pl.program_id(1)
    nblk = i + 1
    nlc = tq // LANES

    for h in range(H_IDX):
        rows = slice((h // 2) * LANES, (h // 2 + 1) * LANES)
        cols = slice(h * tq, (h + 1) * tq)
        mh = _variant_rows(qih_ref[rows, :].astype(F32), h % 2 == 1)
        qi_ref[0:LANES, cols] = mh
        qi_ref[LANES:2 * LANES, cols] = _variant_rows(qil_ref[rows, :].astype(F32), h % 2 == 1)
        qi_ref[2 * LANES:3 * LANES, cols] = mh

    def score_block(j, diag):
        start = pl.multiple_of(j * tq, tq)
        kh = kih_ref[pl.ds(start, tq), :]
        kl = kil_ref[pl.ds(start, tq), :]
        lg_ref[...] = _dot(jnp.concatenate([kh, kh, kl], axis=1), qi_ref[...])
        for c in range(nlc):
            for r in range(tq // KEY_ROWS):
                rows = slice(r * KEY_ROWS, (r + 1) * KEY_ROWS)
                sc = jnp.zeros((KEY_ROWS, LANES), F32)
                for h in range(H_IDX):
                    lane0 = h * tq + c * LANES
                    sc = sc + wi_ref[h:h + 1, c * LANES:(c + 1) * LANES] * jnp.maximum(
                        lg_ref[rows, lane0:lane0 + LANES], 0.0)
                key = _order_key(sc)
                if diag:
                    kidx = lax.broadcasted_iota(jnp.int32, key.shape, 0) + r * KEY_ROWS
                    qidx = lax.broadcasted_iota(jnp.int32, key.shape, 1) + c * LANES
                    key = jnp.where(kidx <= qidx, key, INT_MIN)
                keys_ref[j, rows, c * LANES:(c + 1) * LANES] = key

    def score_body(j, carry):
        score_block(j, False)
        return carry

    lax.fori_loop(0, i, score_body, 0)
    score_block(i, True)

    needs_select = (i + 1) * tq > topk

    @pl.when(jnp.logical_not(needs_select))
    def _():
        def zero_body(j, carry):
            mb_ref[j] = jnp.zeros((tq, tq), F32)
            return carry
        lax.fori_loop(0, nblk, zero_body, 0)

    @pl.when(needs_select)
    def _():
        def count(pred_fn):
            def body(j, acc):
                hit = jnp.where(pred_fn(keys_ref[j]), 1.0, 0.0)
                for r in range(tq // 8):
                    acc = acc + hit[r * 8:(r + 1) * 8, :]
                return acc
            acc = lax.fori_loop(0, nblk, body, jnp.zeros((8, tq), F32))
            return jnp.sum(acc, axis=0, keepdims=True)

        def bit_body(b, t):
            cand = t + lax.shift_left(jnp.int32(1), 31 - b)
            return jnp.where(count(lambda kblk: kblk >= cand) >= topk, cand, t)

        thr = lax.fori_loop(0, 32, bit_body, jnp.full((1, tq), INT_MIN, jnp.int32))
        need = topk - count(lambda kblk: kblk > thr)
        earlier = _strict_lower(tq)

        def mask_body(j, seen):
            kblk = keys_ref[j]
            eq = kblk == thr
            eqf = jnp.where(eq, 1.0, 0.0)
            rank = _dot(earlier, eqf.astype(BF16)) + seen
            mb_ref[j] = jnp.where(kblk > thr, 0.0,
                                  jnp.where(eq, jnp.where(rank < need, 0.0, NEG_BIG), NEG_BIG))
            return seen + jnp.sum(eqf, axis=0, keepdims=True)

        lax.fori_loop(0, nblk, mask_body, jnp.zeros((1, tq), F32))

    pair = 4 * HEAD_DIM
    for m in range(KV_B // 2):
        q = qb_ref[m * pair:(m + 1) * pair, :].astype(F32)
        g = (q[0:2 * HEAD_DIM], q[2 * HEAD_DIM:4 * HEAD_DIM])
        for n, v in enumerate(DSA_VARIANT_ORDER):
            qt_ref[:, n * tq:(n + 1) * tq] = _variant_rows(g[v // 2], v % 2 == 1)
        m_ref[...] = jnp.full_like(m_ref, NEG_BIG)
        acc_ref[...] = jnp.zeros_like(acc_ref)

        far = lambda slot, qcols: rbfar_ref[slot]
        near = lambda slot, qcols: relb_ref[1, slot, :, qcols]
        diag = lambda slot, qcols: relb_ref[0, slot, :, qcols]

        def process(blocks):
            for slot, (j, _) in enumerate(blocks):
                start = pl.multiple_of(j * tq, tq)
                lg_ref[:, slot * 4 * tq:(slot + 1) * 4 * tq] = _dot(
                    k_ref[pl.ds(start, tq), m * LANES:(m + 1) * LANES], qt_ref[...])
            for slot, (j, rel) in enumerate(blocks):
                for n, v in enumerate(DSA_VARIANT_ORDER):
                    for c in range(nlc):
                        qcols = slice(c * LANES, (c + 1) * LANES)
                        cols = slice(n * tq + c * LANES, n * tq + (c + 1) * LANES)
                        lcols = slice(slot * 4 * tq + cols.start, slot * 4 * tq + cols.stop)
                        s = lg_ref[:, lcols] + (mb_ref[j, :, qcols] + rel(4 * m + v, qcols))
                        acols = slice((n % 2) * tq + c * LANES, (n % 2) * tq + (c + 1) * LANES)
                        _softmax_chunk(s, c * LANES if rel is diag else None, m_ref, p_ref.at[slot],
                                       acc_ref.at[n // 2], cols, acols)
                for d in range(2):
                    vt = v_ref[j, m * pair + d * LANES:m * pair + (d + 1) * LANES, :]
                    acc_ref[d] += _dot(vt, p_ref[slot, :, d * 2 * tq:(d + 1) * 2 * tq])

        n_far = jnp.maximum(i - 1, 0)

        def far_body(t, carry):
            process(((2 * t, far), (2 * t + 1, far)))
            return carry

        lax.fori_loop(0, n_far // 2, far_body, 0)

        @pl.when(n_far % 2 == 1)
        def _():
            process(((i - 2, far),))

        @pl.when(i >= 1)
        def _():
            process(((i - 1, near), (i, diag)))

        @pl.when(i == 0)
        def _():
            process(((i, diag),))

        o_ref[m * pair:(m + 1) * pair, :] = _finish_pair(acc_ref[0], acc_ref[1], tq)


def _dsa_attention(p, wi_t, relb, rb_far, batch, seq, tq, topk):
    nq = seq // tq
    nf = H_B * HEAD_DIM
    qcol = lambda r: pl.BlockSpec((None, r, tq), lambda b, i: (b, 0, i))
    seqblk = lambda n: pl.BlockSpec((seq, n), lambda b, i: (b, 0))
    return pl.pallas_call(
        functools.partial(_dsa_kernel, tq=tq, topk=topk),
        grid=(batch, nq),
        in_specs=[qcol(H_IDX * D_IDX), qcol(H_IDX * D_IDX), seqblk(LANES), seqblk(LANES), qcol(H_IDX),
                  qcol(nf), seqblk(KV_B * HEAD_DIM),
                  pl.BlockSpec((None, nq, nf, tq), lambda b, i: (b, 0, 0, 0)),
                  pl.BlockSpec(relb.shape, lambda b, i: (0, 0, 0, 0)),
                  pl.BlockSpec(memory_space=pltpu.SMEM)],
        out_specs=qcol(nf),
        out_shape=jax.ShapeDtypeStruct((batch, nf, seq), BF16),
        scratch_shapes=[pltpu.VMEM((nq, tq, tq), jnp.int32), pltpu.VMEM((nq, tq, tq), F32),
                        pltpu.VMEM((3 * LANES, H_IDX * tq), BF16), pltpu.VMEM((tq, H_IDX * tq), F32),
                        pltpu.VMEM((LANES, 4 * tq), BF16), pltpu.VMEM((2, tq, 4 * tq), BF16),
                        pltpu.VMEM((8, 4 * tq), F32), pltpu.VMEM((2, LANES, 2 * tq), F32)],
        compiler_params=_cparams(("arbitrary", "arbitrary")),
        name="dsa_attention",
    )(p['qit_hi'], p['qit_lo'], p['ki2_hi'], p['ki2_lo'], wi_t, p['qbt'], p['kbn'], p['vbx'], relb, rb_far)


def _page_specs(n, block, layer, pages_per_step):
    def make(r):
        return pl.BlockSpec((None, None) + block,
                            lambda b, g, pt: (layer, pt[b, g * pages_per_step + r]) + (0,) * len(block))
    return [make(r) for r in range(n)]


def _sscore_kernel(pt_ref, q16_ref, wi_ref, *refs, pps):
    page_refs, out_ref = refs[:pps], refs[pps]
    q16 = q16_ref[...]
    qh = q16[0:H_IDX]
    wi = wi_ref[...]
    for r in range(pps):
        kh, kl = _split2(page_refs[r][...])
        a = _dot(q16, kh)
        logits = (a[0:H_IDX] + a[H_IDX:2 * H_IDX]) + _dot(qh, kl)
        out_ref[:, r * PAGE_SIZE:(r + 1) * PAGE_SIZE] = jnp.sum(
            wi * jnp.maximum(logits, 0.0), axis=0, keepdims=True)


def _sample_scores(page_table, q16, wi_col, cache_idx_k, layer, pps):
    db, n_pages = page_table.shape
    grid_spec = pltpu.PrefetchScalarGridSpec(
        num_scalar_prefetch=1,
        grid=(db, n_pages // pps),
        in_specs=[pl.BlockSpec((None, 2 * H_IDX, D_IDX), lambda b, g, pt: (b, 0, 0)),
                  pl.BlockSpec((None, H_IDX, 1), lambda b, g, pt: (b, 0, 0))]
                 + _page_specs(pps, (D_IDX, PAGE_SIZE), layer, pps),
        out_specs=pl.BlockSpec((None, 1, pps * PAGE_SIZE), lambda b, g, pt: (b, 0, g)),
    )
    return pl.pallas_call(
        functools.partial(_sscore_kernel, pps=pps),
        grid_spec=grid_spec,
        out_shape=jax.ShapeDtypeStruct((db, 1, n_pages * PAGE_SIZE), F32),
        compiler_params=_cparams(("arbitrary", "arbitrary")),
        name="sample_scores",
    )(page_table, q16, wi_col, *([cache_idx_k] * pps))


def _sdecay_kernel(pt_ref, lnew_ref, *refs, pps):
    page_refs, out_ref, carry_ref = refs[:pps], refs[pps], refs[pps + 1]

    @pl.when(pl.program_id(1) == 0)
    def _():
        carry_ref[...] = jnp.broadcast_to(lnew_ref[...], carry_ref.shape)

    r_i = lax.broadcasted_iota(jnp.int32, (PAGE_SIZE, PAGE_SIZE), 0)
    c_i = lax.broadcasted_iota(jnp.int32, (PAGE_SIZE, PAGE_SIZE), 1)
    later = jnp.where(r_i > c_i, 1.0, 0.0).astype(BF16)
    for r in reversed(range(pps)):
        x = page_refs[r][...]
        h1, h2, h3 = _split3(x)
        within = (_dot(h1, later) + _dot(h2, later)) + _dot(h3, later)
        carry = carry_ref[...]
        out_ref[:, r * PAGE_SIZE:(r + 1) * PAGE_SIZE] = within + carry[:, 0:1]
        carry_ref[...] = carry + jnp.sum(x, axis=1, keepdims=True)


def _sample_decay(page_table, logf_new_col, logf_t_cache, layer, pps):
    db, n_pages = page_table.shape
    nsteps = n_pages // pps

    def page_spec(r):
        return pl.BlockSpec((None, None, H_A, PAGE_SIZE),
                            lambda b, g, pt: (layer, pt[b, (nsteps - 1 - g) * pps + r], 0, 0))

    grid_spec = pltpu.PrefetchScalarGridSpec(
        num_scalar_prefetch=1,
        grid=(db, nsteps),
        in_specs=[pl.BlockSpec((None, H_A, 1), lambda b, g, pt: (b, 0, 0))]
                 + [page_spec(r) for r in range(pps)],
        out_specs=pl.BlockSpec((None, H_A, pps * PAGE_SIZE), lambda b, g, pt: (b, 0, nsteps - 1 - g)),
        scratch_shapes=[pltpu.VMEM((H_A, LANES), F32)],
    )
    return pl.pallas_call(
        functools.partial(_sdecay_kernel, pps=pps),
        grid_spec=grid_spec,
        out_shape=jax.ShapeDtypeStruct((db, H_A, n_pages * PAGE_SIZE), F32),
        compiler_params=_cparams(("arbitrary", "arbitrary")),
        name="sample_decay",
    )(page_table, logf_new_col, *([logf_t_cache] * pps))


def _sselect_kernel(sc_ref, q_ref, kt_ref, wi_ref, mb_ref, mbn_ref, keys_ref, *, topk, chunk):
    db, past = sc_ref.shape
    prod = q_ref[...] * kt_ref[...]
    lane = lax.broadcasted_iota(jnp.int32, prod.shape, 1)
    wi = wi_ref[...]
    sc_new = jnp.zeros((db, 1), F32)
    for h in range(H_IDX):
        seg = (lane >= h * D_IDX) & (lane < (h + 1) * D_IDX)
        logit = jnp.sum(jnp.where(seg, prod, 0.0), axis=1, keepdims=True)
        sc_new = sc_new + wi[:, h:h + 1] * jnp.maximum(logit, 0.0)
    key_new = _order_key(sc_new)
    keys_ref[...] = _order_key(sc_ref[...])

    def count(pred_fn):
        n = jnp.sum(jnp.where(pred_fn(keys_ref[...]), 1.0, 0.0), axis=1, keepdims=True)
        return n + jnp.where(pred_fn(key_new), 1.0, 0.0)

    def bit_body(b, t):
        cand = t + lax.shift_left(jnp.int32(1), 31 - b)
        return jnp.where(count(lambda k: k >= cand) >= topk, cand, t)

    thr = lax.fori_loop(0, 32, bit_body, jnp.full((db, 1), INT_MIN, jnp.int32))
    need = topk - count(lambda k: k > thr)
    sut = _strict_upper(chunk)
    seen = jnp.zeros((db, 1), F32)
    for c in range(past // chunk):
        cols = slice(c * chunk, (c + 1) * chunk)
        kblk = keys_ref[:, cols]
        eq = kblk == thr
        eqf = jnp.where(eq, 1.0, 0.0)
        rank = _dot(eqf.astype(BF16), sut) + seen
        mb_ref[:, cols] = jnp.where(kblk > thr, 0.0,
                                    jnp.where(eq, jnp.where(rank < need, 0.0, NEG_BIG), NEG_BIG))
        seen = seen + jnp.sum(eqf, axis=1, keepdims=True)
    sel_new = jnp.where(key_new > thr, 0.0,
                        jnp.where(key_new == thr, jnp.where(seen < need, 0.0, NEG_BIG), NEG_BIG))
    mbn_ref[...] = jnp.broadcast_to(sel_new, mbn_ref.shape)


def _sample_select(scores, q_f32, ki_tiled, wi, topk):
    db, past = scores.shape
    return pl.pallas_call(
        functools.partial(_sselect_kernel, topk=topk, chunk=2 * LANES),
        out_shape=[jax.ShapeDtypeStruct((db, past), F32), jax.ShapeDtypeStruct((db, LANES), F32)],
        scratch_shapes=[pltpu.VMEM((db, past), jnp.int32)],
        compiler_params=pltpu.CompilerParams(vmem_limit_bytes=VMEM_LIMIT_BYTES),
        name="sample_select",
    )(scores, q_f32, ki_tiled, wi)


def _sattn_kernel(pt_ref, qa_ref, qb_ref, dec_ref, mb_ref, rbt_ref, kan_ref, van_ref, kbn_ref, vbn_ref,
                  mbn_ref, *refs, pps, past):
    ak, av = refs[0:pps], refs[pps:2 * pps]
    bk, bv = refs[2 * pps:3 * pps], refs[3 * pps:4 * pps]
    oa_ref, ob_ref, ma_ref, la_ref, acca_ref, mb_m_ref, lb_ref, accb_ref = refs[4 * pps:]
    g = pl.program_id(1)
    width = pps * PAGE_SIZE

    @pl.when(g == 0)
    def _():
        for m_ref, l_ref, acc_ref in ((ma_ref, la_ref, acca_ref), (mb_m_ref, lb_ref, accb_ref)):
            m_ref[...] = jnp.full_like(m_ref, NEG_BIG)
            l_ref[...] = jnp.zeros_like(l_ref)
            acc_ref[...] = jnp.zeros_like(acc_ref)

    def update(s, v_refs, m_ref, l_ref, acc_ref):
        m_old = m_ref[...]
        m_new = jnp.maximum(m_old, jnp.max(s, axis=1, keepdims=True))
        a = jnp.exp(m_old - m_new)
        p = jnp.exp(s - m_new)
        l_ref[...] = a * l_ref[...] + jnp.sum(p, axis=1, keepdims=True)
        pv = jnp.zeros(acc_ref.shape, F32)
        for r in range(pps):
            pv = pv + _dot_nt(p[:, r * PAGE_SIZE:(r + 1) * PAGE_SIZE].astype(BF16),
                              v_refs[r][...].astype(BF16))
        acc_ref[...] = a * acc_ref[...] + pv
        m_ref[...] = m_new

    qa, qb = qa_ref[...], qb_ref[...]
    s_a = jnp.concatenate([_dot(qa, ak[r][...].astype(BF16)) for r in range(pps)], axis=1)
    update(s_a + dec_ref[...], av, ma_ref, la_ref, acca_ref)

    pos = g * width + lax.broadcasted_iota(jnp.int32, (1, width), 1)
    bucket = _rel_bucket(past - pos)
    rbt = rbt_ref[...]
    rel = jnp.zeros((H_B, width), F32)
    for n in range(N_BUCKETS):
        rel = jnp.where(bucket == n, rbt[:, n:n + 1], rel)
    s_b = jnp.concatenate([_dot(qb, bk[r][...].astype(BF16)) for r in range(pps)], axis=1)
    update(s_b + rel + mb_ref[...], bv, mb_m_ref, lb_ref, accb_ref)

    @pl.when(g == pl.num_programs(1) - 1)
    def _():
        def finish(q, kn, vn, extra, m_ref, l_ref, acc_ref, o_ref):
            s = jnp.sum(q.astype(F32) * kn, axis=1, keepdims=True) + extra
            m_old = m_ref[...]
            m_new = jnp.maximum(m_old, s)
            a = jnp.exp(m_old - m_new)
            p = jnp.exp(s - m_new)
            l = a * l_ref[...] + p
            o_ref[...] = (a * acc_ref[...] + p * vn) / l

        finish(qa, kan_ref[...], van_ref[...], 0.0, ma_ref, la_ref, acca_ref, oa_ref)
        finish(qb, kbn_ref[...], vbn_ref[...], rbt[:, 0:1] + mbn_ref[:, 0:1],
               mb_m_ref, lb_ref, accb_ref, ob_ref)


def _sample_attention(page_table, qa_blk, qb_blk, dec, mb, rbt, new, caches, layer, pps):
    db, n_pages = page_table.shape
    past = n_pages * PAGE_SIZE
    width = KV_A * HEAD_DIM
    per_b = lambda shape: pl.BlockSpec((None,) + shape, lambda b, g, pt: (b,) + (0,) * len(shape))
    chunked = lambda rows: pl.BlockSpec((None, rows, pps * PAGE_SIZE), lambda b, g, pt: (b, 0, g))
    in_specs = [per_b((H_A, width)), per_b((H_B, width)), chunked(H_A), chunked(1),
                pl.BlockSpec(rbt.shape, lambda b, g, pt: (0, 0)),
                per_b((1, width)), per_b((1, width)), per_b((1, width)), per_b((1, width)),
                per_b((1, LANES))]
    for _ in range(4):
        in_specs += _page_specs(pps, (width, PAGE_SIZE), layer, pps)
    grid_spec = pltpu.PrefetchScalarGridSpec(
        num_scalar_prefetch=1,
        grid=(db, n_pages // pps),
        in_specs=in_specs,
        out_specs=[per_b((H_A, width)), per_b((H_B, width))],
        scratch_shapes=[pltpu.VMEM((H_A, 1), F32), pltpu.VMEM((H_A, 1), F32), pltpu.VMEM((H_A, width), F32),
                        pltpu.VMEM((H_B, 1), F32), pltpu.VMEM((H_B, 1), F32), pltpu.VMEM((H_B, width), F32)],
    )
    pages = []
    for c in caches:
        pages += [c] * pps
    return pl.pallas_call(
        functools.partial(_sattn_kernel, pps=pps, past=past),
        grid_spec=grid_spec,
        out_shape=[jax.ShapeDtypeStruct((db, H_A, width), F32), jax.ShapeDtypeStruct((db, H_B, width), F32)],
        compiler_params=_cparams(("arbitrary", "arbitrary")),
        name="sample_attention",
    )(page_table, qa_blk, qb_blk, dec, mb, rbt, new['ka'], new['va'], new['kb'], new['vb'], new['mbn'], *pages)


def _head_cols(w, perm):
    d = w.shape[0]
    return w.reshape(d, len(perm), HEAD_DIM)[:, list(perm), :].reshape(d, -1)


def _pad_cols(w, n):
    return jnp.pad(w, ((0, 0), (0, n - w.shape[1])))


def _split_w_in(w_in_l):
    sizes = (H_A * HEAD_DIM, KV_A * HEAD_DIM, KV_A * HEAD_DIM, H_A, H_B * HEAD_DIM, KV_B * HEAD_DIM,
             KV_B * HEAD_DIM, H_IDX * D_IDX, D_IDX, H_IDX)
    offs = np.cumsum((0,) + sizes)
    return [w_in_l[:, offs[n]:offs[n + 1]] for n in range(len(sizes))]


def _prep_proj_weights(w_in_l, perm):
    qa, ka, va, fa, qb, kb, vb, qi, ki, wi = _split_w_in(w_in_l)
    scale = HEAD_DIM ** -0.5
    big = jnp.concatenate([_head_cols(qa, perm) * scale, _head_cols(qb, perm) * scale, ka, va, kb, vb],
                          axis=1).astype(BF16)
    small = _pad_cols(jnp.concatenate([fa, wi], axis=1), LANES)
    idx = jnp.concatenate([qi * (D_IDX ** -0.5), ki, ki, small], axis=1)
    idx_hi, idx_lo = _split2(idx)
    return dict(big=big, idx_hi=idx_hi, idx_lo=idx_lo)


def _prep_proj_t_weights(w_in_l, perm):
    qa, ka, va, fa, qb, kb, vb, qi, ki, wi = _split_w_in(w_in_l)
    d = w_in_l.shape[0]
    scale = HEAD_DIM ** -0.5
    t_big = jnp.concatenate([_head_cols(qa, perm) * scale, _head_cols(qb, perm) * scale, va, vb, ka, kb],
                            axis=1).T.astype(BF16)
    zero = jnp.zeros((d, HEAD_DIM), w_in_l.dtype)
    spread = []
    for m in range(KV_A // 2):
        spread += [ka[:, 2 * m * HEAD_DIM:(2 * m + 1) * HEAD_DIM], zero, zero,
                   ka[:, (2 * m + 1) * HEAD_DIM:(2 * m + 2) * HEAD_DIM]]
    n_keys = jnp.concatenate(spread + [kb], axis=1).astype(BF16)
    small = _pad_cols(jnp.concatenate([fa, wi], axis=1), LANES)
    n_idx_hi, n_idx_lo = _split2(jnp.concatenate([ki, ki, small], axis=1))
    t_idx_hi, t_idx_lo = _split2((qi * (D_IDX ** -0.5)).T)
    return dict(t_big=t_big, n_keys=n_keys, n_idx_hi=n_idx_hi, n_idx_lo=n_idx_lo,
                t_idx_hi=t_idx_hi, t_idx_lo=t_idx_lo)


def _prep_post_weights(w_out_l, perm_a, perm_b, ln1_g, ln1_b, w_up_l, w_down_l, ln2_g, ln2_b, ff_chunk):
    d_mix, d = w_out_l.shape
    na = H_A * HEAD_DIM
    rows = lambda w, perm: w.reshape(len(perm), HEAD_DIM, d)[np.asarray(perm)].reshape(-1, d)
    d_ff = w_up_l.shape[1]
    nc = d_ff // ff_chunk
    row = lambda a: a.reshape(1, -1)
    return dict(out_a=rows(w_out_l[:na], perm_a).astype(BF16), out_b=rows(w_out_l[na:], perm_b).astype(BF16),
                ln1_g=row(ln1_g), ln1_b=row(ln1_b), ln2_g=row(ln2_g), ln2_b=row(ln2_b),
                up=w_up_l.reshape(d, nc, ff_chunk).transpose(1, 0, 2).astype(BF16),
                down=w_down_l.reshape(nc, ff_chunk, d).astype(BF16))


ATTN_BLOCK = 256
FF_CHUNK = 512
POST_ROWS = 512
SCORE_PAGES_PER_STEP = 16
DECAY_PAGES_PER_STEP = 16
ATTN_PAGES_PER_STEP = 8


def _prompt_layer(x2d, batch, seq, wproj, wpost, bf_pad, relb, rb_far, alpha):
    tq = min(ATTN_BLOCK, seq)
    p = _project_t(x2d, wproj, bf_pad, batch=batch, seq=seq, tk=tq)
    oa_t = _fox_attention(p['qat'], p['kaug'], p['vax'], batch, seq, tq)
    wi_t = p['wi'].reshape(batch, seq, H_IDX).transpose(0, 2, 1)
    ob_t = _dsa_attention(p, wi_t, relb, rb_far, batch, seq, tq, min(TOPK_MAX, seq // 4))
    y = _post_block(x2d, oa_t, ob_t, wpost, alpha, seq=seq)
    kv = lambda a: a.reshape(batch, KV_A, HEAD_DIM, seq).transpose(0, 3, 1, 2)
    return y, (kv(p['kat']), kv(p['vat']), p['logf'].reshape(batch, seq, H_A), kv(p['kbt']), kv(p['vbt']),
               p['ki'].reshape(batch, seq, D_IDX))


def _block_diag_q(q):
    db = q.shape[0]
    qh = q.reshape(db, H_A, 1, HEAD_DIM)
    kv_of_head = jnp.arange(H_A) // GROUP
    onehot = (kv_of_head[:, None] == jnp.arange(KV_A)[None, :])[None, :, :, None]
    return jnp.where(onehot, qh, jnp.zeros((), q.dtype)).reshape(db, H_A, KV_A * HEAD_DIM)


def _diag_heads(o_wide):
    db = o_wide.shape[0]
    o = o_wide.reshape(db, H_A, KV_A, HEAD_DIM)
    idx = (jnp.arange(H_A) // GROUP)[None, :, None, None]
    return jnp.take_along_axis(o, jnp.broadcast_to(idx, (db, H_A, 1, HEAD_DIM)), axis=2).reshape(db, -1)


def _sample_layer(x2d, page_table, caches, layer, wproj, wpost, bf_pad, rbt, alpha):
    db = x2d.shape[0]
    n_pages = page_table.shape[1]
    past = n_pages * PAGE_SIZE
    p = _project(x2d, wproj, bf_pad)
    q16 = jnp.concatenate([p['qi_hi'].reshape(db, H_IDX, D_IDX), p['qi_lo'].reshape(db, H_IDX, D_IDX)], axis=1)
    scores = _sample_scores(page_table, q16, p['wi'].reshape(db, H_IDX, 1), caches['idx_k'], layer,
                            min(SCORE_PAGES_PER_STEP, n_pages))
    dec = _sample_decay(page_table, p['logf'].reshape(db, H_A, 1), caches['logf_t'], layer,
                        min(DECAY_PAGES_PER_STEP, n_pages))
    q_f32 = p['qi_hi'].astype(F32) + p['qi_lo'].astype(F32)
    mb, mbn = _sample_select(scores.reshape(db, past), q_f32, jnp.tile(p['ki'], (1, H_IDX)), p['wi'],
                             min(TOPK_MAX, (past + 1) // 4))
    new = dict(ka=p['ka'].reshape(db, 1, -1), va=p['va'].reshape(db, 1, -1), kb=p['kb'].reshape(db, 1, -1),
               vb=p['vb'].reshape(db, 1, -1), mbn=mbn.reshape(db, 1, LANES))
    oa_w, ob_w = _sample_attention(page_table, _block_diag_q(p['qa']), _block_diag_q(p['qb']), dec,
                                   mb.reshape(db, 1, past), rbt, new,
                                   (caches['a_k'], caches['a_v'], caches['b_k'], caches['b_v']), layer,
                                   min(ATTN_PAGES_PER_STEP, n_pages))
    y = _post_block(x2d, _diag_heads(oa_w).astype(BF16), _diag_heads(ob_w).astype(BF16), wpost, alpha)
    return y, (p['ka'], p['va'], p['logf'], p['kb'], p['vb'], p['ki'])


def kernel(x_prompt, x_sample, cache_a_k, cache_a_v, cache_a_logf, cache_b_k, cache_b_v, cache_idx_k,
           page_table, w_in, b_f, w_out, ln1_g, ln1_b, w_up, w_down, ln2_g, ln2_b, rel_bias):
    depth = w_in.shape[0]
    batch, seq, d_model = x_prompt.shape
    db, dec_seq, _ = x_sample.shape
    assert dec_seq == 1
    alpha = (2 * depth) ** 0.25
    natural = tuple(range(H_A))
    tq = min(ATTN_BLOCK, seq)

    kv_t = lambda c: c.transpose(0, 1, 3, 4, 2).reshape(c.shape[:2] + (-1, PAGE_SIZE))
    caches = dict(a_k=kv_t(cache_a_k), a_v=kv_t(cache_a_v), b_k=kv_t(cache_b_k), b_v=kv_t(cache_b_v),
                  idx_k=cache_idx_k.transpose(0, 1, 3, 2),
                  logf_t=cache_a_logf.astype(F32).transpose(0, 1, 3, 2))
    relb = _rel_bias_tiles(rel_bias[:, list(HEAD_PERM)], tq)
    rb_far = rel_bias[N_BUCKETS - 1, list(HEAD_PERM)]
    rbt = rel_bias.T

    xp = x_prompt.reshape(batch * seq, d_model)
    xs = x_sample.reshape(db, d_model)
    p_rows, s_rows = [], []
    for l in range(depth):
        bf_pad = _pad_cols(b_f[l].reshape(1, -1), LANES)
        post = lambda perm: _prep_post_weights(w_out[l], perm, perm, ln1_g[l], ln1_b[l], w_up[l], w_down[l],
                                               ln2_g[l], ln2_b[l], FF_CHUNK)
        xp, rows = _prompt_layer(xp, batch, seq, _prep_proj_t_weights(w_in[l], HEAD_PERM), post(HEAD_PERM),
                                 bf_pad, relb, rb_far, alpha)
        p_rows.append(rows)
        xs, rows = _sample_layer(xs, page_table, caches, l, _prep_proj_weights(w_in[l], natural),
                                 post(natural), bf_pad, rbt, alpha)
        s_rows.append(rows)

    stack = lambda rows, j: jnp.stack([r[j] for r in rows])
    sample = lambda j, tail: stack(s_rows, j).reshape((depth, db, dec_seq) + tail)
    kv = (KV_A, HEAD_DIM)
    return (xp.reshape(batch, seq, d_model), xs.reshape(db, dec_seq, d_model),
            stack(p_rows, 0), stack(p_rows, 1), stack(p_rows, 2), stack(p_rows, 3), stack(p_rows, 4),
            stack(p_rows, 5),
            sample(0, kv), sample(1, kv), sample(2, (H_A,)), sample(3, kv), sample(4, kv), sample(5, (D_IDX,)))
```

```python
import functools
import math

import numpy as np
import jax
import jax.numpy as jnp
from jax import lax
from jax.experimental import pallas as pl
from jax.experimental.pallas import tpu as pltpu

HEAD_DIM = 64
H_A = 8
KV_A = 4
H_B = 8
KV_B = 4
GROUP = 2
H_IDX = 8
D_IDX = 64
TOPK_MAX = 256
N_BUCKETS = 32
MAX_DISTANCE = 128
PAGE_SIZE = 128
LN_EPS = 1e-5

LANES = 128
VMEM_LIMIT_BYTES = 56 * 1024 * 1024

NEG_BIG = -1e30
INT_MIN = -(2 ** 31)

BF16 = jnp.bfloat16
F32 = jnp.float32

HEAD_PERM = (0, 2, 1, 3, 4, 6, 5, 7)


def _cparams(semantics):
    return pltpu.CompilerParams(dimension_semantics=semantics,
                                vmem_limit_bytes=VMEM_LIMIT_BYTES)


def _dot(a, b):
    return jnp.dot(a, b, preferred_element_type=F32)


def _dot_nt(a, b):
    return lax.dot_general(a, b, (((1,), (1,)), ((), ())), preferred_element_type=F32)


def _split2(x):
    hi = x.astype(BF16)
    lo = (x - hi.astype(F32)).astype(BF16)
    return hi, lo


def _split3(x):
    h1 = x.astype(BF16)
    r1 = x - h1.astype(F32)
    h2 = r1.astype(BF16)
    h3 = (r1 - h2.astype(F32)).astype(BF16)
    return h1, h2, h3


def _dot3(xh, xl, wh, wl):
    return _dot(xh, wh) + _dot(xl, wh) + _dot(xh, wl)


def _log_sigmoid(x):
    return jnp.minimum(x, 0.0) - jnp.log1p(jnp.exp(-jnp.abs(x)))


def _layer_norm(x, g, b):
    mu = jnp.mean(x, axis=-1, keepdims=True)
    xc = x - mu
    var = jnp.mean(xc * xc, axis=-1, keepdims=True)
    return xc * lax.rsqrt(var + LN_EPS) * g + b


def _rel_bucket(dist):
    max_exact = N_BUCKETS // 2
    d = jnp.maximum(dist, 1).astype(F32)
    large = max_exact + (jnp.log(d / max_exact) / math.log(MAX_DISTANCE / max_exact)
                         * (N_BUCKETS - max_exact)).astype(jnp.int32)
    large = jnp.minimum(large, N_BUCKETS - 1)
    return jnp.where(dist < max_exact, dist, large)


def _order_key(score):
    bits = pltpu.bitcast(score, jnp.int32)
    key = bits ^ (lax.shift_right_arithmetic(bits, 31) & jnp.int32(0x7FFFFFFF))
    return jnp.where(bits == jnp.int32(INT_MIN), 0, key)


def _proj_kernel(x_ref, wbig_ref, wih_ref, wil_ref, bf_ref,
                 qa_ref, qb_ref, ka_ref, va_ref, kb_ref, vb_ref, qih_ref, qil_ref, ki_ref, kih_ref, kil_ref,
                 wi_ref, logf_ref):
    x = x_ref[...]
    xh, xl = _split2(x)
    nq = H_A * HEAD_DIM
    nk = KV_A * HEAD_DIM
    qa_ref[...] = _dot(xh, wbig_ref[:, 0:nq]).astype(BF16)
    qb_ref[...] = _dot(xh, wbig_ref[:, nq:2 * nq]).astype(BF16)
    o = 2 * nq
    ka_ref[...] = _dot(xh, wbig_ref[:, o:o + nk])
    va_ref[...] = _dot(xh, wbig_ref[:, o + nk:o + 2 * nk])
    kb_ref[...] = _dot(xh, wbig_ref[:, o + 2 * nk:o + 3 * nk])
    vb_ref[...] = _dot(xh, wbig_ref[:, o + 3 * nk:o + 4 * nk])

    ni = H_IDX * D_IDX
    qi = _dot3(xh, xl, wih_ref[:, 0:ni], wil_ref[:, 0:ni])
    qh, ql = _split2(qi)
    qih_ref[...] = qh
    qil_ref[...] = ql
    rest = _dot3(xh, xl, wih_ref[:, ni:ni + 2 * LANES], wil_ref[:, ni:ni + 2 * LANES])
    ki2 = rest[:, 0:LANES]
    ki_ref[...] = ki2[:, 0:D_IDX]
    kh, kl = _split2(ki2)
    kih_ref[...] = kh
    kil_ref[...] = kl

    small = rest[:, LANES:2 * LANES]
    wi_ref[...] = small[:, H_A:H_A + H_IDX] * (H_IDX ** -0.5)
    logf = _log_sigmoid(small + bf_ref[...])
    logf_ref[...] = logf[:, 0:H_A]


def _project(x2d, wts, bf_pad):
    m, d = x2d.shape
    tm = min(512, m)
    assert m % tm == 0
    nq, nk, ni = H_A * HEAD_DIM, KV_A * HEAD_DIM, H_IDX * D_IDX
    full = lambda a: pl.BlockSpec(a.shape, lambda i: (0,) * a.ndim)
    row = lambda n: pl.BlockSpec((tm, n), lambda i: (i, 0))
    inputs = [x2d, wts['big'], wts['idx_hi'], wts['idx_lo'], bf_pad]
    out_shapes = [((m, nq), BF16), ((m, nq), BF16), ((m, nk), F32), ((m, nk), F32), ((m, nk), F32),
                  ((m, nk), F32), ((m, ni), BF16), ((m, ni), BF16), ((m, D_IDX), F32),
                  ((m, LANES), BF16), ((m, LANES), BF16), ((m, H_IDX), F32), ((m, H_A), F32)]
    outs = pl.pallas_call(
        _proj_kernel,
        grid=(m // tm,),
        in_specs=[row(d)] + [full(a) for a in inputs[1:]],
        out_specs=[row(s[0][1]) for s in out_shapes],
        out_shape=[jax.ShapeDtypeStruct(*s) for s in out_shapes],
        compiler_params=_cparams(("arbitrary",)),
        name="proj",
    )(*inputs)
    names = ['qa', 'qb', 'ka', 'va', 'kb', 'vb', 'qi_hi', 'qi_lo', 'ki', 'ki2_hi', 'ki2_lo', 'wi', 'logf']
    return dict(zip(names, outs))


def _proj_t_kernel(x_ref, wt_ref, wn_ref, wnh_ref, wnl_ref, wth_ref, wtl_ref, bf_ref, tri_ref,
                   p1_ref, p2_ref, p3_ref,
                   qat_ref, qbt_ref, vat_ref, vbt_ref, kat_ref, kbt_ref, vax_ref, vbx_ref,
                   kaug_ref, kbn_ref, qith_ref, qitl_ref, ki_ref, kih_ref, kil_ref, wi_ref, logf_ref,
                   carry_ref, *, tiles_per_seq, tk):
    x = x_ref[...]
    xh, xl = _split2(x)
    tm = x.shape[0]
    nq = H_A * HEAD_DIM
    nk = KV_A * HEAD_DIM

    qat_ref[...] = _dot_nt(wt_ref[0:nq, :], xh).astype(BF16)
    qbt_ref[...] = _dot_nt(wt_ref[nq:2 * nq, :], xh).astype(BF16)
    o = 2 * nq
    row = lax.broadcasted_iota(jnp.int32, (2 * HEAD_DIM, tm), 0)
    lo_rows = row < HEAD_DIM
    for t_ref, x_out_ref, r0 in ((vat_ref, vax_ref, o), (vbt_ref, vbx_ref, o + nk)):
        vt = _dot_nt(wt_ref[r0:r0 + nk, :], xh)
        t_ref[...] = vt
        for m in range(KV_A // 2):
            blk = vt[m * 2 * HEAD_DIM:(m + 1) * 2 * HEAD_DIM, :]
            lo = jnp.where(lo_rows, blk, 1.0).astype(BF16)
            hi = jnp.where(lo_rows, 1.0, blk).astype(BF16)
            for t in range(tm // tk):
                cols = slice(t * tk, (t + 1) * tk)
                x_out_ref[t, m * 4 * HEAD_DIM:m * 4 * HEAD_DIM + 2 * HEAD_DIM, :] = lo[:, cols]
                x_out_ref[t, m * 4 * HEAD_DIM + 2 * HEAD_DIM:(m + 1) * 4 * HEAD_DIM, :] = hi[:, cols]
    kat_ref[...] = _dot_nt(wt_ref[o + 2 * nk:o + 3 * nk, :], xh)
    kbt_ref[...] = _dot_nt(wt_ref[o + 3 * nk:o + 4 * nk, :], xh)

    kn = _dot(xh, wn_ref[...])
    kbn_ref[...] = kn[:, 2 * nk:3 * nk].astype(BF16)

    qit = (_dot_nt(wth_ref[...], xh) + _dot_nt(wth_ref[...], xl)) + _dot_nt(wtl_ref[...], xh)
    qh, ql = _split2(qit)
    qith_ref[...] = qh
    qitl_ref[...] = ql

    rest = _dot3(xh, xl, wnh_ref[...], wnl_ref[...])
    ki2 = rest[:, 0:LANES]
    ki_ref[...] = ki2[:, 0:D_IDX]
    kh, kl = _split2(ki2)
    kih_ref[...] = kh
    kil_ref[...] = kl
    small = rest[:, LANES:2 * LANES]
    wi_ref[...] = small[:, H_A:H_A + H_IDX] * (H_IDX ** -0.5)
    logf = _log_sigmoid(small + bf_ref[...])
    logf_ref[...] = logf[:, 0:H_A]

    @pl.when(pl.program_id(0) % tiles_per_seq == 0)
    def _():
        carry_ref[...] = jnp.zeros_like(carry_ref)

    h1, h2, h3 = _split3(logf)
    tri = tri_ref[...]
    c = (_dot(tri, h1) + _dot(tri, h2)) + _dot(tri, h3) + carry_ref[...]
    carry_ref[...] = c[tm - 1:tm, :]
    c1, c2, c3 = _split3(c)
    aug = (_dot(c1, p1_ref[...]) + _dot(c2, p2_ref[...])) + _dot(c3, p3_ref[...])
    kaug_ref[...] = (kn[:, 0:2 * nk] + aug).astype(BF16)


def _decay_placement():
    mats = np.zeros((3, LANES, 2 * KV_A * HEAD_DIM), np.float32)
    for m in range(KV_A // 2):
        for v in range(2 * GROUP):
            head = HEAD_PERM[4 * m + v]
            hi_variant, g = v % 2, v // 2
            base = m * 4 * HEAD_DIM + (2 * HEAD_DIM if hi_variant else HEAD_DIM)
            for t in range(3):
                mats[t, head, base + 3 * g + t] = -1.0
    return [jnp.asarray(mats[t], BF16) for t in range(3)]


def _project_t(x2d, wts, bf_pad, *, batch, seq, tk):
    m, d = x2d.shape
    tm = min(512, seq)
    assert seq % tm == 0 and tm % tk == 0
    tps = seq // tm
    nq, nk, ni = H_A * HEAD_DIM, KV_A * HEAD_DIM, H_IDX * D_IDX
    nkv = seq // tk
    full = lambda a: pl.BlockSpec(a.shape, lambda i: (0,) * a.ndim)
    row = lambda n: pl.BlockSpec((tm, n), lambda i: (i, 0))
    feat = lambda r: pl.BlockSpec((None, r, tm), lambda i: (i // tps, 0, i % tps))
    vx = pl.BlockSpec((None, tm // tk, 2 * nk, tk), lambda i: (i // tps, i % tps, 0, 0))
    tri = jnp.tril(jnp.ones((tm, tm), F32)).astype(BF16)
    inputs = [x2d, wts['t_big'], wts['n_keys'], wts['n_idx_hi'], wts['n_idx_lo'], wts['t_idx_hi'],
              wts['t_idx_lo'], bf_pad, tri] + _decay_placement()
    fshape = lambda r, dt: ((batch, r, seq), dt)
    outs = [
        (fshape(nq, BF16), feat(nq)), (fshape(nq, BF16), feat(nq)),
        (fshape(nk, F32), feat(nk)), (fshape(nk, F32), feat(nk)),
        (fshape(nk, F32), feat(nk)), (fshape(nk, F32), feat(nk)),
        (((batch, nkv, 2 * nk, tk), BF16), vx), (((batch, nkv, 2 * nk, tk), BF16), vx),
        (((m, 2 * nk), BF16), row(2 * nk)), (((m, nk), BF16), row(nk)),
        (fshape(ni, BF16), feat(ni)), (fshape(ni, BF16), feat(ni)),
        (((m, D_IDX), F32), row(D_IDX)), (((m, LANES), BF16), row(LANES)), (((m, LANES), BF16), row(LANES)),
        (((m, H_IDX), F32), row(H_IDX)), (((m, H_A), F32), row(H_A)),
    ]
    res = pl.pallas_call(
        functools.partial(_proj_t_kernel, tiles_per_seq=tps, tk=tk),
        grid=(m // tm,),
        in_specs=[row(d)] + [full(a) for a in inputs[1:]],
        out_specs=[o[1] for o in outs],
        out_shape=[jax.ShapeDtypeStruct(*o[0]) for o in outs],
        scratch_shapes=[pltpu.VMEM((1, LANES), F32)],
        compiler_params=_cparams(("arbitrary",)),
        name="proj_prompt",
    )(*inputs)
    names = ['qat', 'qbt', 'vat', 'vbt', 'kat', 'kbt', 'vax', 'vbx', 'kaug', 'kbn', 'qit_hi', 'qit_lo',
             'ki', 'ki2_hi', 'ki2_lo', 'wi', 'logf']
    return dict(zip(names, res))


def _post_kernel(x_ref, oa_ref, ob_ref, woa_ref, wob_ref, g1_ref, b1_ref, wup_ref, wdn_ref,
                 g2_ref, b2_ref, y_ref, acc_ref, *, alpha, n_chunks, feature_major):
    x = x_ref[...]
    if feature_major:
        tn = lambda a, w: lax.dot_general(a, w, (((0,), (0,)), ((), ())), preferred_element_type=F32)
        att = tn(oa_ref[...], woa_ref[...]) + tn(ob_ref[...], wob_ref[...])
    else:
        att = _dot(oa_ref[...], woa_ref[...]) + _dot(ob_ref[...], wob_ref[...])
    x1 = _layer_norm(alpha * x + att, g1_ref[...], b1_ref[...])
    x1b = x1.astype(BF16)
    acc_ref[...] = jnp.zeros_like(acc_ref)

    def body(c, carry):
        u = jnp.maximum(_dot(x1b, wup_ref[c]), 0.0)
        acc_ref[...] += _dot((u * u).astype(BF16), wdn_ref[c])
        return carry

    lax.fori_loop(0, n_chunks, body, 0)
    y_ref[...] = _layer_norm(alpha * x1 + acc_ref[...], g2_ref[...], b2_ref[...])


def _post_block(x2d, oa, ob, wts, alpha, seq=None):
    m, d = x2d.shape
    tm = min(POST_ROWS, m if seq is None else seq)
    n_chunks = wts['up'].shape[0]
    full = lambda a: pl.BlockSpec(a.shape, lambda i: (0,) * a.ndim, pipeline_mode=pl.Buffered(1))
    row = lambda n: pl.BlockSpec((tm, n), lambda i: (i, 0))
    if seq is None:
        o_spec = lambda a: row(a.shape[1])
    else:
        tps = seq // tm
        o_spec = lambda a: pl.BlockSpec((None, a.shape[1], tm), lambda i: (i // tps, 0, i % tps))
    weights = [wts['out_a'], wts['out_b'], wts['ln1_g'], wts['ln1_b'], wts['up'], wts['down'],
               wts['ln2_g'], wts['ln2_b']]
    return pl.pallas_call(
        functools.partial(_post_kernel, alpha=alpha, n_chunks=n_chunks, feature_major=seq is not None),
        grid=(m // tm,),
        in_specs=[row(d), o_spec(oa), o_spec(ob)] + [full(w) for w in weights],
        out_specs=row(d),
        out_shape=jax.ShapeDtypeStruct((m, d), F32),
        scratch_shapes=[pltpu.VMEM((tm, d), F32)],
        compiler_params=_cparams(("arbitrary",)),
        name="post_block",
    )(x2d, oa, ob, *weights)


def _variant_rows(g, hi_half, ones_from=None):
    row = lax.broadcasted_iota(jnp.int32, g.shape, 0)
    keep = (row >= HEAD_DIM) if hi_half else (row < HEAD_DIM)
    fill = 0.0
    if ones_from is not None:
        fill = jnp.where((row >= ones_from) & (row < ones_from + 3), 1.0, 0.0)
    return jnp.where(keep, g, fill).astype(BF16)


def _softmax_chunk(s, causal_q0, m_ref, p_ref, acc_ref, cols, acols):
    if causal_q0 is not None:
        key = lax.broadcasted_iota(jnp.int32, s.shape, 0)
        qry = lax.broadcasted_iota(jnp.int32, s.shape, 1) + causal_q0
        s = jnp.where(key <= qry, s, NEG_BIG)
    m_old = m_ref[:, cols]
    m_new = jnp.maximum(m_old, jnp.max(s, axis=0, keepdims=True))
    m_ref[:, cols] = m_new
    p_ref[:, cols] = jnp.exp(s - m_new[0:1, :]).astype(BF16)
    acc_ref[:, acols] = acc_ref[:, acols] * jnp.exp(m_old - m_new)[0:1, :]


def _finish_pair(acc_lo, acc_hi, tq):
    row = lax.broadcasted_iota(jnp.int32, (2 * HEAD_DIM, tq), 0)
    lo_rows = row < HEAD_DIM
    out = []
    for g in range(GROUP):
        cols = slice(g * tq, (g + 1) * tq)
        a_lo, a_hi = acc_lo[:, cols], acc_hi[:, cols]
        o_lo = a_lo / a_lo[HEAD_DIM:HEAD_DIM + 1, :]
        o_hi = a_hi / a_hi[0:1, :]
        out.append(jnp.where(lo_rows, o_lo, o_hi))
    return jnp.concatenate(out, axis=0).astype(BF16)


def _fox_kernel(q_ref, k_ref, v_ref, o_ref, qt_ref, s_ref, p_ref, m_ref, acc_ref, *, tq):
    i = pl.program_id(2)
    q = q_ref[...].astype(F32)
    g0, g1 = q[0:2 * HEAD_DIM], q[2 * HEAD_DIM:4 * HEAD_DIM]
    qt_ref[0, :, 0:tq] = _variant_rows(g0, False, HEAD_DIM)
    qt_ref[0, :, tq:2 * tq] = _variant_rows(g1, False, HEAD_DIM + 3)
    qt_ref[1, :, 0:tq] = _variant_rows(g0, True, 0)
    qt_ref[1, :, tq:2 * tq] = _variant_rows(g1, True, 3)
    m_ref[...] = jnp.full_like(m_ref, NEG_BIG)
    acc_ref[...] = jnp.zeros_like(acc_ref)

    def process(blocks):
        for slot, (j, _) in enumerate(blocks):
            start = pl.multiple_of(j * tq, tq)
            for d in range(2):
                s_ref[slot, d] = _dot(k_ref[pl.ds(start, tq), d * LANES:(d + 1) * LANES], qt_ref[d])
        for slot, (j, causal) in enumerate(blocks):
            for d in range(2):
                for c in range(2 * tq // LANES):
                    cols = slice(c * LANES, (c + 1) * LANES)
                    q0 = (c * LANES) % tq if causal else None
                    _softmax_chunk(s_ref[slot, d, :, cols], q0, m_ref.at[d], p_ref.at[slot, d],
                                   acc_ref.at[d], cols, cols)
                acc_ref[d] += _dot(v_ref[j, d * LANES:(d + 1) * LANES, :], p_ref[slot, d])

    def pair_body(t, carry):
        process(((2 * t, False), (2 * t + 1, False)))
        return carry

    lax.fori_loop(0, i // 2, pair_body, 0)

    @pl.when(i % 2 == 1)
    def _():
        process(((i - 1, False), (i, True)))

    @pl.when(i % 2 == 0)
    def _():
        process(((i, True),))

    o_ref[...] = _finish_pair(acc_ref[0], acc_ref[1], tq)


def _fox_attention(qt, kaug, vx, batch, seq, tq):
    nq = seq // tq
    pair = 4 * HEAD_DIM
    return pl.pallas_call(
        functools.partial(_fox_kernel, tq=tq),
        grid=(batch, KV_A // 2, nq),
        in_specs=[
            pl.BlockSpec((None, pair, tq), lambda b, m, i: (b, m, i)),
            pl.BlockSpec((seq, pair), lambda b, m, i: (b, m)),
            pl.BlockSpec((None, nq, pair, tq), lambda b, m, i: (b, 0, m, 0)),
        ],
        out_specs=pl.BlockSpec((None, pair, tq), lambda b, m, i: (b, m, i)),
        out_shape=jax.ShapeDtypeStruct(qt.shape, BF16),
        scratch_shapes=[pltpu.VMEM((2, LANES, 2 * tq), BF16), pltpu.VMEM((2, 2, tq, 2 * tq), F32),
                        pltpu.VMEM((2, 2, tq, 2 * tq), BF16), pltpu.VMEM((2, 8, 2 * tq), F32),
                        pltpu.VMEM((2, LANES, 2 * tq), F32)],
        compiler_params=_cparams(("arbitrary", "arbitrary", "arbitrary")),
        name="fox_attention",
    )(qt, kaug, vx)


def _relb_kernel(rb_ref, o_ref, *, tq):
    delta = pl.program_id(0)
    h = pl.program_id(1)
    key = lax.broadcasted_iota(jnp.int32, (tq, tq), 0)
    qry = lax.broadcasted_iota(jnp.int32, (tq, tq), 1)
    bucket = _rel_bucket(jnp.maximum(delta * tq + qry - key, 0))
    acc = jnp.zeros((tq, tq), F32)
    for n in range(N_BUCKETS):
        acc = jnp.where(bucket == n, rb_ref[n, h], acc)
    o_ref[...] = acc


def _rel_bias_tiles(rb_perm, tq):
    return pl.pallas_call(
        functools.partial(_relb_kernel, tq=tq),
        grid=(2, H_B),
        in_specs=[pl.BlockSpec(memory_space=pltpu.SMEM)],
        out_specs=pl.BlockSpec((None, None, tq, tq), lambda d, h: (d, h, 0, 0)),
        out_shape=jax.ShapeDtypeStruct((2, H_B, tq, tq), F32),
        compiler_params=_cparams(("arbitrary", "arbitrary")),
        name="rel_bias_tiles",
    )(rb_perm)


def _strict_upper(n):
    r = lax.broadcasted_iota(jnp.int32, (n, n), 0)
    c = lax.broadcasted_iota(jnp.int32, (n, n), 1)
    return jnp.where(r < c, 1.0, 0.0).astype(BF16)


def _strict_lower(n):
    r = lax.broadcasted_iota(jnp.int32, (n, n), 0)
    c = lax.broadcasted_iota(jnp.int32, (n, n), 1)
    return jnp.where(c < r, 1.0, 0.0).astype(BF16)


KEY_ROWS = 64

DSA_VARIANT_ORDER = (0, 2, 1, 3)


def _dsa_kernel(qih_ref, qil_ref, kih_ref, kil_ref, wi_ref, qb_ref, k_ref, v_ref, relb_ref, rbfar_ref,
                o_ref, keys_ref, mb_ref, qi_ref, lg_ref, qt_ref, p_ref, m_ref, acc_ref, *, tq, topk):
    i = pl.program_id(1)
    nblk = i + 1
    nlc = tq // LANES

    for h in range(H_IDX):
        rows = slice((h // 2) * LANES, (h // 2 + 1) * LANES)
        cols = slice(h * tq, (h + 1) * tq)
        mh = _variant_rows(qih_ref[rows, :].astype(F32), h % 2 == 1)
        qi_ref[0:LANES, cols] = mh
        qi_ref[LANES:2 * LANES, cols] = _variant_rows(qil_ref[rows, :].astype(F32), h % 2 == 1)
        qi_ref[2 * LANES:3 * LANES, cols] = mh

    def score_block(j, diag):
        start = pl.multiple_of(j * tq, tq)
        kh = kih_ref[pl.ds(start, tq), :]
        kl = kil_ref[pl.ds(start, tq), :]
        lg_ref[...] = _dot(jnp.concatenate([kh, kh, kl], axis=1), qi_ref[...])
        for c in range(nlc):
            for r in range(tq // KEY_ROWS):
                rows = slice(r * KEY_ROWS, (r + 1) * KEY_ROWS)
                sc = jnp.zeros((KEY_ROWS, LANES), F32)
                for h in range(H_IDX):
                    lane0 = h * tq + c * LANES
                    sc = sc + wi_ref[h:h + 1, c * LANES:(c + 1) * LANES] * jnp.maximum(
                        lg_ref[rows, lane0:lane0 + LANES], 0.0)
                key = _order_key(sc)
                if diag:
                    kidx = lax.broadcasted_iota(jnp.int32, key.shape, 0) + r * KEY_ROWS
                    qidx = lax.broadcasted_iota(jnp.int32, key.shape, 1) + c * LANES
                    key = jnp.where(kidx <= qidx, key, INT_MIN)
                keys_ref[j, rows, c * LANES:(c + 1) * LANES] = key

    def score_body(j, carry):
        score_block(j, False)
        return carry

    lax.fori_loop(0, i, score_body, 0)
    score_block(i, True)

    needs_select = (i + 1) * tq > topk

    @pl.when(jnp.logical_not(needs_select))
    def _():
        def zero_body(j, carry):
            mb_ref[j] = jnp.zeros((tq, tq), F32)
            return carry
        lax.fori_loop(0, nblk, zero_body, 0)

    @pl.when(needs_select)
    def _():
        def count(pred_fn):
            def body(j, acc):
                hit = jnp.where(pred_fn(keys_ref[j]), 1.0, 0.0)
                for r in range(tq // 8):
                    acc = acc + hit[r * 8:(r + 1) * 8, :]
                return acc
            acc = lax.fori_loop(0, nblk, body, jnp.zeros((8, tq), F32))
            return jnp.sum(acc, axis=0, keepdims=True)

        def bit_body(b, t):
            cand = t + lax.shift_left(jnp.int32(1), 31 - b)
            return jnp.where(count(lambda kblk: kblk >= cand) >= topk, cand, t)

        thr = lax.fori_loop(0, 32, bit_body, jnp.full((1, tq), INT_MIN, jnp.int32))
        need = topk - count(lambda kblk: kblk > thr)
        earlier = _strict_lower(tq)

        def mask_body(j, seen):
            kblk = keys_ref[j]
            eq = kblk == thr
            eqf = jnp.where(eq, 1.0, 0.0)
            rank = _dot(earlier, eqf.astype(BF16)) + seen
            mb_ref[j] = jnp.where(kblk > thr, 0.0,
                                  jnp.where(eq, jnp.where(rank < need, 0.0, NEG_BIG), NEG_BIG))
            return seen + jnp.sum(eqf, axis=0, keepdims=True)

        lax.fori_loop(0, nblk, mask_body, jnp.zeros((1, tq), F32))

    pair = 4 * HEAD_DIM
    for m in range(KV_B // 2):
        q = qb_ref[m * pair:(m + 1) * pair, :].astype(F32)
        g = (q[0:2 * HEAD_DIM], q[2 * HEAD_DIM:4 * HEAD_DIM])
        for n, v in enumerate(DSA_VARIANT_ORDER):
            qt_ref[:, n * tq:(n + 1) * tq] = _variant_rows(g[v // 2], v % 2 == 1)
        m_ref[...] = jnp.full_like(m_ref, NEG_BIG)
        acc_ref[...] = jnp.zeros_like(acc_ref)

        far = lambda slot, qcols: rbfar_ref[slot]
        near = lambda slot, qcols: relb_ref[1, slot, :, qcols]
        diag = lambda slot, qcols: relb_ref[0, slot, :, qcols]

        def process(blocks):
            for slot, (j, _) in enumerate(blocks):
                start = pl.multiple_of(j * tq, tq)
                lg_ref[:, slot * 4 * tq:(slot + 1) * 4 * tq] = _dot(
                    k_ref[pl.ds(start, tq), m * LANES:(m + 1) * LANES], qt_ref[...])
            for slot, (j, rel) in enumerate(blocks):
                for n, v in enumerate(DSA_VARIANT_ORDER):
                    for c in range(nlc):
                        qcols = slice(c * LANES, (c + 1) * LANES)
                        cols = slice(n * tq + c * LANES, n * tq + (c + 1) * LANES)
                        lcols = slice(slot * 4 * tq + cols.start, slot * 4 * tq + cols.stop)
                        s = lg_ref[:, lcols] + (mb_ref[j, :, qcols] + rel(4 * m + v, qcols))
                        acols = slice((n % 2) * tq + c * LANES, (n % 2) * tq + (c + 1) * LANES)
                        _softmax_chunk(s, c * LANES if rel is diag else None, m_ref, p_ref.at[slot],
                                       acc_ref.at[n // 2], cols, acols)
                for d in range(2):
                    vt = v_ref[j, m * pair + d * LANES:m * pair + (d + 1) * LANES, :]
                    acc_ref[d] += _dot(vt, p_ref[slot, :, d * 2 * tq:(d + 1) * 2 * tq])

        n_far = jnp.maximum(i - 1, 0)

        def far_body(t, carry):
            process(((2 * t, far), (2 * t + 1, far)))
            return carry

        lax.fori_loop(0, n_far // 2, far_body, 0)

        @pl.when(n_far % 2 == 1)
        def _():
            process(((i - 2, far),))

        @pl.when(i >= 1)
        def _():
            process(((i - 1, near), (i, diag)))

        @pl.when(i == 0)
        def _():
            process(((i, diag),))

        o_ref[m * pair:(m + 1) * pair, :] = _finish_pair(acc_ref[0], acc_ref[1], tq)


def _dsa_attention(p, wi_t, relb, rb_far, batch, seq, tq, topk):
    nq = seq // tq
    nf = H_B * HEAD_DIM
    qcol = lambda r: pl.BlockSpec((None, r, tq), lambda b, i: (b, 0, i))
    seqblk = lambda n: pl.BlockSpec((seq, n), lambda b, i: (b, 0))
    return pl.pallas_call(
        functools.partial(_dsa_kernel, tq=tq, topk=topk),
        grid=(batch, nq),
        in_specs=[qcol(H_IDX * D_IDX), qcol(H_IDX * D_IDX), seqblk(LANES), seqblk(LANES), qcol(H_IDX),
                  qcol(nf), seqblk(KV_B * HEAD_DIM),
                  pl.BlockSpec((None, nq, nf, tq), lambda b, i: (b, 0, 0, 0)),
                  pl.BlockSpec(relb.shape, lambda b, i: (0, 0, 0, 0)),
                  pl.BlockSpec(memory_space=pltpu.SMEM)],
        out_specs=qcol(nf),
        out_shape=jax.ShapeDtypeStruct((batch, nf, seq), BF16),
        scratch_shapes=[pltpu.VMEM((nq, tq, tq), jnp.int32), pltpu.VMEM((nq, tq, tq), F32),
                        pltpu.VMEM((3 * LANES, H_IDX * tq), BF16), pltpu.VMEM((tq, H_IDX * tq), F32),
                        pltpu.VMEM((LANES, 4 * tq), BF16), pltpu.VMEM((2, tq, 4 * tq), BF16),
                        pltpu.VMEM((8, 4 * tq), F32), pltpu.VMEM((2, LANES, 2 * tq), F32)],
        compiler_params=_cparams(("arbitrary", "arbitrary")),
        name="dsa_attention",
    )(p['qit_hi'], p['qit_lo'], p['ki2_hi'], p['ki2_lo'], wi_t, p['qbt'], p['kbn'], p['vbx'], relb, rb_far)


def _page_specs(n, block, layer, pages_per_step):
    def make(r):
        return pl.BlockSpec((None, None) + block,
                            lambda b, g, pt: (layer, pt[b, g * pages_per_step + r]) + (0,) * len(block))
    return [make(r) for r in range(n)]


def _sscore_kernel(pt_ref, q16_ref, wi_ref, *refs, pps):
    page_refs, out_ref = refs[:pps], refs[pps]
    q16 = q16_ref[...]
    qh = q16[0:H_IDX]
    wi = wi_ref[...]
    for r in range(pps):
        kh, kl = _split2(page_refs[r][...])
        a = _dot(q16, kh)
        logits = (a[0:H_IDX] + a[H_IDX:2 * H_IDX]) + _dot(qh, kl)
        out_ref[:, r * PAGE_SIZE:(r + 1) * PAGE_SIZE] = jnp.sum(
            wi * jnp.maximum(logits, 0.0), axis=0, keepdims=True)


def _sample_scores(page_table, q16, wi_col, cache_idx_k, layer, pps):
    db, n_pages = page_table.shape
    grid_spec = pltpu.PrefetchScalarGridSpec(
        num_scalar_prefetch=1,
        grid=(db, n_pages // pps),
        in_specs=[pl.BlockSpec((None, 2 * H_IDX, D_IDX), lambda b, g, pt: (b, 0, 0)),
                  pl.BlockSpec((None, H_IDX, 1), lambda b, g, pt: (b, 0, 0))]
                 + _page_specs(pps, (D_IDX, PAGE_SIZE), layer, pps),
        out_specs=pl.BlockSpec((None, 1, pps * PAGE_SIZE), lambda b, g, pt: (b, 0, g)),
    )
    return pl.pallas_call(
        functools.partial(_sscore_kernel, pps=pps),
        grid_spec=grid_spec,
        out_shape=jax.ShapeDtypeStruct((db, 1, n_pages * PAGE_SIZE), F32),
        compiler_params=_cparams(("arbitrary", "arbitrary")),
        name="sample_scores",
    )(page_table, q16, wi_col, *([cache_idx_k] * pps))


def _sdecay_kernel(pt_ref, lnew_ref, *refs, pps):
    page_refs, out_ref, carry_ref = refs[:pps], refs[pps], refs[pps + 1]

    @pl.when(pl.program_id(1) == 0)
    def _():
        carry_ref[...] = jnp.broadcast_to(lnew_ref[...], carry_ref.shape)

    r_i = lax.broadcasted_iota(jnp.int32, (PAGE_SIZE, PAGE_SIZE), 0)
    c_i = lax.broadcasted_iota(jnp.int32, (PAGE_SIZE, PAGE_SIZE), 1)
    later = jnp.where(r_i > c_i, 1.0, 0.0).astype(BF16)
    for r in reversed(range(pps)):
        x = page_refs[r][...]
        h1, h2, h3 = _split3(x)
        within = (_dot(h1, later) + _dot(h2, later)) + _dot(h3, later)
        carry = carry_ref[...]
        out_ref[:, r * PAGE_SIZE:(r + 1) * PAGE_SIZE] = within + carry[:, 0:1]
        carry_ref[...] = carry + jnp.sum(x, axis=1, keepdims=True)


def _sample_decay(page_table, logf_new_col, logf_t_cache, layer, pps):
    db, n_pages = page_table.shape
    nsteps = n_pages // pps

    def page_spec(r):
        return pl.BlockSpec((None, None, H_A, PAGE_SIZE),
                            lambda b, g, pt: (layer, pt[b, (nsteps - 1 - g) * pps + r], 0, 0))

    grid_spec = pltpu.PrefetchScalarGridSpec(
        num_scalar_prefetch=1,
        grid=(db, nsteps),
        in_specs=[pl.BlockSpec((None, H_A, 1), lambda b, g, pt: (b, 0, 0))]
                 + [page_spec(r) for r in range(pps)],
        out_specs=pl.BlockSpec((None, H_A, pps * PAGE_SIZE), lambda b, g, pt: (b, 0, nsteps - 1 - g)),
        scratch_shapes=[pltpu.VMEM((H_A, LANES), F32)],
    )
    return pl.pallas_call(
        functools.partial(_sdecay_kernel, pps=pps),
        grid_spec=grid_spec,
        out_shape=jax.ShapeDtypeStruct((db, H_A, n_pages * PAGE_SIZE), F32),
        compiler_params=_cparams(("arbitrary", "arbitrary")),
        name="sample_decay",
    )(page_table, logf_new_col, *([logf_t_cache] * pps))


def _sselect_kernel(sc_ref, q_ref, kt_ref, wi_ref, mb_ref, mbn_ref, keys_ref, *, topk, chunk):
    db, past = sc_ref.shape
    prod = q_ref[...] * kt_ref[...]
    lane = lax.broadcasted_iota(jnp.int32, prod.shape, 1)
    wi = wi_ref[...]
    sc_new = jnp.zeros((db, 1), F32)
    for h in range(H_IDX):
        seg = (lane >= h * D_IDX) & (lane < (h + 1) * D_IDX)
        logit = jnp.sum(jnp.where(seg, prod, 0.0), axis=1, keepdims=True)
        sc_new = sc_new + wi[:, h:h + 1] * jnp.maximum(logit, 0.0)
    key_new = _order_key(sc_new)
    keys_ref[...] = _order_key(sc_ref[...])

    def count(pred_fn):
        n = jnp.sum(jnp.where(pred_fn(keys_ref[...]), 1.0, 0.0), axis=1, keepdims=True)
        return n + jnp.where(pred_fn(key_new), 1.0, 0.0)

    def bit_body(b, t):
        cand = t + lax.shift_left(jnp.int32(1), 31 - b)
        return jnp.where(count(lambda k: k >= cand) >= topk, cand, t)

    thr = lax.fori_loop(0, 32, bit_body, jnp.full((db, 1), INT_MIN, jnp.int32))
    need = topk - count(lambda k: k > thr)
    sut = _strict_upper(chunk)
    seen = jnp.zeros((db, 1), F32)
    for c in range(past // chunk):
        cols = slice(c * chunk, (c + 1) * chunk)
        kblk = keys_ref[:, cols]
        eq = kblk == thr
        eqf = jnp.where(eq, 1.0, 0.0)
        rank = _dot(eqf.astype(BF16), sut) + seen
        mb_ref[:, cols] = jnp.where(kblk > thr, 0.0,
                                    jnp.where(eq, jnp.where(rank < need, 0.0, NEG_BIG), NEG_BIG))
        seen = seen + jnp.sum(eqf, axis=1, keepdims=True)
    sel_new = jnp.where(key_new > thr, 0.0,
                        jnp.where(key_new == thr, jnp.where(seen < need, 0.0, NEG_BIG), NEG_BIG))
    mbn_ref[...] = jnp.broadcast_to(sel_new, mbn_ref.shape)


def _sample_select(scores, q_f32, ki_tiled, wi, topk):
    db, past = scores.shape
    return pl.pallas_call(
        functools.partial(_sselect_kernel, topk=topk, chunk=2 * LANES),
        out_shape=[jax.ShapeDtypeStruct((db, past), F32), jax.ShapeDtypeStruct((db, LANES), F32)],
        scratch_shapes=[pltpu.VMEM((db, past), jnp.int32)],
        compiler_params=pltpu.CompilerParams(vmem_limit_bytes=VMEM_LIMIT_BYTES),
        name="sample_select",
    )(scores, q_f32, ki_tiled, wi)


def _sattn_kernel(pt_ref, qa_ref, qb_ref, dec_ref, mb_ref, rbt_ref, kan_ref, van_ref, kbn_ref, vbn_ref,
                  mbn_ref, ak_hbm, av_hbm, bk_hbm, bv_hbm, oa_ref, ob_ref,
                  buf, sem, ma_ref, la_ref, acca_ref, mb_m_ref, lb_ref, accb_ref, *, pps, past, layer):
    b = pl.program_id(0)
    n_rows = pl.num_programs(0)
    width = pps * PAGE_SIZE
    n_chunks = past // width
    caches = (ak_hbm, av_hbm, bk_hbm, bv_hbm)

    def page_copy(row, chunk, slot, t, r):
        page = pt_ref[row, chunk * pps + r]
        return pltpu.make_async_copy(caches[t].at[layer, page], buf.at[slot, t, r], sem.at[slot])

    def start_chunk(row, chunk, slot):
        for t in range(len(caches)):
            for r in range(pps):
                page_copy(row, chunk, slot, t, r).start()

    def wait_chunk(row, chunk, slot):
        for t in range(len(caches)):
            for r in range(pps):
                page_copy(row, chunk, slot, t, r).wait()

    @pl.when(b == 0)
    def _():
        start_chunk(0, 0, 0)

    for m_ref, l_ref, acc_ref in ((ma_ref, la_ref, acca_ref), (mb_m_ref, lb_ref, accb_ref)):
        m_ref[...] = jnp.full_like(m_ref, NEG_BIG)
        l_ref[...] = jnp.zeros_like(l_ref)
        acc_ref[...] = jnp.zeros_like(acc_ref)

    qa, qb = qa_ref[...], qb_ref[...]
    rbt = rbt_ref[...]

    def update(s, slot, t, m_ref, l_ref, acc_ref):
        m_old = m_ref[...]
        m_new = jnp.maximum(m_old, jnp.max(s, axis=1, keepdims=True))
        a = jnp.exp(m_old - m_new)
        p = jnp.exp(s - m_new)
        l_ref[...] = a * l_ref[...] + jnp.sum(p, axis=1, keepdims=True)
        pv = jnp.zeros(acc_ref.shape, F32)
        for r in range(pps):
            pv = pv + _dot_nt(p[:, r * PAGE_SIZE:(r + 1) * PAGE_SIZE].astype(BF16),
                              buf[slot, t, r].astype(BF16))
        acc_ref[...] = a * acc_ref[...] + pv
        m_ref[...] = m_new

    def compute(chunk, slot):
        s_a = jnp.concatenate([_dot(qa, buf[slot, 0, r].astype(BF16)) for r in range(pps)], axis=1)
        update(s_a + dec_ref[chunk], slot, 1, ma_ref, la_ref, acca_ref)
        pos = chunk * width + lax.broadcasted_iota(jnp.int32, (1, width), 1)
        bucket = _rel_bucket(past - pos)
        rel = jnp.zeros((H_B, width), F32)
        for n in range(N_BUCKETS):
            rel = jnp.where(bucket == n, rbt[:, n:n + 1], rel)
        s_b = jnp.concatenate([_dot(qb, buf[slot, 2, r].astype(BF16)) for r in range(pps)], axis=1)
        update(s_b + rel + mb_ref[chunk], slot, 3, mb_m_ref, lb_ref, accb_ref)

    def pair_body(cp, carry):
        c0 = 2 * cp
        start_chunk(b, c0 + 1, 1)
        wait_chunk(b, c0, 0)
        compute(c0, 0)

        @pl.when(c0 + 2 < n_chunks)
        def _():
            start_chunk(b, c0 + 2, 0)

        @pl.when(jnp.logical_and(c0 + 2 == n_chunks, b + 1 < n_rows))
        def _():
            start_chunk(b + 1, 0, 0)

        wait_chunk(b, c0 + 1, 1)
        compute(c0 + 1, 1)
        return carry

    lax.fori_loop(0, n_chunks // 2, pair_body, 0)

    def finish(q, kn, vn, extra, m_ref, l_ref, acc_ref, o_ref):
        s = jnp.sum(q.astype(F32) * kn, axis=1, keepdims=True) + extra
        m_old = m_ref[...]
        m_new = jnp.maximum(m_old, s)
        a = jnp.exp(m_old - m_new)
        p = jnp.exp(s - m_new)
        l = a * l_ref[...] + p
        o_ref[...] = (a * acc_ref[...] + p * vn) / l

    finish(qa, kan_ref[...], van_ref[...], 0.0, ma_ref, la_ref, acca_ref, oa_ref)
    finish(qb, kbn_ref[...], vbn_ref[...], rbt[:, 0:1] + mbn_ref[:, 0:1], mb_m_ref, lb_ref, accb_ref, ob_ref)


def _sample_attention(page_table, qa_blk, qb_blk, dec, mb, rbt, new, caches, layer, pps):
    db, n_pages = page_table.shape
    past = n_pages * PAGE_SIZE
    width = KV_A * HEAD_DIM
    n_chunks = n_pages // pps
    assert n_chunks % 2 == 0 and n_chunks * pps == n_pages
    per_b = lambda shape: pl.BlockSpec((None,) + shape, lambda b, pt: (b,) + (0,) * len(shape))
    hbm = pl.BlockSpec(memory_space=pl.ANY)
    in_specs = [per_b((H_A, width)), per_b((H_B, width)),
                per_b((n_chunks, H_A, pps * PAGE_SIZE)), per_b((n_chunks, 1, pps * PAGE_SIZE)),
                pl.BlockSpec(rbt.shape, lambda b, pt: (0, 0)),
                per_b((1, width)), per_b((1, width)), per_b((1, width)), per_b((1, width)),
                per_b((1, LANES)), hbm, hbm, hbm, hbm]
    grid_spec = pltpu.PrefetchScalarGridSpec(
        num_scalar_prefetch=1,
        grid=(db,),
        in_specs=in_specs,
        out_specs=[per_b((H_A, width)), per_b((H_B, width))],
        scratch_shapes=[pltpu.VMEM((2, len(caches), pps, width, PAGE_SIZE), F32), pltpu.SemaphoreType.DMA((2,)),
                        pltpu.VMEM((H_A, 1), F32), pltpu.VMEM((H_A, 1), F32), pltpu.VMEM((H_A, width), F32),
                        pltpu.VMEM((H_B, 1), F32), pltpu.VMEM((H_B, 1), F32), pltpu.VMEM((H_B, width), F32)],
    )
    return pl.pallas_call(
        functools.partial(_sattn_kernel, pps=pps, past=past, layer=layer),
        grid_spec=grid_spec,
        out_shape=[jax.ShapeDtypeStruct((db, H_A, width), F32), jax.ShapeDtypeStruct((db, H_B, width), F32)],
        compiler_params=_cparams(("arbitrary",)),
        name="sample_attention",
    )(page_table, qa_blk, qb_blk, dec, mb, rbt, new['ka'], new['va'], new['kb'], new['vb'], new['mbn'], *caches)


def _head_cols(w, perm):
    d = w.shape[0]
    return w.reshape(d, len(perm), HEAD_DIM)[:, list(perm), :].reshape(d, -1)


def _pad_cols(w, n):
    return jnp.pad(w, ((0, 0), (0, n - w.shape[1])))


def _split_w_in(w_in_l):
    sizes = (H_A * HEAD_DIM, KV_A * HEAD_DIM, KV_A * HEAD_DIM, H_A, H_B * HEAD_DIM, KV_B * HEAD_DIM,
             KV_B * HEAD_DIM, H_IDX * D_IDX, D_IDX, H_IDX)
    offs = np.cumsum((0,) + sizes)
    return [w_in_l[:, offs[n]:offs[n + 1]] for n in range(len(sizes))]


def _prep_proj_weights(w_in_l, perm):
    qa, ka, va, fa, qb, kb, vb, qi, ki, wi = _split_w_in(w_in_l)
    scale = HEAD_DIM ** -0.5
    big = jnp.concatenate([_head_cols(qa, perm) * scale, _head_cols(qb, perm) * scale, ka, va, kb, vb],
                          axis=1).astype(BF16)
    small = _pad_cols(jnp.concatenate([fa, wi], axis=1), LANES)
    idx = jnp.concatenate([qi * (D_IDX ** -0.5), ki, ki, small], axis=1)
    idx_hi, idx_lo = _split2(idx)
    return dict(big=big, idx_hi=idx_hi, idx_lo=idx_lo)


def _prep_proj_t_weights(w_in_l, perm):
    qa, ka, va, fa, qb, kb, vb, qi, ki, wi = _split_w_in(w_in_l)
    d = w_in_l.shape[0]
    scale = HEAD_DIM ** -0.5
    t_big = jnp.concatenate([_head_cols(qa, perm) * scale, _head_cols(qb, perm) * scale, va, vb, ka, kb],
                            axis=1).T.astype(BF16)
    zero = jnp.zeros((d, HEAD_DIM), w_in_l.dtype)
    spread = []
    for m in range(KV_A // 2):
        spread += [ka[:, 2 * m * HEAD_DIM:(2 * m + 1) * HEAD_DIM], zero, zero,
                   ka[:, (2 * m + 1) * HEAD_DIM:(2 * m + 2) * HEAD_DIM]]
    n_keys = jnp.concatenate(spread + [kb], axis=1).astype(BF16)
    small = _pad_cols(jnp.concatenate([fa, wi], axis=1), LANES)
    n_idx_hi, n_idx_lo = _split2(jnp.concatenate([ki, ki, small], axis=1))
    t_idx_hi, t_idx_lo = _split2((qi * (D_IDX ** -0.5)).T)
    return dict(t_big=t_big, n_keys=n_keys, n_idx_hi=n_idx_hi, n_idx_lo=n_idx_lo,
                t_idx_hi=t_idx_hi, t_idx_lo=t_idx_lo)


def _prep_post_weights(w_out_l, perm_a, perm_b, ln1_g, ln1_b, w_up_l, w_down_l, ln2_g, ln2_b, ff_chunk):
    d_mix, d = w_out_l.shape
    na = H_A * HEAD_DIM
    rows = lambda w, perm: w.reshape(len(perm), HEAD_DIM, d)[np.asarray(perm)].reshape(-1, d)
    d_ff = w_up_l.shape[1]
    nc = d_ff // ff_chunk
    row = lambda a: a.reshape(1, -1)
    return dict(out_a=rows(w_out_l[:na], perm_a).astype(BF16), out_b=rows(w_out_l[na:], perm_b).astype(BF16),
                ln1_g=row(ln1_g), ln1_b=row(ln1_b), ln2_g=row(ln2_g), ln2_b=row(ln2_b),
                up=w_up_l.reshape(d, nc, ff_chunk).transpose(1, 0, 2).astype(BF16),
                down=w_down_l.reshape(nc, ff_chunk, d).astype(BF16))


ATTN_BLOCK = 256
FF_CHUNK = 512
POST_ROWS = 512
SCORE_PAGES_PER_STEP = 16
DECAY_PAGES_PER_STEP = 16
ATTN_PAGES_PER_STEP = 8


def _prompt_layer(x2d, batch, seq, wproj, wpost, bf_pad, relb, rb_far, alpha):
    tq = min(ATTN_BLOCK, seq)
    p = _project_t(x2d, wproj, bf_pad, batch=batch, seq=seq, tk=tq)
    oa_t = _fox_attention(p['qat'], p['kaug'], p['vax'], batch, seq, tq)
    wi_t = p['wi'].reshape(batch, seq, H_IDX).transpose(0, 2, 1)
    ob_t = _dsa_attention(p, wi_t, relb, rb_far, batch, seq, tq, min(TOPK_MAX, seq // 4))
    y = _post_block(x2d, oa_t, ob_t, wpost, alpha, seq=seq)
    kv = lambda a: a.reshape(batch, KV_A, HEAD_DIM, seq).transpose(0, 3, 1, 2)
    return y, (kv(p['kat']), kv(p['vat']), p['logf'].reshape(batch, seq, H_A), kv(p['kbt']), kv(p['vbt']),
               p['ki'].reshape(batch, seq, D_IDX))


def _block_diag_q(q):
    db = q.shape[0]
    qh = q.reshape(db, H_A, 1, HEAD_DIM)
    kv_of_head = jnp.arange(H_A) // GROUP
    onehot = (kv_of_head[:, None] == jnp.arange(KV_A)[None, :])[None, :, :, None]
    return jnp.where(onehot, qh, jnp.zeros((), q.dtype)).reshape(db, H_A, KV_A * HEAD_DIM)


def _diag_heads(o_wide):
    db = o_wide.shape[0]
    o = o_wide.reshape(db, H_A, KV_A, HEAD_DIM)
    idx = (jnp.arange(H_A) // GROUP)[None, :, None, None]
    return jnp.take_along_axis(o, jnp.broadcast_to(idx, (db, H_A, 1, HEAD_DIM)), axis=2).reshape(db, -1)


def _sample_layer(x2d, page_table, caches, layer, wproj, wpost, bf_pad, rbt, alpha):
    db = x2d.shape[0]
    n_pages = page_table.shape[1]
    past = n_pages * PAGE_SIZE
    p = _project(x2d, wproj, bf_pad)
    q16 = jnp.concatenate([p['qi_hi'].reshape(db, H_IDX, D_IDX), p['qi_lo'].reshape(db, H_IDX, D_IDX)], axis=1)
    scores = _sample_scores(page_table, q16, p['wi'].reshape(db, H_IDX, 1), caches['idx_k'], layer,
                            min(SCORE_PAGES_PER_STEP, n_pages))
    dec = _sample_decay(page_table, p['logf'].reshape(db, H_A, 1), caches['logf_t'], layer,
                        min(DECAY_PAGES_PER_STEP, n_pages))
    q_f32 = p['qi_hi'].astype(F32) + p['qi_lo'].astype(F32)
    mb, mbn = _sample_select(scores.reshape(db, past), q_f32, jnp.tile(p['ki'], (1, H_IDX)), p['wi'],
                             min(TOPK_MAX, (past + 1) // 4))
    new = dict(ka=p['ka'].reshape(db, 1, -1), va=p['va'].reshape(db, 1, -1), kb=p['kb'].reshape(db, 1, -1),
               vb=p['vb'].reshape(db, 1, -1), mbn=mbn.reshape(db, 1, LANES))
    pps = min(ATTN_PAGES_PER_STEP, n_pages // 2)
    n_chunks = n_pages // pps
    dec_c = dec.reshape(db, H_A, n_chunks, pps * PAGE_SIZE).transpose(0, 2, 1, 3)
    oa_w, ob_w = _sample_attention(page_table, _block_diag_q(p['qa']), _block_diag_q(p['qb']), dec_c,
                                   mb.reshape(db, n_chunks, 1, pps * PAGE_SIZE), rbt, new,
                                   (caches['a_k'], caches['a_v'], caches['b_k'], caches['b_v']), layer, pps)
    y = _post_block(x2d, _diag_heads(oa_w).astype(BF16), _diag_heads(ob_w).astype(BF16), wpost, alpha)
    return y, (p['ka'], p['va'], p['logf'], p['kb'], p['vb'], p['ki'])


def kernel(x_prompt, x_sample, cache_a_k, cache_a_v, cache_a_logf, cache_b_k, cache_b_v, cache_idx_k,
           page_table, w_in, b_f, w_out, ln1_g, ln1_b, w_up, w_down, ln2_g, ln2_b, rel_bias):
    depth = w_in.shape[0]
    batch, seq, d_model = x_prompt.shape
    db, dec_seq, _ = x_sample.shape
    assert dec_seq == 1
    alpha = (2 * depth) ** 0.25
    natural = tuple(range(H_A))
    tq = min(ATTN_BLOCK, seq)

    kv_t = lambda c: c.transpose(0, 1, 3, 4, 2).reshape(c.shape[:2] + (-1, PAGE_SIZE))
    caches = dict(a_k=kv_t(cache_a_k), a_v=kv_t(cache_a_v), b_k=kv_t(cache_b_k), b_v=kv_t(cache_b_v),
                  idx_k=cache_idx_k.transpose(0, 1, 3, 2),
                  logf_t=cache_a_logf.astype(F32).transpose(0, 1, 3, 2))
    relb = _rel_bias_tiles(rel_bias[:, list(HEAD_PERM)], tq)
    rb_far = rel_bias[N_BUCKETS - 1, list(HEAD_PERM)]
    rbt = rel_bias.T

    xp = x_prompt.reshape(batch * seq, d_model)
    xs = x_sample.reshape(db, d_model)
    p_rows, s_rows = [], []
    for l in range(depth):
        bf_pad = _pad_cols(b_f[l].reshape(1, -1), LANES)
        post = lambda perm: _prep_post_weights(w_out[l], perm, perm, ln1_g[l], ln1_b[l], w_up[l], w_down[l],
                                               ln2_g[l], ln2_b[l], FF_CHUNK)
        xp, rows = _prompt_layer(xp, batch, seq, _prep_proj_t_weights(w_in[l], HEAD_PERM), post(HEAD_PERM),
                                 bf_pad, relb, rb_far, alpha)
        p_rows.append(rows)
        xs, rows = _sample_layer(xs, page_table, caches, l, _prep_proj_weights(w_in[l], natural),
                                 post(natural), bf_pad, rbt, alpha)
        s_rows.append(rows)

    stack = lambda rows, j: jnp.stack([r[j] for r in rows])
    sample = lambda j, tail: stack(s_rows, j).reshape((depth, db, dec_seq) + tail)
    kv = (KV_A, HEAD_DIM)
    return (xp.reshape(batch, seq, d_model), xs.reshape(db, dec_seq, d_model),
            stack(p_rows, 0), stack(p_rows, 1), stack(p_rows, 2), stack(p_rows, 3), stack(p_rows, 4),
            stack(p_rows, 5),
            sample(0, kv), sample(1, kv), sample(2, (H_A,)), sample(3, kv), sample(4, kv), sample(5, (D_IDX,)))
```

```python
import functools
import math

import numpy as np
import jax
import jax.numpy as jnp
from jax import lax
from jax.experimental import pallas as pl
from jax.experimental.pallas import tpu as pltpu

HEAD_DIM = 64
H_A = 8
KV_A = 4
H_B = 8
KV_B = 4
GROUP = 2
H_IDX = 8
D_IDX = 64
TOPK_MAX = 256
N_BUCKETS = 32
MAX_DISTANCE = 128
PAGE_SIZE = 128
LN_EPS = 1e-5

LANES = 128
VMEM_LIMIT_BYTES = 56 * 1024 * 1024

NEG_BIG = -1e30
INT_MIN = -(2 ** 31)

BF16 = jnp.bfloat16
F32 = jnp.float32

HEAD_PERM = (0, 2, 1, 3, 4, 6, 5, 7)


def _cparams(semantics):
    return pltpu.CompilerParams(dimension_semantics=semantics,
                                vmem_limit_bytes=VMEM_LIMIT_BYTES)


def _dot(a, b):
    return jnp.dot(a, b, preferred_element_type=F32)


def _dot_nt(a, b):
    return lax.dot_general(a, b, (((1,), (1,)), ((), ())), preferred_element_type=F32)


def _split2(x):
    hi = x.astype(BF16)
    lo = (x - hi.astype(F32)).astype(BF16)
    return hi, lo


def _split3(x):
    h1 = x.astype(BF16)
    r1 = x - h1.astype(F32)
    h2 = r1.astype(BF16)
    h3 = (r1 - h2.astype(F32)).astype(BF16)
    return h1, h2, h3


def _dot3(xh, xl, wh, wl):
    return _dot(xh, wh) + _dot(xl, wh) + _dot(xh, wl)


def _log_sigmoid(x):
    return jnp.minimum(x, 0.0) - jnp.log1p(jnp.exp(-jnp.abs(x)))


def _layer_norm(x, g, b):
    mu = jnp.mean(x, axis=-1, keepdims=True)
    xc = x - mu
    var = jnp.mean(xc * xc, axis=-1, keepdims=True)
    return xc * lax.rsqrt(var + LN_EPS) * g + b


def _rel_bucket(dist):
    max_exact = N_BUCKETS // 2
    d = jnp.maximum(dist, 1).astype(F32)
    large = max_exact + (jnp.log(d / max_exact) / math.log(MAX_DISTANCE / max_exact)
                         * (N_BUCKETS - max_exact)).astype(jnp.int32)
    large = jnp.minimum(large, N_BUCKETS - 1)
    return jnp.where(dist < max_exact, dist, large)


def _order_key(score):
    bits = pltpu.bitcast(score, jnp.int32)
    key = bits ^ (lax.shift_right_arithmetic(bits, 31) & jnp.int32(0x7FFFFFFF))
    return jnp.where(bits == jnp.int32(INT_MIN), 0, key)


def _proj_kernel(x_ref, wbig_ref, wih_ref, wil_ref, bf_ref,
                 qa_ref, qb_ref, ka_ref, va_ref, kb_ref, vb_ref, qih_ref, qil_ref, ki_ref, kih_ref, kil_ref,
                 wi_ref, logf_ref):
    x = x_ref[...]
    xh, xl = _split2(x)
    nq = H_A * HEAD_DIM
    nk = KV_A * HEAD_DIM
    qa_ref[...] = _dot(xh, wbig_ref[:, 0:nq]).astype(BF16)
    qb_ref[...] = _dot(xh, wbig_ref[:, nq:2 * nq]).astype(BF16)
    o = 2 * nq
    ka_ref[...] = _dot(xh, wbig_ref[:, o:o + nk])
    va_ref[...] = _dot(xh, wbig_ref[:, o + nk:o + 2 * nk])
    kb_ref[...] = _dot(xh, wbig_ref[:, o + 2 * nk:o + 3 * nk])
    vb_ref[...] = _dot(xh, wbig_ref[:, o + 3 * nk:o + 4 * nk])

    ni = H_IDX * D_IDX
    qi = _dot3(xh, xl, wih_ref[:, 0:ni], wil_ref[:, 0:ni])
    qh, ql = _split2(qi)
    qih_ref[...] = qh
    qil_ref[...] = ql
    rest = _dot3(xh, xl, wih_ref[:, ni:ni + 2 * LANES], wil_ref[:, ni:ni + 2 * LANES])
    ki2 = rest[:, 0:LANES]
    ki_ref[...] = ki2[:, 0:D_IDX]
    kh, kl = _split2(ki2)
    kih_ref[...] = kh
    kil_ref[...] = kl

    small = rest[:, LANES:2 * LANES]
    wi_ref[...] = small[:, H_A:H_A + H_IDX] * (H_IDX ** -0.5)
    logf = _log_sigmoid(small + bf_ref[...])
    logf_ref[...] = logf[:, 0:H_A]


def _project(x2d, wts, bf_pad):
    m, d = x2d.shape
    tm = min(512, m)
    assert m % tm == 0
    nq, nk, ni = H_A * HEAD_DIM, KV_A * HEAD_DIM, H_IDX * D_IDX
    full = lambda a: pl.BlockSpec(a.shape, lambda i: (0,) * a.ndim)
    row = lambda n: pl.BlockSpec((tm, n), lambda i: (i, 0))
    inputs = [x2d, wts['big'], wts['idx_hi'], wts['idx_lo'], bf_pad]
    out_shapes = [((m, nq), BF16), ((m, nq), BF16), ((m, nk), F32), ((m, nk), F32), ((m, nk), F32),
                  ((m, nk), F32), ((m, ni), BF16), ((m, ni), BF16), ((m, D_IDX), F32),
                  ((m, LANES), BF16), ((m, LANES), BF16), ((m, H_IDX), F32), ((m, H_A), F32)]
    outs = pl.pallas_call(
        _proj_kernel,
        grid=(m // tm,),
        in_specs=[row(d)] + [full(a) for a in inputs[1:]],
        out_specs=[row(s[0][1]) for s in out_shapes],
        out_shape=[jax.ShapeDtypeStruct(*s) for s in out_shapes],
        compiler_params=_cparams(("arbitrary",)),
        name="proj",
    )(*inputs)
    names = ['qa', 'qb', 'ka', 'va', 'kb', 'vb', 'qi_hi', 'qi_lo', 'ki', 'ki2_hi', 'ki2_lo', 'wi', 'logf']
    return dict(zip(names, outs))


def _proj_t_kernel(x_ref, wt_ref, wn_ref, wnh_ref, wnl_ref, wth_ref, wtl_ref, bf_ref, tri_ref,
                   p1_ref, p2_ref, p3_ref,
                   qat_ref, qbt_ref, vat_ref, vbt_ref, kat_ref, kbt_ref, vax_ref, vbx_ref,
                   kaug_ref, kbn_ref, qith_ref, qitl_ref, ki_ref, kih_ref, kil_ref, wi_ref, logf_ref,
                   carry_ref, *, tiles_per_seq, tk):
    x = x_ref[...]
    xh, xl = _split2(x)
    tm = x.shape[0]
    nq = H_A * HEAD_DIM
    nk = KV_A * HEAD_DIM

    qat_ref[...] = _dot_nt(wt_ref[0:nq, :], xh).astype(BF16)
    qbt_ref[...] = _dot_nt(wt_ref[nq:2 * nq, :], xh).astype(BF16)
    o = 2 * nq
    row = lax.broadcasted_iota(jnp.int32, (2 * HEAD_DIM, tm), 0)
    lo_rows = row < HEAD_DIM
    for t_ref, x_out_ref, r0 in ((vat_ref, vax_ref, o), (vbt_ref, vbx_ref, o + nk)):
        vt = _dot_nt(wt_ref[r0:r0 + nk, :], xh)
        t_ref[...] = vt
        for m in range(KV_A // 2):
            blk = vt[m * 2 * HEAD_DIM:(m + 1) * 2 * HEAD_DIM, :]
            lo = jnp.where(lo_rows, blk, 1.0).astype(BF16)
            hi = jnp.where(lo_rows, 1.0, blk).astype(BF16)
            for t in range(tm // tk):
                cols = slice(t * tk, (t + 1) * tk)
                x_out_ref[t, m * 4 * HEAD_DIM:m * 4 * HEAD_DIM + 2 * HEAD_DIM, :] = lo[:, cols]
                x_out_ref[t, m * 4 * HEAD_DIM + 2 * HEAD_DIM:(m + 1) * 4 * HEAD_DIM, :] = hi[:, cols]
    kat_ref[...] = _dot_nt(wt_ref[o + 2 * nk:o + 3 * nk, :], xh)
    kbt_ref[...] = _dot_nt(wt_ref[o + 3 * nk:o + 4 * nk, :], xh)

    kn = _dot(xh, wn_ref[...])
    kbn_ref[...] = kn[:, 2 * nk:3 * nk].astype(BF16)

    qit = (_dot_nt(wth_ref[...], xh) + _dot_nt(wth_ref[...], xl)) + _dot_nt(wtl_ref[...], xh)
    qh, ql = _split2(qit)
    qith_ref[...] = qh
    qitl_ref[...] = ql

    rest = _dot3(xh, xl, wnh_ref[...], wnl_ref[...])
    ki2 = rest[:, 0:LANES]
    ki_ref[...] = ki2[:, 0:D_IDX]
    kh, kl = _split2(ki2)
    kih_ref[...] = kh
    kil_ref[...] = kl
    small = rest[:, LANES:2 * LANES]
    wi_ref[...] = small[:, H_A:H_A + H_IDX] * (H_IDX ** -0.5)
    logf = _log_sigmoid(small + bf_ref[...])
    logf_ref[...] = logf[:, 0:H_A]

    @pl.when(pl.program_id(0) % tiles_per_seq == 0)
    def _():
        carry_ref[...] = jnp.zeros_like(carry_ref)

    h1, h2, h3 = _split3(logf)
    tri = tri_ref[...]
    c = (_dot(tri, h1) + _dot(tri, h2)) + _dot(tri, h3) + carry_ref[...]
    carry_ref[...] = c[tm - 1:tm, :]
    c1, c2, c3 = _split3(c)
    aug = (_dot(c1, p1_ref[...]) + _dot(c2, p2_ref[...])) + _dot(c3, p3_ref[...])
    kaug_ref[...] = (kn[:, 0:2 * nk] + aug).astype(BF16)


def _decay_placement():
    mats = np.zeros((3, LANES, 2 * KV_A * HEAD_DIM), np.float32)
    for m in range(KV_A // 2):
        for v in range(2 * GROUP):
            head = HEAD_PERM[4 * m + v]
            hi_variant, g = v % 2, v // 2
            base = m * 4 * HEAD_DIM + (2 * HEAD_DIM if hi_variant else HEAD_DIM)
            for t in range(3):
                mats[t, head, base + 3 * g + t] = -1.0
    return [jnp.asarray(mats[t], BF16) for t in range(3)]


def _project_t(x2d, wts, bf_pad, *, batch, seq, tk):
    m, d = x2d.shape
    tm = min(512, seq)
    assert seq % tm == 0 and tm % tk == 0
    tps = seq // tm
    nq, nk, ni = H_A * HEAD_DIM, KV_A * HEAD_DIM, H_IDX * D_IDX
    nkv = seq // tk
    full = lambda a: pl.BlockSpec(a.shape, lambda i: (0,) * a.ndim)
    row = lambda n: pl.BlockSpec((tm, n), lambda i: (i, 0))
    feat = lambda r: pl.BlockSpec((None, r, tm), lambda i: (i // tps, 0, i % tps))
    vx = pl.BlockSpec((None, tm // tk, 2 * nk, tk), lambda i: (i // tps, i % tps, 0, 0))
    tri = jnp.tril(jnp.ones((tm, tm), F32)).astype(BF16)
    inputs = [x2d, wts['t_big'], wts['n_keys'], wts['n_idx_hi'], wts['n_idx_lo'], wts['t_idx_hi'],
              wts['t_idx_lo'], bf_pad, tri] + _decay_placement()
    fshape = lambda r, dt: ((batch, r, seq), dt)
    outs = [
        (fshape(nq, BF16), feat(nq)), (fshape(nq, BF16), feat(nq)),
        (fshape(nk, F32), feat(nk)), (fshape(nk, F32), feat(nk)),
        (fshape(nk, F32), feat(nk)), (fshape(nk, F32), feat(nk)),
        (((batch, nkv, 2 * nk, tk), BF16), vx), (((batch, nkv, 2 * nk, tk), BF16), vx),
        (((m, 2 * nk), BF16), row(2 * nk)), (((m, nk), BF16), row(nk)),
        (fshape(ni, BF16), feat(ni)), (fshape(ni, BF16), feat(ni)),
        (((m, D_IDX), F32), row(D_IDX)), (((m, LANES), BF16), row(LANES)), (((m, LANES), BF16), row(LANES)),
        (((m, H_IDX), F32), row(H_IDX)), (((m, H_A), F32), row(H_A)),
    ]
    res = pl.pallas_call(
        functools.partial(_proj_t_kernel, tiles_per_seq=tps, tk=tk),
        grid=(m // tm,),
        in_specs=[row(d)] + [full(a) for a in inputs[1:]],
        out_specs=[o[1] for o in outs],
        out_shape=[jax.ShapeDtypeStruct(*o[0]) for o in outs],
        scratch_shapes=[pltpu.VMEM((1, LANES), F32)],
        compiler_params=_cparams(("arbitrary",)),
        name="proj_prompt",
    )(*inputs)
    names = ['qat', 'qbt', 'vat', 'vbt', 'kat', 'kbt', 'vax', 'vbx', 'kaug', 'kbn', 'qit_hi', 'qit_lo',
             'ki', 'ki2_hi', 'ki2_lo', 'wi', 'logf']
    return dict(zip(names, res))


def _post_kernel(x_ref, oa_ref, ob_ref, woa_ref, wob_ref, g1_ref, b1_ref, wup_ref, wdn_ref,
                 g2_ref, b2_ref, y_ref, acc_ref, *, alpha, n_chunks, feature_major):
    x = x_ref[...]
    if feature_major:
        tn = lambda a, w: lax.dot_general(a, w, (((0,), (0,)), ((), ())), preferred_element_type=F32)
        att = tn(oa_ref[...], woa_ref[...]) + tn(ob_ref[...], wob_ref[...])
    else:
        att = _dot(oa_ref[...], woa_ref[...]) + _dot(ob_ref[...], wob_ref[...])
    x1 = _layer_norm(alpha * x + att, g1_ref[...], b1_ref[...])
    x1b = x1.astype(BF16)
    acc_ref[...] = jnp.zeros_like(acc_ref)

    def body(c, carry):
        u = jnp.maximum(_dot(x1b, wup_ref[c]), 0.0)
        acc_ref[...] += _dot((u * u).astype(BF16), wdn_ref[c])
        return carry

    lax.fori_loop(0, n_chunks, body, 0)
    y_ref[...] = _layer_norm(alpha * x1 + acc_ref[...], g2_ref[...], b2_ref[...])


def _post_block(x2d, oa, ob, wts, alpha, seq=None):
    m, d = x2d.shape
    tm = min(POST_ROWS, m if seq is None else seq)
    n_chunks = wts['up'].shape[0]
    full = lambda a: pl.BlockSpec(a.shape, lambda i: (0,) * a.ndim, pipeline_mode=pl.Buffered(1))
    row = lambda n: pl.BlockSpec((tm, n), lambda i: (i, 0))
    if seq is None:
        o_spec = lambda a: row(a.shape[1])
    else:
        tps = seq // tm
        o_spec = lambda a: pl.BlockSpec((None, a.shape[1], tm), lambda i: (i // tps, 0, i % tps))
    weights = [wts['out_a'], wts['out_b'], wts['ln1_g'], wts['ln1_b'], wts['up'], wts['down'],
               wts['ln2_g'], wts['ln2_b']]
    return pl.pallas_call(
        functools.partial(_post_kernel, alpha=alpha, n_chunks=n_chunks, feature_major=seq is not None),
        grid=(m // tm,),
        in_specs=[row(d), o_spec(oa), o_spec(ob)] + [full(w) for w in weights],
        out_specs=row(d),
        out_shape=jax.ShapeDtypeStruct((m, d), F32),
        scratch_shapes=[pltpu.VMEM((tm, d), F32)],
        compiler_params=_cparams(("arbitrary",)),
        name="post_block",
    )(x2d, oa, ob, *weights)


def _variant_rows(g, hi_half, ones_from=None):
    row = lax.broadcasted_iota(jnp.int32, g.shape, 0)
    keep = (row >= HEAD_DIM) if hi_half else (row < HEAD_DIM)
    fill = 0.0
    if ones_from is not None:
        fill = jnp.where((row >= ones_from) & (row < ones_from + 3), 1.0, 0.0)
    return jnp.where(keep, g, fill).astype(BF16)


def _softmax_chunk(s, causal_q0, m_ref, p_ref, acc_ref, cols, acols):
    if causal_q0 is not None:
        key = lax.broadcasted_iota(jnp.int32, s.shape, 0)
        qry = lax.broadcasted_iota(jnp.int32, s.shape, 1) + causal_q0
        s = jnp.where(key <= qry, s, NEG_BIG)
    m_old = m_ref[:, cols]
    m_new = jnp.maximum(m_old, jnp.max(s, axis=0, keepdims=True))
    m_ref[:, cols] = m_new
    p_ref[:, cols] = jnp.exp(s - m_new[0:1, :]).astype(BF16)
    acc_ref[:, acols] = acc_ref[:, acols] * jnp.exp(m_old - m_new)[0:1, :]


def _finish_pair(acc_lo, acc_hi, tq):
    row = lax.broadcasted_iota(jnp.int32, (2 * HEAD_DIM, tq), 0)
    lo_rows = row < HEAD_DIM
    out = []
    for g in range(GROUP):
        cols = slice(g * tq, (g + 1) * tq)
        a_lo, a_hi = acc_lo[:, cols], acc_hi[:, cols]
        o_lo = a_lo / a_lo[HEAD_DIM:HEAD_DIM + 1, :]
        o_hi = a_hi / a_hi[0:1, :]
        out.append(jnp.where(lo_rows, o_lo, o_hi))
    return jnp.concatenate(out, axis=0).astype(BF16)


def _fox_kernel(q_ref, k_ref, v_ref, o_ref, qt_ref, s_ref, p_ref, m_ref, acc_ref, *, tq):
    i = pl.program_id(2)
    q = q_ref[...].astype(F32)
    g0, g1 = q[0:2 * HEAD_DIM], q[2 * HEAD_DIM:4 * HEAD_DIM]
    qt_ref[0, :, 0:tq] = _variant_rows(g0, False, HEAD_DIM)
    qt_ref[0, :, tq:2 * tq] = _variant_rows(g1, False, HEAD_DIM + 3)
    qt_ref[1, :, 0:tq] = _variant_rows(g0, True, 0)
    qt_ref[1, :, tq:2 * tq] = _variant_rows(g1, True, 3)
    m_ref[...] = jnp.full_like(m_ref, NEG_BIG)
    acc_ref[...] = jnp.zeros_like(acc_ref)

    def process(blocks):
        for slot, (j, _) in enumerate(blocks):
            start = pl.multiple_of(j * tq, tq)
            for d in range(2):
                s_ref[slot, d] = _dot(k_ref[pl.ds(start, tq), d * LANES:(d + 1) * LANES], qt_ref[d])
        for slot, (j, causal) in enumerate(blocks):
            for d in range(2):
                for c in range(2 * tq // LANES):
                    cols = slice(c * LANES, (c + 1) * LANES)
                    q0 = (c * LANES) % tq if causal else None
                    _softmax_chunk(s_ref[slot, d, :, cols], q0, m_ref.at[d], p_ref.at[slot, d],
                                   acc_ref.at[d], cols, cols)
                acc_ref[d] += _dot(v_ref[j, d * LANES:(d + 1) * LANES, :], p_ref[slot, d])

    def pair_body(t, carry):
        process(((2 * t, False), (2 * t + 1, False)))
        return carry

    lax.fori_loop(0, i // 2, pair_body, 0)

    @pl.when(i % 2 == 1)
    def _():
        process(((i - 1, False), (i, True)))

    @pl.when(i % 2 == 0)
    def _():
        process(((i, True),))

    o_ref[...] = _finish_pair(acc_ref[0], acc_ref[1], tq)


def _fox_attention(qt, kaug, vx, batch, seq, tq):
    nq = seq // tq
    pair = 4 * HEAD_DIM
    return pl.pallas_call(
        functools.partial(_fox_kernel, tq=tq),
        grid=(batch, KV_A // 2, nq),
        in_specs=[
            pl.BlockSpec((None, pair, tq), lambda b, m, i: (b, m, i)),
            pl.BlockSpec((seq, pair), lambda b, m, i: (b, m)),
            pl.BlockSpec((None, nq, pair, tq), lambda b, m, i: (b, 0, m, 0)),
        ],
        out_specs=pl.BlockSpec((None, pair, tq), lambda b, m, i: (b, m, i)),
        out_shape=jax.ShapeDtypeStruct(qt.shape, BF16),
        scratch_shapes=[pltpu.VMEM((2, LANES, 2 * tq), BF16), pltpu.VMEM((2, 2, tq, 2 * tq), F32),
                        pltpu.VMEM((2, 2, tq, 2 * tq), BF16), pltpu.VMEM((2, 8, 2 * tq), F32),
                        pltpu.VMEM((2, LANES, 2 * tq), F32)],
        compiler_params=_cparams(("arbitrary", "arbitrary", "arbitrary")),
        name="fox_attention",
    )(qt, kaug, vx)


def _relb_kernel(rb_ref, o_ref, *, tq):
    delta = pl.program_id(0)
    h = pl.program_id(1)
    key = lax.broadcasted_iota(jnp.int32, (tq, tq), 0)
    qry = lax.broadcasted_iota(jnp.int32, (tq, tq), 1)
    bucket = _rel_bucket(jnp.maximum(delta * tq + qry - key, 0))
    acc = jnp.zeros((tq, tq), F32)
    for n in range(N_BUCKETS):
        acc = jnp.where(bucket == n, rb_ref[n, h], acc)
    o_ref[...] = acc


def _rel_bias_tiles(rb_perm, tq):
    return pl.pallas_call(
        functools.partial(_relb_kernel, tq=tq),
        grid=(2, H_B),
        in_specs=[pl.BlockSpec(memory_space=pltpu.SMEM)],
        out_specs=pl.BlockSpec((None, None, tq, tq), lambda d, h: (d, h, 0, 0)),
        out_shape=jax.ShapeDtypeStruct((2, H_B, tq, tq), F32),
        compiler_params=_cparams(("arbitrary", "arbitrary")),
        name="rel_bias_tiles",
    )(rb_perm)


def _strict_upper(n):
    r = lax.broadcasted_iota(jnp.int32, (n, n), 0)
    c = lax.broadcasted_iota(jnp.int32, (n, n), 1)
    return jnp.where(r < c, 1.0, 0.0).astype(BF16)


def _strict_lower(n):
    r = lax.broadcasted_iota(jnp.int32, (n, n), 0)
    c = lax.broadcasted_iota(jnp.int32, (n, n), 1)
    return jnp.where(c < r, 1.0, 0.0).astype(BF16)


KEY_ROWS = 64

DSA_VARIANT_ORDER = (0, 2, 1, 3)


def _dsa_kernel(qih_ref, qil_ref, kih_ref, kil_ref, wi_ref, qb_ref, k_ref, v_ref, relb_ref, rbfar_ref,
                o_ref, keys_ref, mb_ref, qi_ref, lg_ref, qt_ref, p_ref, m_ref, acc_ref, *, tq, topk):
    i = pl.program_id(1)
    nblk = i + 1
    nlc = tq // LANES

    for h in range(H_IDX):
        rows = slice((h // 2) * LANES, (h // 2 + 1) * LANES)
        cols = slice(h * tq, (h + 1) * tq)
        mh = _variant_rows(qih_ref[rows, :].astype(F32), h % 2 == 1)
        qi_ref[0:LANES, cols] = mh
        qi_ref[LANES:2 * LANES, cols] = _variant_rows(qil_ref[rows, :].astype(F32), h % 2 == 1)
        qi_ref[2 * LANES:3 * LANES, cols] = mh

    def score_block(j, diag):
        start = pl.multiple_of(j * tq, tq)
        kh = kih_ref[pl.ds(start, tq), :]
        kl = kil_ref[pl.ds(start, tq), :]
        lg_ref[...] = _dot(jnp.concatenate([kh, kh, kl], axis=1), qi_ref[...])
        for c in range(nlc):
            for r in range(tq // KEY_ROWS):
                rows = slice(r * KEY_ROWS, (r + 1) * KEY_ROWS)
                sc = jnp.zeros((KEY_ROWS, LANES), F32)
                for h in range(H_IDX):
                    lane0 = h * tq + c * LANES
                    sc = sc + wi_ref[h:h + 1, c * LANES:(c + 1) * LANES] * jnp.maximum(
                        lg_ref[rows, lane0:lane0 + LANES], 0.0)
                key = _order_key(sc)
                if diag:
                    kidx = lax.broadcasted_iota(jnp.int32, key.shape, 0) + r * KEY_ROWS
                    qidx = lax.broadcasted_iota(jnp.int32, key.shape, 1) + c * LANES
                    key = jnp.where(kidx <= qidx, key, INT_MIN)
                keys_ref[j, rows, c * LANES:(c + 1) * LANES] = key

    def score_body(j, carry):
        score_block(j, False)
        return carry

    lax.fori_loop(0, i, score_body, 0)
    score_block(i, True)

    needs_select = (i + 1) * tq > topk

    @pl.when(jnp.logical_not(needs_select))
    def _():
        def zero_body(j, carry):
            mb_ref[j] = jnp.zeros((tq, tq), F32)
            return carry
        lax.fori_loop(0, nblk, zero_body, 0)

    @pl.when(needs_select)
    def _():
        def count(pred_fn):
            def body(j, acc):
                hit = jnp.where(pred_fn(keys_ref[j]), 1.0, 0.0)
                for r in range(tq // 8):
                    acc = acc + hit[r * 8:(r + 1) * 8, :]
                return acc
            acc = lax.fori_loop(0, nblk, body, jnp.zeros((8, tq), F32))
            return jnp.sum(acc, axis=0, keepdims=True)

        def bit_body(b, t):
            cand = t + lax.shift_left(jnp.int32(1), 31 - b)
            return jnp.where(count(lambda kblk: kblk >= cand) >= topk, cand, t)

        thr = lax.fori_loop(0, 32, bit_body, jnp.full((1, tq), INT_MIN, jnp.int32))
        need = topk - count(lambda kblk: kblk > thr)
        earlier = _strict_lower(tq)

        def mask_body(j, seen):
            kblk = keys_ref[j]
            eq = kblk == thr
            eqf = jnp.where(eq, 1.0, 0.0)
            rank = _dot(earlier, eqf.astype(BF16)) + seen
            mb_ref[j] = jnp.where(kblk > thr, 0.0,
                                  jnp.where(eq, jnp.where(rank < need, 0.0, NEG_BIG), NEG_BIG))
            return seen + jnp.sum(eqf, axis=0, keepdims=True)

        lax.fori_loop(0, nblk, mask_body, jnp.zeros((1, tq), F32))

    pair = 4 * HEAD_DIM
    for m in range(KV_B // 2):
        q = qb_ref[m * pair:(m + 1) * pair, :].astype(F32)
        g = (q[0:2 * HEAD_DIM], q[2 * HEAD_DIM:4 * HEAD_DIM])
        for n, v in enumerate(DSA_VARIANT_ORDER):
            qt_ref[:, n * tq:(n + 1) * tq] = _variant_rows(g[v // 2], v % 2 == 1)
        m_ref[...] = jnp.full_like(m_ref, NEG_BIG)
        acc_ref[...] = jnp.zeros_like(acc_ref)

        far = lambda slot, qcols: rbfar_ref[slot]
        near = lambda slot, qcols: relb_ref[1, slot, :, qcols]
        diag = lambda slot, qcols: relb_ref[0, slot, :, qcols]

        def process(blocks):
            for slot, (j, _) in enumerate(blocks):
                start = pl.multiple_of(j * tq, tq)
                lg_ref[:, slot * 4 * tq:(slot + 1) * 4 * tq] = _dot(
                    k_ref[pl.ds(start, tq), m * LANES:(m + 1) * LANES], qt_ref[...])
            for slot, (j, rel) in enumerate(blocks):
                for n, v in enumerate(DSA_VARIANT_ORDER):
                    for c in range(nlc):
                        qcols = slice(c * LANES, (c + 1) * LANES)
                        cols = slice(n * tq + c * LANES, n * tq + (c + 1) * LANES)
                        lcols = slice(slot * 4 * tq + cols.start, slot * 4 * tq + cols.stop)
                        s = lg_ref[:, lcols] + (mb_ref[j, :, qcols] + rel(4 * m + v, qcols))
                        acols = slice((n % 2) * tq + c * LANES, (n % 2) * tq + (c + 1) * LANES)
                        _softmax_chunk(s, c * LANES if rel is diag else None, m_ref, p_ref.at[slot],
                                       acc_ref.at[n // 2], cols, acols)
                for d in range(2):
                    vt = v_ref[j, m * pair + d * LANES:m * pair + (d + 1) * LANES, :]
                    acc_ref[d] += _dot(vt, p_ref[slot, :, d * 2 * tq:(d + 1) * 2 * tq])

        n_far = jnp.maximum(i - 1, 0)

        def far_body(t, carry):
            process(((2 * t, far), (2 * t + 1, far)))
            return carry

        lax.fori_loop(0, n_far // 2, far_body, 0)

        @pl.when(n_far % 2 == 1)
        def _():
            process(((i - 2, far),))

        @pl.when(i >= 1)
        def _():
            process(((i - 1, near), (i, diag)))

        @pl.when(i == 0)
        def _():
            process(((i, diag),))

        o_ref[m * pair:(m + 1) * pair, :] = _finish_pair(acc_ref[0], acc_ref[1], tq)


def _dsa_attention(p, wi_t, relb, rb_far, batch, seq, tq, topk):
    nq = seq // tq
    nf = H_B * HEAD_DIM
    qcol = lambda r: pl.BlockSpec((None, r, tq), lambda b, i: (b, 0, i))
    seqblk = lambda n: pl.BlockSpec((seq, n), lambda b, i: (b, 0))
    return pl.pallas_call(
        functools.partial(_dsa_kernel, tq=tq, topk=topk),
        grid=(batch, nq),
        in_specs=[qcol(H_IDX * D_IDX), qcol(H_IDX * D_IDX), seqblk(LANES), seqblk(LANES), qcol(H_IDX),
                  qcol(nf), seqblk(KV_B * HEAD_DIM),
                  pl.BlockSpec((None, nq, nf, tq), lambda b, i: (b, 0, 0, 0)),
                  pl.BlockSpec(relb.shape, lambda b, i: (0, 0, 0, 0)),
                  pl.BlockSpec(memory_space=pltpu.SMEM)],
        out_specs=qcol(nf),
        out_shape=jax.ShapeDtypeStruct((batch, nf, seq), BF16),
        scratch_shapes=[pltpu.VMEM((nq, tq, tq), jnp.int32), pltpu.VMEM((nq, tq, tq), F32),
                        pltpu.VMEM((3 * LANES, H_IDX * tq), BF16), pltpu.VMEM((tq, H_IDX * tq), F32),
                        pltpu.VMEM((LANES, 4 * tq), BF16), pltpu.VMEM((2, tq, 4 * tq), BF16),
                        pltpu.VMEM((8, 4 * tq), F32), pltpu.VMEM((2, LANES, 2 * tq), F32)],
        compiler_params=_cparams(("arbitrary", "arbitrary")),
        name="dsa_attention",
    )(p['qit_hi'], p['qit_lo'], p['ki2_hi'], p['ki2_lo'], wi_t, p['qbt'], p['kbn'], p['vbx'], relb, rb_far)


def _sscan_kernel(pt_ref, q16_ref, wi_ref, lnew_ref, idx_hbm, logf_hbm, sc_ref, dec_ref,
                  kbuf, fbuf, sem, carry_ref, *, pps, n_chunks, attn_pps, layer):
    b = pl.program_id(0)
    n_rows = pl.num_programs(0)

    def copies(row, step, slot):
        chunk = n_chunks - 1 - step
        out = []
        for r in range(pps):
            page = pt_ref[row, chunk * pps + r]
            out.append(pltpu.make_async_copy(idx_hbm.at[layer, page], kbuf.at[slot, r], sem.at[slot]))
            out.append(pltpu.make_async_copy(logf_hbm.at[layer, page], fbuf.at[slot, r], sem.at[slot]))
        return out

    def start(row, step, slot):
        for n, c in enumerate(copies(row, step, slot)):
            c.start(priority=n % 2)

    def wait(row, step, slot):
        for c in copies(row, step, slot):
            c.wait()

    @pl.when(b == 0)
    def _():
        start(0, 0, 0)

    carry_ref[...] = jnp.broadcast_to(lnew_ref[...], carry_ref.shape)
    q16 = q16_ref[...]
    qh = q16[0:H_IDX]
    wi = wi_ref[...]
    r_i = lax.broadcasted_iota(jnp.int32, (PAGE_SIZE, PAGE_SIZE), 0)
    c_i = lax.broadcasted_iota(jnp.int32, (PAGE_SIZE, PAGE_SIZE), 1)
    later = jnp.where(r_i > c_i, 1.0, 0.0).astype(BF16)
    per = pps // attn_pps

    def compute(step, slot):
        chunk = n_chunks - 1 - step
        for r in reversed(range(pps)):
            x = fbuf[slot, r]
            h1, h2, h3 = _split3(x)
            within = (_dot(h1, later) + _dot(h2, later)) + _dot(h3, later)
            carry = carry_ref[...]
            lanes = slice((r % attn_pps) * PAGE_SIZE, (r % attn_pps + 1) * PAGE_SIZE)
            dec_ref[chunk * per + r // attn_pps, :, lanes] = within + carry[:, 0:1]
            carry_ref[...] = carry + jnp.sum(x, axis=1, keepdims=True)
        for r in range(pps):
            kh, kl = _split2(kbuf[slot, r])
            a = _dot(q16, kh)
            logits = (a[0:H_IDX] + a[H_IDX:2 * H_IDX]) + _dot(qh, kl)
            sc_ref[chunk, :, r * PAGE_SIZE:(r + 1) * PAGE_SIZE] = jnp.sum(
                wi * jnp.maximum(logits, 0.0), axis=0, keepdims=True)

    def pair_body(sp, carry):
        s0 = 2 * sp
        start(b, s0 + 1, 1)
        wait(b, s0, 0)
        compute(s0, 0)

        @pl.when(s0 + 2 < n_chunks)
        def _():
            start(b, s0 + 2, 0)

        @pl.when(jnp.logical_and(s0 + 2 == n_chunks, b + 1 < n_rows))
        def _():
            start(b + 1, 0, 0)

        wait(b, s0 + 1, 1)
        compute(s0 + 1, 1)
        return carry

    lax.fori_loop(0, n_chunks // 2, pair_body, 0)


def _sample_scan(page_table, q16, wi_col, logf_new_col, cache_idx_k, logf_t_cache, layer, pps, attn_pps):
    db, n_pages = page_table.shape
    n_chunks = n_pages // pps
    assert n_chunks % 2 == 0 and n_chunks * pps == n_pages and pps % attn_pps == 0
    n_attn = n_pages // attn_pps
    per_b = lambda shape: pl.BlockSpec((None,) + shape, lambda b, pt: (b,) + (0,) * len(shape))
    hbm = pl.BlockSpec(memory_space=pl.ANY)
    grid_spec = pltpu.PrefetchScalarGridSpec(
        num_scalar_prefetch=1,
        grid=(db,),
        in_specs=[per_b((2 * H_IDX, D_IDX)), per_b((H_IDX, 1)), per_b((H_A, 1)), hbm, hbm],
        out_specs=[per_b((n_chunks, 1, pps * PAGE_SIZE)), per_b((n_attn, H_A, attn_pps * PAGE_SIZE))],
        scratch_shapes=[pltpu.VMEM((2, pps, D_IDX, PAGE_SIZE), F32), pltpu.VMEM((2, pps, H_A, PAGE_SIZE), F32),
                        pltpu.SemaphoreType.DMA((2,)), pltpu.VMEM((H_A, LANES), F32)],
    )
    return pl.pallas_call(
        functools.partial(_sscan_kernel, pps=pps, n_chunks=n_chunks, attn_pps=attn_pps, layer=layer),
        grid_spec=grid_spec,
        out_shape=[jax.ShapeDtypeStruct((db, n_chunks, 1, pps * PAGE_SIZE), F32),
                   jax.ShapeDtypeStruct((db, n_attn, H_A, attn_pps * PAGE_SIZE), F32)],
        compiler_params=_cparams(("arbitrary",)),
        name="sample_scan",
    )(page_table, q16, wi_col, logf_new_col, cache_idx_k, logf_t_cache)


def _sselect_kernel(sc_ref, q_ref, kt_ref, wi_ref, mb_ref, mbn_ref, keys_ref, *, topk, chunk):
    db, past = sc_ref.shape
    prod = q_ref[...] * kt_ref[...]
    lane = lax.broadcasted_iota(jnp.int32, prod.shape, 1)
    wi = wi_ref[...]
    sc_new = jnp.zeros((db, 1), F32)
    for h in range(H_IDX):
        seg = (lane >= h * D_IDX) & (lane < (h + 1) * D_IDX)
        logit = jnp.sum(jnp.where(seg, prod, 0.0), axis=1, keepdims=True)
        sc_new = sc_new + wi[:, h:h + 1] * jnp.maximum(logit, 0.0)
    key_new = _order_key(sc_new)
    keys_ref[...] = _order_key(sc_ref[...])

    def count(pred_fn):
        n = jnp.sum(jnp.where(pred_fn(keys_ref[...]), 1.0, 0.0), axis=1, keepdims=True)
        return n + jnp.where(pred_fn(key_new), 1.0, 0.0)

    def bit_body(b, t):
        cand = t + lax.shift_left(jnp.int32(1), 31 - b)
        return jnp.where(count(lambda k: k >= cand) >= topk, cand, t)

    thr = lax.fori_loop(0, 32, bit_body, jnp.full((db, 1), INT_MIN, jnp.int32))
    need = topk - count(lambda k: k > thr)
    sut = _strict_upper(chunk)
    seen = jnp.zeros((db, 1), F32)
    for c in range(past // chunk):
        cols = slice(c * chunk, (c + 1) * chunk)
        kblk = keys_ref[:, cols]
        eq = kblk == thr
        eqf = jnp.where(eq, 1.0, 0.0)
        rank = _dot(eqf.astype(BF16), sut) + seen
        mb_ref[:, cols] = jnp.where(kblk > thr, 0.0,
                                    jnp.where(eq, jnp.where(rank < need, 0.0, NEG_BIG), NEG_BIG))
        seen = seen + jnp.sum(eqf, axis=1, keepdims=True)
    sel_new = jnp.where(key_new > thr, 0.0,
                        jnp.where(key_new == thr, jnp.where(seen < need, 0.0, NEG_BIG), NEG_BIG))
    mbn_ref[...] = jnp.broadcast_to(sel_new, mbn_ref.shape)


def _sample_select(scores, q_f32, ki_tiled, wi, topk):
    db, past = scores.shape
    return pl.pallas_call(
        functools.partial(_sselect_kernel, topk=topk, chunk=2 * LANES),
        out_shape=[jax.ShapeDtypeStruct((db, past), F32), jax.ShapeDtypeStruct((db, LANES), F32)],
        scratch_shapes=[pltpu.VMEM((db, past), jnp.int32)],
        compiler_params=pltpu.CompilerParams(vmem_limit_bytes=VMEM_LIMIT_BYTES),
        name="sample_select",
    )(scores, q_f32, ki_tiled, wi)


def _sattn_kernel(pt_ref, qa_ref, qb_ref, dec_ref, mb_ref, rbt_ref, kan_ref, van_ref, kbn_ref, vbn_ref,
                  mbn_ref, ak_hbm, av_hbm, bk_hbm, bv_hbm, oa_ref, ob_ref,
                  buf, sem, ma_ref, la_ref, acca_ref, mb_m_ref, lb_ref, accb_ref, *, pps, past, layer):
    b = pl.program_id(0)
    n_rows = pl.num_programs(0)
    width = pps * PAGE_SIZE
    n_chunks = past // width
    caches = (ak_hbm, av_hbm, bk_hbm, bv_hbm)

    def page_copy(row, chunk, slot, t, r):
        page = pt_ref[row, chunk * pps + r]
        return pltpu.make_async_copy(caches[t].at[layer, page], buf.at[slot, t, r], sem.at[slot])

    def start_chunk(row, chunk, slot):
        for t in range(len(caches)):
            for r in range(pps):
                page_copy(row, chunk, slot, t, r).start(priority=r % 2)

    def wait_chunk(row, chunk, slot):
        for t in range(len(caches)):
            for r in range(pps):
                page_copy(row, chunk, slot, t, r).wait()

    @pl.when(b == 0)
    def _():
        start_chunk(0, 0, 0)

    for m_ref, l_ref, acc_ref in ((ma_ref, la_ref, acca_ref), (mb_m_ref, lb_ref, accb_ref)):
        m_ref[...] = jnp.full_like(m_ref, NEG_BIG)
        l_ref[...] = jnp.zeros_like(l_ref)
        acc_ref[...] = jnp.zeros_like(acc_ref)

    qa, qb = qa_ref[...], qb_ref[...]
    rbt = rbt_ref[...]

    def update(s, slot, t, m_ref, l_ref, acc_ref):
        m_old = m_ref[...]
        m_new = jnp.maximum(m_old, jnp.max(s, axis=1, keepdims=True))
        a = jnp.exp(m_old - m_new)
        p = jnp.exp(s - m_new)
        l_ref[...] = a * l_ref[...] + jnp.sum(p, axis=1, keepdims=True)
        pv = jnp.zeros(acc_ref.shape, F32)
        for r in range(pps):
            pv = pv + _dot_nt(p[:, r * PAGE_SIZE:(r + 1) * PAGE_SIZE].astype(BF16),
                              buf[slot, t, r].astype(BF16))
        acc_ref[...] = a * acc_ref[...] + pv
        m_ref[...] = m_new

    def compute(chunk, slot):
        s_a = jnp.concatenate([_dot(qa, buf[slot, 0, r].astype(BF16)) for r in range(pps)], axis=1)
        update(s_a + dec_ref[chunk], slot, 1, ma_ref, la_ref, acca_ref)
        pos = chunk * width + lax.broadcasted_iota(jnp.int32, (1, width), 1)
        bucket = _rel_bucket(past - pos)
        rel = jnp.zeros((H_B, width), F32)
        for n in range(N_BUCKETS):
            rel = jnp.where(bucket == n, rbt[:, n:n + 1], rel)
        s_b = jnp.concatenate([_dot(qb, buf[slot, 2, r].astype(BF16)) for r in range(pps)], axis=1)
        update(s_b + rel + mb_ref[chunk], slot, 3, mb_m_ref, lb_ref, accb_ref)

    def pair_body(cp, carry):
        c0 = 2 * cp
        start_chunk(b, c0 + 1, 1)
        wait_chunk(b, c0, 0)
        compute(c0, 0)

        @pl.when(c0 + 2 < n_chunks)
        def _():
            start_chunk(b, c0 + 2, 0)

        @pl.when(jnp.logical_and(c0 + 2 == n_chunks, b + 1 < n_rows))
        def _():
            start_chunk(b + 1, 0, 0)

        wait_chunk(b, c0 + 1, 1)
        compute(c0 + 1, 1)
        return carry

    lax.fori_loop(0, n_chunks // 2, pair_body, 0)

    def finish(q, kn, vn, extra, m_ref, l_ref, acc_ref, o_ref):
        s = jnp.sum(q.astype(F32) * kn, axis=1, keepdims=True) + extra
        m_old = m_ref[...]
        m_new = jnp.maximum(m_old, s)
        a = jnp.exp(m_old - m_new)
        p = jnp.exp(s - m_new)
        l = a * l_ref[...] + p
        o_ref[...] = (a * acc_ref[...] + p * vn) / l

    finish(qa, kan_ref[...], van_ref[...], 0.0, ma_ref, la_ref, acca_ref, oa_ref)
    finish(qb, kbn_ref[...], vbn_ref[...], rbt[:, 0:1] + mbn_ref[:, 0:1], mb_m_ref, lb_ref, accb_ref, ob_ref)


def _sample_attention(page_table, qa_blk, qb_blk, dec, mb, rbt, new, caches, layer, pps):
    db, n_pages = page_table.shape
    past = n_pages * PAGE_SIZE
    width = KV_A * HEAD_DIM
    n_chunks = n_pages // pps
    assert n_chunks % 2 == 0 and n_chunks * pps == n_pages
    per_b = lambda shape: pl.BlockSpec((None,) + shape, lambda b, pt: (b,) + (0,) * len(shape))
    hbm = pl.BlockSpec(memory_space=pl.ANY)
    in_specs = [per_b((H_A, width)), per_b((H_B, width)),
                per_b((n_chunks, H_A, pps * PAGE_SIZE)), per_b((n_chunks, 1, pps * PAGE_SIZE)),
                pl.BlockSpec(rbt.shape, lambda b, pt: (0, 0)),
                per_b((1, width)), per_b((1, width)), per_b((1, width)), per_b((1, width)),
                per_b((1, LANES)), hbm, hbm, hbm, hbm]
    grid_spec = pltpu.PrefetchScalarGridSpec(
        num_scalar_prefetch=1,
        grid=(db,),
        in_specs=in_specs,
        out_specs=[per_b((H_A, width)), per_b((H_B, width))],
        scratch_shapes=[pltpu.VMEM((2, len(caches), pps, width, PAGE_SIZE), F32), pltpu.SemaphoreType.DMA((2,)),
                        pltpu.VMEM((H_A, 1), F32), pltpu.VMEM((H_A, 1), F32), pltpu.VMEM((H_A, width), F32),
                        pltpu.VMEM((H_B, 1), F32), pltpu.VMEM((H_B, 1), F32), pltpu.VMEM((H_B, width), F32)],
    )
    return pl.pallas_call(
        functools.partial(_sattn_kernel, pps=pps, past=past, layer=layer),
        grid_spec=grid_spec,
        out_shape=[jax.ShapeDtypeStruct((db, H_A, width), F32), jax.ShapeDtypeStruct((db, H_B, width), F32)],
        compiler_params=_cparams(("arbitrary",)),
        name="sample_attention",
    )(page_table, qa_blk, qb_blk, dec, mb, rbt, new['ka'], new['va'], new['kb'], new['vb'], new['mbn'], *caches)


def _head_cols(w, perm):
    d = w.shape[0]
    return w.reshape(d, len(perm), HEAD_DIM)[:, list(perm), :].reshape(d, -1)


def _pad_cols(w, n):
    return jnp.pad(w, ((0, 0), (0, n - w.shape[1])))


def _split_w_in(w_in_l):
    sizes = (H_A * HEAD_DIM, KV_A * HEAD_DIM, KV_A * HEAD_DIM, H_A, H_B * HEAD_DIM, KV_B * HEAD_DIM,
             KV_B * HEAD_DIM, H_IDX * D_IDX, D_IDX, H_IDX)
    offs = np.cumsum((0,) + sizes)
    return [w_in_l[:, offs[n]:offs[n + 1]] for n in range(len(sizes))]


def _prep_proj_weights(w_in_l, perm):
    qa, ka, va, fa, qb, kb, vb, qi, ki, wi = _split_w_in(w_in_l)
    scale = HEAD_DIM ** -0.5
    big = jnp.concatenate([_head_cols(qa, perm) * scale, _head_cols(qb, perm) * scale, ka, va, kb, vb],
                          axis=1).astype(BF16)
    small = _pad_cols(jnp.concatenate([fa, wi], axis=1), LANES)
    idx = jnp.concatenate([qi * (D_IDX ** -0.5), ki, ki, small], axis=1)
    idx_hi, idx_lo = _split2(idx)
    return dict(big=big, idx_hi=idx_hi, idx_lo=idx_lo)


def _prep_proj_t_weights(w_in_l, perm):
    qa, ka, va, fa, qb, kb, vb, qi, ki, wi = _split_w_in(w_in_l)
    d = w_in_l.shape[0]
    scale = HEAD_DIM ** -0.5
    t_big = jnp.concatenate([_head_cols(qa, perm) * scale, _head_cols(qb, perm) * scale, va, vb, ka, kb],
                            axis=1).T.astype(BF16)
    zero = jnp.zeros((d, HEAD_DIM), w_in_l.dtype)
    spread = []
    for m in range(KV_A // 2):
        spread += [ka[:, 2 * m * HEAD_DIM:(2 * m + 1) * HEAD_DIM], zero, zero,
                   ka[:, (2 * m + 1) * HEAD_DIM:(2 * m + 2) * HEAD_DIM]]
    n_keys = jnp.concatenate(spread + [kb], axis=1).astype(BF16)
    small = _pad_cols(jnp.concatenate([fa, wi], axis=1), LANES)
    n_idx_hi, n_idx_lo = _split2(jnp.concatenate([ki, ki, small], axis=1))
    t_idx_hi, t_idx_lo = _split2((qi * (D_IDX ** -0.5)).T)
    return dict(t_big=t_big, n_keys=n_keys, n_idx_hi=n_idx_hi, n_idx_lo=n_idx_lo,
                t_idx_hi=t_idx_hi, t_idx_lo=t_idx_lo)


def _prep_post_weights(w_out_l, perm_a, perm_b, ln1_g, ln1_b, w_up_l, w_down_l, ln2_g, ln2_b, ff_chunk):
    d_mix, d = w_out_l.shape
    na = H_A * HEAD_DIM
    rows = lambda w, perm: w.reshape(len(perm), HEAD_DIM, d)[np.asarray(perm)].reshape(-1, d)
    d_ff = w_up_l.shape[1]
    nc = d_ff // ff_chunk
    row = lambda a: a.reshape(1, -1)
    return dict(out_a=rows(w_out_l[:na], perm_a).astype(BF16), out_b=rows(w_out_l[na:], perm_b).astype(BF16),
                ln1_g=row(ln1_g), ln1_b=row(ln1_b), ln2_g=row(ln2_g), ln2_b=row(ln2_b),
                up=w_up_l.reshape(d, nc, ff_chunk).transpose(1, 0, 2).astype(BF16),
                down=w_down_l.reshape(nc, ff_chunk, d).astype(BF16))


ATTN_BLOCK = 256
FF_CHUNK = 512
POST_ROWS = 512
SCAN_PAGES_PER_STEP = 16
ATTN_PAGES_PER_STEP = 8


def _prompt_layer(x2d, batch, seq, wproj, wpost, bf_pad, relb, rb_far, alpha):
    tq = min(ATTN_BLOCK, seq)
    p = _project_t(x2d, wproj, bf_pad, batch=batch, seq=seq, tk=tq)
    oa_t = _fox_attention(p['qat'], p['kaug'], p['vax'], batch, seq, tq)
    wi_t = p['wi'].reshape(batch, seq, H_IDX).transpose(0, 2, 1)
    ob_t = _dsa_attention(p, wi_t, relb, rb_far, batch, seq, tq, min(TOPK_MAX, seq // 4))
    y = _post_block(x2d, oa_t, ob_t, wpost, alpha, seq=seq)
    kv = lambda a: a.reshape(batch, KV_A, HEAD_DIM, seq).transpose(0, 3, 1, 2)
    return y, (kv(p['kat']), kv(p['vat']), p['logf'].reshape(batch, seq, H_A), kv(p['kbt']), kv(p['vbt']),
               p['ki'].reshape(batch, seq, D_IDX))


def _block_diag_q(q):
    db = q.shape[0]
    qh = q.reshape(db, H_A, 1, HEAD_DIM)
    kv_of_head = jnp.arange(H_A) // GROUP
    onehot = (kv_of_head[:, None] == jnp.arange(KV_A)[None, :])[None, :, :, None]
    return jnp.where(onehot, qh, jnp.zeros((), q.dtype)).reshape(db, H_A, KV_A * HEAD_DIM)


def _diag_heads(o_wide):
    db = o_wide.shape[0]
    o = o_wide.reshape(db, H_A, KV_A, HEAD_DIM)
    idx = (jnp.arange(H_A) // GROUP)[None, :, None, None]
    return jnp.take_along_axis(o, jnp.broadcast_to(idx, (db, H_A, 1, HEAD_DIM)), axis=2).reshape(db, -1)


def _sample_layer(x2d, page_table, caches, layer, wproj, wpost, bf_pad, rbt, alpha):
    db = x2d.shape[0]
    n_pages = page_table.shape[1]
    past = n_pages * PAGE_SIZE
    p = _project(x2d, wproj, bf_pad)
    q16 = jnp.concatenate([p['qi_hi'].reshape(db, H_IDX, D_IDX), p['qi_lo'].reshape(db, H_IDX, D_IDX)], axis=1)
    pps = min(ATTN_PAGES_PER_STEP, n_pages // 2)
    scan_pps = min(SCAN_PAGES_PER_STEP, n_pages // 2)
    scores, dec_c = _sample_scan(page_table, q16, p['wi'].reshape(db, H_IDX, 1), p['logf'].reshape(db, H_A, 1),
                                 caches['idx_k'], caches['logf_t'], layer, scan_pps, pps)
    q_f32 = p['qi_hi'].astype(F32) + p['qi_lo'].astype(F32)
    mb, mbn = _sample_select(scores.reshape(db, past), q_f32, jnp.tile(p['ki'], (1, H_IDX)), p['wi'],
                             min(TOPK_MAX, (past + 1) // 4))
    new = dict(ka=p['ka'].reshape(db, 1, -1), va=p['va'].reshape(db, 1, -1), kb=p['kb'].reshape(db, 1, -1),
               vb=p['vb'].reshape(db, 1, -1), mbn=mbn.reshape(db, 1, LANES))
    n_chunks = n_pages // pps
    oa_w, ob_w = _sample_attention(page_table, _block_diag_q(p['qa']), _block_diag_q(p['qb']), dec_c,
                                   mb.reshape(db, n_chunks, 1, pps * PAGE_SIZE), rbt, new,
                                   (caches['a_k'], caches['a_v'], caches['b_k'], caches['b_v']), layer, pps)
    y = _post_block(x2d, _diag_heads(oa_w).astype(BF16), _diag_heads(ob_w).astype(BF16), wpost, alpha)
    return y, (p['ka'], p['va'], p['logf'], p['kb'], p['vb'], p['ki'])


def kernel(x_prompt, x_sample, cache_a_k, cache_a_v, cache_a_logf, cache_b_k, cache_b_v, cache_idx_k,
           page_table, w_in, b_f, w_out, ln1_g, ln1_b, w_up, w_down, ln2_g, ln2_b, rel_bias):
    depth = w_in.shape[0]
    batch, seq, d_model = x_prompt.shape
    db, dec_seq, _ = x_sample.shape
    assert dec_seq == 1
    alpha = (2 * depth) ** 0.25
    natural = tuple(range(H_A))
    tq = min(ATTN_BLOCK, seq)

    kv_t = lambda c: c.transpose(0, 1, 3, 4, 2).reshape(c.shape[:2] + (-1, PAGE_SIZE))
    caches = dict(a_k=kv_t(cache_a_k), a_v=kv_t(cache_a_v), b_k=kv_t(cache_b_k), b_v=kv_t(cache_b_v),
                  idx_k=cache_idx_k.transpose(0, 1, 3, 2),
                  logf_t=cache_a_logf.astype(F32).transpose(0, 1, 3, 2))
    relb = _rel_bias_tiles(rel_bias[:, list(HEAD_PERM)], tq)
    rb_far = rel_bias[N_BUCKETS - 1, list(HEAD_PERM)]
    rbt = rel_bias.T

    xp = x_prompt.reshape(batch * seq, d_model)
    xs = x_sample.reshape(db, d_model)
    p_rows, s_rows = [], []
    for l in range(depth):
        bf_pad = _pad_cols(b_f[l].reshape(1, -1), LANES)
        post = lambda perm: _prep_post_weights(w_out[l], perm, perm, ln1_g[l], ln1_b[l], w_up[l], w_down[l],
                                               ln2_g[l], ln2_b[l], FF_CHUNK)
        xp, rows = _prompt_layer(xp, batch, seq, _prep_proj_t_weights(w_in[l], HEAD_PERM), post(HEAD_PERM),
                                 bf_pad, relb, rb_far, alpha)
        p_rows.append(rows)
        xs, rows = _sample_layer(xs, page_table, caches, l, _prep_proj_weights(w_in[l], natural),
                                 post(natural), bf_pad, rbt, alpha)
        s_rows.append(rows)

    stack = lambda rows, j: jnp.stack([r[j] for r in rows])
    sample = lambda j, tail: stack(s_rows, j).reshape((depth, db, dec_seq) + tail)
    kv = (KV_A, HEAD_DIM)
    return (xp.reshape(batch, seq, d_model), xs.reshape(db, dec_seq, d_model),
            stack(p_rows, 0), stack(p_rows, 1), stack(p_rows, 2), stack(p_rows, 3), stack(p_rows, 4),
            stack(p_rows, 5),
            sample(0, kv), sample(1, kv), sample(2, (H_A,)), sample(3, kv), sample(4, kv), sample(5, (D_IDX,)))
```

```python
import functools
import math

import numpy as np
import jax
import jax.numpy as jnp
from jax import lax
from jax.experimental import pallas as pl
from jax.experimental.pallas import tpu as pltpu

HEAD_DIM = 64
H_A = 8
KV_A = 4
H_B = 8
KV_B = 4
GROUP = 2
H_IDX = 8
D_IDX = 64
TOPK_MAX = 256
N_BUCKETS = 32
MAX_DISTANCE = 128
PAGE_SIZE = 128
LN_EPS = 1e-5

LANES = 128
VMEM_LIMIT_BYTES = 56 * 1024 * 1024

NEG_BIG = -1e30
INT_MIN = -(2 ** 31)
HALF_BITS = 16
I16_MIN = -(2 ** 15)

BF16 = jnp.bfloat16
F32 = jnp.float32

HEAD_PERM = (0, 2, 1, 3, 4, 6, 5, 7)


def _cparams(semantics):
    return pltpu.CompilerParams(dimension_semantics=semantics,
                                vmem_limit_bytes=VMEM_LIMIT_BYTES)


def _dot(a, b):
    return jnp.dot(a, b, preferred_element_type=F32)


def _dot_nt(a, b):
    return lax.dot_general(a, b, (((1,), (1,)), ((), ())), preferred_element_type=F32)


def _split2(x):
    hi = x.astype(BF16)
    lo = (x - hi.astype(F32)).astype(BF16)
    return hi, lo


def _split3(x):
    h1 = x.astype(BF16)
    r1 = x - h1.astype(F32)
    h2 = r1.astype(BF16)
    h3 = (r1 - h2.astype(F32)).astype(BF16)
    return h1, h2, h3


def _dot3(xh, xl, wh, wl):
    return _dot(xh, wh) + _dot(xl, wh) + _dot(xh, wl)


def _log_sigmoid(x):
    return jnp.minimum(x, 0.0) - jnp.log1p(jnp.exp(-jnp.abs(x)))


def _layer_norm(x, g, b):
    mu = jnp.mean(x, axis=-1, keepdims=True)
    xc = x - mu
    var = jnp.mean(xc * xc, axis=-1, keepdims=True)
    return xc * lax.rsqrt(var + LN_EPS) * g + b


def _rel_bucket(dist):
    max_exact = N_BUCKETS // 2
    d = jnp.maximum(dist, 1).astype(F32)
    large = max_exact + (jnp.log(d / max_exact) / math.log(MAX_DISTANCE / max_exact)
                         * (N_BUCKETS - max_exact)).astype(jnp.int32)
    large = jnp.minimum(large, N_BUCKETS - 1)
    return jnp.where(dist < max_exact, dist, large)


def _order_key(score):
    bits = pltpu.bitcast(score, jnp.int32)
    key = bits ^ (lax.shift_right_arithmetic(bits, 31) & jnp.int32(0x7FFFFFFF))
    return jnp.where(bits == jnp.int32(INT_MIN), 0, key)


def _proj_kernel(x_ref, wbig_ref, wih_ref, wil_ref, bf_ref,
                 qa_ref, qb_ref, ka_ref, va_ref, kb_ref, vb_ref, qih_ref, qil_ref, ki_ref, kih_ref, kil_ref,
                 wi_ref, logf_ref):
    x = x_ref[...]
    xh, xl = _split2(x)
    nq = H_A * HEAD_DIM
    nk = KV_A * HEAD_DIM
    qa_ref[...] = _dot(xh, wbig_ref[:, 0:nq]).astype(BF16)
    qb_ref[...] = _dot(xh, wbig_ref[:, nq:2 * nq]).astype(BF16)
    o = 2 * nq
    ka_ref[...] = _dot(xh, wbig_ref[:, o:o + nk])
    va_ref[...] = _dot(xh, wbig_ref[:, o + nk:o + 2 * nk])
    kb_ref[...] = _dot(xh, wbig_ref[:, o + 2 * nk:o + 3 * nk])
    vb_ref[...] = _dot(xh, wbig_ref[:, o + 3 * nk:o + 4 * nk])

    ni = H_IDX * D_IDX
    qi = _dot3(xh, xl, wih_ref[:, 0:ni], wil_ref[:, 0:ni])
    qh, ql = _split2(qi)
    qih_ref[...] = qh
    qil_ref[...] = ql
    rest = _dot3(xh, xl, wih_ref[:, ni:ni + 2 * LANES], wil_ref[:, ni:ni + 2 * LANES])
    ki2 = rest[:, 0:LANES]
    ki_ref[...] = ki2[:, 0:D_IDX]
    kh, kl = _split2(ki2)
    kih_ref[...] = kh
    kil_ref[...] = kl

    small = rest[:, LANES:2 * LANES]
    wi_ref[...] = small[:, H_A:H_A + H_IDX] * (H_IDX ** -0.5)
    logf = _log_sigmoid(small + bf_ref[...])
    logf_ref[...] = logf[:, 0:H_A]


def _project(x2d, wts, bf_pad):
    m, d = x2d.shape
    tm = min(512, m)
    assert m % tm == 0
    nq, nk, ni = H_A * HEAD_DIM, KV_A * HEAD_DIM, H_IDX * D_IDX
    full = lambda a: pl.BlockSpec(a.shape, lambda i: (0,) * a.ndim)
    row = lambda n: pl.BlockSpec((tm, n), lambda i: (i, 0))
    inputs = [x2d, wts['big'], wts['idx_hi'], wts['idx_lo'], bf_pad]
    out_shapes = [((m, nq), BF16), ((m, nq), BF16), ((m, nk), F32), ((m, nk), F32), ((m, nk), F32),
                  ((m, nk), F32), ((m, ni), BF16), ((m, ni), BF16), ((m, D_IDX), F32),
                  ((m, LANES), BF16), ((m, LANES), BF16), ((m, H_IDX), F32), ((m, H_A), F32)]
    outs = pl.pallas_call(
        _proj_kernel,
        grid=(m // tm,),
        in_specs=[row(d)] + [full(a) for a in inputs[1:]],
        out_specs=[row(s[0][1]) for s in out_shapes],
        out_shape=[jax.ShapeDtypeStruct(*s) for s in out_shapes],
        compiler_params=_cparams(("arbitrary",)),
        name="proj",
    )(*inputs)
    names = ['qa', 'qb', 'ka', 'va', 'kb', 'vb', 'qi_hi', 'qi_lo', 'ki', 'ki2_hi', 'ki2_lo', 'wi', 'logf']
    return dict(zip(names, outs))


def _proj_t_kernel(x_ref, wt_ref, wn_ref, wnh_ref, wnl_ref, wth_ref, wtl_ref, bf_ref, tri_ref,
                   p1_ref, p2_ref, p3_ref,
                   qat_ref, qbt_ref, vat_ref, vbt_ref, kat_ref, kbt_ref, vax_ref, vbx_ref,
                   kaug_ref, kbn_ref, qith_ref, qitl_ref, ki_ref, kih_ref, kil_ref, wi_ref, logf_ref,
                   carry_ref, *, tiles_per_seq, tk):
    x = x_ref[...]
    xh, xl = _split2(x)
    tm = x.shape[0]
    nq = H_A * HEAD_DIM
    nk = KV_A * HEAD_DIM

    qat_ref[...] = _dot_nt(wt_ref[0:nq, :], xh).astype(BF16)
    qbt_ref[...] = _dot_nt(wt_ref[nq:2 * nq, :], xh).astype(BF16)
    o = 2 * nq
    row = lax.broadcasted_iota(jnp.int32, (2 * HEAD_DIM, tm), 0)
    lo_rows = row < HEAD_DIM
    for t_ref, x_out_ref, r0 in ((vat_ref, vax_ref, o), (vbt_ref, vbx_ref, o + nk)):
        vt = _dot_nt(wt_ref[r0:r0 + nk, :], xh)
        t_ref[...] = vt
        for m in range(KV_A // 2):
            blk = vt[m * 2 * HEAD_DIM:(m + 1) * 2 * HEAD_DIM, :]
            lo = jnp.where(lo_rows, blk, 1.0).astype(BF16)
            hi = jnp.where(lo_rows, 1.0, blk).astype(BF16)
            for t in range(tm // tk):
                cols = slice(t * tk, (t + 1) * tk)
                x_out_ref[t, m * 4 * HEAD_DIM:m * 4 * HEAD_DIM + 2 * HEAD_DIM, :] = lo[:, cols]
                x_out_ref[t, m * 4 * HEAD_DIM + 2 * HEAD_DIM:(m + 1) * 4 * HEAD_DIM, :] = hi[:, cols]
    kat_ref[...] = _dot_nt(wt_ref[o + 2 * nk:o + 3 * nk, :], xh)
    kbt_ref[...] = _dot_nt(wt_ref[o + 3 * nk:o + 4 * nk, :], xh)

    kn = _dot(xh, wn_ref[...])
    kbn_ref[...] = kn[:, 2 * nk:3 * nk].astype(BF16)

    qit = (_dot_nt(wth_ref[...], xh) + _dot_nt(wth_ref[...], xl)) + _dot_nt(wtl_ref[...], xh)
    qh, ql = _split2(qit)
    qith_ref[...] = qh
    qitl_ref[...] = ql

    rest = _dot3(xh, xl, wnh_ref[...], wnl_ref[...])
    ki2 = rest[:, 0:LANES]
    ki_ref[...] = ki2[:, 0:D_IDX]
    kh, kl = _split2(ki2)
    kih_ref[...] = kh
    kil_ref[...] = kl
    small = rest[:, LANES:2 * LANES]
    wi_ref[...] = small[:, H_A:H_A + H_IDX] * (H_IDX ** -0.5)
    logf = _log_sigmoid(small + bf_ref[...])
    logf_ref[...] = logf[:, 0:H_A]

    @pl.when(pl.program_id(0) % tiles_per_seq == 0)
    def _():
        carry_ref[...] = jnp.zeros_like(carry_ref)

    h1, h2, h3 = _split3(logf)
    tri = tri_ref[...]
    c = (_dot(tri, h1) + _dot(tri, h2)) + _dot(tri, h3) + carry_ref[...]
    carry_ref[...] = c[tm - 1:tm, :]
    c1, c2, c3 = _split3(c)
    aug = (_dot(c1, p1_ref[...]) + _dot(c2, p2_ref[...])) + _dot(c3, p3_ref[...])
    kaug_ref[...] = (kn[:, 0:2 * nk] + aug).astype(BF16)


def _decay_placement():
    mats = np.zeros((3, LANES, 2 * KV_A * HEAD_DIM), np.float32)
    for m in range(KV_A // 2):
        for v in range(2 * GROUP):
            head = HEAD_PERM[4 * m + v]
            hi_variant, g = v % 2, v // 2
            base = m * 4 * HEAD_DIM + (2 * HEAD_DIM if hi_variant else HEAD_DIM)
            for t in range(3):
                mats[t, head, base + 3 * g + t] = -1.0
    return [jnp.asarray(mats[t], BF16) for t in range(3)]


def _project_t(x2d, wts, bf_pad, *, batch, seq, tk):
    m, d = x2d.shape
    tm = min(512, seq)
    assert seq % tm == 0 and tm % tk == 0
    tps = seq // tm
    nq, nk, ni = H_A * HEAD_DIM, KV_A * HEAD_DIM, H_IDX * D_IDX
    nkv = seq // tk
    full = lambda a: pl.BlockSpec(a.shape, lambda i: (0,) * a.ndim)
    row = lambda n: pl.BlockSpec((tm, n), lambda i: (i, 0))
    feat = lambda r: pl.BlockSpec((None, r, tm), lambda i: (i // tps, 0, i % tps))
    vx = pl.BlockSpec((None, tm // tk, 2 * nk, tk), lambda i: (i // tps, i % tps, 0, 0))
    tri = jnp.tril(jnp.ones((tm, tm), F32)).astype(BF16)
    inputs = [x2d, wts['t_big'], wts['n_keys'], wts['n_idx_hi'], wts['n_idx_lo'], wts['t_idx_hi'],
              wts['t_idx_lo'], bf_pad, tri] + _decay_placement()
    fshape = lambda r, dt: ((batch, r, seq), dt)
    outs = [
        (fshape(nq, BF16), feat(nq)), (fshape(nq, BF16), feat(nq)),
        (fshape(nk, F32), feat(nk)), (fshape(nk, F32), feat(nk)),
        (fshape(nk, F32), feat(nk)), (fshape(nk, F32), feat(nk)),
        (((batch, nkv, 2 * nk, tk), BF16), vx), (((batch, nkv, 2 * nk, tk), BF16), vx),
        (((m, 2 * nk), BF16), row(2 * nk)), (((m, nk), BF16), row(nk)),
        (fshape(ni, BF16), feat(ni)), (fshape(ni, BF16), feat(ni)),
        (((m, D_IDX), F32), row(D_IDX)), (((m, LANES), BF16), row(LANES)), (((m, LANES), BF16), row(LANES)),
        (((m, H_IDX), F32), row(H_IDX)), (((m, H_A), F32), row(H_A)),
    ]
    res = pl.pallas_call(
        functools.partial(_proj_t_kernel, tiles_per_seq=tps, tk=tk),
        grid=(m // tm,),
        in_specs=[row(d)] + [full(a) for a in inputs[1:]],
        out_specs=[o[1] for o in outs],
        out_shape=[jax.ShapeDtypeStruct(*o[0]) for o in outs],
        scratch_shapes=[pltpu.VMEM((1, LANES), F32)],
        compiler_params=_cparams(("arbitrary",)),
        name="proj_prompt",
    )(*inputs)
    names = ['qat', 'qbt', 'vat', 'vbt', 'kat', 'kbt', 'vax', 'vbx', 'kaug', 'kbn', 'qit_hi', 'qit_lo',
             'ki', 'ki2_hi', 'ki2_lo', 'wi', 'logf']
    return dict(zip(names, res))


def _post_kernel(x_ref, oa_ref, ob_ref, woa_ref, wob_ref, g1_ref, b1_ref, wup_ref, wdn_ref,
                 g2_ref, b2_ref, y_ref, acc_ref, *, alpha, n_chunks, feature_major):
    x = x_ref[...]
    if feature_major:
        tn = lambda a, w: lax.dot_general(a, w, (((0,), (0,)), ((), ())), preferred_element_type=F32)
        att = tn(oa_ref[...], woa_ref[...]) + tn(ob_ref[...], wob_ref[...])
    else:
        att = _dot(oa_ref[...], woa_ref[...]) + _dot(ob_ref[...], wob_ref[...])
    x1 = _layer_norm(alpha * x + att, g1_ref[...], b1_ref[...])
    x1b = x1.astype(BF16)
    acc_ref[...] = jnp.zeros_like(acc_ref)

    def body(c, carry):
        u = jnp.maximum(_dot(x1b, wup_ref[c]), 0.0)
        acc_ref[...] += _dot((u * u).astype(BF16), wdn_ref[c])
        return carry

    lax.fori_loop(0, n_chunks, body, 0)
    y_ref[...] = _layer_norm(alpha * x1 + acc_ref[...], g2_ref[...], b2_ref[...])


def _post_block(x2d, oa, ob, wts, alpha, seq=None):
    m, d = x2d.shape
    tm = min(POST_ROWS, m if seq is None else seq)
    n_chunks = wts['up'].shape[0]
    full = lambda a: pl.BlockSpec(a.shape, lambda i: (0,) * a.ndim, pipeline_mode=pl.Buffered(1))
    row = lambda n: pl.BlockSpec((tm, n), lambda i: (i, 0))
    if seq is None:
        o_spec = lambda a: row(a.shape[1])
    else:
        tps = seq // tm
        o_spec = lambda a: pl.BlockSpec((None, a.shape[1], tm), lambda i: (i // tps, 0, i % tps))
    weights = [wts['out_a'], wts['out_b'], wts['ln1_g'], wts['ln1_b'], wts['up'], wts['down'],
               wts['ln2_g'], wts['ln2_b']]
    return pl.pallas_call(
        functools.partial(_post_kernel, alpha=alpha, n_chunks=n_chunks, feature_major=seq is not None),
        grid=(m // tm,),
        in_specs=[row(d), o_spec(oa), o_spec(ob)] + [full(w) for w in weights],
        out_specs=row(d),
        out_shape=jax.ShapeDtypeStruct((m, d), F32),
        scratch_shapes=[pltpu.VMEM((tm, d), F32)],
        compiler_params=_cparams(("arbitrary",)),
        name="post_block",
    )(x2d, oa, ob, *weights)


def _variant_rows(g, hi_half, ones_from=None):
    row = lax.broadcasted_iota(jnp.int32, g.shape, 0)
    keep = (row >= HEAD_DIM) if hi_half else (row < HEAD_DIM)
    fill = 0.0
    if ones_from is not None:
        fill = jnp.where((row >= ones_from) & (row < ones_from + 3), 1.0, 0.0)
    return jnp.where(keep, g, fill).astype(BF16)


def _softmax_chunk(s, causal_q0, m_ref, p_ref, acc_ref, cols, acols):
    if causal_q0 is not None:
        key = lax.broadcasted_iota(jnp.int32, s.shape, 0)
        qry = lax.broadcasted_iota(jnp.int32, s.shape, 1) + causal_q0
        s = jnp.where(key <= qry, s, NEG_BIG)
    m_old = m_ref[:, cols]
    m_new = jnp.maximum(m_old, jnp.max(s, axis=0, keepdims=True))
    m_ref[:, cols] = m_new
    p_ref[:, cols] = jnp.exp(s - m_new[0:1, :]).astype(BF16)
    acc_ref[:, acols] = acc_ref[:, acols] * jnp.exp(m_old - m_new)[0:1, :]


def _finish_pair(acc_lo, acc_hi, tq):
    row = lax.broadcasted_iota(jnp.int32, (2 * HEAD_DIM, tq), 0)
    lo_rows = row < HEAD_DIM
    out = []
    for g in range(GROUP):
        cols = slice(g * tq, (g + 1) * tq)
        a_lo, a_hi = acc_lo[:, cols], acc_hi[:, cols]
        o_lo = a_lo / a_lo[HEAD_DIM:HEAD_DIM + 1, :]
        o_hi = a_hi / a_hi[0:1, :]
        out.append(jnp.where(lo_rows, o_lo, o_hi))
    return jnp.concatenate(out, axis=0).astype(BF16)


def _fox_kernel(q_ref, k_ref, v_ref, o_ref, qt_ref, s_ref, p_ref, m_ref, acc_ref, *, tq):
    i = pl.program_id(2)
    q = q_ref[...].astype(F32)
    g0, g1 = q[0:2 * HEAD_DIM], q[2 * HEAD_DIM:4 * HEAD_DIM]
    qt_ref[0, :, 0:tq] = _variant_rows(g0, False, HEAD_DIM)
    qt_ref[0, :, tq:2 * tq] = _variant_rows(g1, False, HEAD_DIM + 3)
    qt_ref[1, :, 0:tq] = _variant_rows(g0, True, 0)
    qt_ref[1, :, tq:2 * tq] = _variant_rows(g1, True, 3)
    m_ref[...] = jnp.full_like(m_ref, NEG_BIG)
    acc_ref[...] = jnp.zeros_like(acc_ref)

    def process(blocks):
        for slot, (j, _) in enumerate(blocks):
            start = pl.multiple_of(j * tq, tq)
            for d in range(2):
                s_ref[slot, d] = _dot(k_ref[pl.ds(start, tq), d * LANES:(d + 1) * LANES], qt_ref[d])
        for slot, (j, causal) in enumerate(blocks):
            for d in range(2):
                for c in range(2 * tq // LANES):
                    cols = slice(c * LANES, (c + 1) * LANES)
                    q0 = (c * LANES) % tq if causal else None
                    _softmax_chunk(s_ref[slot, d, :, cols], q0, m_ref.at[d], p_ref.at[slot, d],
                                   acc_ref.at[d], cols, cols)
                acc_ref[d] += _dot(v_ref[j, d * LANES:(d + 1) * LANES, :], p_ref[slot, d])

    def pair_body(t, carry):
        process(((2 * t, False), (2 * t + 1, False)))
        return carry

    lax.fori_loop(0, i // 2, pair_body, 0)

    @pl.when(i % 2 == 1)
    def _():
        process(((i - 1, False), (i, True)))

    @pl.when(i % 2 == 0)
    def _():
        process(((i, True),))

    o_ref[...] = _finish_pair(acc_ref[0], acc_ref[1], tq)


def _fox_attention(qt, kaug, vx, batch, seq, tq):
    nq = seq // tq
    pair = 4 * HEAD_DIM
    return pl.pallas_call(
        functools.partial(_fox_kernel, tq=tq),
        grid=(batch, KV_A // 2, nq),
        in_specs=[
            pl.BlockSpec((None, pair, tq), lambda b, m, i: (b, m, i)),
            pl.BlockSpec((seq, pair), lambda b, m, i: (b, m)),
            pl.BlockSpec((None, nq, pair, tq), lambda b, m, i: (b, 0, m, 0)),
        ],
        out_specs=pl.BlockSpec((None, pair, tq), lambda b, m, i: (b, m, i)),
        out_shape=jax.ShapeDtypeStruct(qt.shape, BF16),
        scratch_shapes=[pltpu.VMEM((2, LANES, 2 * tq), BF16), pltpu.VMEM((2, 2, tq, 2 * tq), F32),
                        pltpu.VMEM((2, 2, tq, 2 * tq), BF16), pltpu.VMEM((2, 8, 2 * tq), F32),
                        pltpu.VMEM((2, LANES, 2 * tq), F32)],
        compiler_params=_cparams(("arbitrary", "arbitrary", "arbitrary")),
        name="fox_attention",
    )(qt, kaug, vx)


def _relb_kernel(rb_ref, o_ref, *, tq):
    delta = pl.program_id(0)
    h = pl.program_id(1)
    key = lax.broadcasted_iota(jnp.int32, (tq, tq), 0)
    qry = lax.broadcasted_iota(jnp.int32, (tq, tq), 1)
    bucket = _rel_bucket(jnp.maximum(delta * tq + qry - key, 0))
    acc = jnp.zeros((tq, tq), F32)
    for n in range(N_BUCKETS):
        acc = jnp.where(bucket == n, rb_ref[n, h], acc)
    o_ref[...] = acc


def _rel_bias_tiles(rb_perm, tq):
    return pl.pallas_call(
        functools.partial(_relb_kernel, tq=tq),
        grid=(2, H_B),
        in_specs=[pl.BlockSpec(memory_space=pltpu.SMEM)],
        out_specs=pl.BlockSpec((None, None, tq, tq), lambda d, h: (d, h, 0, 0)),
        out_shape=jax.ShapeDtypeStruct((2, H_B, tq, tq), F32),
        compiler_params=_cparams(("arbitrary", "arbitrary")),
        name="rel_bias_tiles",
    )(rb_perm)


def _strict_upper(n):
    r = lax.broadcasted_iota(jnp.int32, (n, n), 0)
    c = lax.broadcasted_iota(jnp.int32, (n, n), 1)
    return jnp.where(r < c, 1.0, 0.0).astype(BF16)


def _strict_lower(n):
    r = lax.broadcasted_iota(jnp.int32, (n, n), 0)
    c = lax.broadcasted_iota(jnp.int32, (n, n), 1)
    return jnp.where(c < r, 1.0, 0.0).astype(BF16)


KEY_ROWS = 64

DSA_VARIANT_ORDER = (0, 2, 1, 3)


def _dsa_kernel(qih_ref, qil_ref, kih_ref, wi_ref, qb_ref, k_ref, v_ref, relb_ref, rbfar_ref,
                o_ref, keys_ref, khi_ref, klo_ref, mb_ref, qi_ref, lg_ref, qt_ref, p_ref, m_ref, acc_ref,
                *, tq, topk):
    i = pl.program_id(1)
    nblk = i + 1
    nlc = tq // LANES

    for h in range(H_IDX):
        rows = slice((h // 2) * LANES, (h // 2 + 1) * LANES)
        cols = slice(h * tq, (h + 1) * tq)
        qi_ref[0:LANES, cols] = _variant_rows(qih_ref[rows, :].astype(F32), h % 2 == 1)
        qi_ref[LANES:2 * LANES, cols] = _variant_rows(qil_ref[rows, :].astype(F32), h % 2 == 1)

    def score_block(j, diag):
        start = pl.multiple_of(j * tq, tq)
        kh = kih_ref[pl.ds(start, tq), :]
        lg_ref[...] = _dot(jnp.concatenate([kh, kh], axis=1), qi_ref[...])
        for c in range(nlc):
            for r in range(tq // KEY_ROWS):
                rows = slice(r * KEY_ROWS, (r + 1) * KEY_ROWS)
                sc = jnp.zeros((KEY_ROWS, LANES), F32)
                for h in range(H_IDX):
                    lane0 = h * tq + c * LANES
                    sc = sc + wi_ref[h:h + 1, c * LANES:(c + 1) * LANES] * jnp.maximum(
                        lg_ref[rows, lane0:lane0 + LANES], 0.0)
                key = _order_key(sc)
                if diag:
                    kidx = lax.broadcasted_iota(jnp.int32, key.shape, 0) + r * KEY_ROWS
                    qidx = lax.broadcasted_iota(jnp.int32, key.shape, 1) + c * LANES
                    key = jnp.where(kidx <= qidx, key, INT_MIN)
                keys_ref[j, rows, c * LANES:(c + 1) * LANES] = key
                khi_ref[j, rows, c * LANES:(c + 1) * LANES] = lax.shift_right_arithmetic(
                    key, HALF_BITS).astype(jnp.int16)

    def score_body(j, carry):
        score_block(j, False)
        return carry

    lax.fori_loop(0, i, score_body, 0)
    score_block(i, True)

    needs_select = (i + 1) * tq > topk

    @pl.when(jnp.logical_not(needs_select))
    def _():
        def zero_body(j, carry):
            mb_ref[j] = jnp.zeros((tq, tq), F32)
            return carry
        lax.fori_loop(0, nblk, zero_body, 0)

    @pl.when(needs_select)
    def _():
        def count16(ref, pred_fn):
            def body(j, acc):
                hit = jnp.where(pred_fn(ref[j]), jnp.int16(1), jnp.int16(0))
                for r in range(tq // 16):
                    acc = acc + hit[r * 16:(r + 1) * 16, :]
                return acc
            acc = lax.fori_loop(0, nblk, body, jnp.zeros((16, tq), jnp.int16))
            return jnp.sum(acc.astype(F32), axis=0, keepdims=True)

        def search16(ref, want):
            def bit_body(b, t):
                cand = t + lax.shift_left(jnp.int32(1), HALF_BITS - 1 - b)
                cand16 = cand.astype(jnp.int16)
                return jnp.where(count16(ref, lambda k: k >= cand16) >= want, cand, t)
            return lax.fori_loop(0, HALF_BITS, bit_body, jnp.full((1, tq), I16_MIN, jnp.int32))

        thr_hi = search16(khi_ref, topk)
        thr_hi16 = thr_hi.astype(jnp.int16)
        above = count16(khi_ref, lambda k: k > thr_hi16)

        def low_body(j, carry):
            kblk = keys_ref[j]
            in_band = lax.shift_right_arithmetic(kblk, HALF_BITS) == thr_hi
            low = (kblk & jnp.int32(2 ** HALF_BITS - 1)) + I16_MIN
            klo_ref[j] = jnp.where(in_band, low, I16_MIN).astype(jnp.int16)
            return carry

        lax.fori_loop(0, nblk, low_body, 0)
        thr_lo = search16(klo_ref, topk - above)
        thr_lo16 = thr_lo.astype(jnp.int16)
        thr = thr_hi * (2 ** HALF_BITS) + (thr_lo - I16_MIN)
        need = topk - (above + count16(klo_ref, lambda k: k > thr_lo16))
        earlier = _strict_lower(tq)

        def mask_body(j, seen):
            kblk = keys_ref[j]
            eq = kblk == thr
            eqf = jnp.where(eq, 1.0, 0.0)
            rank = _dot(earlier, eqf.astype(BF16)) + seen
            mb_ref[j] = jnp.where(kblk > thr, 0.0,
                                  jnp.where(eq, jnp.where(rank < need, 0.0, NEG_BIG), NEG_BIG))
            return seen + jnp.sum(eqf, axis=0, keepdims=True)

        lax.fori_loop(0, nblk, mask_body, jnp.zeros((1, tq), F32))

    pair = 4 * HEAD_DIM
    for m in range(KV_B // 2):
        q = qb_ref[m * pair:(m + 1) * pair, :].astype(F32)
        g = (q[0:2 * HEAD_DIM], q[2 * HEAD_DIM:4 * HEAD_DIM])
        for n, v in enumerate(DSA_VARIANT_ORDER):
            qt_ref[:, n * tq:(n + 1) * tq] = _variant_rows(g[v // 2], v % 2 == 1)
        m_ref[...] = jnp.full_like(m_ref, NEG_BIG)
        acc_ref[...] = jnp.zeros_like(acc_ref)

        far = lambda slot, qcols: rbfar_ref[slot]
        near = lambda slot, qcols: relb_ref[1, slot, :, qcols]
        diag = lambda slot, qcols: relb_ref[0, slot, :, qcols]

        def process(blocks):
            for slot, (j, _) in enumerate(blocks):
                start = pl.multiple_of(j * tq, tq)
                lg_ref[:, slot * 4 * tq:(slot + 1) * 4 * tq] = _dot(
                    k_ref[pl.ds(start, tq), m * LANES:(m + 1) * LANES], qt_ref[...])
            for slot, (j, rel) in enumerate(blocks):
                for n, v in enumerate(DSA_VARIANT_ORDER):
                    for c in range(nlc):
                        qcols = slice(c * LANES, (c + 1) * LANES)
                        cols = slice(n * tq + c * LANES, n * tq + (c + 1) * LANES)
                        lcols = slice(slot * 4 * tq + cols.start, slot * 4 * tq + cols.stop)
                        s = lg_ref[:, lcols] + (mb_ref[j, :, qcols] + rel(4 * m + v, qcols))
                        acols = slice((n % 2) * tq + c * LANES, (n % 2) * tq + (c + 1) * LANES)
                        _softmax_chunk(s, c * LANES if rel is diag else None, m_ref, p_ref.at[slot],
                                       acc_ref.at[n // 2], cols, acols)
                for d in range(2):
                    vt = v_ref[j, m * pair + d * LANES:m * pair + (d + 1) * LANES, :]
                    acc_ref[d] += _dot(vt, p_ref[slot, :, d * 2 * tq:(d + 1) * 2 * tq])

        n_far = jnp.maximum(i - 1, 0)

        def far_body(t, carry):
            process(((2 * t, far), (2 * t + 1, far)))
            return carry

        lax.fori_loop(0, n_far // 2, far_body, 0)

        @pl.when(n_far % 2 == 1)
        def _():
            process(((i - 2, far),))

        @pl.when(i >= 1)
        def _():
            process(((i - 1, near), (i, diag)))

        @pl.when(i == 0)
        def _():
            process(((i, diag),))

        o_ref[m * pair:(m + 1) * pair, :] = _finish_pair(acc_ref[0], acc_ref[1], tq)


def _dsa_attention(p, wi_t, relb, rb_far, batch, seq, tq, topk):
    nq = seq // tq
    nf = H_B * HEAD_DIM
    qcol = lambda r: pl.BlockSpec((None, r, tq), lambda b, i: (b, 0, i))
    seqblk = lambda n: pl.BlockSpec((seq, n), lambda b, i: (b, 0))
    return pl.pallas_call(
        functools.partial(_dsa_kernel, tq=tq, topk=topk),
        grid=(batch, nq),
        in_specs=[qcol(H_IDX * D_IDX), qcol(H_IDX * D_IDX), seqblk(LANES), qcol(H_IDX),
                  qcol(nf), seqblk(KV_B * HEAD_DIM),
                  pl.BlockSpec((None, nq, nf, tq), lambda b, i: (b, 0, 0, 0)),
                  pl.BlockSpec(relb.shape, lambda b, i: (0, 0, 0, 0)),
                  pl.BlockSpec(memory_space=pltpu.SMEM)],
        out_specs=qcol(nf),
        out_shape=jax.ShapeDtypeStruct((batch, nf, seq), BF16),
        scratch_shapes=[pltpu.VMEM((nq, tq, tq), jnp.int32), pltpu.VMEM((nq, tq, tq), jnp.int16),
                        pltpu.VMEM((nq, tq, tq), jnp.int16), pltpu.VMEM((nq, tq, tq), F32),
                        pltpu.VMEM((2 * LANES, H_IDX * tq), BF16), pltpu.VMEM((tq, H_IDX * tq), F32),
                        pltpu.VMEM((LANES, 4 * tq), BF16), pltpu.VMEM((2, tq, 4 * tq), BF16),
                        pltpu.VMEM((8, 4 * tq), F32), pltpu.VMEM((2, LANES, 2 * tq), F32)],
        compiler_params=_cparams(("arbitrary", "arbitrary")),
        name="dsa_attention",
    )(p['qit_hi'], p['qit_lo'], p['ki2_hi'], wi_t, p['qbt'], p['kbn'], p['vbx'], relb, rb_far)


def _sscan_kernel(pt_ref, q16_ref, wi_ref, lnew_ref, idx_hbm, logf_hbm, sc_ref, dec_ref,
                  kbuf, fbuf, sem, carry_ref, *, pps, n_chunks, attn_pps, layer):
    b = pl.program_id(0)
    n_rows = pl.num_programs(0)

    def copies(row, step, slot):
        chunk = n_chunks - 1 - step
        out = []
        for r in range(pps):
            page = pt_ref[row, chunk * pps + r]
            out.append(pltpu.make_async_copy(idx_hbm.at[layer, page], kbuf.at[slot, r], sem.at[slot]))
            out.append(pltpu.make_async_copy(logf_hbm.at[layer, page], fbuf.at[slot, r], sem.at[slot]))
        return out

    def start(row, step, slot):
        for n, c in enumerate(copies(row, step, slot)):
            c.start(priority=n % 2)

    def wait(row, step, slot):
        for c in copies(row, step, slot):
            c.wait()

    @pl.when(b == 0)
    def _():
        start(0, 0, 0)

    carry_ref[...] = jnp.broadcast_to(lnew_ref[...], carry_ref.shape)
    q16 = q16_ref[...]
    qh = q16[0:H_IDX]
    wi = wi_ref[...]
    r_i = lax.broadcasted_iota(jnp.int32, (PAGE_SIZE, PAGE_SIZE), 0)
    c_i = lax.broadcasted_iota(jnp.int32, (PAGE_SIZE, PAGE_SIZE), 1)
    later = jnp.where(r_i > c_i, 1.0, 0.0).astype(BF16)
    per = pps // attn_pps

    def compute(step, slot):
        chunk = n_chunks - 1 - step
        for r in reversed(range(pps)):
            x = fbuf[slot, r]
            h1, h2, h3 = _split3(x)
            within = (_dot(h1, later) + _dot(h2, later)) + _dot(h3, later)
            carry = carry_ref[...]
            lanes = slice((r % attn_pps) * PAGE_SIZE, (r % attn_pps + 1) * PAGE_SIZE)
            dec_ref[chunk * per + r // attn_pps, :, lanes] = within + carry[:, 0:1]
            carry_ref[...] = carry + jnp.sum(x, axis=1, keepdims=True)
        for r in range(pps):
            kh, kl = _split2(kbuf[slot, r])
            a = _dot(q16, kh)
            logits = (a[0:H_IDX] + a[H_IDX:2 * H_IDX]) + _dot(qh, kl)
            sc_ref[chunk, :, r * PAGE_SIZE:(r + 1) * PAGE_SIZE] = jnp.sum(
                wi * jnp.maximum(logits, 0.0), axis=0, keepdims=True)

    def pair_body(sp, carry):
        s0 = 2 * sp
        start(b, s0 + 1, 1)
        wait(b, s0, 0)
        compute(s0, 0)

        @pl.when(s0 + 2 < n_chunks)
        def _():
            start(b, s0 + 2, 0)

        @pl.when(jnp.logical_and(s0 + 2 == n_chunks, b + 1 < n_rows))
        def _():
            start(b + 1, 0, 0)

        wait(b, s0 + 1, 1)
        compute(s0 + 1, 1)
        return carry

    lax.fori_loop(0, n_chunks // 2, pair_body, 0)


def _sample_scan(page_table, q16, wi_col, logf_new_col, cache_idx_k, logf_t_cache, layer, pps, attn_pps):
    db, n_pages = page_table.shape
    n_chunks = n_pages // pps
    assert n_chunks % 2 == 0 and n_chunks * pps == n_pages and pps % attn_pps == 0
    n_attn = n_pages // attn_pps
    per_b = lambda shape: pl.BlockSpec((None,) + shape, lambda b, pt: (b,) + (0,) * len(shape))
    hbm = pl.BlockSpec(memory_space=pl.ANY)
    grid_spec = pltpu.PrefetchScalarGridSpec(
        num_scalar_prefetch=1,
        grid=(db,),
        in_specs=[per_b((2 * H_IDX, D_IDX)), per_b((H_IDX, 1)), per_b((H_A, 1)), hbm, hbm],
        out_specs=[per_b((n_chunks, 1, pps * PAGE_SIZE)), per_b((n_attn, H_A, attn_pps * PAGE_SIZE))],
        scratch_shapes=[pltpu.VMEM((2, pps, D_IDX, PAGE_SIZE), F32), pltpu.VMEM((2, pps, H_A, PAGE_SIZE), F32),
                        pltpu.SemaphoreType.DMA((2,)), pltpu.VMEM((H_A, LANES), F32)],
    )
    return pl.pallas_call(
        functools.partial(_sscan_kernel, pps=pps, n_chunks=n_chunks, attn_pps=attn_pps, layer=layer),
        grid_spec=grid_spec,
        out_shape=[jax.ShapeDtypeStruct((db, n_chunks, 1, pps * PAGE_SIZE), F32),
                   jax.ShapeDtypeStruct((db, n_attn, H_A, attn_pps * PAGE_SIZE), F32)],
        compiler_params=_cparams(("arbitrary",)),
        name="sample_scan",
    )(page_table, q16, wi_col, logf_new_col, cache_idx_k, logf_t_cache)


def _sselect_kernel(sc_ref, q_ref, kt_ref, wi_ref, mb_ref, mbn_ref, keys_ref, *, topk, chunk):
    db, past = sc_ref.shape
    prod = q_ref[...] * kt_ref[...]
    lane = lax.broadcasted_iota(jnp.int32, prod.shape, 1)
    wi = wi_ref[...]
    sc_new = jnp.zeros((db, 1), F32)
    for h in range(H_IDX):
        seg = (lane >= h * D_IDX) & (lane < (h + 1) * D_IDX)
        logit = jnp.sum(jnp.where(seg, prod, 0.0), axis=1, keepdims=True)
        sc_new = sc_new + wi[:, h:h + 1] * jnp.maximum(logit, 0.0)
    key_new = _order_key(sc_new)
    keys_ref[...] = _order_key(sc_ref[...])

    def count(pred_fn):
        n = jnp.sum(jnp.where(pred_fn(keys_ref[...]), 1.0, 0.0), axis=1, keepdims=True)
        return n + jnp.where(pred_fn(key_new), 1.0, 0.0)

    def bit_body(b, t):
        cand = t + lax.shift_left(jnp.int32(1), 31 - b)
        return jnp.where(count(lambda k: k >= cand) >= topk, cand, t)

    thr = lax.fori_loop(0, 32, bit_body, jnp.full((db, 1), INT_MIN, jnp.int32))
    need = topk - count(lambda k: k > thr)
    sut = _strict_upper(chunk)
    seen = jnp.zeros((db, 1), F32)
    for c in range(past // chunk):
        cols = slice(c * chunk, (c + 1) * chunk)
        kblk = keys_ref[:, cols]
        eq = kblk == thr
        eqf = jnp.where(eq, 1.0, 0.0)
        rank = _dot(eqf.astype(BF16), sut) + seen
        mb_ref[:, cols] = jnp.where(kblk > thr, 0.0,
                                    jnp.where(eq, jnp.where(rank < need, 0.0, NEG_BIG), NEG_BIG))
        seen = seen + jnp.sum(eqf, axis=1, keepdims=True)
    sel_new = jnp.where(key_new > thr, 0.0,
                        jnp.where(key_new == thr, jnp.where(seen < need, 0.0, NEG_BIG), NEG_BIG))
    mbn_ref[...] = jnp.broadcast_to(sel_new, mbn_ref.shape)


def _sample_select(scores, q_f32, ki_tiled, wi, topk):
    db, past = scores.shape
    return pl.pallas_call(
        functools.partial(_sselect_kernel, topk=topk, chunk=2 * LANES),
        out_shape=[jax.ShapeDtypeStruct((db, past), F32), jax.ShapeDtypeStruct((db, LANES), F32)],
        scratch_shapes=[pltpu.VMEM((db, past), jnp.int32)],
        compiler_params=pltpu.CompilerParams(vmem_limit_bytes=VMEM_LIMIT_BYTES),
        name="sample_select",
    )(scores, q_f32, ki_tiled, wi)


def _sattn_kernel(pt_ref, qa_ref, qb_ref, dec_ref, mb_ref, rbt_ref, kan_ref, van_ref, kbn_ref, vbn_ref,
                  mbn_ref, ak_hbm, av_hbm, bk_hbm, bv_hbm, oa_ref, ob_ref,
                  buf, sem, ma_ref, la_ref, acca_ref, mb_m_ref, lb_ref, accb_ref, *, pps, past, layer):
    b = pl.program_id(0)
    n_rows = pl.num_programs(0)
    width = pps * PAGE_SIZE
    n_chunks = past // width
    caches = (ak_hbm, av_hbm, bk_hbm, bv_hbm)

    def page_copy(row, chunk, slot, t, r):
        page = pt_ref[row, chunk * pps + r]
        return pltpu.make_async_copy(caches[t].at[layer, page], buf.at[slot, t, r], sem.at[slot])

    def start_chunk(row, chunk, slot):
        for t in range(len(caches)):
            for r in range(pps):
                page_copy(row, chunk, slot, t, r).start(priority=r % 2)

    def wait_chunk(row, chunk, slot):
        for t in range(len(caches)):
            for r in range(pps):
                page_copy(row, chunk, slot, t, r).wait()

    @pl.when(b == 0)
    def _():
        start_chunk(0, 0, 0)

    for m_ref, l_ref, acc_ref in ((ma_ref, la_ref, acca_ref), (mb_m_ref, lb_ref, accb_ref)):
        m_ref[...] = jnp.full_like(m_ref, NEG_BIG)
        l_ref[...] = jnp.zeros_like(l_ref)
        acc_ref[...] = jnp.zeros_like(acc_ref)

    qa, qb = qa_ref[...], qb_ref[...]
    rbt = rbt_ref[...]

    def update(s, slot, t, m_ref, l_ref, acc_ref):
        m_old = m_ref[...]
        m_new = jnp.maximum(m_old, jnp.max(s, axis=1, keepdims=True))
        a = jnp.exp(m_old - m_new)
        p = jnp.exp(s - m_new)
        l_ref[...] = a * l_ref[...] + jnp.sum(p, axis=1, keepdims=True)
        pv = jnp.zeros(acc_ref.shape, F32)
        for r in range(pps):
            pv = pv + _dot_nt(p[:, r * PAGE_SIZE:(r + 1) * PAGE_SIZE].astype(BF16),
                              buf[slot, t, r].astype(BF16))
        acc_ref[...] = a * acc_ref[...] + pv
        m_ref[...] = m_new

    def compute(chunk, slot):
        s_a = jnp.concatenate([_dot(qa, buf[slot, 0, r].astype(BF16)) for r in range(pps)], axis=1)
        update(s_a + dec_ref[chunk], slot, 1, ma_ref, la_ref, acca_ref)
        pos = chunk * width + lax.broadcasted_iota(jnp.int32, (1, width), 1)
        bucket = _rel_bucket(past - pos)
        rel = jnp.zeros((H_B, width), F32)
        for n in range(N_BUCKETS):
            rel = jnp.where(bucket == n, rbt[:, n:n + 1], rel)
        s_b = jnp.concatenate([_dot(qb, buf[slot, 2, r].astype(BF16)) for r in range(pps)], axis=1)
        update(s_b + rel + mb_ref[chunk], slot, 3, mb_m_ref, lb_ref, accb_ref)

    def pair_body(cp, carry):
        c0 = 2 * cp
        start_chunk(b, c0 + 1, 1)
        wait_chunk(b, c0, 0)
        compute(c0, 0)

        @pl.when(c0 + 2 < n_chunks)
        def _():
            start_chunk(b, c0 + 2, 0)

        @pl.when(jnp.logical_and(c0 + 2 == n_chunks, b + 1 < n_rows))
        def _():
            start_chunk(b + 1, 0, 0)

        wait_chunk(b, c0 + 1, 1)
        compute(c0 + 1, 1)
        return carry

    lax.fori_loop(0, n_chunks // 2, pair_body, 0)

    def finish(q, kn, vn, extra, m_ref, l_ref, acc_ref, o_ref):
        s = jnp.sum(q.astype(F32) * kn, axis=1, keepdims=True) + extra
        m_old = m_ref[...]
        m_new = jnp.maximum(m_old, s)
        a = jnp.exp(m_old - m_new)
        p = jnp.exp(s - m_new)
        l = a * l_ref[...] + p
        o_ref[...] = (a * acc_ref[...] + p * vn) / l

    finish(qa, kan_ref[...], van_ref[...], 0.0, ma_ref, la_ref, acca_ref, oa_ref)
    finish(qb, kbn_ref[...], vbn_ref[...], rbt[:, 0:1] + mbn_ref[:, 0:1], mb_m_ref, lb_ref, accb_ref, ob_ref)


def _sample_attention(page_table, qa_blk, qb_blk, dec, mb, rbt, new, caches, layer, pps):
    db, n_pages = page_table.shape
    past = n_pages * PAGE_SIZE
    width = KV_A * HEAD_DIM
    n_chunks = n_pages // pps
    assert n_chunks % 2 == 0 and n_chunks * pps == n_pages
    per_b = lambda shape: pl.BlockSpec((None,) + shape, lambda b, pt: (b,) + (0,) * len(shape))
    hbm = pl.BlockSpec(memory_space=pl.ANY)
    in_specs = [per_b((H_A, width)), per_b((H_B, width)),
                per_b((n_chunks, H_A, pps * PAGE_SIZE)), per_b((n_chunks, 1, pps * PAGE_SIZE)),
                pl.BlockSpec(rbt.shape, lambda b, pt: (0, 0)),
                per_b((1, width)), per_b((1, width)), per_b((1, width)), per_b((1, width)),
                per_b((1, LANES)), hbm, hbm, hbm, hbm]
    grid_spec = pltpu.PrefetchScalarGridSpec(
        num_scalar_prefetch=1,
        grid=(db,),
        in_specs=in_specs,
        out_specs=[per_b((H_A, width)), per_b((H_B, width))],
        scratch_shapes=[pltpu.VMEM((2, len(caches), pps, width, PAGE_SIZE), F32), pltpu.SemaphoreType.DMA((2,)),
                        pltpu.VMEM((H_A, 1), F32), pltpu.VMEM((H_A, 1), F32), pltpu.VMEM((H_A, width), F32),
                        pltpu.VMEM((H_B, 1), F32), pltpu.VMEM((H_B, 1), F32), pltpu.VMEM((H_B, width), F32)],
    )
    return pl.pallas_call(
        functools.partial(_sattn_kernel, pps=pps, past=past, layer=layer),
        grid_spec=grid_spec,
        out_shape=[jax.ShapeDtypeStruct((db, H_A, width), F32), jax.ShapeDtypeStruct((db, H_B, width), F32)],
        compiler_params=_cparams(("arbitrary",)),
        name="sample_attention",
    )(page_table, qa_blk, qb_blk, dec, mb, rbt, new['ka'], new['va'], new['kb'], new['vb'], new['mbn'], *caches)


def _head_cols(w, perm):
    d = w.shape[0]
    return w.reshape(d, len(perm), HEAD_DIM)[:, list(perm), :].reshape(d, -1)


def _pad_cols(w, n):
    return jnp.pad(w, ((0, 0), (0, n - w.shape[1])))


def _split_w_in(w_in_l):
    sizes = (H_A * HEAD_DIM, KV_A * HEAD_DIM, KV_A * HEAD_DIM, H_A, H_B * HEAD_DIM, KV_B * HEAD_DIM,
             KV_B * HEAD_DIM, H_IDX * D_IDX, D_IDX, H_IDX)
    offs = np.cumsum((0,) + sizes)
    return [w_in_l[:, offs[n]:offs[n + 1]] for n in range(len(sizes))]


def _prep_proj_weights(w_in_l, perm):
    qa, ka, va, fa, qb, kb, vb, qi, ki, wi = _split_w_in(w_in_l)
    scale = HEAD_DIM ** -0.5
    big = jnp.concatenate([_head_cols(qa, perm) * scale, _head_cols(qb, perm) * scale, ka, va, kb, vb],
                          axis=1).astype(BF16)
    small = _pad_cols(jnp.concatenate([fa, wi], axis=1), LANES)
    idx = jnp.concatenate([qi * (D_IDX ** -0.5), ki, ki, small], axis=1)
    idx_hi, idx_lo = _split2(idx)
    return dict(big=big, idx_hi=idx_hi, idx_lo=idx_lo)


def _prep_proj_t_weights(w_in_l, perm):
    qa, ka, va, fa, qb, kb, vb, qi, ki, wi = _split_w_in(w_in_l)
    d = w_in_l.shape[0]
    scale = HEAD_DIM ** -0.5
    t_big = jnp.concatenate([_head_cols(qa, perm) * scale, _head_cols(qb, perm) * scale, va, vb, ka, kb],
                            axis=1).T.astype(BF16)
    zero = jnp.zeros((d, HEAD_DIM), w_in_l.dtype)
    spread = []
    for m in range(KV_A // 2):
        spread += [ka[:, 2 * m * HEAD_DIM:(2 * m + 1) * HEAD_DIM], zero, zero,
                   ka[:, (2 * m + 1) * HEAD_DIM:(2 * m + 2) * HEAD_DIM]]
    n_keys = jnp.concatenate(spread + [kb], axis=1).astype(BF16)
    small = _pad_cols(jnp.concatenate([fa, wi], axis=1), LANES)
    n_idx_hi, n_idx_lo = _split2(jnp.concatenate([ki, ki, small], axis=1))
    t_idx_hi, t_idx_lo = _split2((qi * (D_IDX ** -0.5)).T)
    return dict(t_big=t_big, n_keys=n_keys, n_idx_hi=n_idx_hi, n_idx_lo=n_idx_lo,
                t_idx_hi=t_idx_hi, t_idx_lo=t_idx_lo)


def _prep_post_weights(w_out_l, perm_a, perm_b, ln1_g, ln1_b, w_up_l, w_down_l, ln2_g, ln2_b, ff_chunk):
    d_mix, d = w_out_l.shape
    na = H_A * HEAD_DIM
    rows = lambda w, perm: w.reshape(len(perm), HEAD_DIM, d)[np.asarray(perm)].reshape(-1, d)
    d_ff = w_up_l.shape[1]
    nc = d_ff // ff_chunk
    row = lambda a: a.reshape(1, -1)
    return dict(out_a=rows(w_out_l[:na], perm_a).astype(BF16), out_b=rows(w_out_l[na:], perm_b).astype(BF16),
                ln1_g=row(ln1_g), ln1_b=row(ln1_b), ln2_g=row(ln2_g), ln2_b=row(ln2_b),
                up=w_up_l.reshape(d, nc, ff_chunk).transpose(1, 0, 2).astype(BF16),
                down=w_down_l.reshape(nc, ff_chunk, d).astype(BF16))


ATTN_BLOCK = 256
FF_CHUNK = 512
POST_ROWS = 512
SCAN_PAGES_PER_STEP = 16
ATTN_PAGES_PER_STEP = 16


def _prompt_layer(x2d, batch, seq, wproj, wpost, bf_pad, relb, rb_far, alpha):
    tq = min(ATTN_BLOCK, seq)
    p = _project_t(x2d, wproj, bf_pad, batch=batch, seq=seq, tk=tq)
    oa_t = _fox_attention(p['qat'], p['kaug'], p['vax'], batch, seq, tq)
    wi_t = p['wi'].reshape(batch, seq, H_IDX).transpose(0, 2, 1)
    ob_t = _dsa_attention(p, wi_t, relb, rb_far, batch, seq, tq, min(TOPK_MAX, seq // 4))
    y = _post_block(x2d, oa_t, ob_t, wpost, alpha, seq=seq)
    kv = lambda a: a.reshape(batch, KV_A, HEAD_DIM, seq).transpose(0, 3, 1, 2)
    return y, (kv(p['kat']), kv(p['vat']), p['logf'].reshape(batch, seq, H_A), kv(p['kbt']), kv(p['vbt']),
               p['ki'].reshape(batch, seq, D_IDX))


def _block_diag_q(q):
    db = q.shape[0]
    qh = q.reshape(db, H_A, 1, HEAD_DIM)
    kv_of_head = jnp.arange(H_A) // GROUP
    onehot = (kv_of_head[:, None] == jnp.arange(KV_A)[None, :])[None, :, :, None]
    return jnp.where(onehot, qh, jnp.zeros((), q.dtype)).reshape(db, H_A, KV_A * HEAD_DIM)


def _diag_heads(o_wide):
    db = o_wide.shape[0]
    o = o_wide.reshape(db, H_A, KV_A, HEAD_DIM)
    idx = (jnp.arange(H_A) // GROUP)[None, :, None, None]
    return jnp.take_along_axis(o, jnp.broadcast_to(idx, (db, H_A, 1, HEAD_DIM)), axis=2).reshape(db, -1)


def _sample_layer(x2d, page_table, caches, layer, wproj, wpost, bf_pad, rbt, alpha):
    db = x2d.shape[0]
    n_pages = page_table.shape[1]
    past = n_pages * PAGE_SIZE
    p = _project(x2d, wproj, bf_pad)
    q16 = jnp.concatenate([p['qi_hi'].reshape(db, H_IDX, D_IDX), p['qi_lo'].reshape(db, H_IDX, D_IDX)], axis=1)
    pps = min(ATTN_PAGES_PER_STEP, n_pages // 2)
    scan_pps = min(SCAN_PAGES_PER_STEP, n_pages // 2)
    scores, dec_c = _sample_scan(page_table, q16, p['wi'].reshape(db, H_IDX, 1), p['logf'].reshape(db, H_A, 1),
                                 caches['idx_k'], caches['logf_t'], layer, scan_pps, pps)
    q_f32 = p['qi_hi'].astype(F32) + p['qi_lo'].astype(F32)
    mb, mbn = _sample_select(scores.reshape(db, past), q_f32, jnp.tile(p['ki'], (1, H_IDX)), p['wi'],
                             min(TOPK_MAX, (past + 1) // 4))
    new = dict(ka=p['ka'].reshape(db, 1, -1), va=p['va'].reshape(db, 1, -1), kb=p['kb'].reshape(db, 1, -1),
               vb=p['vb'].reshape(db, 1, -1), mbn=mbn.reshape(db, 1, LANES))
    n_chunks = n_pages // pps
    oa_w, ob_w = _sample_attention(page_table, _block_diag_q(p['qa']), _block_diag_q(p['qb']), dec_c,
                                   mb.reshape(db, n_chunks, 1, pps * PAGE_SIZE), rbt, new,
                                   (caches['a_k'], caches['a_v'], caches['b_k'], caches['b_v']), layer, pps)
    y = _post_block(x2d, _diag_heads(oa_w).astype(BF16), _diag_heads(ob_w).astype(BF16), wpost, alpha)
    return y, (p['ka'], p['va'], p['logf'], p['kb'], p['vb'], p['ki'])


def kernel(x_prompt, x_sample, cache_a_k, cache_a_v, cache_a_logf, cache_b_k, cache_b_v, cache_idx_k,
           page_table, w_in, b_f, w_out, ln1_g, ln1_b, w_up, w_down, ln2_g, ln2_b, rel_bias):
    depth = w_in.shape[0]
    batch, seq, d_model = x_prompt.shape
    db, dec_seq, _ = x_sample.shape
    assert dec_seq == 1
    alpha = (2 * depth) ** 0.25
    natural = tuple(range(H_A))
    tq = min(ATTN_BLOCK, seq)

    kv_t = lambda c: c.transpose(0, 1, 3, 4, 2).reshape(c.shape[:2] + (-1, PAGE_SIZE))
    caches = dict(a_k=kv_t(cache_a_k), a_v=kv_t(cache_a_v), b_k=kv_t(cache_b_k), b_v=kv_t(cache_b_v),
                  idx_k=cache_idx_k.transpose(0, 1, 3, 2),
                  logf_t=cache_a_logf.astype(F32).transpose(0, 1, 3, 2))
    relb = _rel_bias_tiles(rel_bias[:, list(HEAD_PERM)], tq)
    rb_far = rel_bias[N_BUCKETS - 1, list(HEAD_PERM)]
    rbt = rel_bias.T

    xp = x_prompt.reshape(batch * seq, d_model)
    xs = x_sample.reshape(db, d_model)
    p_rows, s_rows = [], []
    for l in range(depth):
        bf_pad = _pad_cols(b_f[l].reshape(1, -1), LANES)
        post = lambda perm: _prep_post_weights(w_out[l], perm, perm, ln1_g[l], ln1_b[l], w_up[l], w_down[l],
                                               ln2_g[l], ln2_b[l], FF_CHUNK)
        xp, rows = _prompt_layer(xp, batch, seq, _prep_proj_t_weights(w_in[l], HEAD_PERM), post(HEAD_PERM),
                                 bf_pad, relb, rb_far, alpha)
        p_rows.append(rows)
        xs, rows = _sample_layer(xs, page_table, caches, l, _prep_proj_weights(w_in[l], natural),
                                 post(natural), bf_pad, rbt, alpha)
        s_rows.append(rows)

    stack = lambda rows, j: jnp.stack([r[j] for r in rows])
    sample = lambda j, tail: stack(s_rows, j).reshape((depth, db, dec_seq) + tail)
    kv = (KV_A, HEAD_DIM)
    return (xp.reshape(batch, seq, d_model), xs.reshape(db, dec_seq, d_model),
            stack(p_rows, 0), stack(p_rows, 1), stack(p_rows, 2), stack(p_rows, 3), stack(p_rows, 4),
            stack(p_rows, 5),
            sample(0, kv), sample(1, kv), sample(2, (H_A,)), sample(3, kv), sample(4, kv), sample(5, (D_IDX,)))
```

```python
import functools
import math

import numpy as np
import jax
import jax.numpy as jnp
from jax import lax
from jax.experimental import pallas as pl
from jax.experimental.pallas import tpu as pltpu

HEAD_DIM = 64
H_A = 8
KV_A = 4
H_B = 8
KV_B = 4
GROUP = 2
H_IDX = 8
D_IDX = 64
TOPK_MAX = 256
N_BUCKETS = 32
MAX_DISTANCE = 128
PAGE_SIZE = 128
LN_EPS = 1e-5

LANES = 128
VMEM_LIMIT_BYTES = 56 * 1024 * 1024

NEG_BIG = -1e30
INT_MIN = -(2 ** 31)
HALF_BITS = 16
I16_MIN = -(2 ** 15)

BF16 = jnp.bfloat16
F32 = jnp.float32

HEAD_PERM = (0, 2, 1, 3, 4, 6, 5, 7)


def _cparams(semantics):
    return pltpu.CompilerParams(dimension_semantics=semantics,
                                vmem_limit_bytes=VMEM_LIMIT_BYTES)


def _dot(a, b):
    return jnp.dot(a, b, preferred_element_type=F32)


def _dot_nt(a, b):
    return lax.dot_general(a, b, (((1,), (1,)), ((), ())), preferred_element_type=F32)


def _split2(x):
    hi = x.astype(BF16)
    lo = (x - hi.astype(F32)).astype(BF16)
    return hi, lo


def _split3(x):
    h1 = x.astype(BF16)
    r1 = x - h1.astype(F32)
    h2 = r1.astype(BF16)
    h3 = (r1 - h2.astype(F32)).astype(BF16)
    return h1, h2, h3


def _dot3(xh, xl, wh, wl):
    return _dot(xh, wh) + _dot(xl, wh) + _dot(xh, wl)


def _log_sigmoid(x):
    return jnp.minimum(x, 0.0) - jnp.log1p(jnp.exp(-jnp.abs(x)))


def _layer_norm(x, g, b):
    mu = jnp.mean(x, axis=-1, keepdims=True)
    xc = x - mu
    var = jnp.mean(xc * xc, axis=-1, keepdims=True)
    return xc * lax.rsqrt(var + LN_EPS) * g + b


def _rel_bucket(dist):
    max_exact = N_BUCKETS // 2
    d = jnp.maximum(dist, 1).astype(F32)
    large = max_exact + (jnp.log(d / max_exact) / math.log(MAX_DISTANCE / max_exact)
                         * (N_BUCKETS - max_exact)).astype(jnp.int32)
    large = jnp.minimum(large, N_BUCKETS - 1)
    return jnp.where(dist < max_exact, dist, large)


def _order_key(score):
    bits = pltpu.bitcast(score, jnp.int32)
    key = bits ^ (lax.shift_right_arithmetic(bits, 31) & jnp.int32(0x7FFFFFFF))
    return jnp.where(bits == jnp.int32(INT_MIN), 0, key)


def _proj_kernel(x_ref, wbig_ref, wih_ref, wil_ref, bf_ref,
                 qa_ref, qb_ref, ka_ref, va_ref, kb_ref, vb_ref, qih_ref, qil_ref, ki_ref, kih_ref, kil_ref,
                 wi_ref, logf_ref):
    x = x_ref[...]
    xh, xl = _split2(x)
    nq = H_A * HEAD_DIM
    nk = KV_A * HEAD_DIM
    qa_ref[...] = _dot(xh, wbig_ref[:, 0:nq]).astype(BF16)
    qb_ref[...] = _dot(xh, wbig_ref[:, nq:2 * nq]).astype(BF16)
    o = 2 * nq
    ka_ref[...] = _dot(xh, wbig_ref[:, o:o + nk])
    va_ref[...] = _dot(xh, wbig_ref[:, o + nk:o + 2 * nk])
    kb_ref[...] = _dot(xh, wbig_ref[:, o + 2 * nk:o + 3 * nk])
    vb_ref[...] = _dot(xh, wbig_ref[:, o + 3 * nk:o + 4 * nk])

    ni = H_IDX * D_IDX
    qi = _dot3(xh, xl, wih_ref[:, 0:ni], wil_ref[:, 0:ni])
    qh, ql = _split2(qi)
    qih_ref[...] = qh
    qil_ref[...] = ql
    rest = _dot3(xh, xl, wih_ref[:, ni:ni + 2 * LANES], wil_ref[:, ni:ni + 2 * LANES])
    ki2 = rest[:, 0:LANES]
    ki_ref[...] = ki2[:, 0:D_IDX]
    kh, kl = _split2(ki2)
    kih_ref[...] = kh
    kil_ref[...] = kl

    small = rest[:, LANES:2 * LANES]
    wi_ref[...] = small[:, H_A:H_A + H_IDX] * (H_IDX ** -0.5)
    logf = _log_sigmoid(small + bf_ref[...])
    logf_ref[...] = logf[:, 0:H_A]


def _project(x2d, wts, bf_pad):
    m, d = x2d.shape
    tm = min(512, m)
    assert m % tm == 0
    nq, nk, ni = H_A * HEAD_DIM, KV_A * HEAD_DIM, H_IDX * D_IDX
    full = lambda a: pl.BlockSpec(a.shape, lambda i: (0,) * a.ndim)
    row = lambda n: pl.BlockSpec((tm, n), lambda i: (i, 0))
    inputs = [x2d, wts['big'], wts['idx_hi'], wts['idx_lo'], bf_pad]
    out_shapes = [((m, nq), BF16), ((m, nq), BF16), ((m, nk), F32), ((m, nk), F32), ((m, nk), F32),
                  ((m, nk), F32), ((m, ni), BF16), ((m, ni), BF16), ((m, D_IDX), F32),
                  ((m, LANES), BF16), ((m, LANES), BF16), ((m, H_IDX), F32), ((m, H_A), F32)]
    outs = pl.pallas_call(
        _proj_kernel,
        grid=(m // tm,),
        in_specs=[row(d)] + [full(a) for a in inputs[1:]],
        out_specs=[row(s[0][1]) for s in out_shapes],
        out_shape=[jax.ShapeDtypeStruct(*s) for s in out_shapes],
        compiler_params=_cparams(("arbitrary",)),
        name="proj",
    )(*inputs)
    names = ['qa', 'qb', 'ka', 'va', 'kb', 'vb', 'qi_hi', 'qi_lo', 'ki', 'ki2_hi', 'ki2_lo', 'wi', 'logf']
    return dict(zip(names, outs))


def _proj_t_kernel(x_ref, wt_ref, wn_ref, wnh_ref, wnl_ref, wth_ref, wtl_ref, bf_ref, tri_ref,
                   p1_ref, p2_ref, p3_ref,
                   qat_ref, qbt_ref, vat_ref, vbt_ref, kat_ref, kbt_ref, vax_ref, vbx_ref,
                   kaug_ref, kbn_ref, qith_ref, qitl_ref, ki_ref, kih_ref, kil_ref, wi_ref, logf_ref,
                   carry_ref, *, tiles_per_seq, tk):
    x = x_ref[...]
    xh, xl = _split2(x)
    tm = x.shape[0]
    nq = H_A * HEAD_DIM
    nk = KV_A * HEAD_DIM

    qat_ref[...] = _dot_nt(wt_ref[0:nq, :], xh).astype(BF16)
    qbt_ref[...] = _dot_nt(wt_ref[nq:2 * nq, :], xh).astype(BF16)
    o = 2 * nq
    row = lax.broadcasted_iota(jnp.int32, (2 * HEAD_DIM, tm), 0)
    lo_rows = row < HEAD_DIM
    for t_ref, x_out_ref, r0 in ((vat_ref, vax_ref, o), (vbt_ref, vbx_ref, o + nk)):
        vt = _dot_nt(wt_ref[r0:r0 + nk, :], xh)
        t_ref[...] = vt
        for m in range(KV_A // 2):
            blk = vt[m * 2 * HEAD_DIM:(m + 1) * 2 * HEAD_DIM, :]
            lo = jnp.where(lo_rows, blk, 1.0).astype(BF16)
            hi = jnp.where(lo_rows, 1.0, blk).astype(BF16)
            for t in range(tm // tk):
                cols = slice(t * tk, (t + 1) * tk)
                x_out_ref[t, m * 4 * HEAD_DIM:m * 4 * HEAD_DIM + 2 * HEAD_DIM, :] = lo[:, cols]
                x_out_ref[t, m * 4 * HEAD_DIM + 2 * HEAD_DIM:(m + 1) * 4 * HEAD_DIM, :] = hi[:, cols]
    kat_ref[...] = _dot_nt(wt_ref[o + 2 * nk:o + 3 * nk, :], xh)
    kbt_ref[...] = _dot_nt(wt_ref[o + 3 * nk:o + 4 * nk, :], xh)

    kn = _dot(xh, wn_ref[...])
    kbn_ref[...] = kn[:, 2 * nk:3 * nk].astype(BF16)

    qit = (_dot_nt(wth_ref[...], xh) + _dot_nt(wth_ref[...], xl)) + _dot_nt(wtl_ref[...], xh)
    qh, ql = _split2(qit)
    qith_ref[...] = qh
    qitl_ref[...] = ql

    rest = _dot3(xh, xl, wnh_ref[...], wnl_ref[...])
    ki2 = rest[:, 0:LANES]
    ki_ref[...] = ki2[:, 0:D_IDX]
    kh, kl = _split2(ki2)
    kih_ref[...] = kh
    kil_ref[...] = kl
    small = rest[:, LANES:2 * LANES]
    wi_ref[...] = small[:, H_A:H_A + H_IDX] * (H_IDX ** -0.5)
    logf = _log_sigmoid(small + bf_ref[...])
    logf_ref[...] = logf[:, 0:H_A]

    @pl.when(pl.program_id(0) % tiles_per_seq == 0)
    def _():
        carry_ref[...] = jnp.zeros_like(carry_ref)

    h1, h2, h3 = _split3(logf)
    tri = tri_ref[...]
    c = (_dot(tri, h1) + _dot(tri, h2)) + _dot(tri, h3) + carry_ref[...]
    carry_ref[...] = c[tm - 1:tm, :]
    c1, c2, c3 = _split3(c)
    aug = (_dot(c1, p1_ref[...]) + _dot(c2, p2_ref[...])) + _dot(c3, p3_ref[...])
    kaug_ref[...] = (kn[:, 0:2 * nk] + aug).astype(BF16)


def _decay_placement():
    mats = np.zeros((3, LANES, 2 * KV_A * HEAD_DIM), np.float32)
    for m in range(KV_A // 2):
        for v in range(2 * GROUP):
            head = HEAD_PERM[4 * m + v]
            hi_variant, g = v % 2, v // 2
            base = m * 4 * HEAD_DIM + (2 * HEAD_DIM if hi_variant else HEAD_DIM)
            for t in range(3):
                mats[t, head, base + 3 * g + t] = -1.0
    return [jnp.asarray(mats[t], BF16) for t in range(3)]


def _project_t(x2d, wts, bf_pad, *, batch, seq, tk):
    m, d = x2d.shape
    tm = min(512, seq)
    assert seq % tm == 0 and tm % tk == 0
    tps = seq // tm
    nq, nk, ni = H_A * HEAD_DIM, KV_A * HEAD_DIM, H_IDX * D_IDX
    nkv = seq // tk
    full = lambda a: pl.BlockSpec(a.shape, lambda i: (0,) * a.ndim)
    row = lambda n: pl.BlockSpec((tm, n), lambda i: (i, 0))
    feat = lambda r: pl.BlockSpec((None, r, tm), lambda i: (i // tps, 0, i % tps))
    vx = pl.BlockSpec((None, tm // tk, 2 * nk, tk), lambda i: (i // tps, i % tps, 0, 0))
    tri = jnp.tril(jnp.ones((tm, tm), F32)).astype(BF16)
    inputs = [x2d, wts['t_big'], wts['n_keys'], wts['n_idx_hi'], wts['n_idx_lo'], wts['t_idx_hi'],
              wts['t_idx_lo'], bf_pad, tri] + _decay_placement()
    fshape = lambda r, dt: ((batch, r, seq), dt)
    outs = [
        (fshape(nq, BF16), feat(nq)), (fshape(nq, BF16), feat(nq)),
        (fshape(nk, F32), feat(nk)), (fshape(nk, F32), feat(nk)),
        (fshape(nk, F32), feat(nk)), (fshape(nk, F32), feat(nk)),
        (((batch, nkv, 2 * nk, tk), BF16), vx), (((batch, nkv, 2 * nk, tk), BF16), vx),
        (((m, 2 * nk), BF16), row(2 * nk)), (((m, nk), BF16), row(nk)),
        (fshape(ni, BF16), feat(ni)), (fshape(ni, BF16), feat(ni)),
        (((m, D_IDX), F32), row(D_IDX)), (((m, LANES), BF16), row(LANES)), (((m, LANES), BF16), row(LANES)),
        (((m, H_IDX), F32), row(H_IDX)), (((m, H_A), F32), row(H_A)),
    ]
    res = pl.pallas_call(
        functools.partial(_proj_t_kernel, tiles_per_seq=tps, tk=tk),
        grid=(m // tm,),
        in_specs=[row(d)] + [full(a) for a in inputs[1:]],
        out_specs=[o[1] for o in outs],
        out_shape=[jax.ShapeDtypeStruct(*o[0]) for o in outs],
        scratch_shapes=[pltpu.VMEM((1, LANES), F32)],
        compiler_params=_cparams(("arbitrary",)),
        name="proj_prompt",
    )(*inputs)
    names = ['qat', 'qbt', 'vat', 'vbt', 'kat', 'kbt', 'vax', 'vbx', 'kaug', 'kbn', 'qit_hi', 'qit_lo',
             'ki', 'ki2_hi', 'ki2_lo', 'wi', 'logf']
    return dict(zip(names, res))


def _post_kernel(x_ref, oa_ref, ob_ref, woa_ref, wob_ref, g1_ref, b1_ref, wup_ref, wdn_ref,
                 g2_ref, b2_ref, y_ref, acc_ref, *, alpha, n_chunks, feature_major):
    x = x_ref[...]
    if feature_major:
        tn = lambda a, w: lax.dot_general(a, w, (((0,), (0,)), ((), ())), preferred_element_type=F32)
        att = tn(oa_ref[...], woa_ref[...]) + tn(ob_ref[...], wob_ref[...])
    else:
        att = _dot(oa_ref[...], woa_ref[...]) + _dot(ob_ref[...], wob_ref[...])
    x1 = _layer_norm(alpha * x + att, g1_ref[...], b1_ref[...])
    x1b = x1.astype(BF16)
    acc_ref[...] = jnp.zeros_like(acc_ref)

    def body(c, carry):
        u = jnp.maximum(_dot(x1b, wup_ref[c]), 0.0)
        acc_ref[...] += _dot((u * u).astype(BF16), wdn_ref[c])
        return carry

    lax.fori_loop(0, n_chunks, body, 0)
    y_ref[...] = _layer_norm(alpha * x1 + acc_ref[...], g2_ref[...], b2_ref[...])


def _post_block(x2d, oa, ob, wts, alpha, seq=None):
    m, d = x2d.shape
    tm = min(POST_ROWS, m if seq is None else seq)
    n_chunks = wts['up'].shape[0]
    full = lambda a: pl.BlockSpec(a.shape, lambda i: (0,) * a.ndim, pipeline_mode=pl.Buffered(1))
    row = lambda n: pl.BlockSpec((tm, n), lambda i: (i, 0))
    if seq is None:
        o_spec = lambda a: row(a.shape[1])
    else:
        tps = seq // tm
        o_spec = lambda a: pl.BlockSpec((None, a.shape[1], tm), lambda i: (i // tps, 0, i % tps))
    weights = [wts['out_a'], wts['out_b'], wts['ln1_g'], wts['ln1_b'], wts['up'], wts['down'],
               wts['ln2_g'], wts['ln2_b']]
    return pl.pallas_call(
        functools.partial(_post_kernel, alpha=alpha, n_chunks=n_chunks, feature_major=seq is not None),
        grid=(m // tm,),
        in_specs=[row(d), o_spec(oa), o_spec(ob)] + [full(w) for w in weights],
        out_specs=row(d),
        out_shape=jax.ShapeDtypeStruct((m, d), F32),
        scratch_shapes=[pltpu.VMEM((tm, d), F32)],
        compiler_params=_cparams(("arbitrary",)),
        name="post_block",
    )(x2d, oa, ob, *weights)


def _variant_rows(g, hi_half, ones_from=None):
    row = lax.broadcasted_iota(jnp.int32, g.shape, 0)
    keep = (row >= HEAD_DIM) if hi_half else (row < HEAD_DIM)
    fill = 0.0
    if ones_from is not None:
        fill = jnp.where((row >= ones_from) & (row < ones_from + 3), 1.0, 0.0)
    return jnp.where(keep, g, fill).astype(BF16)


def _softmax_chunk(s, causal_q0, m_ref, p_ref, acc_ref, cols, acols):
    if causal_q0 is not None:
        key = lax.broadcasted_iota(jnp.int32, s.shape, 0)
        qry = lax.broadcasted_iota(jnp.int32, s.shape, 1) + causal_q0
        s = jnp.where(key <= qry, s, NEG_BIG)
    m_old = m_ref[:, cols]
    m_new = jnp.maximum(m_old, jnp.max(s, axis=0, keepdims=True))
    m_ref[:, cols] = m_new
    p_ref[:, cols] = jnp.exp(s - m_new[0:1, :]).astype(BF16)
    acc_ref[:, acols] = acc_ref[:, acols] * jnp.exp(m_old - m_new)[0:1, :]


def _finish_pair(acc_lo, acc_hi, tq):
    row = lax.broadcasted_iota(jnp.int32, (2 * HEAD_DIM, tq), 0)
    lo_rows = row < HEAD_DIM
    out = []
    for g in range(GROUP):
        cols = slice(g * tq, (g + 1) * tq)
        a_lo, a_hi = acc_lo[:, cols], acc_hi[:, cols]
        o_lo = a_lo / a_lo[HEAD_DIM:HEAD_DIM + 1, :]
        o_hi = a_hi / a_hi[0:1, :]
        out.append(jnp.where(lo_rows, o_lo, o_hi))
    return jnp.concatenate(out, axis=0).astype(BF16)


def _fox_kernel(q_ref, k_ref, v_ref, o_ref, qt_ref, s_ref, p_ref, m_ref, acc_ref, *, tq):
    i = pl.program_id(2)
    q = q_ref[...].astype(F32)
    g0, g1 = q[0:2 * HEAD_DIM], q[2 * HEAD_DIM:4 * HEAD_DIM]
    qt_ref[0, :, 0:tq] = _variant_rows(g0, False, HEAD_DIM)
    qt_ref[0, :, tq:2 * tq] = _variant_rows(g1, False, HEAD_DIM + 3)
    qt_ref[1, :, 0:tq] = _variant_rows(g0, True, 0)
    qt_ref[1, :, tq:2 * tq] = _variant_rows(g1, True, 3)
    m_ref[...] = jnp.full_like(m_ref, NEG_BIG)
    acc_ref[...] = jnp.zeros_like(acc_ref)

    def process(blocks):
        for slot, (j, _) in enumerate(blocks):
            start = pl.multiple_of(j * tq, tq)
            for d in range(2):
                s_ref[slot, d] = _dot(k_ref[pl.ds(start, tq), d * LANES:(d + 1) * LANES], qt_ref[d])
        for slot, (j, causal) in enumerate(blocks):
            for d in range(2):
                for c in range(2 * tq // LANES):
                    cols = slice(c * LANES, (c + 1) * LANES)
                    q0 = (c * LANES) % tq if causal else None
                    _softmax_chunk(s_ref[slot, d, :, cols], q0, m_ref.at[d], p_ref.at[slot, d],
                                   acc_ref.at[d], cols, cols)
                acc_ref[d] += _dot(v_ref[j, d * LANES:(d + 1) * LANES, :], p_ref[slot, d])

    def pair_body(t, carry):
        process(((2 * t, False), (2 * t + 1, False)))
        return carry

    lax.fori_loop(0, i // 2, pair_body, 0)

    @pl.when(i % 2 == 1)
    def _():
        process(((i - 1, False), (i, True)))

    @pl.when(i % 2 == 0)
    def _():
        process(((i, True),))

    o_ref[...] = _finish_pair(acc_ref[0], acc_ref[1], tq)


def _fox_attention(qt, kaug, vx, batch, seq, tq):
    nq = seq // tq
    pair = 4 * HEAD_DIM
    return pl.pallas_call(
        functools.partial(_fox_kernel, tq=tq),
        grid=(batch, KV_A // 2, nq),
        in_specs=[
            pl.BlockSpec((None, pair, tq), lambda b, m, i: (b, m, i)),
            pl.BlockSpec((seq, pair), lambda b, m, i: (b, m)),
            pl.BlockSpec((None, nq, pair, tq), lambda b, m, i: (b, 0, m, 0)),
        ],
        out_specs=pl.BlockSpec((None, pair, tq), lambda b, m, i: (b, m, i)),
        out_shape=jax.ShapeDtypeStruct(qt.shape, BF16),
        scratch_shapes=[pltpu.VMEM((2, LANES, 2 * tq), BF16), pltpu.VMEM((2, 2, tq, 2 * tq), F32),
                        pltpu.VMEM((2, 2, tq, 2 * tq), BF16), pltpu.VMEM((2, 8, 2 * tq), F32),
                        pltpu.VMEM((2, LANES, 2 * tq), F32)],
        compiler_params=_cparams(("arbitrary", "arbitrary", "arbitrary")),
        name="fox_attention",
    )(qt, kaug, vx)


def _relb_kernel(rb_ref, o_ref, *, tq):
    delta = pl.program_id(0)
    h = pl.program_id(1)
    key = lax.broadcasted_iota(jnp.int32, (tq, tq), 0)
    qry = lax.broadcasted_iota(jnp.int32, (tq, tq), 1)
    bucket = _rel_bucket(jnp.maximum(delta * tq + qry - key, 0))
    acc = jnp.zeros((tq, tq), F32)
    for n in range(N_BUCKETS):
        acc = jnp.where(bucket == n, rb_ref[n, h], acc)
    o_ref[...] = acc


def _rel_bias_tiles(rb_perm, tq):
    return pl.pallas_call(
        functools.partial(_relb_kernel, tq=tq),
        grid=(2, H_B),
        in_specs=[pl.BlockSpec(memory_space=pltpu.SMEM)],
        out_specs=pl.BlockSpec((None, None, tq, tq), lambda d, h: (d, h, 0, 0)),
        out_shape=jax.ShapeDtypeStruct((2, H_B, tq, tq), F32),
        compiler_params=_cparams(("arbitrary", "arbitrary")),
        name="rel_bias_tiles",
    )(rb_perm)


def _strict_upper(n):
    r = lax.broadcasted_iota(jnp.int32, (n, n), 0)
    c = lax.broadcasted_iota(jnp.int32, (n, n), 1)
    return jnp.where(r < c, 1.0, 0.0).astype(BF16)


def _strict_lower(n):
    r = lax.broadcasted_iota(jnp.int32, (n, n), 0)
    c = lax.broadcasted_iota(jnp.int32, (n, n), 1)
    return jnp.where(c < r, 1.0, 0.0).astype(BF16)


KEY_ROWS = 64

DSA_VARIANT_ORDER = (0, 2, 1, 3)


def _dsa_kernel(qih_ref, qil_ref, kih_ref, wi_ref, qb_ref, k_ref, v_ref, relb_ref, rbfar_ref,
                o_ref, keys_ref, khi_ref, klo_ref, mb_ref, qi_ref, lg_ref, qt_ref, p_ref, m_ref, acc_ref,
                *, tq, topk):
    i = pl.program_id(1)
    nblk = i + 1
    nlc = tq // LANES

    for h in range(H_IDX):
        rows = slice((h // 2) * LANES, (h // 2 + 1) * LANES)
        cols = slice(h * tq, (h + 1) * tq)
        qi_ref[0:LANES, cols] = _variant_rows(qih_ref[rows, :].astype(F32), h % 2 == 1)
        qi_ref[LANES:2 * LANES, cols] = _variant_rows(qil_ref[rows, :].astype(F32), h % 2 == 1)

    def score_block(j, diag):
        start = pl.multiple_of(j * tq, tq)
        kh = kih_ref[pl.ds(start, tq), :]
        lg_ref[...] = _dot(jnp.concatenate([kh, kh], axis=1), qi_ref[...])
        for c in range(nlc):
            for r in range(tq // KEY_ROWS):
                rows = slice(r * KEY_ROWS, (r + 1) * KEY_ROWS)
                sc = jnp.zeros((KEY_ROWS, LANES), F32)
                for h in range(H_IDX):
                    lane0 = h * tq + c * LANES
                    sc = sc + wi_ref[h:h + 1, c * LANES:(c + 1) * LANES] * jnp.maximum(
                        lg_ref[rows, lane0:lane0 + LANES], 0.0)
                key = _order_key(sc)
                if diag:
                    kidx = lax.broadcasted_iota(jnp.int32, key.shape, 0) + r * KEY_ROWS
                    qidx = lax.broadcasted_iota(jnp.int32, key.shape, 1) + c * LANES
                    key = jnp.where(kidx <= qidx, key, INT_MIN)
                keys_ref[j, rows, c * LANES:(c + 1) * LANES] = key
                khi_ref[j, rows, c * LANES:(c + 1) * LANES] = lax.shift_right_arithmetic(
                    key, HALF_BITS).astype(jnp.int16)

    def score_body(j, carry):
        score_block(j, False)
        return carry

    lax.fori_loop(0, i, score_body, 0)
    score_block(i, True)

    needs_select = (i + 1) * tq > topk

    @pl.when(jnp.logical_not(needs_select))
    def _():
        def zero_body(j, carry):
            mb_ref[j] = jnp.zeros((tq, tq), F32)
            return carry
        lax.fori_loop(0, nblk, zero_body, 0)

    @pl.when(needs_select)
    def _():
        def count16(ref, pred_fn):
            def body(j, acc):
                hit = jnp.where(pred_fn(ref[j]), jnp.int16(1), jnp.int16(0))
                for r in range(tq // 16):
                    acc = acc + hit[r * 16:(r + 1) * 16, :]
                return acc
            acc = lax.fori_loop(0, nblk, body, jnp.zeros((16, tq), jnp.int16))
            return jnp.sum(acc.astype(F32), axis=0, keepdims=True)

        def search16(ref, want):
            def bit_body(b, t):
                cand = t + lax.shift_left(jnp.int32(1), HALF_BITS - 1 - b)
                cand16 = cand.astype(jnp.int16)
                return jnp.where(count16(ref, lambda k: k >= cand16) >= want, cand, t)
            return lax.fori_loop(0, HALF_BITS, bit_body, jnp.full((1, tq), I16_MIN, jnp.int32))

        thr_hi = search16(khi_ref, topk)
        thr_hi16 = thr_hi.astype(jnp.int16)
        above = count16(khi_ref, lambda k: k > thr_hi16)

        def low_body(j, carry):
            kblk = keys_ref[j]
            in_band = lax.shift_right_arithmetic(kblk, HALF_BITS) == thr_hi
            low = (kblk & jnp.int32(2 ** HALF_BITS - 1)) + I16_MIN
            klo_ref[j] = jnp.where(in_band, low, I16_MIN).astype(jnp.int16)
            return carry

        lax.fori_loop(0, nblk, low_body, 0)
        thr_lo = search16(klo_ref, topk - above)
        thr_lo16 = thr_lo.astype(jnp.int16)
        thr = thr_hi * (2 ** HALF_BITS) + (thr_lo - I16_MIN)
        need = topk - (above + count16(klo_ref, lambda k: k > thr_lo16))
        earlier = _strict_lower(tq)

        def mask_body(j, seen):
            kblk = keys_ref[j]
            eq = kblk == thr
            eqf = jnp.where(eq, 1.0, 0.0)
            rank = _dot(earlier, eqf.astype(BF16)) + seen
            mb_ref[j] = jnp.where(kblk > thr, 0.0,
                                  jnp.where(eq, jnp.where(rank < need, 0.0, NEG_BIG), NEG_BIG))
            return seen + jnp.sum(eqf, axis=0, keepdims=True)

        lax.fori_loop(0, nblk, mask_body, jnp.zeros((1, tq), F32))

    pair = 4 * HEAD_DIM
    for m in range(KV_B // 2):
        q = qb_ref[m * pair:(m + 1) * pair, :].astype(F32)
        g = (q[0:2 * HEAD_DIM], q[2 * HEAD_DIM:4 * HEAD_DIM])
        for n, v in enumerate(DSA_VARIANT_ORDER):
            qt_ref[:, n * tq:(n + 1) * tq] = _variant_rows(g[v // 2], v % 2 == 1)
        m_ref[...] = jnp.full_like(m_ref, NEG_BIG)
        acc_ref[...] = jnp.zeros_like(acc_ref)

        far = lambda slot, qcols: rbfar_ref[slot]
        near = lambda slot, qcols: relb_ref[1, slot, :, qcols]
        diag = lambda slot, qcols: relb_ref[0, slot, :, qcols]

        def process(blocks):
            for slot, (j, _) in enumerate(blocks):
                start = pl.multiple_of(j * tq, tq)
                lg_ref[:, slot * 4 * tq:(slot + 1) * 4 * tq] = _dot(
                    k_ref[pl.ds(start, tq), m * LANES:(m + 1) * LANES], qt_ref[...])
            for slot, (j, rel) in enumerate(blocks):
                for n, v in enumerate(DSA_VARIANT_ORDER):
                    for c in range(nlc):
                        qcols = slice(c * LANES, (c + 1) * LANES)
                        cols = slice(n * tq + c * LANES, n * tq + (c + 1) * LANES)
                        lcols = slice(slot * 4 * tq + cols.start, slot * 4 * tq + cols.stop)
                        s = lg_ref[:, lcols] + (mb_ref[j, :, qcols] + rel(4 * m + v, qcols))
                        acols = slice((n % 2) * tq + c * LANES, (n % 2) * tq + (c + 1) * LANES)
                        _softmax_chunk(s, c * LANES if rel is diag else None, m_ref, p_ref.at[slot],
                                       acc_ref.at[n // 2], cols, acols)
                for d in range(2):
                    vt = v_ref[j, m * pair + d * LANES:m * pair + (d + 1) * LANES, :]
                    acc_ref[d] += _dot(vt, p_ref[slot, :, d * 2 * tq:(d + 1) * 2 * tq])

        n_far = jnp.maximum(i - 1, 0)

        def far_body(t, carry):
            process(((2 * t, far), (2 * t + 1, far)))
            return carry

        lax.fori_loop(0, n_far // 2, far_body, 0)

        @pl.when(n_far % 2 == 1)
        def _():
            process(((i - 2, far),))

        @pl.when(i >= 1)
        def _():
            process(((i - 1, near), (i, diag)))

        @pl.when(i == 0)
        def _():
            process(((i, diag),))

        o_ref[m * pair:(m + 1) * pair, :] = _finish_pair(acc_ref[0], acc_ref[1], tq)


def _dsa_attention(p, wi_t, relb, rb_far, batch, seq, tq, topk):
    nq = seq // tq
    nf = H_B * HEAD_DIM
    qcol = lambda r: pl.BlockSpec((None, r, tq), lambda b, i: (b, 0, i))
    seqblk = lambda n: pl.BlockSpec((seq, n), lambda b, i: (b, 0))
    return pl.pallas_call(
        functools.partial(_dsa_kernel, tq=tq, topk=topk),
        grid=(batch, nq),
        in_specs=[qcol(H_IDX * D_IDX), qcol(H_IDX * D_IDX), seqblk(LANES), qcol(H_IDX),
                  qcol(nf), seqblk(KV_B * HEAD_DIM),
                  pl.BlockSpec((None, nq, nf, tq), lambda b, i: (b, 0, 0, 0)),
                  pl.BlockSpec(relb.shape, lambda b, i: (0, 0, 0, 0)),
                  pl.BlockSpec(memory_space=pltpu.SMEM)],
        out_specs=qcol(nf),
        out_shape=jax.ShapeDtypeStruct((batch, nf, seq), BF16),
        scratch_shapes=[pltpu.VMEM((nq, tq, tq), jnp.int32), pltpu.VMEM((nq, tq, tq), jnp.int16),
                        pltpu.VMEM((nq, tq, tq), jnp.int16), pltpu.VMEM((nq, tq, tq), F32),
                        pltpu.VMEM((2 * LANES, H_IDX * tq), BF16), pltpu.VMEM((tq, H_IDX * tq), F32),
                        pltpu.VMEM((LANES, 4 * tq), BF16), pltpu.VMEM((2, tq, 4 * tq), BF16),
                        pltpu.VMEM((8, 4 * tq), F32), pltpu.VMEM((2, LANES, 2 * tq), F32)],
        compiler_params=_cparams(("arbitrary", "arbitrary")),
        name="dsa_attention",
    )(p['qit_hi'], p['qit_lo'], p['ki2_hi'], wi_t, p['qbt'], p['kbn'], p['vbx'], relb, rb_far)


def _sscan_kernel(pt_ref, q16_ref, wi_ref, lnew_ref, idx_hbm, logf_hbm, sc_ref, dec_ref,
                  kbuf, fbuf, sem, carry_ref, *, pps, n_chunks, attn_pps, layer):
    b = pl.program_id(0)
    n_rows = pl.num_programs(0)

    def copies(row, step, slot):
        chunk = n_chunks - 1 - step
        out = []
        for r in range(pps):
            page = pt_ref[row, chunk * pps + r]
            out.append(pltpu.make_async_copy(idx_hbm.at[layer, page], kbuf.at[slot, r], sem.at[slot]))
            out.append(pltpu.make_async_copy(logf_hbm.at[layer, page], fbuf.at[slot, r], sem.at[slot]))
        return out

    def start(row, step, slot):
        for n, c in enumerate(copies(row, step, slot)):
            c.start(priority=n % 2)

    def wait(row, step, slot):
        for c in copies(row, step, slot):
            c.wait()

    @pl.when(b == 0)
    def _():
        start(0, 0, 0)

    carry_ref[...] = jnp.broadcast_to(lnew_ref[...], carry_ref.shape)
    q16 = q16_ref[...]
    qh = q16[0:H_IDX]
    wi = wi_ref[...]
    r_i = lax.broadcasted_iota(jnp.int32, (PAGE_SIZE, PAGE_SIZE), 0)
    c_i = lax.broadcasted_iota(jnp.int32, (PAGE_SIZE, PAGE_SIZE), 1)
    later = jnp.where(r_i > c_i, 1.0, 0.0).astype(BF16)
    per = pps // attn_pps

    def compute(step, slot):
        chunk = n_chunks - 1 - step
        x_all = jnp.concatenate([fbuf[slot, r] for r in range(pps)], axis=0)
        h1, h2, h3 = _split3(x_all)
        within = (_dot(h1, later) + _dot(h2, later)) + _dot(h3, later)
        totals = jnp.sum(x_all, axis=1, keepdims=True)
        carry = carry_ref[...]
        for r in reversed(range(pps)):
            rows = slice(r * H_A, (r + 1) * H_A)
            lanes = slice((r % attn_pps) * PAGE_SIZE, (r % attn_pps + 1) * PAGE_SIZE)
            dec_ref[chunk * per + r // attn_pps, :, lanes] = within[rows, :] + carry[:, 0:1]
            carry = carry + totals[rows, :]
        carry_ref[...] = carry
        k_all = jnp.concatenate([kbuf[slot, r] for r in range(pps)], axis=1)
        kh, kl = _split2(k_all)
        a = _dot(q16, kh)
        logits = (a[0:H_IDX] + a[H_IDX:2 * H_IDX]) + _dot(qh, kl)
        sc_ref[chunk] = jnp.sum(wi * jnp.maximum(logits, 0.0), axis=0, keepdims=True)

    def pair_body(sp, carry):
        s0 = 2 * sp
        start(b, s0 + 1, 1)
        wait(b, s0, 0)
        compute(s0, 0)

        @pl.when(s0 + 2 < n_chunks)
        def _():
            start(b, s0 + 2, 0)

        @pl.when(jnp.logical_and(s0 + 2 == n_chunks, b + 1 < n_rows))
        def _():
            start(b + 1, 0, 0)

        wait(b, s0 + 1, 1)
        compute(s0 + 1, 1)
        return carry

    lax.fori_loop(0, n_chunks // 2, pair_body, 0)


def _sample_scan(page_table, q16, wi_col, logf_new_col, cache_idx_k, logf_t_cache, layer, pps, attn_pps):
    db, n_pages = page_table.shape
    n_chunks = n_pages // pps
    assert n_chunks % 2 == 0 and n_chunks * pps == n_pages and pps % attn_pps == 0
    n_attn = n_pages // attn_pps
    per_b = lambda shape: pl.BlockSpec((None,) + shape, lambda b, pt: (b,) + (0,) * len(shape))
    hbm = pl.BlockSpec(memory_space=pl.ANY)
    grid_spec = pltpu.PrefetchScalarGridSpec(
        num_scalar_prefetch=1,
        grid=(db,),
        in_specs=[per_b((2 * H_IDX, D_IDX)), per_b((H_IDX, 1)), per_b((H_A, 1)), hbm, hbm],
        out_specs=[per_b((n_chunks, 1, pps * PAGE_SIZE)), per_b((n_attn, H_A, attn_pps * PAGE_SIZE))],
        scratch_shapes=[pltpu.VMEM((2, pps, D_IDX, PAGE_SIZE), F32), pltpu.VMEM((2, pps, H_A, PAGE_SIZE), F32),
                        pltpu.SemaphoreType.DMA((2,)), pltpu.VMEM((H_A, LANES), F32)],
    )
    return pl.pallas_call(
        functools.partial(_sscan_kernel, pps=pps, n_chunks=n_chunks, attn_pps=attn_pps, layer=layer),
        grid_spec=grid_spec,
        out_shape=[jax.ShapeDtypeStruct((db, n_chunks, 1, pps * PAGE_SIZE), F32),
                   jax.ShapeDtypeStruct((db, n_attn, H_A, attn_pps * PAGE_SIZE), F32)],
        compiler_params=_cparams(("arbitrary",)),
        name="sample_scan",
    )(page_table, q16, wi_col, logf_new_col, cache_idx_k, logf_t_cache)


def _sselect_kernel(sc_ref, q_ref, kt_ref, wi_ref, mb_ref, mbn_ref, keys_ref, *, topk, chunk):
    db, past = sc_ref.shape
    prod = q_ref[...] * kt_ref[...]
    lane = lax.broadcasted_iota(jnp.int32, prod.shape, 1)
    wi = wi_ref[...]
    sc_new = jnp.zeros((db, 1), F32)
    for h in range(H_IDX):
        seg = (lane >= h * D_IDX) & (lane < (h + 1) * D_IDX)
        logit = jnp.sum(jnp.where(seg, prod, 0.0), axis=1, keepdims=True)
        sc_new = sc_new + wi[:, h:h + 1] * jnp.maximum(logit, 0.0)
    key_new = _order_key(sc_new)
    keys_ref[...] = _order_key(sc_ref[...])

    def count(pred_fn):
        n = jnp.sum(jnp.where(pred_fn(keys_ref[...]), 1.0, 0.0), axis=1, keepdims=True)
        return n + jnp.where(pred_fn(key_new), 1.0, 0.0)

    def bit_body(b, t):
        cand = t + lax.shift_left(jnp.int32(1), 31 - b)
        return jnp.where(count(lambda k: k >= cand) >= topk, cand, t)

    thr = lax.fori_loop(0, 32, bit_body, jnp.full((db, 1), INT_MIN, jnp.int32))
    need = topk - count(lambda k: k > thr)
    sut = _strict_upper(chunk)
    seen = jnp.zeros((db, 1), F32)
    for c in range(past // chunk):
        cols = slice(c * chunk, (c + 1) * chunk)
        kblk = keys_ref[:, cols]
        eq = kblk == thr
        eqf = jnp.where(eq, 1.0, 0.0)
        rank = _dot(eqf.astype(BF16), sut) + seen
        mb_ref[:, cols] = jnp.where(kblk > thr, 0.0,
                                    jnp.where(eq, jnp.where(rank < need, 0.0, NEG_BIG), NEG_BIG))
        seen = seen + jnp.sum(eqf, axis=1, keepdims=True)
    sel_new = jnp.where(key_new > thr, 0.0,
                        jnp.where(key_new == thr, jnp.where(seen < need, 0.0, NEG_BIG), NEG_BIG))
    mbn_ref[...] = jnp.broadcast_to(sel_new, mbn_ref.shape)


def _sample_select(scores, q_f32, ki_tiled, wi, topk):
    db, past = scores.shape
    return pl.pallas_call(
        functools.partial(_sselect_kernel, topk=topk, chunk=2 * LANES),
        out_shape=[jax.ShapeDtypeStruct((db, past), F32), jax.ShapeDtypeStruct((db, LANES), F32)],
        scratch_shapes=[pltpu.VMEM((db, past), jnp.int32)],
        compiler_params=pltpu.CompilerParams(vmem_limit_bytes=VMEM_LIMIT_BYTES),
        name="sample_select",
    )(scores, q_f32, ki_tiled, wi)


def _sattn_kernel(pt_ref, qa_ref, qb_ref, dec_ref, mb_ref, rbt_ref, kan_ref, van_ref, kbn_ref, vbn_ref,
                  mbn_ref, ak_hbm, av_hbm, bk_hbm, bv_hbm, oa_ref, ob_ref,
                  buf, sem, ma_ref, la_ref, acca_ref, mb_m_ref, lb_ref, accb_ref, *, pps, past, layer):
    b = pl.program_id(0)
    n_rows = pl.num_programs(0)
    width = pps * PAGE_SIZE
    n_chunks = past // width
    caches = (ak_hbm, av_hbm, bk_hbm, bv_hbm)

    def page_copy(row, chunk, slot, t, r):
        page = pt_ref[row, chunk * pps + r]
        return pltpu.make_async_copy(caches[t].at[layer, page], buf.at[slot, t, r], sem.at[slot])

    def start_chunk(row, chunk, slot):
        for t in range(len(caches)):
            for r in range(pps):
                page_copy(row, chunk, slot, t, r).start(priority=r % 2)

    def wait_chunk(row, chunk, slot):
        for t in range(len(caches)):
            for r in range(pps):
                page_copy(row, chunk, slot, t, r).wait()

    @pl.when(b == 0)
    def _():
        start_chunk(0, 0, 0)

    for m_ref, l_ref, acc_ref in ((ma_ref, la_ref, acca_ref), (mb_m_ref, lb_ref, accb_ref)):
        m_ref[...] = jnp.full_like(m_ref, NEG_BIG)
        l_ref[...] = jnp.zeros_like(l_ref)
        acc_ref[...] = jnp.zeros_like(acc_ref)

    qa, qb = qa_ref[...], qb_ref[...]
    rbt = rbt_ref[...]

    def update(s, slot, t, m_ref, l_ref, acc_ref):
        m_old = m_ref[...]
        m_new = jnp.maximum(m_old, jnp.max(s, axis=1, keepdims=True))
        a = jnp.exp(m_old - m_new)
        p = jnp.exp(s - m_new)
        l_ref[...] = a * l_ref[...] + jnp.sum(p, axis=1, keepdims=True)
        pv = jnp.zeros(acc_ref.shape, F32)
        for r in range(pps):
            pv = pv + _dot_nt(p[:, r * PAGE_SIZE:(r + 1) * PAGE_SIZE].astype(BF16),
                              buf[slot, t, r].astype(BF16))
        acc_ref[...] = a * acc_ref[...] + pv
        m_ref[...] = m_new

    def compute(chunk, slot):
        s_a = jnp.concatenate([_dot(qa, buf[slot, 0, r].astype(BF16)) for r in range(pps)], axis=1)
        update(s_a + dec_ref[chunk], slot, 1, ma_ref, la_ref, acca_ref)
        pos = chunk * width + lax.broadcasted_iota(jnp.int32, (1, width), 1)
        bucket = _rel_bucket(past - pos)
        rel = jnp.zeros((H_B, width), F32)
        for n in range(N_BUCKETS):
            rel = jnp.where(bucket == n, rbt[:, n:n + 1], rel)
        s_b = jnp.concatenate([_dot(qb, buf[slot, 2, r].astype(BF16)) for r in range(pps)], axis=1)
        update(s_b + rel + mb_ref[chunk], slot, 3, mb_m_ref, lb_ref, accb_ref)

    def pair_body(cp, carry):
        c0 = 2 * cp
        start_chunk(b, c0 + 1, 1)
        wait_chunk(b, c0, 0)
        compute(c0, 0)

        @pl.when(c0 + 2 < n_chunks)
        def _():
            start_chunk(b, c0 + 2, 0)

        @pl.when(jnp.logical_and(c0 + 2 == n_chunks, b + 1 < n_rows))
        def _():
            start_chunk(b + 1, 0, 0)

        wait_chunk(b, c0 + 1, 1)
        compute(c0 + 1, 1)
        return carry

    lax.fori_loop(0, n_chunks // 2, pair_body, 0)

    def finish(q, kn, vn, extra, m_ref, l_ref, acc_ref, o_ref):
        s = jnp.sum(q.astype(F32) * kn, axis=1, keepdims=True) + extra
        m_old = m_ref[...]
        m_new = jnp.maximum(m_old, s)
        a = jnp.exp(m_old - m_new)
        p = jnp.exp(s - m_new)
        l = a * l_ref[...] + p
        o_ref[...] = (a * acc_ref[...] + p * vn) / l

    finish(qa, kan_ref[...], van_ref[...], 0.0, ma_ref, la_ref, acca_ref, oa_ref)
    finish(qb, kbn_ref[...], vbn_ref[...], rbt[:, 0:1] + mbn_ref[:, 0:1], mb_m_ref, lb_ref, accb_ref, ob_ref)


def _sample_attention(page_table, qa_blk, qb_blk, dec, mb, rbt, new, caches, layer, pps):
    db, n_pages = page_table.shape
    past = n_pages * PAGE_SIZE
    width = KV_A * HEAD_DIM
    n_chunks = n_pages // pps
    assert n_chunks % 2 == 0 and n_chunks * pps == n_pages
    per_b = lambda shape: pl.BlockSpec((None,) + shape, lambda b, pt: (b,) + (0,) * len(shape))
    hbm = pl.BlockSpec(memory_space=pl.ANY)
    in_specs = [per_b((H_A, width)), per_b((H_B, width)),
                per_b((n_chunks, H_A, pps * PAGE_SIZE)), per_b((n_chunks, 1, pps * PAGE_SIZE)),
                pl.BlockSpec(rbt.shape, lambda b, pt: (0, 0)),
                per_b((1, width)), per_b((1, width)), per_b((1, width)), per_b((1, width)),
                per_b((1, LANES)), hbm, hbm, hbm, hbm]
    grid_spec = pltpu.PrefetchScalarGridSpec(
        num_scalar_prefetch=1,
        grid=(db,),
        in_specs=in_specs,
        out_specs=[per_b((H_A, width)), per_b((H_B, width))],
        scratch_shapes=[pltpu.VMEM((2, len(caches), pps, width, PAGE_SIZE), F32), pltpu.SemaphoreType.DMA((2,)),
                        pltpu.VMEM((H_A, 1), F32), pltpu.VMEM((H_A, 1), F32), pltpu.VMEM((H_A, width), F32),
                        pltpu.VMEM((H_B, 1), F32), pltpu.VMEM((H_B, 1), F32), pltpu.VMEM((H_B, width), F32)],
    )
    return pl.pallas_call(
        functools.partial(_sattn_kernel, pps=pps, past=past, layer=layer),
        grid_spec=grid_spec,
        out_shape=[jax.ShapeDtypeStruct((db, H_A, width), F32), jax.ShapeDtypeStruct((db, H_B, width), F32)],
        compiler_params=_cparams(("arbitrary",)),
        name="sample_attention",
    )(page_table, qa_blk, qb_blk, dec, mb, rbt, new['ka'], new['va'], new['kb'], new['vb'], new['mbn'], *caches)


def _head_cols(w, perm):
    return jnp.concatenate([w[:, h * HEAD_DIM:(h + 1) * HEAD_DIM] for h in perm], axis=1)


def _pad_cols(w, n):
    return jnp.pad(w, ((0, 0), (0, n - w.shape[1])))


def _split_w_in(w_in_l):
    sizes = (H_A * HEAD_DIM, KV_A * HEAD_DIM, KV_A * HEAD_DIM, H_A, H_B * HEAD_DIM, KV_B * HEAD_DIM,
             KV_B * HEAD_DIM, H_IDX * D_IDX, D_IDX, H_IDX)
    offs = np.cumsum((0,) + sizes)
    return [w_in_l[:, offs[n]:offs[n + 1]] for n in range(len(sizes))]


def _prep_proj_weights(w_in_l, perm):
    qa, ka, va, fa, qb, kb, vb, qi, ki, wi = _split_w_in(w_in_l)
    scale = HEAD_DIM ** -0.5
    big = jnp.concatenate([_head_cols(qa, perm) * scale, _head_cols(qb, perm) * scale, ka, va, kb, vb],
                          axis=1).astype(BF16)
    small = _pad_cols(jnp.concatenate([fa, wi], axis=1), LANES)
    idx = jnp.concatenate([qi * (D_IDX ** -0.5), ki, ki, small], axis=1)
    idx_hi, idx_lo = _split2(idx)
    return dict(big=big, idx_hi=idx_hi, idx_lo=idx_lo)


def _prep_proj_t_weights(w_in_l, perm):
    qa, ka, va, fa, qb, kb, vb, qi, ki, wi = _split_w_in(w_in_l)
    d = w_in_l.shape[0]
    scale = HEAD_DIM ** -0.5
    t_big = jnp.concatenate([_head_cols(qa, perm) * scale, _head_cols(qb, perm) * scale, va, vb, ka, kb],
                            axis=1).T.astype(BF16)
    zero = jnp.zeros((d, HEAD_DIM), w_in_l.dtype)
    spread = []
    for m in range(KV_A // 2):
        spread += [ka[:, 2 * m * HEAD_DIM:(2 * m + 1) * HEAD_DIM], zero, zero,
                   ka[:, (2 * m + 1) * HEAD_DIM:(2 * m + 2) * HEAD_DIM]]
    n_keys = jnp.concatenate(spread + [kb], axis=1).astype(BF16)
    small = _pad_cols(jnp.concatenate([fa, wi], axis=1), LANES)
    n_idx_hi, n_idx_lo = _split2(jnp.concatenate([ki, ki, small], axis=1))
    t_idx_hi, t_idx_lo = _split2((qi * (D_IDX ** -0.5)).T)
    return dict(t_big=t_big, n_keys=n_keys, n_idx_hi=n_idx_hi, n_idx_lo=n_idx_lo,
                t_idx_hi=t_idx_hi, t_idx_lo=t_idx_lo)


def _prep_post_weights(w_out_l, perm_a, perm_b, ln1_g, ln1_b, w_up_l, w_down_l, ln2_g, ln2_b, ff_chunk):
    d_mix, d = w_out_l.shape
    na = H_A * HEAD_DIM
    rows = lambda w, perm: jnp.concatenate([w[h * HEAD_DIM:(h + 1) * HEAD_DIM] for h in perm], axis=0)
    d_ff = w_up_l.shape[1]
    nc = d_ff // ff_chunk
    row = lambda a: a.reshape(1, -1)
    return dict(out_a=rows(w_out_l[:na], perm_a).astype(BF16), out_b=rows(w_out_l[na:], perm_b).astype(BF16),
                ln1_g=row(ln1_g), ln1_b=row(ln1_b), ln2_g=row(ln2_g), ln2_b=row(ln2_b),
                up=w_up_l.reshape(d, nc, ff_chunk).transpose(1, 0, 2).astype(BF16),
                down=w_down_l.reshape(nc, ff_chunk, d).astype(BF16))


ATTN_BLOCK = 256
FF_CHUNK = 512
POST_ROWS = 512
SCAN_PAGES_PER_STEP = 32
ATTN_PAGES_PER_STEP = 32


def _prompt_layer(x2d, batch, seq, wproj, wpost, bf_pad, relb, rb_far, alpha):
    tq = min(ATTN_BLOCK, seq)
    p = _project_t(x2d, wproj, bf_pad, batch=batch, seq=seq, tk=tq)
    oa_t = _fox_attention(p['qat'], p['kaug'], p['vax'], batch, seq, tq)
    wi_t = p['wi'].reshape(batch, seq, H_IDX).transpose(0, 2, 1)
    ob_t = _dsa_attention(p, wi_t, relb, rb_far, batch, seq, tq, min(TOPK_MAX, seq // 4))
    y = _post_block(x2d, oa_t, ob_t, wpost, alpha, seq=seq)
    kv = lambda a: a.reshape(batch, KV_A, HEAD_DIM, seq).transpose(0, 3, 1, 2)
    return y, (kv(p['kat']), kv(p['vat']), p['logf'].reshape(batch, seq, H_A), kv(p['kbt']), kv(p['vbt']),
               p['ki'].reshape(batch, seq, D_IDX))


def _block_diag_q(q):
    db = q.shape[0]
    qh = q.reshape(db, H_A, 1, HEAD_DIM)
    kv_of_head = jnp.arange(H_A) // GROUP
    onehot = (kv_of_head[:, None] == jnp.arange(KV_A)[None, :])[None, :, :, None]
    return jnp.where(onehot, qh, jnp.zeros((), q.dtype)).reshape(db, H_A, KV_A * HEAD_DIM)


def _diag_heads(o_wide):
    db = o_wide.shape[0]
    o = o_wide.reshape(db, H_A, KV_A, HEAD_DIM)
    idx = (jnp.arange(H_A) // GROUP)[None, :, None, None]
    return jnp.take_along_axis(o, jnp.broadcast_to(idx, (db, H_A, 1, HEAD_DIM)), axis=2).reshape(db, -1)


def _sample_layer(x2d, page_table, caches, layer, wproj, wpost, bf_pad, rbt, alpha):
    db = x2d.shape[0]
    n_pages = page_table.shape[1]
    past = n_pages * PAGE_SIZE
    p = _project(x2d, wproj, bf_pad)
    q16 = jnp.concatenate([p['qi_hi'].reshape(db, H_IDX, D_IDX), p['qi_lo'].reshape(db, H_IDX, D_IDX)], axis=1)
    pps = min(ATTN_PAGES_PER_STEP, n_pages // 2)
    scan_pps = min(SCAN_PAGES_PER_STEP, n_pages // 2)
    scores, dec_c = _sample_scan(page_table, q16, p['wi'].reshape(db, H_IDX, 1), p['logf'].reshape(db, H_A, 1),
                                 caches['idx_k'], caches['logf_t'], layer, scan_pps, pps)
    q_f32 = p['qi_hi'].astype(F32) + p['qi_lo'].astype(F32)
    mb, mbn = _sample_select(scores.reshape(db, past), q_f32, jnp.tile(p['ki'], (1, H_IDX)), p['wi'],
                             min(TOPK_MAX, (past + 1) // 4))
    new = dict(ka=p['ka'].reshape(db, 1, -1), va=p['va'].reshape(db, 1, -1), kb=p['kb'].reshape(db, 1, -1),
               vb=p['vb'].reshape(db, 1, -1), mbn=mbn.reshape(db, 1, LANES))
    n_chunks = n_pages // pps
    oa_w, ob_w = _sample_attention(page_table, _block_diag_q(p['qa']), _block_diag_q(p['qb']), dec_c,
                                   mb.reshape(db, n_chunks, 1, pps * PAGE_SIZE), rbt, new,
                                   (caches['a_k'], caches['a_v'], caches['b_k'], caches['b_v']), layer, pps)
    y = _post_block(x2d, _diag_heads(oa_w).astype(BF16), _diag_heads(ob_w).astype(BF16), wpost, alpha)
    return y, (p['ka'], p['va'], p['logf'], p['kb'], p['vb'], p['ki'])


def kernel(x_prompt, x_sample, cache_a_k, cache_a_v, cache_a_logf, cache_b_k, cache_b_v, cache_idx_k,
           page_table, w_in, b_f, w_out, ln1_g, ln1_b, w_up, w_down, ln2_g, ln2_b, rel_bias):
    depth = w_in.shape[0]
    batch, seq, d_model = x_prompt.shape
    db, dec_seq, _ = x_sample.shape
    assert dec_seq == 1
    alpha = (2 * depth) ** 0.25
    natural = tuple(range(H_A))
    tq = min(ATTN_BLOCK, seq)

    kv_t = lambda c: c.transpose(0, 1, 3, 4, 2).reshape(c.shape[:2] + (-1, PAGE_SIZE))
    caches = dict(a_k=kv_t(cache_a_k), a_v=kv_t(cache_a_v), b_k=kv_t(cache_b_k), b_v=kv_t(cache_b_v),
                  idx_k=cache_idx_k.transpose(0, 1, 3, 2),
                  logf_t=cache_a_logf.astype(F32).transpose(0, 1, 3, 2))
    relb = _rel_bias_tiles(rel_bias[:, list(HEAD_PERM)], tq)
    rb_far = rel_bias[N_BUCKETS - 1, list(HEAD_PERM)]
    rbt = rel_bias.T

    xp = x_prompt.reshape(batch * seq, d_model)
    xs = x_sample.reshape(db, d_model)
    p_rows, s_rows = [], []
    for l in range(depth):
        bf_pad = _pad_cols(b_f[l].reshape(1, -1), LANES)
        post = lambda perm: _prep_post_weights(w_out[l], perm, perm, ln1_g[l], ln1_b[l], w_up[l], w_down[l],
                                               ln2_g[l], ln2_b[l], FF_CHUNK)
        xp, rows = _prompt_layer(xp, batch, seq, _prep_proj_t_weights(w_in[l], HEAD_PERM), post(HEAD_PERM),
                                 bf_pad, relb, rb_far, alpha)
        p_rows.append(rows)
        xs, rows = _sample_layer(xs, page_table, caches, l, _prep_proj_weights(w_in[l], natural),
                                 post(natural), bf_pad, rbt, alpha)
        s_rows.append(rows)

    stack = lambda rows, j: jnp.stack([r[j] for r in rows])
    sample = lambda j, tail: stack(s_rows, j).reshape((depth, db, dec_seq) + tail)
    kv = (KV_A, HEAD_DIM)
    return (xp.reshape(batch, seq, d_model), xs.reshape(db, dec_seq, d_model),
            stack(p_rows, 0), stack(p_rows, 1), stack(p_rows, 2), stack(p_rows, 3), stack(p_rows, 4),
            stack(p_rows, 5),
            sample(0, kv), sample(1, kv), sample(2, (H_A,)), sample(3, kv), sample(4, kv), sample(5, (D_IDX,)))
```

```python
import functools
import math

import numpy as np
import jax
import jax.numpy as jnp
from jax import lax
from jax.experimental import pallas as pl
from jax.experimental.pallas import tpu as pltpu

HEAD_DIM = 64
H_A = 8
KV_A = 4
H_B = 8
KV_B = 4
GROUP = 2
H_IDX = 8
D_IDX = 64
TOPK_MAX = 256
N_BUCKETS = 32
MAX_DISTANCE = 128
PAGE_SIZE = 128
LN_EPS = 1e-5

LANES = 128
VMEM_LIMIT_BYTES = 56 * 1024 * 1024

NEG_BIG = -1e30
INT_MIN = -(2 ** 31)
HALF_BITS = 16
I16_MIN = -(2 ** 15)

BF16 = jnp.bfloat16
F32 = jnp.float32

HEAD_PERM = (0, 2, 1, 3, 4, 6, 5, 7)


def _cparams(semantics):
    return pltpu.CompilerParams(dimension_semantics=semantics,
                                vmem_limit_bytes=VMEM_LIMIT_BYTES)


def _dot(a, b):
    return jnp.dot(a, b, preferred_element_type=F32)


def _dot_nt(a, b):
    return lax.dot_general(a, b, (((1,), (1,)), ((), ())), preferred_element_type=F32)


def _split2(x):
    hi = x.astype(BF16)
    lo = (x - hi.astype(F32)).astype(BF16)
    return hi, lo


def _split3(x):
    h1 = x.astype(BF16)
    r1 = x - h1.astype(F32)
    h2 = r1.astype(BF16)
    h3 = (r1 - h2.astype(F32)).astype(BF16)
    return h1, h2, h3


def _dot3(xh, xl, wh, wl):
    return _dot(xh, wh) + _dot(xl, wh) + _dot(xh, wl)


def _log_sigmoid(x):
    return jnp.minimum(x, 0.0) - jnp.log1p(jnp.exp(-jnp.abs(x)))


def _layer_norm(x, g, b):
    mu = jnp.mean(x, axis=-1, keepdims=True)
    xc = x - mu
    var = jnp.mean(xc * xc, axis=-1, keepdims=True)
    return xc * lax.rsqrt(var + LN_EPS) * g + b


def _rel_bucket(dist):
    max_exact = N_BUCKETS // 2
    d = jnp.maximum(dist, 1).astype(F32)
    large = max_exact + (jnp.log(d / max_exact) / math.log(MAX_DISTANCE / max_exact)
                         * (N_BUCKETS - max_exact)).astype(jnp.int32)
    large = jnp.minimum(large, N_BUCKETS - 1)
    return jnp.where(dist < max_exact, dist, large)


def _order_key(score):
    bits = pltpu.bitcast(score, jnp.int32)
    key = bits ^ (lax.shift_right_arithmetic(bits, 31) & jnp.int32(0x7FFFFFFF))
    return jnp.where(bits == jnp.int32(INT_MIN), 0, key)


def _proj_kernel(x_ref, wbig_ref, wih_ref, wil_ref, bf_ref,
                 qa_ref, qb_ref, ka_ref, va_ref, kb_ref, vb_ref, qih_ref, qil_ref, ki_ref, kih_ref, kil_ref,
                 wi_ref, logf_ref):
    x = x_ref[...]
    xh, xl = _split2(x)
    nq = H_A * HEAD_DIM
    nk = KV_A * HEAD_DIM
    qa_ref[...] = _dot(xh, wbig_ref[:, 0:nq]).astype(BF16)
    qb_ref[...] = _dot(xh, wbig_ref[:, nq:2 * nq]).astype(BF16)
    o = 2 * nq
    ka_ref[...] = _dot(xh, wbig_ref[:, o:o + nk])
    va_ref[...] = _dot(xh, wbig_ref[:, o + nk:o + 2 * nk])
    kb_ref[...] = _dot(xh, wbig_ref[:, o + 2 * nk:o + 3 * nk])
    vb_ref[...] = _dot(xh, wbig_ref[:, o + 3 * nk:o + 4 * nk])

    ni = H_IDX * D_IDX
    qi = _dot3(xh, xl, wih_ref[:, 0:ni], wil_ref[:, 0:ni])
    qh, ql = _split2(qi)
    qih_ref[...] = qh
    qil_ref[...] = ql
    rest = _dot3(xh, xl, wih_ref[:, ni:ni + 2 * LANES], wil_ref[:, ni:ni + 2 * LANES])
    ki2 = rest[:, 0:LANES]
    ki_ref[...] = ki2[:, 0:D_IDX]
    kh, kl = _split2(ki2)
    kih_ref[...] = kh
    kil_ref[...] = kl

    small = rest[:, LANES:2 * LANES]
    wi_ref[...] = small[:, H_A:H_A + H_IDX] * (H_IDX ** -0.5)
    logf = _log_sigmoid(small + bf_ref[...])
    logf_ref[...] = logf[:, 0:H_A]


def _project(x2d, wts, bf_pad):
    m, d = x2d.shape
    tm = min(512, m)
    assert m % tm == 0
    nq, nk, ni = H_A * HEAD_DIM, KV_A * HEAD_DIM, H_IDX * D_IDX
    full = lambda a: pl.BlockSpec(a.shape, lambda i: (0,) * a.ndim)
    row = lambda n: pl.BlockSpec((tm, n), lambda i: (i, 0))
    inputs = [x2d, wts['big'], wts['idx_hi'], wts['idx_lo'], bf_pad]
    out_shapes = [((m, nq), BF16), ((m, nq), BF16), ((m, nk), F32), ((m, nk), F32), ((m, nk), F32),
                  ((m, nk), F32), ((m, ni), BF16), ((m, ni), BF16), ((m, D_IDX), F32),
                  ((m, LANES), BF16), ((m, LANES), BF16), ((m, H_IDX), F32), ((m, H_A), F32)]
    outs = pl.pallas_call(
        _proj_kernel,
        grid=(m // tm,),
        in_specs=[row(d)] + [full(a) for a in inputs[1:]],
        out_specs=[row(s[0][1]) for s in out_shapes],
        out_shape=[jax.ShapeDtypeStruct(*s) for s in out_shapes],
        compiler_params=_cparams(("arbitrary",)),
        name="proj",
    )(*inputs)
    names = ['qa', 'qb', 'ka', 'va', 'kb', 'vb', 'qi_hi', 'qi_lo', 'ki', 'ki2_hi', 'ki2_lo', 'wi', 'logf']
    return dict(zip(names, outs))


def _proj_t_kernel(x_ref, wt_ref, wn_ref, wnh_ref, wnl_ref, wth_ref, wtl_ref, bf_ref, tri_ref,
                   p1_ref, p2_ref, p3_ref,
                   qat_ref, qbt_ref, vat_ref, vbt_ref, kat_ref, kbt_ref, vax_ref, vbx_ref,
                   kaug_ref, kbn_ref, qith_ref, qitl_ref, ki_ref, kih_ref, kil_ref, wi_ref, logf_ref,
                   carry_ref, *, tiles_per_seq, tk):
    x = x_ref[...]
    xh, xl = _split2(x)
    tm = x.shape[0]
    nq = H_A * HEAD_DIM
    nk = KV_A * HEAD_DIM

    qat_ref[...] = _dot_nt(wt_ref[0:nq, :], xh).astype(BF16)
    qbt_ref[...] = _dot_nt(wt_ref[nq:2 * nq, :], xh).astype(BF16)
    o = 2 * nq
    row = lax.broadcasted_iota(jnp.int32, (2 * HEAD_DIM, tm), 0)
    lo_rows = row < HEAD_DIM
    for t_ref, x_out_ref, r0 in ((vat_ref, vax_ref, o), (vbt_ref, vbx_ref, o + nk)):
        vt = _dot_nt(wt_ref[r0:r0 + nk, :], xh)
        t_ref[...] = vt
        for m in range(KV_A // 2):
            blk = vt[m * 2 * HEAD_DIM:(m + 1) * 2 * HEAD_DIM, :]
            lo = jnp.where(lo_rows, blk, 1.0).astype(BF16)
            hi = jnp.where(lo_rows, 1.0, blk).astype(BF16)
            for t in range(tm // tk):
                cols = slice(t * tk, (t + 1) * tk)
                x_out_ref[t, m * 4 * HEAD_DIM:m * 4 * HEAD_DIM + 2 * HEAD_DIM, :] = lo[:, cols]
                x_out_ref[t, m * 4 * HEAD_DIM + 2 * HEAD_DIM:(m + 1) * 4 * HEAD_DIM, :] = hi[:, cols]
    kat_ref[...] = _dot_nt(wt_ref[o + 2 * nk:o + 3 * nk, :], xh)
    kbt_ref[...] = _dot_nt(wt_ref[o + 3 * nk:o + 4 * nk, :], xh)

    kn = _dot(xh, wn_ref[...])
    kbn_ref[...] = kn[:, 2 * nk:3 * nk].astype(BF16)

    qit = (_dot_nt(wth_ref[...], xh) + _dot_nt(wth_ref[...], xl)) + _dot_nt(wtl_ref[...], xh)
    qh, ql = _split2(qit)
    qith_ref[...] = qh
    qitl_ref[...] = ql

    rest = _dot3(xh, xl, wnh_ref[...], wnl_ref[...])
    ki2 = rest[:, 0:LANES]
    ki_ref[...] = ki2[:, 0:D_IDX]
    kh, kl = _split2(ki2)
    kih_ref[...] = kh
    kil_ref[...] = kl
    small = rest[:, LANES:2 * LANES]
    wi_ref[...] = small[:, H_A:H_A + H_IDX] * (H_IDX ** -0.5)
    logf = _log_sigmoid(small + bf_ref[...])
    logf_ref[...] = logf[:, 0:H_A]

    @pl.when(pl.program_id(0) % tiles_per_seq == 0)
    def _():
        carry_ref[...] = jnp.zeros_like(carry_ref)

    h1, h2, h3 = _split3(logf)
    tri = tri_ref[...]
    c = (_dot(tri, h1) + _dot(tri, h2)) + _dot(tri, h3) + carry_ref[...]
    carry_ref[...] = c[tm - 1:tm, :]
    c1, c2, c3 = _split3(c)
    aug = (_dot(c1, p1_ref[...]) + _dot(c2, p2_ref[...])) + _dot(c3, p3_ref[...])
    kaug_ref[...] = (kn[:, 0:2 * nk] + aug).astype(BF16)


def _decay_placement():
    mats = np.zeros((3, LANES, 2 * KV_A * HEAD_DIM), np.float32)
    for m in range(KV_A // 2):
        for v in range(2 * GROUP):
            head = HEAD_PERM[4 * m + v]
            hi_variant, g = v % 2, v // 2
            base = m * 4 * HEAD_DIM + (2 * HEAD_DIM if hi_variant else HEAD_DIM)
            for t in range(3):
                mats[t, head, base + 3 * g + t] = -1.0
    return [jnp.asarray(mats[t], BF16) for t in range(3)]


def _project_t(x2d, wts, bf_pad, *, batch, seq, tk):
    m, d = x2d.shape
    tm = min(512, seq)
    assert seq % tm == 0 and tm % tk == 0
    tps = seq // tm
    nq, nk, ni = H_A * HEAD_DIM, KV_A * HEAD_DIM, H_IDX * D_IDX
    nkv = seq // tk
    full = lambda a: pl.BlockSpec(a.shape, lambda i: (0,) * a.ndim)
    row = lambda n: pl.BlockSpec((tm, n), lambda i: (i, 0))
    feat = lambda r: pl.BlockSpec((None, r, tm), lambda i: (i // tps, 0, i % tps))
    vx = pl.BlockSpec((None, tm // tk, 2 * nk, tk), lambda i: (i // tps, i % tps, 0, 0))
    tri = jnp.tril(jnp.ones((tm, tm), F32)).astype(BF16)
    inputs = [x2d, wts['t_big'], wts['n_keys'], wts['n_idx_hi'], wts['n_idx_lo'], wts['t_idx_hi'],
              wts['t_idx_lo'], bf_pad, tri] + _decay_placement()
    fshape = lambda r, dt: ((batch, r, seq), dt)
    outs = [
        (fshape(nq, BF16), feat(nq)), (fshape(nq, BF16), feat(nq)),
        (fshape(nk, F32), feat(nk)), (fshape(nk, F32), feat(nk)),
        (fshape(nk, F32), feat(nk)), (fshape(nk, F32), feat(nk)),
        (((batch, nkv, 2 * nk, tk), BF16), vx), (((batch, nkv, 2 * nk, tk), BF16), vx),
        (((m, 2 * nk), BF16), row(2 * nk)), (((m, nk), BF16), row(nk)),
        (fshape(ni, BF16), feat(ni)), (fshape(ni, BF16), feat(ni)),
        (((m, D_IDX), F32), row(D_IDX)), (((m, LANES), BF16), row(LANES)), (((m, LANES), BF16), row(LANES)),
        (((m, H_IDX), F32), row(H_IDX)), (((m, H_A), F32), row(H_A)),
    ]
    res = pl.pallas_call(
        functools.partial(_proj_t_kernel, tiles_per_seq=tps, tk=tk),
        grid=(m // tm,),
        in_specs=[row(d)] + [full(a) for a in inputs[1:]],
        out_specs=[o[1] for o in outs],
        out_shape=[jax.ShapeDtypeStruct(*o[0]) for o in outs],
        scratch_shapes=[pltpu.VMEM((1, LANES), F32)],
        compiler_params=_cparams(("arbitrary",)),
        name="proj_prompt",
    )(*inputs)
    names = ['qat', 'qbt', 'vat', 'vbt', 'kat', 'kbt', 'vax', 'vbx', 'kaug', 'kbn', 'qit_hi', 'qit_lo',
             'ki', 'ki2_hi', 'ki2_lo', 'wi', 'logf']
    return dict(zip(names, res))


def _post_kernel(x_ref, oa_ref, ob_ref, woa_ref, wob_ref, g1_ref, b1_ref, wup_ref, wdn_ref,
                 g2_ref, b2_ref, y_ref, acc_ref, *, alpha, n_chunks, feature_major):
    x = x_ref[...]
    if feature_major:
        tn = lambda a, w: lax.dot_general(a, w, (((0,), (0,)), ((), ())), preferred_element_type=F32)
        att = tn(oa_ref[...], woa_ref[...]) + tn(ob_ref[...], wob_ref[...])
    else:
        att = _dot(oa_ref[...], woa_ref[...]) + _dot(ob_ref[...], wob_ref[...])
    x1 = _layer_norm(alpha * x + att, g1_ref[...], b1_ref[...])
    x1b = x1.astype(BF16)
    acc_ref[...] = jnp.zeros_like(acc_ref)

    def body(c, carry):
        u = jnp.maximum(_dot(x1b, wup_ref[c]), 0.0)
        acc_ref[...] += _dot((u * u).astype(BF16), wdn_ref[c])
        return carry

    lax.fori_loop(0, n_chunks, body, 0)
    y_ref[...] = _layer_norm(alpha * x1 + acc_ref[...], g2_ref[...], b2_ref[...])


def _post_block(x2d, oa, ob, wts, alpha, seq=None):
    m, d = x2d.shape
    tm = min(POST_ROWS, m if seq is None else seq)
    n_chunks = wts['up'].shape[0]
    full = lambda a: pl.BlockSpec(a.shape, lambda i: (0,) * a.ndim, pipeline_mode=pl.Buffered(1))
    row = lambda n: pl.BlockSpec((tm, n), lambda i: (i, 0))
    if seq is None:
        o_spec = lambda a: row(a.shape[1])
    else:
        tps = seq // tm
        o_spec = lambda a: pl.BlockSpec((None, a.shape[1], tm), lambda i: (i // tps, 0, i % tps))
    weights = [wts['out_a'], wts['out_b'], wts['ln1_g'], wts['ln1_b'], wts['up'], wts['down'],
               wts['ln2_g'], wts['ln2_b']]
    return pl.pallas_call(
        functools.partial(_post_kernel, alpha=alpha, n_chunks=n_chunks, feature_major=seq is not None),
        grid=(m // tm,),
        in_specs=[row(d), o_spec(oa), o_spec(ob)] + [full(w) for w in weights],
        out_specs=row(d),
        out_shape=jax.ShapeDtypeStruct((m, d), F32),
        scratch_shapes=[pltpu.VMEM((tm, d), F32)],
        compiler_params=_cparams(("arbitrary",)),
        name="post_block",
    )(x2d, oa, ob, *weights)


def _variant_rows(g, hi_half, ones_from=None):
    row = lax.broadcasted_iota(jnp.int32, g.shape, 0)
    keep = (row >= HEAD_DIM) if hi_half else (row < HEAD_DIM)
    fill = 0.0
    if ones_from is not None:
        fill = jnp.where((row >= ones_from) & (row < ones_from + 3), 1.0, 0.0)
    return jnp.where(keep, g, fill).astype(BF16)


def _softmax_chunk(s, causal_q0, m_ref, p_ref, acc_ref, cols, acols):
    if causal_q0 is not None:
        key = lax.broadcasted_iota(jnp.int32, s.shape, 0)
        qry = lax.broadcasted_iota(jnp.int32, s.shape, 1) + causal_q0
        s = jnp.where(key <= qry, s, NEG_BIG)
    m_old = m_ref[:, cols]
    m_new = jnp.maximum(m_old, jnp.max(s, axis=0, keepdims=True))
    m_ref[:, cols] = m_new
    p_ref[:, cols] = jnp.exp(s - m_new[0:1, :]).astype(BF16)
    acc_ref[:, acols] = acc_ref[:, acols] * jnp.exp(m_old - m_new)[0:1, :]


def _finish_pair(acc_lo, acc_hi, tq):
    row = lax.broadcasted_iota(jnp.int32, (2 * HEAD_DIM, tq), 0)
    lo_rows = row < HEAD_DIM
    out = []
    for g in range(GROUP):
        cols = slice(g * tq, (g + 1) * tq)
        a_lo, a_hi = acc_lo[:, cols], acc_hi[:, cols]
        o_lo = a_lo / a_lo[HEAD_DIM:HEAD_DIM + 1, :]
        o_hi = a_hi / a_hi[0:1, :]
        out.append(jnp.where(lo_rows, o_lo, o_hi))
    return jnp.concatenate(out, axis=0).astype(BF16)


def _fox_kernel(q_ref, k_ref, v_ref, o_ref, qt_ref, s_ref, p_ref, m_ref, acc_ref, *, tq):
    i = pl.program_id(2)
    q = q_ref[...].astype(F32)
    g0, g1 = q[0:2 * HEAD_DIM], q[2 * HEAD_DIM:4 * HEAD_DIM]
    qt_ref[0, :, 0:tq] = _variant_rows(g0, False, HEAD_DIM)
    qt_ref[0, :, tq:2 * tq] = _variant_rows(g1, False, HEAD_DIM + 3)
    qt_ref[1, :, 0:tq] = _variant_rows(g0, True, 0)
    qt_ref[1, :, tq:2 * tq] = _variant_rows(g1, True, 3)
    m_ref[...] = jnp.full_like(m_ref, NEG_BIG)
    acc_ref[...] = jnp.zeros_like(acc_ref)

    def process(blocks):
        for slot, (j, _) in enumerate(blocks):
            start = pl.multiple_of(j * tq, tq)
            for d in range(2):
                s_ref[slot, d] = _dot(k_ref[pl.ds(start, tq), d * LANES:(d + 1) * LANES], qt_ref[d])
        for slot, (j, causal) in enumerate(blocks):
            for d in range(2):
                for c in range(2 * tq // LANES):
                    cols = slice(c * LANES, (c + 1) * LANES)
                    q0 = (c * LANES) % tq if causal else None
                    _softmax_chunk(s_ref[slot, d, :, cols], q0, m_ref.at[d], p_ref.at[slot, d],
                                   acc_ref.at[d], cols, cols)
                acc_ref[d] += _dot(v_ref[j, d * LANES:(d + 1) * LANES, :], p_ref[slot, d])

    def pair_body(t, carry):
        process(((2 * t, False), (2 * t + 1, False)))
        return carry

    lax.fori_loop(0, i // 2, pair_body, 0)

    @pl.when(i % 2 == 1)
    def _():
        process(((i - 1, False), (i, True)))

    @pl.when(i % 2 == 0)
    def _():
        process(((i, True),))

    o_ref[...] = _finish_pair(acc_ref[0], acc_ref[1], tq)


def _fox_attention(qt, kaug, vx, batch, seq, tq):
    nq = seq // tq
    pair = 4 * HEAD_DIM
    return pl.pallas_call(
        functools.partial(_fox_kernel, tq=tq),
        grid=(batch, KV_A // 2, nq),
        in_specs=[
            pl.BlockSpec((None, pair, tq), lambda b, m, i: (b, m, i)),
            pl.BlockSpec((seq, pair), lambda b, m, i: (b, m)),
            pl.BlockSpec((None, nq, pair, tq), lambda b, m, i: (b, 0, m, 0)),
        ],
        out_specs=pl.BlockSpec((None, pair, tq), lambda b, m, i: (b, m, i)),
        out_shape=jax.ShapeDtypeStruct(qt.shape, BF16),
        scratch_shapes=[pltpu.VMEM((2, LANES, 2 * tq), BF16), pltpu.VMEM((2, 2, tq, 2 * tq), F32),
                        pltpu.VMEM((2, 2, tq, 2 * tq), BF16), pltpu.VMEM((2, 8, 2 * tq), F32),
                        pltpu.VMEM((2, LANES, 2 * tq), F32)],
        compiler_params=_cparams(("arbitrary", "arbitrary", "arbitrary")),
        name="fox_attention",
    )(qt, kaug, vx)


def _relb_kernel(rb_ref, o_ref, *, tq):
    delta = pl.program_id(0)
    h = pl.program_id(1)
    key = lax.broadcasted_iota(jnp.int32, (tq, tq), 0)
    qry = lax.broadcasted_iota(jnp.int32, (tq, tq), 1)
    bucket = _rel_bucket(jnp.maximum(delta * tq + qry - key, 0))
    acc = jnp.zeros((tq, tq), F32)
    for n in range(N_BUCKETS):
        acc = jnp.where(bucket == n, rb_ref[n, h], acc)
    o_ref[...] = acc


def _rel_bias_tiles(rb_perm, tq):
    return pl.pallas_call(
        functools.partial(_relb_kernel, tq=tq),
        grid=(2, H_B),
        in_specs=[pl.BlockSpec(memory_space=pltpu.SMEM)],
        out_specs=pl.BlockSpec((None, None, tq, tq), lambda d, h: (d, h, 0, 0)),
        out_shape=jax.ShapeDtypeStruct((2, H_B, tq, tq), F32),
        compiler_params=_cparams(("arbitrary", "arbitrary")),
        name="rel_bias_tiles",
    )(rb_perm)


def _strict_upper(n):
    r = lax.broadcasted_iota(jnp.int32, (n, n), 0)
    c = lax.broadcasted_iota(jnp.int32, (n, n), 1)
    return jnp.where(r < c, 1.0, 0.0).astype(BF16)


def _strict_lower(n):
    r = lax.broadcasted_iota(jnp.int32, (n, n), 0)
    c = lax.broadcasted_iota(jnp.int32, (n, n), 1)
    return jnp.where(c < r, 1.0, 0.0).astype(BF16)


KEY_ROWS = 64

DSA_VARIANT_ORDER = (0, 2, 1, 3)


def _dsa_kernel(qih_ref, qil_ref, kih_ref, wi_ref, qb_ref, k_ref, v_ref, relb_ref, rbfar_ref,
                o_ref, keys_ref, khi_ref, klo_ref, mb_ref, qi_ref, lg_ref, qt_ref, p_ref, m_ref, acc_ref,
                *, tq, topk):
    i = pl.program_id(1)
    nblk = i + 1
    nlc = tq // LANES

    for h in range(H_IDX):
        rows = slice((h // 2) * LANES, (h // 2 + 1) * LANES)
        cols = slice(h * tq, (h + 1) * tq)
        qi_ref[0:LANES, cols] = _variant_rows(qih_ref[rows, :].astype(F32), h % 2 == 1)
        qi_ref[LANES:2 * LANES, cols] = _variant_rows(qil_ref[rows, :].astype(F32), h % 2 == 1)

    def score_block(j, diag):
        start = pl.multiple_of(j * tq, tq)
        kh = kih_ref[pl.ds(start, tq), :]
        lg_ref[...] = _dot(jnp.concatenate([kh, kh], axis=1), qi_ref[...])
        for c in range(nlc):
            for r in range(tq // KEY_ROWS):
                rows = slice(r * KEY_ROWS, (r + 1) * KEY_ROWS)
                sc = jnp.zeros((KEY_ROWS, LANES), F32)
                for h in range(H_IDX):
                    lane0 = h * tq + c * LANES
                    sc = sc + wi_ref[h:h + 1, c * LANES:(c + 1) * LANES] * jnp.maximum(
                        lg_ref[rows, lane0:lane0 + LANES], 0.0)
                key = _order_key(sc)
                if diag:
                    kidx = lax.broadcasted_iota(jnp.int32, key.shape, 0) + r * KEY_ROWS
                    qidx = lax.broadcasted_iota(jnp.int32, key.shape, 1) + c * LANES
                    key = jnp.where(kidx <= qidx, key, INT_MIN)
                keys_ref[j, rows, c * LANES:(c + 1) * LANES] = key
                khi_ref[j, rows, c * LANES:(c + 1) * LANES] = lax.shift_right_arithmetic(
                    key, HALF_BITS).astype(jnp.int16)

    def score_body(j, carry):
        score_block(j, False)
        return carry

    lax.fori_loop(0, i, score_body, 0)
    score_block(i, True)

    needs_select = (i + 1) * tq > topk

    @pl.when(jnp.logical_not(needs_select))
    def _():
        def zero_body(j, carry):
            mb_ref[j] = jnp.zeros((tq, tq), F32)
            return carry
        lax.fori_loop(0, nblk, zero_body, 0)

    @pl.when(needs_select)
    def _():
        npairs = (nblk + 1) // 2

        @pl.when(nblk % 2 == 1)
        def _():
            khi_ref[nblk] = jnp.full((tq, tq), I16_MIN, jnp.int16)
            klo_ref[nblk] = jnp.full((tq, tq), I16_MIN, jnp.int16)

        def count16(ref, pred_fn):
            def body(t, acc):
                for j in (2 * t, 2 * t + 1):
                    hit = jnp.where(pred_fn(ref[j]), jnp.int16(1), jnp.int16(0))
                    for r in range(tq // 16):
                        acc = acc + hit[r * 16:(r + 1) * 16, :]
                return acc
            acc = lax.fori_loop(0, npairs, body, jnp.zeros((16, tq), jnp.int16))
            return jnp.sum(acc.astype(F32), axis=0, keepdims=True)

        def search16(ref, want):
            def bit_body(b, t):
                cand = t + lax.shift_left(jnp.int32(1), HALF_BITS - 1 - b)
                cand16 = cand.astype(jnp.int16)
                return jnp.where(count16(ref, lambda k: k >= cand16) >= want, cand, t)
            return lax.fori_loop(0, HALF_BITS, bit_body, jnp.full((1, tq), I16_MIN, jnp.int32))

        thr_hi = search16(khi_ref, topk)
        thr_hi16 = thr_hi.astype(jnp.int16)
        above = count16(khi_ref, lambda k: k > thr_hi16)

        def low_body(j, carry):
            kblk = keys_ref[j]
            in_band = lax.shift_right_arithmetic(kblk, HALF_BITS) == thr_hi
            low = (kblk & jnp.int32(2 ** HALF_BITS - 1)) + I16_MIN
            klo_ref[j] = jnp.where(in_band, low, I16_MIN).astype(jnp.int16)
            return carry

        lax.fori_loop(0, nblk, low_body, 0)
        thr_lo = search16(klo_ref, topk - above)
        thr_lo16 = thr_lo.astype(jnp.int16)
        thr = thr_hi * (2 ** HALF_BITS) + (thr_lo - I16_MIN)
        need = topk - (above + count16(klo_ref, lambda k: k > thr_lo16))
        n_eq = count16(klo_ref, lambda k: k == thr_lo16)
        tie_break = jnp.max(jnp.where(n_eq > need, 1.0, 0.0)) > 0.0

        @pl.when(jnp.logical_not(tie_break))
        def _():
            def plain_body(j, carry):
                mb_ref[j] = jnp.where(keys_ref[j] >= thr, 0.0, NEG_BIG)
                return carry
            lax.fori_loop(0, nblk, plain_body, 0)

        @pl.when(tie_break)
        def _():
            earlier = _strict_lower(tq)

            def mask_body(j, seen):
                kblk = keys_ref[j]
                eq = kblk == thr
                eqf = jnp.where(eq, 1.0, 0.0)
                rank = _dot(earlier, eqf.astype(BF16)) + seen
                mb_ref[j] = jnp.where(kblk > thr, 0.0,
                                      jnp.where(eq, jnp.where(rank < need, 0.0, NEG_BIG), NEG_BIG))
                return seen + jnp.sum(eqf, axis=0, keepdims=True)

            lax.fori_loop(0, nblk, mask_body, jnp.zeros((1, tq), F32))

    pair = 4 * HEAD_DIM
    for m in range(KV_B // 2):
        q = qb_ref[m * pair:(m + 1) * pair, :].astype(F32)
        g = (q[0:2 * HEAD_DIM], q[2 * HEAD_DIM:4 * HEAD_DIM])
        for n, v in enumerate(DSA_VARIANT_ORDER):
            qt_ref[:, n * tq:(n + 1) * tq] = _variant_rows(g[v // 2], v % 2 == 1)
        m_ref[...] = jnp.full_like(m_ref, NEG_BIG)
        acc_ref[...] = jnp.zeros_like(acc_ref)

        far = lambda slot, qcols: rbfar_ref[slot]
        near = lambda slot, qcols: relb_ref[1, slot, :, qcols]
        diag = lambda slot, qcols: relb_ref[0, slot, :, qcols]

        def process(blocks):
            for slot, (j, _) in enumerate(blocks):
                start = pl.multiple_of(j * tq, tq)
                lg_ref[:, slot * 4 * tq:(slot + 1) * 4 * tq] = _dot(
                    k_ref[pl.ds(start, tq), m * LANES:(m + 1) * LANES], qt_ref[...])
            for slot, (j, rel) in enumerate(blocks):
                for n, v in enumerate(DSA_VARIANT_ORDER):
                    for c in range(nlc):
                        qcols = slice(c * LANES, (c + 1) * LANES)
                        cols = slice(n * tq + c * LANES, n * tq + (c + 1) * LANES)
                        lcols = slice(slot * 4 * tq + cols.start, slot * 4 * tq + cols.stop)
                        s = lg_ref[:, lcols] + (mb_ref[j, :, qcols] + rel(4 * m + v, qcols))
                        acols = slice((n % 2) * tq + c * LANES, (n % 2) * tq + (c + 1) * LANES)
                        _softmax_chunk(s, c * LANES if rel is diag else None, m_ref, p_ref.at[slot],
                                       acc_ref.at[n // 2], cols, acols)
                for d in range(2):
                    vt = v_ref[j, m * pair + d * LANES:m * pair + (d + 1) * LANES, :]
                    acc_ref[d] += _dot(vt, p_ref[slot, :, d * 2 * tq:(d + 1) * 2 * tq])

        n_far = jnp.maximum(i - 1, 0)

        def far_body(t, carry):
            process(((2 * t, far), (2 * t + 1, far)))
            return carry

        lax.fori_loop(0, n_far // 2, far_body, 0)

        @pl.when(n_far % 2 == 1)
        def _():
            process(((i - 2, far),))

        @pl.when(i >= 1)
        def _():
            process(((i - 1, near), (i, diag)))

        @pl.when(i == 0)
        def _():
            process(((i, diag),))

        o_ref[m * pair:(m + 1) * pair, :] = _finish_pair(acc_ref[0], acc_ref[1], tq)


def _dsa_attention(p, wi_t, relb, rb_far, batch, seq, tq, topk):
    nq = seq // tq
    nf = H_B * HEAD_DIM
    qcol = lambda r: pl.BlockSpec((None, r, tq), lambda b, i: (b, 0, i))
    seqblk = lambda n: pl.BlockSpec((seq, n), lambda b, i: (b, 0))
    return pl.pallas_call(
        functools.partial(_dsa_kernel, tq=tq, topk=topk),
        grid=(batch, nq),
        in_specs=[qcol(H_IDX * D_IDX), qcol(H_IDX * D_IDX), seqblk(LANES), qcol(H_IDX),
                  qcol(nf), seqblk(KV_B * HEAD_DIM),
                  pl.BlockSpec((None, nq, nf, tq), lambda b, i: (b, 0, 0, 0)),
                  pl.BlockSpec(relb.shape, lambda b, i: (0, 0, 0, 0)),
                  pl.BlockSpec(memory_space=pltpu.SMEM)],
        out_specs=qcol(nf),
        out_shape=jax.ShapeDtypeStruct((batch, nf, seq), BF16),
        scratch_shapes=[pltpu.VMEM((nq, tq, tq), jnp.int32), pltpu.VMEM((nq, tq, tq), jnp.int16),
                        pltpu.VMEM((nq, tq, tq), jnp.int16), pltpu.VMEM((nq, tq, tq), F32),
                        pltpu.VMEM((2 * LANES, H_IDX * tq), BF16), pltpu.VMEM((tq, H_IDX * tq), F32),
                        pltpu.VMEM((LANES, 4 * tq), BF16), pltpu.VMEM((2, tq, 4 * tq), BF16),
                        pltpu.VMEM((8, 4 * tq), F32), pltpu.VMEM((2, LANES, 2 * tq), F32)],
        compiler_params=_cparams(("arbitrary", "arbitrary")),
        name="dsa_attention",
    )(p['qit_hi'], p['qit_lo'], p['ki2_hi'], wi_t, p['qbt'], p['kbn'], p['vbx'], relb, rb_far)


def _sscan_kernel(pt_ref, q16_ref, wi_ref, lnew_ref, idx_hbm, logf_hbm, sc_ref, dec_ref,
                  kbuf, fbuf, sem, carry_ref, *, pps, n_chunks, attn_pps, layer):
    b = pl.program_id(0)
    n_rows = pl.num_programs(0)

    def copies(row, step, slot):
        chunk = n_chunks - 1 - step
        out = []
        for r in range(pps):
            page = pt_ref[row, chunk * pps + r]
            out.append(pltpu.make_async_copy(idx_hbm.at[layer, page], kbuf.at[slot, r], sem.at[slot]))
            out.append(pltpu.make_async_copy(logf_hbm.at[layer, page], fbuf.at[slot, r], sem.at[slot]))
        return out

    def start(row, step, slot):
        for n, c in enumerate(copies(row, step, slot)):
            c.start(priority=n % 2)

    def wait(row, step, slot):
        for c in copies(row, step, slot):
            c.wait()

    @pl.when(b == 0)
    def _():
        start(0, 0, 0)

    carry_ref[...] = jnp.broadcast_to(lnew_ref[...], carry_ref.shape)
    q16 = q16_ref[...]
    qh = q16[0:H_IDX]
    wi = wi_ref[...]
    r_i = lax.broadcasted_iota(jnp.int32, (PAGE_SIZE, PAGE_SIZE), 0)
    c_i = lax.broadcasted_iota(jnp.int32, (PAGE_SIZE, PAGE_SIZE), 1)
    later = jnp.where(r_i > c_i, 1.0, 0.0).astype(BF16)
    per = pps // attn_pps

    def compute(step, slot):
        chunk = n_chunks - 1 - step
        x_all = jnp.concatenate([fbuf[slot, r] for r in range(pps)], axis=0)
        h1, h2, h3 = _split3(x_all)
        within = (_dot(h1, later) + _dot(h2, later)) + _dot(h3, later)
        totals = jnp.sum(x_all, axis=1, keepdims=True)
        carry = carry_ref[...]
        for r in reversed(range(pps)):
            rows = slice(r * H_A, (r + 1) * H_A)
            lanes = slice((r % attn_pps) * PAGE_SIZE, (r % attn_pps + 1) * PAGE_SIZE)
            dec_ref[chunk * per + r // attn_pps, :, lanes] = within[rows, :] + carry[:, 0:1]
            carry = carry + totals[rows, :]
        carry_ref[...] = carry
        k_all = jnp.concatenate([kbuf[slot, r] for r in range(pps)], axis=1)
        kh, kl = _split2(k_all)
        a = _dot(q16, kh)
        logits = (a[0:H_IDX] + a[H_IDX:2 * H_IDX]) + _dot(qh, kl)
        sc_ref[chunk] = jnp.sum(wi * jnp.maximum(logits, 0.0), axis=0, keepdims=True)

    def pair_body(sp, carry):
        s0 = 2 * sp
        start(b, s0 + 1, 1)
        wait(b, s0, 0)
        compute(s0, 0)

        @pl.when(s0 + 2 < n_chunks)
        def _():
            start(b, s0 + 2, 0)

        @pl.when(jnp.logical_and(s0 + 2 == n_chunks, b + 1 < n_rows))
        def _():
            start(b + 1, 0, 0)

        wait(b, s0 + 1, 1)
        compute(s0 + 1, 1)
        return carry

    lax.fori_loop(0, n_chunks // 2, pair_body, 0)


def _sample_scan(page_table, q16, wi_col, logf_new_col, cache_idx_k, logf_t_cache, layer, pps, attn_pps):
    db, n_pages = page_table.shape
    n_chunks = n_pages // pps
    assert n_chunks % 2 == 0 and n_chunks * pps == n_pages and pps % attn_pps == 0
    n_attn = n_pages // attn_pps
    per_b = lambda shape: pl.BlockSpec((None,) + shape, lambda b, pt: (b,) + (0,) * len(shape))
    hbm = pl.BlockSpec(memory_space=pl.ANY)
    grid_spec = pltpu.PrefetchScalarGridSpec(
        num_scalar_prefetch=1,
        grid=(db,),
        in_specs=[per_b((2 * H_IDX, D_IDX)), per_b((H_IDX, 1)), per_b((H_A, 1)), hbm, hbm],
        out_specs=[per_b((n_chunks, 1, pps * PAGE_SIZE)), per_b((n_attn, H_A, attn_pps * PAGE_SIZE))],
        scratch_shapes=[pltpu.VMEM((2, pps, D_IDX, PAGE_SIZE), F32), pltpu.VMEM((2, pps, H_A, PAGE_SIZE), F32),
                        pltpu.SemaphoreType.DMA((2,)), pltpu.VMEM((H_A, LANES), F32)],
    )
    return pl.pallas_call(
        functools.partial(_sscan_kernel, pps=pps, n_chunks=n_chunks, attn_pps=attn_pps, layer=layer),
        grid_spec=grid_spec,
        out_shape=[jax.ShapeDtypeStruct((db, n_chunks, 1, pps * PAGE_SIZE), F32),
                   jax.ShapeDtypeStruct((db, n_attn, H_A, attn_pps * PAGE_SIZE), F32)],
        compiler_params=_cparams(("arbitrary",)),
        name="sample_scan",
    )(page_table, q16, wi_col, logf_new_col, cache_idx_k, logf_t_cache)


def _sselect_kernel(sc_ref, q_ref, kt_ref, wi_ref, mb_ref, mbn_ref, keys_ref, *, topk, chunk):
    db, past = sc_ref.shape
    prod = q_ref[...] * kt_ref[...]
    lane = lax.broadcasted_iota(jnp.int32, prod.shape, 1)
    wi = wi_ref[...]
    sc_new = jnp.zeros((db, 1), F32)
    for h in range(H_IDX):
        seg = (lane >= h * D_IDX) & (lane < (h + 1) * D_IDX)
        logit = jnp.sum(jnp.where(seg, prod, 0.0), axis=1, keepdims=True)
        sc_new = sc_new + wi[:, h:h + 1] * jnp.maximum(logit, 0.0)
    key_new = _order_key(sc_new)
    keys_ref[...] = _order_key(sc_ref[...])

    def count(pred_fn):
        n = jnp.sum(jnp.where(pred_fn(keys_ref[...]), 1.0, 0.0), axis=1, keepdims=True)
        return n + jnp.where(pred_fn(key_new), 1.0, 0.0)

    def bit_body(b, t):
        cand = t + lax.shift_left(jnp.int32(1), 31 - b)
        return jnp.where(count(lambda k: k >= cand) >= topk, cand, t)

    thr = lax.fori_loop(0, 32, bit_body, jnp.full((db, 1), INT_MIN, jnp.int32))
    need = topk - count(lambda k: k > thr)
    sut = _strict_upper(chunk)
    seen = jnp.zeros((db, 1), F32)
    for c in range(past // chunk):
        cols = slice(c * chunk, (c + 1) * chunk)
        kblk = keys_ref[:, cols]
        eq = kblk == thr
        eqf = jnp.where(eq, 1.0, 0.0)
        rank = _dot(eqf.astype(BF16), sut) + seen
        mb_ref[:, cols] = jnp.where(kblk > thr, 0.0,
                                    jnp.where(eq, jnp.where(rank < need, 0.0, NEG_BIG), NEG_BIG))
        seen = seen + jnp.sum(eqf, axis=1, keepdims=True)
    sel_new = jnp.where(key_new > thr, 0.0,
                        jnp.where(key_new == thr, jnp.where(seen < need, 0.0, NEG_BIG), NEG_BIG))
    mbn_ref[...] = jnp.broadcast_to(sel_new, mbn_ref.shape)


def _sample_select(scores, q_f32, ki_tiled, wi, topk):
    db, past = scores.shape
    return pl.pallas_call(
        functools.partial(_sselect_kernel, topk=topk, chunk=2 * LANES),
        out_shape=[jax.ShapeDtypeStruct((db, past), F32), jax.ShapeDtypeStruct((db, LANES), F32)],
        scratch_shapes=[pltpu.VMEM((db, past), jnp.int32)],
        compiler_params=pltpu.CompilerParams(vmem_limit_bytes=VMEM_LIMIT_BYTES),
        name="sample_select",
    )(scores, q_f32, ki_tiled, wi)


def _sattn_kernel(pt_ref, qa_ref, qb_ref, dec_ref, mb_ref, rbt_ref, kan_ref, van_ref, kbn_ref, vbn_ref,
                  mbn_ref, ak_hbm, av_hbm, bk_hbm, bv_hbm, oa_ref, ob_ref,
                  buf, sem, ma_ref, la_ref, acca_ref, mb_m_ref, lb_ref, accb_ref, *, pps, past, layer):
    b = pl.program_id(0)
    n_rows = pl.num_programs(0)
    width = pps * PAGE_SIZE
    n_chunks = past // width
    caches = (ak_hbm, av_hbm, bk_hbm, bv_hbm)

    def page_copy(row, chunk, slot, t, r):
        page = pt_ref[row, chunk * pps + r]
        return pltpu.make_async_copy(caches[t].at[layer, page], buf.at[slot, t, r], sem.at[slot])

    def start_chunk(row, chunk, slot):
        for t in range(len(caches)):
            for r in range(pps):
                page_copy(row, chunk, slot, t, r).start(priority=r % 2)

    def wait_chunk(row, chunk, slot):
        for t in range(len(caches)):
            for r in range(pps):
                page_copy(row, chunk, slot, t, r).wait()

    @pl.when(b == 0)
    def _():
        start_chunk(0, 0, 0)

    for m_ref, l_ref, acc_ref in ((ma_ref, la_ref, acca_ref), (mb_m_ref, lb_ref, accb_ref)):
        m_ref[...] = jnp.full_like(m_ref, NEG_BIG)
        l_ref[...] = jnp.zeros_like(l_ref)
        acc_ref[...] = jnp.zeros_like(acc_ref)

    qa, qb = qa_ref[...], qb_ref[...]
    rbt = rbt_ref[...]

    def update(s, slot, t, m_ref, l_ref, acc_ref):
        m_old = m_ref[...]
        m_new = jnp.maximum(m_old, jnp.max(s, axis=1, keepdims=True))
        a = jnp.exp(m_old - m_new)
        p = jnp.exp(s - m_new)
        l_ref[...] = a * l_ref[...] + jnp.sum(p, axis=1, keepdims=True)
        pv = jnp.zeros(acc_ref.shape, F32)
        for r in range(pps):
            pv = pv + _dot_nt(p[:, r * PAGE_SIZE:(r + 1) * PAGE_SIZE].astype(BF16),
                              buf[slot, t, r].astype(BF16))
        acc_ref[...] = a * acc_ref[...] + pv
        m_ref[...] = m_new

    def compute(chunk, slot):
        s_a = jnp.concatenate([_dot(qa, buf[slot, 0, r].astype(BF16)) for r in range(pps)], axis=1)
        update(s_a + dec_ref[chunk], slot, 1, ma_ref, la_ref, acca_ref)
        pos = chunk * width + lax.broadcasted_iota(jnp.int32, (1, width), 1)
        bucket = _rel_bucket(past - pos)
        rel = jnp.zeros((H_B, width), F32)
        for n in range(N_BUCKETS):
            rel = jnp.where(bucket == n, rbt[:, n:n + 1], rel)
        s_b = jnp.concatenate([_dot(qb, buf[slot, 2, r].astype(BF16)) for r in range(pps)], axis=1)
        update(s_b + rel + mb_ref[chunk], slot, 3, mb_m_ref, lb_ref, accb_ref)

    def pair_body(cp, carry):
        c0 = 2 * cp
        start_chunk(b, c0 + 1, 1)
        wait_chunk(b, c0, 0)
        compute(c0, 0)

        @pl.when(c0 + 2 < n_chunks)
        def _():
            start_chunk(b, c0 + 2, 0)

        @pl.when(jnp.logical_and(c0 + 2 == n_chunks, b + 1 < n_rows))
        def _():
            start_chunk(b + 1, 0, 0)

        wait_chunk(b, c0 + 1, 1)
        compute(c0 + 1, 1)
        return carry

    lax.fori_loop(0, n_chunks // 2, pair_body, 0)

    def finish(q, kn, vn, extra, m_ref, l_ref, acc_ref, o_ref):
        s = jnp.sum(q.astype(F32) * kn, axis=1, keepdims=True) + extra
        m_old = m_ref[...]
        m_new = jnp.maximum(m_old, s)
        a = jnp.exp(m_old - m_new)
        p = jnp.exp(s - m_new)
        l = a * l_ref[...] + p
        o_ref[...] = (a * acc_ref[...] + p * vn) / l

    finish(qa, kan_ref[...], van_ref[...], 0.0, ma_ref, la_ref, acca_ref, oa_ref)
    finish(qb, kbn_ref[...], vbn_ref[...], rbt[:, 0:1] + mbn_ref[:, 0:1], mb_m_ref, lb_ref, accb_ref, ob_ref)


def _sample_attention(page_table, qa_blk, qb_blk, dec, mb, rbt, new, caches, layer, pps):
    db, n_pages = page_table.shape
    past = n_pages * PAGE_SIZE
    width = KV_A * HEAD_DIM
    n_chunks = n_pages // pps
    assert n_chunks % 2 == 0 and n_chunks * pps == n_pages
    per_b = lambda shape: pl.BlockSpec((None,) + shape, lambda b, pt: (b,) + (0,) * len(shape))
    hbm = pl.BlockSpec(memory_space=pl.ANY)
    in_specs = [per_b((H_A, width)), per_b((H_B, width)),
                per_b((n_chunks, H_A, pps * PAGE_SIZE)), per_b((n_chunks, 1, pps * PAGE_SIZE)),
                pl.BlockSpec(rbt.shape, lambda b, pt: (0, 0)),
                per_b((1, width)), per_b((1, width)), per_b((1, width)), per_b((1, width)),
                per_b((1, LANES)), hbm, hbm, hbm, hbm]
    grid_spec = pltpu.PrefetchScalarGridSpec(
        num_scalar_prefetch=1,
        grid=(db,),
        in_specs=in_specs,
        out_specs=[per_b((H_A, width)), per_b((H_B, width))],
        scratch_shapes=[pltpu.VMEM((2, len(caches), pps, width, PAGE_SIZE), F32), pltpu.SemaphoreType.DMA((2,)),
                        pltpu.VMEM((H_A, 1), F32), pltpu.VMEM((H_A, 1), F32), pltpu.VMEM((H_A, width), F32),
                        pltpu.VMEM((H_B, 1), F32), pltpu.VMEM((H_B, 1), F32), pltpu.VMEM((H_B, width), F32)],
    )
    return pl.pallas_call(
        functools.partial(_sattn_kernel, pps=pps, past=past, layer=layer),
        grid_spec=grid_spec,
        out_shape=[jax.ShapeDtypeStruct((db, H_A, width), F32), jax.ShapeDtypeStruct((db, H_B, width), F32)],
        compiler_params=_cparams(("arbitrary",)),
        name="sample_attention",
    )(page_table, qa_blk, qb_blk, dec, mb, rbt, new['ka'], new['va'], new['kb'], new['vb'], new['mbn'], *caches)


def _head_cols(w, perm):
    return jnp.concatenate([w[:, h * HEAD_DIM:(h + 1) * HEAD_DIM] for h in perm], axis=1)


def _pad_cols(w, n):
    return jnp.pad(w, ((0, 0), (0, n - w.shape[1])))


def _split_w_in(w_in_l):
    sizes = (H_A * HEAD_DIM, KV_A * HEAD_DIM, KV_A * HEAD_DIM, H_A, H_B * HEAD_DIM, KV_B * HEAD_DIM,
             KV_B * HEAD_DIM, H_IDX * D_IDX, D_IDX, H_IDX)
    offs = np.cumsum((0,) + sizes)
    return [w_in_l[:, offs[n]:offs[n + 1]] for n in range(len(sizes))]


def _prep_proj_weights(w_in_l, perm):
    qa, ka, va, fa, qb, kb, vb, qi, ki, wi = _split_w_in(w_in_l)
    scale = HEAD_DIM ** -0.5
    big = jnp.concatenate([_head_cols(qa, perm) * scale, _head_cols(qb, perm) * scale, ka, va, kb, vb],
                          axis=1).astype(BF16)
    small = _pad_cols(jnp.concatenate([fa, wi], axis=1), LANES)
    idx = jnp.concatenate([qi * (D_IDX ** -0.5), ki, ki, small], axis=1)
    idx_hi, idx_lo = _split2(idx)
    return dict(big=big, idx_hi=idx_hi, idx_lo=idx_lo)


def _prep_proj_t_weights(w_in_l, perm):
    qa, ka, va, fa, qb, kb, vb, qi, ki, wi = _split_w_in(w_in_l)
    d = w_in_l.shape[0]
    scale = HEAD_DIM ** -0.5
    t_big = jnp.concatenate([_head_cols(qa, perm) * scale, _head_cols(qb, perm) * scale, va, vb, ka, kb],
                            axis=1).T.astype(BF16)
    zero = jnp.zeros((d, HEAD_DIM), w_in_l.dtype)
    spread = []
    for m in range(KV_A // 2):
        spread += [ka[:, 2 * m * HEAD_DIM:(2 * m + 1) * HEAD_DIM], zero, zero,
                   ka[:, (2 * m + 1) * HEAD_DIM:(2 * m + 2) * HEAD_DIM]]
    n_keys = jnp.concatenate(spread + [kb], axis=1).astype(BF16)
    small = _pad_cols(jnp.concatenate([fa, wi], axis=1), LANES)
    n_idx_hi, n_idx_lo = _split2(jnp.concatenate([ki, ki, small], axis=1))
    t_idx_hi, t_idx_lo = _split2((qi * (D_IDX ** -0.5)).T)
    return dict(t_big=t_big, n_keys=n_keys, n_idx_hi=n_idx_hi, n_idx_lo=n_idx_lo,
                t_idx_hi=t_idx_hi, t_idx_lo=t_idx_lo)


def _prep_post_weights(w_out_l, perm_a, perm_b, ln1_g, ln1_b, w_up_l, w_down_l, ln2_g, ln2_b, ff_chunk):
    d_mix, d = w_out_l.shape
    na = H_A * HEAD_DIM
    rows = lambda w, perm: jnp.concatenate([w[h * HEAD_DIM:(h + 1) * HEAD_DIM] for h in perm], axis=0)
    d_ff = w_up_l.shape[1]
    nc = d_ff // ff_chunk
    row = lambda a: a.reshape(1, -1)
    return dict(out_a=rows(w_out_l[:na], perm_a).astype(BF16), out_b=rows(w_out_l[na:], perm_b).astype(BF16),
                ln1_g=row(ln1_g), ln1_b=row(ln1_b), ln2_g=row(ln2_g), ln2_b=row(ln2_b),
                up=w_up_l.reshape(d, nc, ff_chunk).transpose(1, 0, 2).astype(BF16),
                down=w_down_l.reshape(nc, ff_chunk, d).astype(BF16))


ATTN_BLOCK = 256
FF_CHUNK = 512
POST_ROWS = 512
SCAN_PAGES_PER_STEP = 32
ATTN_PAGES_PER_STEP = 16


def _prompt_layer(x2d, batch, seq, wproj, wpost, bf_pad, relb, rb_far, alpha):
    tq = min(ATTN_BLOCK, seq)
    p = _project_t(x2d, wproj, bf_pad, batch=batch, seq=seq, tk=tq)
    oa_t = _fox_attention(p['qat'], p['kaug'], p['vax'], batch, seq, tq)
    wi_t = p['wi'].reshape(batch, seq, H_IDX).transpose(0, 2, 1)
    ob_t = _dsa_attention(p, wi_t, relb, rb_far, batch, seq, tq, min(TOPK_MAX, seq // 4))
    y = _post_block(x2d, oa_t, ob_t, wpost, alpha, seq=seq)
    kv = lambda a: a.reshape(batch, KV_A, HEAD_DIM, seq).transpose(0, 3, 1, 2)
    return y, (kv(p['kat']), kv(p['vat']), p['logf'].reshape(batch, seq, H_A), kv(p['kbt']), kv(p['vbt']),
               p['ki'].reshape(batch, seq, D_IDX))


def _block_diag_q(q):
    db = q.shape[0]
    qh = q.reshape(db, H_A, 1, HEAD_DIM)
    kv_of_head = jnp.arange(H_A) // GROUP
    onehot = (kv_of_head[:, None] == jnp.arange(KV_A)[None, :])[None, :, :, None]
    return jnp.where(onehot, qh, jnp.zeros((), q.dtype)).reshape(db, H_A, KV_A * HEAD_DIM)


def _diag_heads(o_wide):
    db = o_wide.shape[0]
    o = o_wide.reshape(db, H_A, KV_A, HEAD_DIM)
    idx = (jnp.arange(H_A) // GROUP)[None, :, None, None]
    return jnp.take_along_axis(o, jnp.broadcast_to(idx, (db, H_A, 1, HEAD_DIM)), axis=2).reshape(db, -1)


def _sample_layer(x2d, page_table, caches, layer, wproj, wpost, bf_pad, rbt, alpha):
    db = x2d.shape[0]
    n_pages = page_table.shape[1]
    past = n_pages * PAGE_SIZE
    p = _project(x2d, wproj, bf_pad)
    q16 = jnp.concatenate([p['qi_hi'].reshape(db, H_IDX, D_IDX), p['qi_lo'].reshape(db, H_IDX, D_IDX)], axis=1)
    pps = min(ATTN_PAGES_PER_STEP, n_pages // 2)
    scan_pps = min(SCAN_PAGES_PER_STEP, n_pages // 2)
    scores, dec_c = _sample_scan(page_table, q16, p['wi'].reshape(db, H_IDX, 1), p['logf'].reshape(db, H_A, 1),
                                 caches['idx_k'], caches['logf_t'], layer, scan_pps, pps)
    q_f32 = p['qi_hi'].astype(F32) + p['qi_lo'].astype(F32)
    mb, mbn = _sample_select(scores.reshape(db, past), q_f32, jnp.tile(p['ki'], (1, H_IDX)), p['wi'],
                             min(TOPK_MAX, (past + 1) // 4))
    new = dict(ka=p['ka'].reshape(db, 1, -1), va=p['va'].reshape(db, 1, -1), kb=p['kb'].reshape(db, 1, -1),
               vb=p['vb'].reshape(db, 1, -1), mbn=mbn.reshape(db, 1, LANES))
    n_chunks = n_pages // pps
    oa_w, ob_w = _sample_attention(page_table, _block_diag_q(p['qa']), _block_diag_q(p['qb']), dec_c,
                                   mb.reshape(db, n_chunks, 1, pps * PAGE_SIZE), rbt, new,
                                   (caches['a_k'], caches['a_v'], caches['b_k'], caches['b_v']), layer, pps)
    y = _post_block(x2d, _diag_heads(oa_w).astype(BF16), _diag_heads(ob_w).astype(BF16), wpost, alpha)
    return y, (p['ka'], p['va'], p['logf'], p['kb'], p['vb'], p['ki'])


def kernel(x_prompt, x_sample, cache_a_k, cache_a_v, cache_a_logf, cache_b_k, cache_b_v, cache_idx_k,
           page_table, w_in, b_f, w_out, ln1_g, ln1_b, w_up, w_down, ln2_g, ln2_b, rel_bias):
    depth = w_in.shape[0]
    batch, seq, d_model = x_prompt.shape
    db, dec_seq, _ = x_sample.shape
    assert dec_seq == 1
    alpha = (2 * depth) ** 0.25
    natural = tuple(range(H_A))
    tq = min(ATTN_BLOCK, seq)

    kv_t = lambda c: c.transpose(0, 1, 3, 4, 2).reshape(c.shape[:2] + (-1, PAGE_SIZE))
    caches = dict(a_k=kv_t(cache_a_k), a_v=kv_t(cache_a_v), b_k=kv_t(cache_b_k), b_v=kv_t(cache_b_v),
                  idx_k=cache_idx_k.transpose(0, 1, 3, 2),
                  logf_t=cache_a_logf.astype(F32).transpose(0, 1, 3, 2))
    relb = _rel_bias_tiles(rel_bias[:, list(HEAD_PERM)], tq)
    rb_far = rel_bias[N_BUCKETS - 1, list(HEAD_PERM)]
    rbt = rel_bias.T

    xp = x_prompt.reshape(batch * seq, d_model)
    xs = x_sample.reshape(db, d_model)
    p_rows, s_rows = [], []
    for l in range(depth):
        bf_pad = _pad_cols(b_f[l].reshape(1, -1), LANES)
        post = lambda perm: _prep_post_weights(w_out[l], perm, perm, ln1_g[l], ln1_b[l], w_up[l], w_down[l],
                                               ln2_g[l], ln2_b[l], FF_CHUNK)
        xp, rows = _prompt_layer(xp, batch, seq, _prep_proj_t_weights(w_in[l], HEAD_PERM), post(HEAD_PERM),
                                 bf_pad, relb, rb_far, alpha)
        p_rows.append(rows)
        xs, rows = _sample_layer(xs, page_table, caches, l, _prep_proj_weights(w_in[l], natural),
                                 post(natural), bf_pad, rbt, alpha)
        s_rows.append(rows)

    stack = lambda rows, j: jnp.stack([r[j] for r in rows])
    sample = lambda j, tail: stack(s_rows, j).reshape((depth, db, dec_seq) + tail)
    kv = (KV_A, HEAD_DIM)
    return (xp.reshape(batch, seq, d_model), xs.reshape(db, dec_seq, d_model),
            stack(p_rows, 0), stack(p_rows, 1), stack(p_rows, 2), stack(p_rows, 3), stack(p_rows, 4),
            stack(p_rows, 5),
            sample(0, kv), sample(1, kv), sample(2, (H_A,)), sample(3, kv), sample(4, kv), sample(5, (D_IDX,)))
```

```python
import functools
import math

import numpy as np
import jax
import jax.numpy as jnp
from jax import lax
from jax.experimental import pallas as pl
from jax.experimental.pallas import tpu as pltpu

HEAD_DIM = 64
H_A = 8
KV_A = 4
H_B = 8
KV_B = 4
GROUP = 2
H_IDX = 8
D_IDX = 64
TOPK_MAX = 256
N_BUCKETS = 32
MAX_DISTANCE = 128
PAGE_SIZE = 128
LN_EPS = 1e-5

LANES = 128
VMEM_LIMIT_BYTES = 56 * 1024 * 1024

NEG_BIG = -1e30
INT_MIN = -(2 ** 31)
HALF_BITS = 16
I16_MIN = -(2 ** 15)

BF16 = jnp.bfloat16
F32 = jnp.float32

HEAD_PERM = (0, 2, 1, 3, 4, 6, 5, 7)


def _cparams(semantics):
    return pltpu.CompilerParams(dimension_semantics=semantics,
                                vmem_limit_bytes=VMEM_LIMIT_BYTES)


def _dot(a, b):
    return jnp.dot(a, b, preferred_element_type=F32)


def _dot_nt(a, b):
    return lax.dot_general(a, b, (((1,), (1,)), ((), ())), preferred_element_type=F32)


def _split2(x):
    hi = x.astype(BF16)
    lo = (x - hi.astype(F32)).astype(BF16)
    return hi, lo


def _split3(x):
    h1 = x.astype(BF16)
    r1 = x - h1.astype(F32)
    h2 = r1.astype(BF16)
    h3 = (r1 - h2.astype(F32)).astype(BF16)
    return h1, h2, h3


def _dot3(xh, xl, wh, wl):
    return _dot(xh, wh) + _dot(xl, wh) + _dot(xh, wl)


def _log_sigmoid(x):
    return jnp.minimum(x, 0.0) - jnp.log1p(jnp.exp(-jnp.abs(x)))


def _layer_norm(x, g, b):
    mu = jnp.mean(x, axis=-1, keepdims=True)
    xc = x - mu
    var = jnp.mean(xc * xc, axis=-1, keepdims=True)
    return xc * lax.rsqrt(var + LN_EPS) * g + b


def _rel_bucket(dist):
    max_exact = N_BUCKETS // 2
    d = jnp.maximum(dist, 1).astype(F32)
    large = max_exact + (jnp.log(d / max_exact) / math.log(MAX_DISTANCE / max_exact)
                         * (N_BUCKETS - max_exact)).astype(jnp.int32)
    large = jnp.minimum(large, N_BUCKETS - 1)
    return jnp.where(dist < max_exact, dist, large)


def _order_key(score):
    bits = pltpu.bitcast(score, jnp.int32)
    key = bits ^ (lax.shift_right_arithmetic(bits, 31) & jnp.int32(0x7FFFFFFF))
    return jnp.where(bits == jnp.int32(INT_MIN), 0, key)


def _proj_kernel(x_ref, wbig_ref, wih_ref, wil_ref, bf_ref,
                 qa_ref, qb_ref, ka_ref, va_ref, kb_ref, vb_ref, qih_ref, qil_ref, ki_ref, kih_ref, kil_ref,
                 wi_ref, logf_ref):
    x = x_ref[...]
    xh, xl = _split2(x)
    nq = H_A * HEAD_DIM
    nk = KV_A * HEAD_DIM
    qa_ref[...] = _dot(xh, wbig_ref[:, 0:nq]).astype(BF16)
    qb_ref[...] = _dot(xh, wbig_ref[:, nq:2 * nq]).astype(BF16)
    o = 2 * nq
    ka_ref[...] = _dot(xh, wbig_ref[:, o:o + nk])
    va_ref[...] = _dot(xh, wbig_ref[:, o + nk:o + 2 * nk])
    kb_ref[...] = _dot(xh, wbig_ref[:, o + 2 * nk:o + 3 * nk])
    vb_ref[...] = _dot(xh, wbig_ref[:, o + 3 * nk:o + 4 * nk])

    ni = H_IDX * D_IDX
    qi = _dot3(xh, xl, wih_ref[:, 0:ni], wil_ref[:, 0:ni])
    qh, ql = _split2(qi)
    qih_ref[...] = qh
    qil_ref[...] = ql
    rest = _dot3(xh, xl, wih_ref[:, ni:ni + 2 * LANES], wil_ref[:, ni:ni + 2 * LANES])
    ki2 = rest[:, 0:LANES]
    ki_ref[...] = ki2[:, 0:D_IDX]
    kh, kl = _split2(ki2)
    kih_ref[...] = kh
    kil_ref[...] = kl

    small = rest[:, LANES:2 * LANES]
    wi_ref[...] = small[:, H_A:H_A + H_IDX] * (H_IDX ** -0.5)
    logf = _log_sigmoid(small + bf_ref[...])
    logf_ref[...] = logf[:, 0:H_A]


def _project(x2d, wts, bf_pad):
    m, d = x2d.shape
    tm = min(512, m)
    assert m % tm == 0
    nq, nk, ni = H_A * HEAD_DIM, KV_A * HEAD_DIM, H_IDX * D_IDX
    full = lambda a: pl.BlockSpec(a.shape, lambda i: (0,) * a.ndim)
    row = lambda n: pl.BlockSpec((tm, n), lambda i: (i, 0))
    inputs = [x2d, wts['big'], wts['idx_hi'], wts['idx_lo'], bf_pad]
    out_shapes = [((m, nq), BF16), ((m, nq), BF16), ((m, nk), F32), ((m, nk), F32), ((m, nk), F32),
                  ((m, nk), F32), ((m, ni), BF16), ((m, ni), BF16), ((m, D_IDX), F32),
                  ((m, LANES), BF16), ((m, LANES), BF16), ((m, H_IDX), F32), ((m, H_A), F32)]
    outs = pl.pallas_call(
        _proj_kernel,
        grid=(m // tm,),
        in_specs=[row(d)] + [full(a) for a in inputs[1:]],
        out_specs=[row(s[0][1]) for s in out_shapes],
        out_shape=[jax.ShapeDtypeStruct(*s) for s in out_shapes],
        compiler_params=_cparams(("arbitrary",)),
        name="proj",
    )(*inputs)
    names = ['qa', 'qb', 'ka', 'va', 'kb', 'vb', 'qi_hi', 'qi_lo', 'ki', 'ki2_hi', 'ki2_lo', 'wi', 'logf']
    return dict(zip(names, outs))


def _proj_t_kernel(*refs, tiles_per_seq, tk, n_prev):
    (x_ref, wt_ref, wn_ref, wnh_ref, wnl_ref, wth_ref, wtl_ref, bf_ref, tri_ref,
     p1_ref, p2_ref, p3_ref) = refs[:12]
    prev = refs[12:16] if n_prev else (None,) * 4
    (qat_ref, qbt_ref, vat_ref, vbt_ref, kat_ref, kbt_ref, vax_ref, vbx_ref,
     kaug_ref, kbn_ref, qith_ref, qitl_ref, ki_ref, kih_ref, wi_ref, logf_ref,
     carry_ref) = refs[12 + (4 if n_prev else 0):]

    def emit(out_ref, prev_ref, value):
        if n_prev:
            out_ref[0:n_prev] = prev_ref[...]
        out_ref[n_prev] = value

    x = x_ref[...]
    xh, xl = _split2(x)
    tm = x.shape[0]
    nq = H_A * HEAD_DIM
    nk = KV_A * HEAD_DIM

    qat_ref[...] = _dot_nt(wt_ref[0:nq, :], xh).astype(BF16)
    qbt_ref[...] = _dot_nt(wt_ref[nq:2 * nq, :], xh).astype(BF16)
    o = 2 * nq
    row = lax.broadcasted_iota(jnp.int32, (2 * HEAD_DIM, tm), 0)
    lo_rows = row < HEAD_DIM
    for t_ref, p_ref, x_out_ref, r0 in ((vat_ref, prev[0], vax_ref, o), (vbt_ref, prev[1], vbx_ref, o + nk)):
        vt = _dot_nt(wt_ref[r0:r0 + nk, :], xh)
        emit(t_ref, p_ref, vt)
        for m in range(KV_A // 2):
            blk = vt[m * 2 * HEAD_DIM:(m + 1) * 2 * HEAD_DIM, :]
            lo = jnp.where(lo_rows, blk, 1.0).astype(BF16)
            hi = jnp.where(lo_rows, 1.0, blk).astype(BF16)
            for t in range(tm // tk):
                cols = slice(t * tk, (t + 1) * tk)
                x_out_ref[t, m * 4 * HEAD_DIM:m * 4 * HEAD_DIM + 2 * HEAD_DIM, :] = lo[:, cols]
                x_out_ref[t, m * 4 * HEAD_DIM + 2 * HEAD_DIM:(m + 1) * 4 * HEAD_DIM, :] = hi[:, cols]
    emit(kat_ref, prev[2], _dot_nt(wt_ref[o + 2 * nk:o + 3 * nk, :], xh))
    emit(kbt_ref, prev[3], _dot_nt(wt_ref[o + 3 * nk:o + 4 * nk, :], xh))

    kn = _dot(xh, wn_ref[...])
    kbn_ref[...] = kn[:, 2 * nk:3 * nk].astype(BF16)

    qit = (_dot_nt(wth_ref[...], xh) + _dot_nt(wth_ref[...], xl)) + _dot_nt(wtl_ref[...], xh)
    qh, ql = _split2(qit)
    qith_ref[...] = qh
    qitl_ref[...] = ql

    rest = _dot3(xh, xl, wnh_ref[...], wnl_ref[...])
    ki2 = rest[:, 0:LANES]
    ki_ref[...] = ki2[:, 0:D_IDX]
    kih_ref[...] = ki2.astype(BF16)
    small = rest[:, LANES:2 * LANES]
    wi_ref[...] = small[:, H_A:H_A + H_IDX] * (H_IDX ** -0.5)
    logf = _log_sigmoid(small + bf_ref[...])
    logf_ref[...] = logf[:, 0:H_A]

    @pl.when(pl.program_id(0) % tiles_per_seq == 0)
    def _():
        carry_ref[...] = jnp.zeros_like(carry_ref)

    h1, h2, h3 = _split3(logf)
    tri = tri_ref[...]
    c = (_dot(tri, h1) + _dot(tri, h2)) + _dot(tri, h3) + carry_ref[...]
    carry_ref[...] = c[tm - 1:tm, :]
    c1, c2, c3 = _split3(c)
    aug = (_dot(c1, p1_ref[...]) + _dot(c2, p2_ref[...])) + _dot(c3, p3_ref[...])
    kaug_ref[...] = (kn[:, 0:2 * nk] + aug).astype(BF16)


def _decay_placement():
    mats = np.zeros((3, LANES, 2 * KV_A * HEAD_DIM), np.float32)
    for m in range(KV_A // 2):
        for v in range(2 * GROUP):
            head = HEAD_PERM[4 * m + v]
            hi_variant, g = v % 2, v // 2
            base = m * 4 * HEAD_DIM + (2 * HEAD_DIM if hi_variant else HEAD_DIM)
            for t in range(3):
                mats[t, head, base + 3 * g + t] = -1.0
    return [jnp.asarray(mats[t], BF16) for t in range(3)]


def _project_t(x2d, wts, bf_pad, prev, *, batch, seq, tk):
    n_prev = 0 if prev is None else prev[0].shape[0]
    m, d = x2d.shape
    tm = min(512, seq)
    assert seq % tm == 0 and tm % tk == 0
    tps = seq // tm
    nq, nk, ni = H_A * HEAD_DIM, KV_A * HEAD_DIM, H_IDX * D_IDX
    nkv = seq // tk
    full = lambda a: pl.BlockSpec(a.shape, lambda i: (0,) * a.ndim)
    row = lambda n: pl.BlockSpec((tm, n), lambda i: (i, 0))
    feat = lambda r: pl.BlockSpec((None, r, tm), lambda i: (i // tps, 0, i % tps))
    vx = pl.BlockSpec((None, tm // tk, 2 * nk, tk), lambda i: (i // tps, i % tps, 0, 0))
    tri = jnp.tril(jnp.ones((tm, tm), F32)).astype(BF16)
    inputs = [x2d, wts['t_big'], wts['n_keys'], wts['n_idx_hi'], wts['n_idx_lo'], wts['t_idx_hi'],
              wts['t_idx_lo'], bf_pad, tri] + _decay_placement()
    in_specs = [row(d)] + [full(a) for a in inputs[1:]]
    layers = lambda n: pl.BlockSpec((n, None, nk, tm), lambda i: (0, i // tps, 0, i % tps))
    if n_prev:
        inputs += list(prev)
        in_specs += [layers(n_prev)] * 4
    fshape = lambda r, dt: ((batch, r, seq), dt)
    stacked = (((n_prev + 1, batch, nk, seq), F32), layers(n_prev + 1))
    outs = [
        (fshape(nq, BF16), feat(nq)), (fshape(nq, BF16), feat(nq)),
        stacked, stacked, stacked, stacked,
        (((batch, nkv, 2 * nk, tk), BF16), vx), (((batch, nkv, 2 * nk, tk), BF16), vx),
        (((m, 2 * nk), BF16), row(2 * nk)), (((m, nk), BF16), row(nk)),
        (fshape(ni, BF16), feat(ni)), (fshape(ni, BF16), feat(ni)),
        (((m, D_IDX), F32), row(D_IDX)), (((m, LANES), BF16), row(LANES)),
        (((m, H_IDX), F32), row(H_IDX)), (((m, H_A), F32), row(H_A)),
    ]
    res = pl.pallas_call(
        functools.partial(_proj_t_kernel, tiles_per_seq=tps, tk=tk, n_prev=n_prev),
        grid=(m // tm,),
        in_specs=in_specs,
        out_specs=[o[1] for o in outs],
        out_shape=[jax.ShapeDtypeStruct(*o[0]) for o in outs],
        scratch_shapes=[pltpu.VMEM((1, LANES), F32)],
        compiler_params=_cparams(("arbitrary",)),
        name="proj_prompt",
    )(*inputs)
    names = ['qat', 'qbt', 'vat', 'vbt', 'kat', 'kbt', 'vax', 'vbx', 'kaug', 'kbn', 'qit_hi', 'qit_lo',
             'ki', 'ki2_hi', 'wi', 'logf']
    return dict(zip(names, res))


def _post_kernel(x_ref, oa_ref, ob_ref, woa_ref, wob_ref, g1_ref, b1_ref, wup_ref, wdn_ref,
                 g2_ref, b2_ref, y_ref, acc_ref, *, alpha, n_chunks, feature_major):
    x = x_ref[...]
    if feature_major:
        tn = lambda a, w: lax.dot_general(a, w, (((0,), (0,)), ((), ())), preferred_element_type=F32)
        att = tn(oa_ref[...], woa_ref[...]) + tn(ob_ref[...], wob_ref[...])
    else:
        att = _dot(oa_ref[...], woa_ref[...]) + _dot(ob_ref[...], wob_ref[...])
    x1 = _layer_norm(alpha * x + att, g1_ref[...], b1_ref[...])
    x1b = x1.astype(BF16)
    acc_ref[...] = jnp.zeros_like(acc_ref)

    def body(c, carry):
        cols = pl.ds(pl.multiple_of(c * FF_CHUNK, FF_CHUNK), FF_CHUNK)
        u = jnp.maximum(_dot(x1b, wup_ref[:, cols]), 0.0)
        acc_ref[...] += _dot((u * u).astype(BF16), wdn_ref[c])
        return carry

    lax.fori_loop(0, n_chunks, body, 0)
    y_ref[...] = _layer_norm(alpha * x1 + acc_ref[...], g2_ref[...], b2_ref[...])


def _post_block(x2d, oa, ob, wts, alpha, seq=None):
    m, d = x2d.shape
    tm = min(POST_ROWS, m if seq is None else seq)
    n_chunks = wts['down'].shape[0]
    full = lambda a: pl.BlockSpec(a.shape, lambda i: (0,) * a.ndim, pipeline_mode=pl.Buffered(1))
    row = lambda n: pl.BlockSpec((tm, n), lambda i: (i, 0))
    if seq is None:
        o_spec = lambda a: row(a.shape[1])
    else:
        tps = seq // tm
        o_spec = lambda a: pl.BlockSpec((None, a.shape[1], tm), lambda i: (i // tps, 0, i % tps))
    weights = [wts['out_a'], wts['out_b'], wts['ln1_g'], wts['ln1_b'], wts['up'], wts['down'],
               wts['ln2_g'], wts['ln2_b']]
    return pl.pallas_call(
        functools.partial(_post_kernel, alpha=alpha, n_chunks=n_chunks, feature_major=seq is not None),
        grid=(m // tm,),
        in_specs=[row(d), o_spec(oa), o_spec(ob)] + [full(w) for w in weights],
        out_specs=row(d),
        out_shape=jax.ShapeDtypeStruct((m, d), F32),
        scratch_shapes=[pltpu.VMEM((tm, d), F32)],
        compiler_params=_cparams(("arbitrary",)),
        name="post_block",
    )(x2d, oa, ob, *weights)


def _variant_rows(g, hi_half, ones_from=None):
    row = lax.broadcasted_iota(jnp.int32, g.shape, 0)
    keep = (row >= HEAD_DIM) if hi_half else (row < HEAD_DIM)
    fill = 0.0
    if ones_from is not None:
        fill = jnp.where((row >= ones_from) & (row < ones_from + 3), 1.0, 0.0)
    return jnp.where(keep, g, fill).astype(BF16)


def _softmax_chunk(s, causal_q0, m_ref, p_ref, acc_ref, cols, acols):
    if causal_q0 is not None:
        key = lax.broadcasted_iota(jnp.int32, s.shape, 0)
        qry = lax.broadcasted_iota(jnp.int32, s.shape, 1) + causal_q0
        s = jnp.where(key <= qry, s, NEG_BIG)
    m_old = m_ref[:, cols]
    m_new = jnp.maximum(m_old, jnp.max(s, axis=0, keepdims=True))
    m_ref[:, cols] = m_new
    p_ref[:, cols] = jnp.exp(s - m_new[0:1, :]).astype(BF16)
    acc_ref[:, acols] = acc_ref[:, acols] * jnp.exp(m_old - m_new)[0:1, :]


def _finish_pair(acc_lo, acc_hi, tq):
    row = lax.broadcasted_iota(jnp.int32, (2 * HEAD_DIM, tq), 0)
    lo_rows = row < HEAD_DIM
    out = []
    for g in range(GROUP):
        cols = slice(g * tq, (g + 1) * tq)
        a_lo, a_hi = acc_lo[:, cols], acc_hi[:, cols]
        o_lo = a_lo / a_lo[HEAD_DIM:HEAD_DIM + 1, :]
        o_hi = a_hi / a_hi[0:1, :]
        out.append(jnp.where(lo_rows, o_lo, o_hi))
    return jnp.concatenate(out, axis=0).astype(BF16)


def _fox_kernel(q_ref, k_ref, v_ref, o_ref, qt_ref, s_ref, p_ref, m_ref, acc_ref, *, tq):
    i = pl.program_id(2)
    q = q_ref[...].astype(F32)
    g0, g1 = q[0:2 * HEAD_DIM], q[2 * HEAD_DIM:4 * HEAD_DIM]
    qt_ref[0, :, 0:tq] = _variant_rows(g0, False, HEAD_DIM)
    qt_ref[0, :, tq:2 * tq] = _variant_rows(g1, False, HEAD_DIM + 3)
    qt_ref[1, :, 0:tq] = _variant_rows(g0, True, 0)
    qt_ref[1, :, tq:2 * tq] = _variant_rows(g1, True, 3)
    m_ref[...] = jnp.full_like(m_ref, NEG_BIG)
    acc_ref[...] = jnp.zeros_like(acc_ref)

    def process(blocks):
        for slot, (j, _) in enumerate(blocks):
            start = pl.multiple_of(j * tq, tq)
            for d in range(2):
                s_ref[slot, d] = _dot(k_ref[pl.ds(start, tq), d * LANES:(d + 1) * LANES], qt_ref[d])
        for slot, (j, causal) in enumerate(blocks):
            for d in range(2):
                for c in range(2 * tq // LANES):
                    cols = slice(c * LANES, (c + 1) * LANES)
                    q0 = (c * LANES) % tq if causal else None
                    _softmax_chunk(s_ref[slot, d, :, cols], q0, m_ref.at[d], p_ref.at[slot, d],
                                   acc_ref.at[d], cols, cols)
                acc_ref[d] += _dot(v_ref[j, d * LANES:(d + 1) * LANES, :], p_ref[slot, d])

    def pair_body(t, carry):
        process(((2 * t, False), (2 * t + 1, False)))
        return carry

    lax.fori_loop(0, i // 2, pair_body, 0)

    @pl.when(i % 2 == 1)
    def _():
        process(((i - 1, False), (i, True)))

    @pl.when(i % 2 == 0)
    def _():
        process(((i, True),))

    o_ref[...] = _finish_pair(acc_ref[0], acc_ref[1], tq)


def _fox_attention(qt, kaug, vx, batch, seq, tq):
    nq = seq // tq
    pair = 4 * HEAD_DIM
    return pl.pallas_call(
        functools.partial(_fox_kernel, tq=tq),
        grid=(batch, KV_A // 2, nq),
        in_specs=[
            pl.BlockSpec((None, pair, tq), lambda b, m, i: (b, m, i)),
            pl.BlockSpec((seq, pair), lambda b, m, i: (b, m)),
            pl.BlockSpec((None, nq, pair, tq), lambda b, m, i: (b, 0, m, 0)),
        ],
        out_specs=pl.BlockSpec((None, pair, tq), lambda b, m, i: (b, m, i)),
        out_shape=jax.ShapeDtypeStruct(qt.shape, BF16),
        scratch_shapes=[pltpu.VMEM((2, LANES, 2 * tq), BF16), pltpu.VMEM((2, 2, tq, 2 * tq), F32),
                        pltpu.VMEM((2, 2, tq, 2 * tq), BF16), pltpu.VMEM((2, 8, 2 * tq), F32),
                        pltpu.VMEM((2, LANES, 2 * tq), F32)],
        compiler_params=_cparams(("arbitrary", "arbitrary", "arbitrary")),
        name="fox_attention",
    )(qt, kaug, vx)


def _relb_kernel(rb_ref, o_ref, *, tq):
    delta = pl.program_id(0)
    h = pl.program_id(1)
    key = lax.broadcasted_iota(jnp.int32, (tq, tq), 0)
    qry = lax.broadcasted_iota(jnp.int32, (tq, tq), 1)
    bucket = _rel_bucket(jnp.maximum(delta * tq + qry - key, 0))
    acc = jnp.zeros((tq, tq), F32)
    for n in range(N_BUCKETS):
        acc = jnp.where(bucket == n, rb_ref[n, h], acc)
    o_ref[...] = acc


def _rel_bias_tiles(rb_perm, tq):
    return pl.pallas_call(
        functools.partial(_relb_kernel, tq=tq),
        grid=(2, H_B),
        in_specs=[pl.BlockSpec(memory_space=pltpu.SMEM)],
        out_specs=pl.BlockSpec((None, None, tq, tq), lambda d, h: (d, h, 0, 0)),
        out_shape=jax.ShapeDtypeStruct((2, H_B, tq, tq), F32),
        compiler_params=_cparams(("arbitrary", "arbitrary")),
        name="rel_bias_tiles",
    )(rb_perm)


def _strict_upper(n):
    r = lax.broadcasted_iota(jnp.int32, (n, n), 0)
    c = lax.broadcasted_iota(jnp.int32, (n, n), 1)
    return jnp.where(r < c, 1.0, 0.0).astype(BF16)


def _strict_lower(n):
    r = lax.broadcasted_iota(jnp.int32, (n, n), 0)
    c = lax.broadcasted_iota(jnp.int32, (n, n), 1)
    return jnp.where(c < r, 1.0, 0.0).astype(BF16)


KEY_ROWS = 64

DSA_VARIANT_ORDER = (0, 2, 1, 3)


def _dsa_kernel(qih_ref, qil_ref, kih_ref, wi_ref, qb_ref, k_ref, v_ref, relb_ref, rbfar_ref,
                o_ref, keys_ref, khi_ref, klo_ref, mb_ref, qi_ref, lg_ref, qt_ref, p_ref, m_ref, acc_ref,
                *, tq, topk):
    i = pl.program_id(1)
    nblk = i + 1
    nlc = tq // LANES

    for h in range(H_IDX):
        rows = slice((h // 2) * LANES, (h // 2 + 1) * LANES)
        cols = slice(h * tq, (h + 1) * tq)
        qi_ref[0:LANES, cols] = _variant_rows(qih_ref[rows, :].astype(F32), h % 2 == 1)
        qi_ref[LANES:2 * LANES, cols] = _variant_rows(qil_ref[rows, :].astype(F32), h % 2 == 1)

    def score_block(j, diag):
        start = pl.multiple_of(j * tq, tq)
        kh = kih_ref[pl.ds(start, tq), :]
        lg_ref[...] = _dot(jnp.concatenate([kh, kh], axis=1), qi_ref[...])
        for c in range(nlc):
            for r in range(tq // KEY_ROWS):
                rows = slice(r * KEY_ROWS, (r + 1) * KEY_ROWS)
                sc = jnp.zeros((KEY_ROWS, LANES), F32)
                for h in range(H_IDX):
                    lane0 = h * tq + c * LANES
                    sc = sc + wi_ref[h:h + 1, c * LANES:(c + 1) * LANES] * jnp.maximum(
                        lg_ref[rows, lane0:lane0 + LANES], 0.0)
                key = _order_key(sc)
                if diag:
                    kidx = lax.broadcasted_iota(jnp.int32, key.shape, 0) + r * KEY_ROWS
                    qidx = lax.broadcasted_iota(jnp.int32, key.shape, 1) + c * LANES
                    key = jnp.where(kidx <= qidx, key, INT_MIN)
                keys_ref[j, rows, c * LANES:(c + 1) * LANES] = key
                khi_ref[j, rows, c * LANES:(c + 1) * LANES] = lax.shift_right_arithmetic(
                    key, HALF_BITS).astype(jnp.int16)

    def score_body(j, carry):
        score_block(j, False)
        return carry

    lax.fori_loop(0, i, score_body, 0)
    score_block(i, True)

    needs_select = (i + 1) * tq > topk

    @pl.when(jnp.logical_not(needs_select))
    def _():
        def zero_body(j, carry):
            mb_ref[j] = jnp.zeros((tq, tq), F32)
            return carry
        lax.fori_loop(0, nblk, zero_body, 0)

    @pl.when(needs_select)
    def _():
        npairs = (nblk + 1) // 2

        @pl.when(nblk % 2 == 1)
        def _():
            khi_ref[nblk] = jnp.full((tq, tq), I16_MIN, jnp.int16)
            klo_ref[nblk] = jnp.full((tq, tq), I16_MIN, jnp.int16)

        def count16(ref, pred_fn):
            def body(t, acc):
                for j in (2 * t, 2 * t + 1):
                    hit = jnp.where(pred_fn(ref[j]), jnp.int16(1), jnp.int16(0))
                    for r in range(tq // 16):
                        acc = acc + hit[r * 16:(r + 1) * 16, :]
                return acc
            acc = lax.fori_loop(0, npairs, body, jnp.zeros((16, tq), jnp.int16))
            return jnp.sum(acc.astype(F32), axis=0, keepdims=True)

        def search16(ref, want):
            def bit_body(b, t):
                cand = t + lax.shift_left(jnp.int32(1), HALF_BITS - 1 - b)
                cand16 = cand.astype(jnp.int16)
                return jnp.where(count16(ref, lambda k: k >= cand16) >= want, cand, t)
            return lax.fori_loop(0, HALF_BITS, bit_body, jnp.full((1, tq), I16_MIN, jnp.int32))

        thr_hi = search16(khi_ref, topk)
        thr_hi16 = thr_hi.astype(jnp.int16)
        above = count16(khi_ref, lambda k: k > thr_hi16)

        def low_body(j, carry):
            kblk = keys_ref[j]
            in_band = lax.shift_right_arithmetic(kblk, HALF_BITS) == thr_hi
            low = (kblk & jnp.int32(2 ** HALF_BITS - 1)) + I16_MIN
            klo_ref[j] = jnp.where(in_band, low, I16_MIN).astype(jnp.int16)
            return carry

        lax.fori_loop(0, nblk, low_body, 0)
        thr_lo = search16(klo_ref, topk - above)
        thr_lo16 = thr_lo.astype(jnp.int16)
        thr = thr_hi * (2 ** HALF_BITS) + (thr_lo - I16_MIN)
        need = topk - (above + count16(klo_ref, lambda k: k > thr_lo16))
        n_eq = count16(klo_ref, lambda k: k == thr_lo16)
        tie_break = jnp.max(jnp.where(n_eq > need, 1.0, 0.0)) > 0.0

        @pl.when(jnp.logical_not(tie_break))
        def _():
            def plain_body(j, carry):
                mb_ref[j] = jnp.where(keys_ref[j] >= thr, 0.0, NEG_BIG)
                return carry
            lax.fori_loop(0, nblk, plain_body, 0)

        @pl.when(tie_break)
        def _():
            earlier = _strict_lower(tq)

            def mask_body(j, seen):
                kblk = keys_ref[j]
                eq = kblk == thr
                eqf = jnp.where(eq, 1.0, 0.0)
                rank = _dot(earlier, eqf.astype(BF16)) + seen
                mb_ref[j] = jnp.where(kblk > thr, 0.0,
                                      jnp.where(eq, jnp.where(rank < need, 0.0, NEG_BIG), NEG_BIG))
                return seen + jnp.sum(eqf, axis=0, keepdims=True)

            lax.fori_loop(0, nblk, mask_body, jnp.zeros((1, tq), F32))

    pair = 4 * HEAD_DIM
    for m in range(KV_B // 2):
        q = qb_ref[m * pair:(m + 1) * pair, :].astype(F32)
        g = (q[0:2 * HEAD_DIM], q[2 * HEAD_DIM:4 * HEAD_DIM])
        for n, v in enumerate(DSA_VARIANT_ORDER):
            qt_ref[:, n * tq:(n + 1) * tq] = _variant_rows(g[v // 2], v % 2 == 1)
        m_ref[...] = jnp.full_like(m_ref, NEG_BIG)
        acc_ref[...] = jnp.zeros_like(acc_ref)

        far = lambda slot, qcols: rbfar_ref[slot]
        near = lambda slot, qcols: relb_ref[1, slot, :, qcols]
        diag = lambda slot, qcols: relb_ref[0, slot, :, qcols]

        def process(blocks):
            for slot, (j, _) in enumerate(blocks):
                start = pl.multiple_of(j * tq, tq)
                lg_ref[:, slot * 4 * tq:(slot + 1) * 4 * tq] = _dot(
                    k_ref[pl.ds(start, tq), m * LANES:(m + 1) * LANES], qt_ref[...])
            for slot, (j, rel) in enumerate(blocks):
                for n, v in enumerate(DSA_VARIANT_ORDER):
                    for c in range(nlc):
                        qcols = slice(c * LANES, (c + 1) * LANES)
                        cols = slice(n * tq + c * LANES, n * tq + (c + 1) * LANES)
                        lcols = slice(slot * 4 * tq + cols.start, slot * 4 * tq + cols.stop)
                        s = lg_ref[:, lcols] + (mb_ref[j, :, qcols] + rel(4 * m + v, qcols))
                        acols = slice((n % 2) * tq + c * LANES, (n % 2) * tq + (c + 1) * LANES)
                        _softmax_chunk(s, c * LANES if rel is diag else None, m_ref, p_ref.at[slot],
                                       acc_ref.at[n // 2], cols, acols)
                for d in range(2):
                    vt = v_ref[j, m * pair + d * LANES:m * pair + (d + 1) * LANES, :]
                    acc_ref[d] += _dot(vt, p_ref[slot, :, d * 2 * tq:(d + 1) * 2 * tq])

        n_far = jnp.maximum(i - 1, 0)

        def far_body(t, carry):
            process(((2 * t, far), (2 * t + 1, far)))
            return carry

        lax.fori_loop(0, n_far // 2, far_body, 0)

        @pl.when(n_far % 2 == 1)
        def _():
            process(((i - 2, far),))

        @pl.when(i >= 1)
        def _():
            process(((i - 1, near), (i, diag)))

        @pl.when(i == 0)
        def _():
            process(((i, diag),))

        o_ref[m * pair:(m + 1) * pair, :] = _finish_pair(acc_ref[0], acc_ref[1], tq)


def _dsa_attention(p, wi_t, relb, rb_far, batch, seq, tq, topk):
    nq = seq // tq
    nf = H_B * HEAD_DIM
    qcol = lambda r: pl.BlockSpec((None, r, tq), lambda b, i: (b, 0, i))
    seqblk = lambda n: pl.BlockSpec((seq, n), lambda b, i: (b, 0))
    return pl.pallas_call(
        functools.partial(_dsa_kernel, tq=tq, topk=topk),
        grid=(batch, nq),
        in_specs=[qcol(H_IDX * D_IDX), qcol(H_IDX * D_IDX), seqblk(LANES), qcol(H_IDX),
                  qcol(nf), seqblk(KV_B * HEAD_DIM),
                  pl.BlockSpec((None, nq, nf, tq), lambda b, i: (b, 0, 0, 0)),
                  pl.BlockSpec(relb.shape, lambda b, i: (0, 0, 0, 0)),
                  pl.BlockSpec(memory_space=pltpu.SMEM)],
        out_specs=qcol(nf),
        out_shape=jax.ShapeDtypeStruct((batch, nf, seq), BF16),
        scratch_shapes=[pltpu.VMEM((nq, tq, tq), jnp.int32), pltpu.VMEM((nq, tq, tq), jnp.int16),
                        pltpu.VMEM((nq, tq, tq), jnp.int16), pltpu.VMEM((nq, tq, tq), F32),
                        pltpu.VMEM((2 * LANES, H_IDX * tq), BF16), pltpu.VMEM((tq, H_IDX * tq), F32),
                        pltpu.VMEM((LANES, 4 * tq), BF16), pltpu.VMEM((2, tq, 4 * tq), BF16),
                        pltpu.VMEM((8, 4 * tq), F32), pltpu.VMEM((2, LANES, 2 * tq), F32)],
        compiler_params=_cparams(("arbitrary", "arbitrary")),
        name="dsa_attention",
    )(p['qit_hi'], p['qit_lo'], p['ki2_hi'], wi_t, p['qbt'], p['kbn'], p['vbx'], relb, rb_far)


def _sscan_kernel(pt_ref, q16_ref, wi_ref, lnew_ref, idx_hbm, logf_hbm, sc_ref, dec_ref,
                  kbuf, fbuf, sem, carry_ref, *, pps, n_chunks, attn_pps, layer):
    b = pl.program_id(0)
    n_rows = pl.num_programs(0)

    def copies(row, step, slot):
        chunk = n_chunks - 1 - step
        out = []
        for r in range(pps):
            page = pt_ref[row, chunk * pps + r]
            out.append(pltpu.make_async_copy(idx_hbm.at[layer, page], kbuf.at[slot, r], sem.at[slot]))
            out.append(pltpu.make_async_copy(logf_hbm.at[layer, page], fbuf.at[slot, r], sem.at[slot]))
        return out

    def start(row, step, slot):
        for n, c in enumerate(copies(row, step, slot)):
            c.start(priority=n % 2)

    def wait(row, step, slot):
        for c in copies(row, step, slot):
            c.wait()

    @pl.when(b == 0)
    def _():
        start(0, 0, 0)

    carry_ref[...] = jnp.broadcast_to(lnew_ref[...], carry_ref.shape)
    q16 = q16_ref[...]
    qh = q16[0:H_IDX]
    wi = wi_ref[...]
    r_i = lax.broadcasted_iota(jnp.int32, (PAGE_SIZE, PAGE_SIZE), 0)
    c_i = lax.broadcasted_iota(jnp.int32, (PAGE_SIZE, PAGE_SIZE), 1)
    later = jnp.where(r_i > c_i, 1.0, 0.0).astype(BF16)
    per = pps // attn_pps

    def compute(step, slot):
        chunk = n_chunks - 1 - step
        x_all = jnp.concatenate([fbuf[slot, r] for r in range(pps)], axis=0)
        h1, h2, h3 = _split3(x_all)
        within = (_dot(h1, later) + _dot(h2, later)) + _dot(h3, later)
        totals = jnp.sum(x_all, axis=1, keepdims=True)
        carry = carry_ref[...]
        for r in reversed(range(pps)):
            rows = slice(r * H_A, (r + 1) * H_A)
            lanes = slice((r % attn_pps) * PAGE_SIZE, (r % attn_pps + 1) * PAGE_SIZE)
            dec_ref[chunk * per + r // attn_pps, :, lanes] = within[rows, :] + carry[:, 0:1]
            carry = carry + totals[rows, :]
        carry_ref[...] = carry
        k_all = jnp.concatenate([kbuf[slot, r] for r in range(pps)], axis=1)
        kh, kl = _split2(k_all)
        a = _dot(q16, kh)
        logits = (a[0:H_IDX] + a[H_IDX:2 * H_IDX]) + _dot(qh, kl)
        sc_ref[chunk] = jnp.sum(wi * jnp.maximum(logits, 0.0), axis=0, keepdims=True)

    def pair_body(sp, carry):
        s0 = 2 * sp
        start(b, s0 + 1, 1)
        wait(b, s0, 0)
        compute(s0, 0)

        @pl.when(s0 + 2 < n_chunks)
        def _():
            start(b, s0 + 2, 0)

        @pl.when(jnp.logical_and(s0 + 2 == n_chunks, b + 1 < n_rows))
        def _():
            start(b + 1, 0, 0)

        wait(b, s0 + 1, 1)
        compute(s0 + 1, 1)
        return carry

    lax.fori_loop(0, n_chunks // 2, pair_body, 0)


def _sample_scan(page_table, q16, wi_col, logf_new_col, cache_idx_k, logf_t_cache, layer, pps, attn_pps):
    db, n_pages = page_table.shape
    n_chunks = n_pages // pps
    assert n_chunks % 2 == 0 and n_chunks * pps == n_pages and pps % attn_pps == 0
    n_attn = n_pages // attn_pps
    per_b = lambda shape: pl.BlockSpec((None,) + shape, lambda b, pt: (b,) + (0,) * len(shape))
    hbm = pl.BlockSpec(memory_space=pl.ANY)
    grid_spec = pltpu.PrefetchScalarGridSpec(
        num_scalar_prefetch=1,
        grid=(db,),
        in_specs=[per_b((2 * H_IDX, D_IDX)), per_b((H_IDX, 1)), per_b((H_A, 1)), hbm, hbm],
        out_specs=[per_b((n_chunks, 1, pps * PAGE_SIZE)), per_b((n_attn, H_A, attn_pps * PAGE_SIZE))],
        scratch_shapes=[pltpu.VMEM((2, pps, D_IDX, PAGE_SIZE), F32), pltpu.VMEM((2, pps, H_A, PAGE_SIZE), F32),
                        pltpu.SemaphoreType.DMA((2,)), pltpu.VMEM((H_A, LANES), F32)],
    )
    return pl.pallas_call(
        functools.partial(_sscan_kernel, pps=pps, n_chunks=n_chunks, attn_pps=attn_pps, layer=layer),
        grid_spec=grid_spec,
        out_shape=[jax.ShapeDtypeStruct((db, n_chunks, 1, pps * PAGE_SIZE), F32),
                   jax.ShapeDtypeStruct((db, n_attn, H_A, attn_pps * PAGE_SIZE), F32)],
        compiler_params=_cparams(("arbitrary",)),
        name="sample_scan",
    )(page_table, q16, wi_col, logf_new_col, cache_idx_k, logf_t_cache)


def _sselect_kernel(sc_ref, q_ref, kt_ref, wi_ref, mb_ref, mbn_ref, keys_ref, *, topk, chunk):
    db, past = sc_ref.shape
    prod = q_ref[...] * kt_ref[...]
    lane = lax.broadcasted_iota(jnp.int32, prod.shape, 1)
    wi = wi_ref[...]
    sc_new = jnp.zeros((db, 1), F32)
    for h in range(H_IDX):
        seg = (lane >= h * D_IDX) & (lane < (h + 1) * D_IDX)
        logit = jnp.sum(jnp.where(seg, prod, 0.0), axis=1, keepdims=True)
        sc_new = sc_new + wi[:, h:h + 1] * jnp.maximum(logit, 0.0)
    key_new = _order_key(sc_new)
    keys_ref[...] = _order_key(sc_ref[...])

    def count(pred_fn):
        n = jnp.sum(jnp.where(pred_fn(keys_ref[...]), 1.0, 0.0), axis=1, keepdims=True)
        return n + jnp.where(pred_fn(key_new), 1.0, 0.0)

    def bit_body(b, t):
        cand = t + lax.shift_left(jnp.int32(1), 31 - b)
        return jnp.where(count(lambda k: k >= cand) >= topk, cand, t)

    thr = lax.fori_loop(0, 32, bit_body, jnp.full((db, 1), INT_MIN, jnp.int32))
    need = topk - count(lambda k: k > thr)
    sut = _strict_upper(chunk)
    seen = jnp.zeros((db, 1), F32)
    for c in range(past // chunk):
        cols = slice(c * chunk, (c + 1) * chunk)
        kblk = keys_ref[:, cols]
        eq = kblk == thr
        eqf = jnp.where(eq, 1.0, 0.0)
        rank = _dot(eqf.astype(BF16), sut) + seen
        mb_ref[:, cols] = jnp.where(kblk > thr, 0.0,
                                    jnp.where(eq, jnp.where(rank < need, 0.0, NEG_BIG), NEG_BIG))
        seen = seen + jnp.sum(eqf, axis=1, keepdims=True)
    sel_new = jnp.where(key_new > thr, 0.0,
                        jnp.where(key_new == thr, jnp.where(seen < need, 0.0, NEG_BIG), NEG_BIG))
    mbn_ref[...] = jnp.broadcast_to(sel_new, mbn_ref.shape)


def _sample_select(scores, q_f32, ki_tiled, wi, topk):
    db, past = scores.shape
    return pl.pallas_call(
        functools.partial(_sselect_kernel, topk=topk, chunk=2 * LANES),
        out_shape=[jax.ShapeDtypeStruct((db, past), F32), jax.ShapeDtypeStruct((db, LANES), F32)],
        scratch_shapes=[pltpu.VMEM((db, past), jnp.int32)],
        compiler_params=pltpu.CompilerParams(vmem_limit_bytes=VMEM_LIMIT_BYTES),
        name="sample_select",
    )(scores, q_f32, ki_tiled, wi)


def _sattn_kernel(pt_ref, qa_ref, qb_ref, dec_ref, mb_ref, rbt_ref, kan_ref, van_ref, kbn_ref, vbn_ref,
                  mbn_ref, ak_hbm, av_hbm, bk_hbm, bv_hbm, oa_ref, ob_ref,
                  buf, sem, ma_ref, la_ref, acca_ref, mb_m_ref, lb_ref, accb_ref, *, pps, past, layer):
    b = pl.program_id(0)
    n_rows = pl.num_programs(0)
    width = pps * PAGE_SIZE
    n_chunks = past // width
    caches = (ak_hbm, av_hbm, bk_hbm, bv_hbm)

    def page_copy(row, chunk, slot, t, r):
        page = pt_ref[row, chunk * pps + r]
        return pltpu.make_async_copy(caches[t].at[layer, page], buf.at[slot, t, r], sem.at[slot])

    def start_chunk(row, chunk, slot):
        for t in range(len(caches)):
            for r in range(pps):
                page_copy(row, chunk, slot, t, r).start(priority=r % 2)

    def wait_chunk(row, chunk, slot):
        for t in range(len(caches)):
            for r in range(pps):
                page_copy(row, chunk, slot, t, r).wait()

    @pl.when(b == 0)
    def _():
        start_chunk(0, 0, 0)

    for m_ref, l_ref, acc_ref in ((ma_ref, la_ref, acca_ref), (mb_m_ref, lb_ref, accb_ref)):
        m_ref[...] = jnp.full_like(m_ref, NEG_BIG)
        l_ref[...] = jnp.zeros_like(l_ref)
        acc_ref[...] = jnp.zeros_like(acc_ref)

    qa, qb = qa_ref[...], qb_ref[...]
    rbt = rbt_ref[...]

    def update(s, slot, t, m_ref, l_ref, acc_ref):
        m_old = m_ref[...]
        m_new = jnp.maximum(m_old, jnp.max(s, axis=1, keepdims=True))
        a = jnp.exp(m_old - m_new)
        p = jnp.exp(s - m_new)
        l_ref[...] = a * l_ref[...] + jnp.sum(p, axis=1, keepdims=True)
        pv = jnp.zeros(acc_ref.shape, F32)
        for r in range(pps):
            pv = pv + _dot_nt(p[:, r * PAGE_SIZE:(r + 1) * PAGE_SIZE].astype(BF16),
                              buf[slot, t, r].astype(BF16))
        acc_ref[...] = a * acc_ref[...] + pv
        m_ref[...] = m_new

    def compute(chunk, slot):
        s_a = jnp.concatenate([_dot(qa, buf[slot, 0, r].astype(BF16)) for r in range(pps)], axis=1)
        update(s_a + dec_ref[chunk], slot, 1, ma_ref, la_ref, acca_ref)
        pos = chunk * width + lax.broadcasted_iota(jnp.int32, (1, width), 1)
        bucket = _rel_bucket(past - pos)
        rel = jnp.zeros((H_B, width), F32)
        for n in range(N_BUCKETS):
            rel = jnp.where(bucket == n, rbt[:, n:n + 1], rel)
        s_b = jnp.concatenate([_dot(qb, buf[slot, 2, r].astype(BF16)) for r in range(pps)], axis=1)
        update(s_b + rel + mb_ref[chunk], slot, 3, mb_m_ref, lb_ref, accb_ref)

    def pair_body(cp, carry):
        c0 = 2 * cp
        start_chunk(b, c0 + 1, 1)
        wait_chunk(b, c0, 0)
        compute(c0, 0)

        @pl.when(c0 + 2 < n_chunks)
        def _():
            start_chunk(b, c0 + 2, 0)

        @pl.when(jnp.logical_and(c0 + 2 == n_chunks, b + 1 < n_rows))
        def _():
            start_chunk(b + 1, 0, 0)

        wait_chunk(b, c0 + 1, 1)
        compute(c0 + 1, 1)
        return carry

    lax.fori_loop(0, n_chunks // 2, pair_body, 0)

    def finish(q, kn, vn, extra, m_ref, l_ref, acc_ref, o_ref):
        s = jnp.sum(q.astype(F32) * kn, axis=1, keepdims=True) + extra
        m_old = m_ref[...]
        m_new = jnp.maximum(m_old, s)
        a = jnp.exp(m_old - m_new)
        p = jnp.exp(s - m_new)
        l = a * l_ref[...] + p
        o_ref[...] = (a * acc_ref[...] + p * vn) / l

    finish(qa, kan_ref[...], van_ref[...], 0.0, ma_ref, la_ref, acca_ref, oa_ref)
    finish(qb, kbn_ref[...], vbn_ref[...], rbt[:, 0:1] + mbn_ref[:, 0:1], mb_m_ref, lb_ref, accb_ref, ob_ref)


def _sample_attention(page_table, qa_blk, qb_blk, dec, mb, rbt, new, caches, layer, pps):
    db, n_pages = page_table.shape
    past = n_pages * PAGE_SIZE
    width = KV_A * HEAD_DIM
    n_chunks = n_pages // pps
    assert n_chunks % 2 == 0 and n_chunks * pps == n_pages
    per_b = lambda shape: pl.BlockSpec((None,) + shape, lambda b, pt: (b,) + (0,) * len(shape))
    hbm = pl.BlockSpec(memory_space=pl.ANY)
    in_specs = [per_b((H_A, width)), per_b((H_B, width)),
                per_b((n_chunks, H_A, pps * PAGE_SIZE)), per_b((n_chunks, 1, pps * PAGE_SIZE)),
                pl.BlockSpec(rbt.shape, lambda b, pt: (0, 0)),
                per_b((1, width)), per_b((1, width)), per_b((1, width)), per_b((1, width)),
                per_b((1, LANES)), hbm, hbm, hbm, hbm]
    grid_spec = pltpu.PrefetchScalarGridSpec(
        num_scalar_prefetch=1,
        grid=(db,),
        in_specs=in_specs,
        out_specs=[per_b((H_A, width)), per_b((H_B, width))],
        scratch_shapes=[pltpu.VMEM((2, len(caches), pps, width, PAGE_SIZE), F32), pltpu.SemaphoreType.DMA((2,)),
                        pltpu.VMEM((H_A, 1), F32), pltpu.VMEM((H_A, 1), F32), pltpu.VMEM((H_A, width), F32),
                        pltpu.VMEM((H_B, 1), F32), pltpu.VMEM((H_B, 1), F32), pltpu.VMEM((H_B, width), F32)],
    )
    return pl.pallas_call(
        functools.partial(_sattn_kernel, pps=pps, past=past, layer=layer),
        grid_spec=grid_spec,
        out_shape=[jax.ShapeDtypeStruct((db, H_A, width), F32), jax.ShapeDtypeStruct((db, H_B, width), F32)],
        compiler_params=_cparams(("arbitrary",)),
        name="sample_attention",
    )(page_table, qa_blk, qb_blk, dec, mb, rbt, new['ka'], new['va'], new['kb'], new['vb'], new['mbn'], *caches)


def _head_cols(w, perm):
    return jnp.concatenate([w[:, h * HEAD_DIM:(h + 1) * HEAD_DIM] for h in perm], axis=1)


def _pad_cols(w, n):
    return jnp.pad(w, ((0, 0), (0, n - w.shape[1])))


def _split_w_in(w_in_l):
    sizes = (H_A * HEAD_DIM, KV_A * HEAD_DIM, KV_A * HEAD_DIM, H_A, H_B * HEAD_DIM, KV_B * HEAD_DIM,
             KV_B * HEAD_DIM, H_IDX * D_IDX, D_IDX, H_IDX)
    offs = np.cumsum((0,) + sizes)
    return [w_in_l[:, offs[n]:offs[n + 1]] for n in range(len(sizes))]


def _prep_proj_weights(w_in_l, perm):
    qa, ka, va, fa, qb, kb, vb, qi, ki, wi = _split_w_in(w_in_l)
    scale = HEAD_DIM ** -0.5
    big = jnp.concatenate([_head_cols(qa, perm) * scale, _head_cols(qb, perm) * scale, ka, va, kb, vb],
                          axis=1).astype(BF16)
    small = _pad_cols(jnp.concatenate([fa, wi], axis=1), LANES)
    idx = jnp.concatenate([qi * (D_IDX ** -0.5), ki, ki, small], axis=1)
    idx_hi, idx_lo = _split2(idx)
    return dict(big=big, idx_hi=idx_hi, idx_lo=idx_lo)


def _prep_proj_t_weights(w_in_l, perm):
    qa, ka, va, fa, qb, kb, vb, qi, ki, wi = _split_w_in(w_in_l)
    d = w_in_l.shape[0]
    scale = HEAD_DIM ** -0.5
    t_big = jnp.concatenate([_head_cols(qa, perm) * scale, _head_cols(qb, perm) * scale, va, vb, ka, kb],
                            axis=1).T.astype(BF16)
    zero = jnp.zeros((d, HEAD_DIM), w_in_l.dtype)
    spread = []
    for m in range(KV_A // 2):
        spread += [ka[:, 2 * m * HEAD_DIM:(2 * m + 1) * HEAD_DIM], zero, zero,
                   ka[:, (2 * m + 1) * HEAD_DIM:(2 * m + 2) * HEAD_DIM]]
    n_keys = jnp.concatenate(spread + [kb], axis=1).astype(BF16)
    small = _pad_cols(jnp.concatenate([fa, wi], axis=1), LANES)
    n_idx_hi, n_idx_lo = _split2(jnp.concatenate([ki, ki, small], axis=1))
    t_idx_hi, t_idx_lo = _split2((qi * (D_IDX ** -0.5)).T)
    return dict(t_big=t_big, n_keys=n_keys, n_idx_hi=n_idx_hi, n_idx_lo=n_idx_lo,
                t_idx_hi=t_idx_hi, t_idx_lo=t_idx_lo)


def _prep_post_weights(w_out_l, perm_a, perm_b, ln1_g, ln1_b, w_up_l, w_down_l, ln2_g, ln2_b, ff_chunk):
    d_mix, d = w_out_l.shape
    na = H_A * HEAD_DIM
    rows = lambda w, perm: jnp.concatenate([w[h * HEAD_DIM:(h + 1) * HEAD_DIM] for h in perm], axis=0)
    d_ff = w_up_l.shape[1]
    nc = d_ff // ff_chunk
    row = lambda a: a.reshape(1, -1)
    return dict(out_a=rows(w_out_l[:na], perm_a).astype(BF16), out_b=rows(w_out_l[na:], perm_b).astype(BF16),
                ln1_g=row(ln1_g), ln1_b=row(ln1_b), ln2_g=row(ln2_g), ln2_b=row(ln2_b),
                up=w_up_l.astype(BF16),
                down=w_down_l.reshape(nc, ff_chunk, d).astype(BF16))


ATTN_BLOCK = 256
FF_CHUNK = 512
POST_ROWS = 512
SCAN_PAGES_PER_STEP = 32
ATTN_PAGES_PER_STEP = 16


def _prompt_layer(x2d, batch, seq, wproj, wpost, bf_pad, relb, rb_far, alpha, prev_kv):
    tq = min(ATTN_BLOCK, seq)
    p = _project_t(x2d, wproj, bf_pad, prev_kv, batch=batch, seq=seq, tk=tq)
    oa_t = _fox_attention(p['qat'], p['kaug'], p['vax'], batch, seq, tq)
    wi_t = p['wi'].reshape(batch, seq, H_IDX).transpose(0, 2, 1)
    ob_t = _dsa_attention(p, wi_t, relb, rb_far, batch, seq, tq, min(TOPK_MAX, seq // 4))
    y = _post_block(x2d, oa_t, ob_t, wpost, alpha, seq=seq)
    return (y, (p['vat'], p['vbt'], p['kat'], p['kbt']),
            (p['logf'].reshape(batch, seq, H_A), p['ki'].reshape(batch, seq, D_IDX)))


def _block_diag_q(q):
    db = q.shape[0]
    qh = q.reshape(db, H_A, 1, HEAD_DIM)
    kv_of_head = jnp.arange(H_A) // GROUP
    onehot = (kv_of_head[:, None] == jnp.arange(KV_A)[None, :])[None, :, :, None]
    return jnp.where(onehot, qh, jnp.zeros((), q.dtype)).reshape(db, H_A, KV_A * HEAD_DIM)


def _diag_heads(o_wide):
    db = o_wide.shape[0]
    o = o_wide.reshape(db, H_A, KV_A, HEAD_DIM)
    idx = (jnp.arange(H_A) // GROUP)[None, :, None, None]
    return jnp.take_along_axis(o, jnp.broadcast_to(idx, (db, H_A, 1, HEAD_DIM)), axis=2).reshape(db, -1)


def _sample_layer(x2d, page_table, caches, layer, wproj, wpost, bf_pad, rbt, alpha):
    db = x2d.shape[0]
    n_pages = page_table.shape[1]
    past = n_pages * PAGE_SIZE
    p = _project(x2d, wproj, bf_pad)
    q16 = jnp.concatenate([p['qi_hi'].reshape(db, H_IDX, D_IDX), p['qi_lo'].reshape(db, H_IDX, D_IDX)], axis=1)
    pps = min(ATTN_PAGES_PER_STEP, n_pages // 2)
    scan_pps = min(SCAN_PAGES_PER_STEP, n_pages // 2)
    scores, dec_c = _sample_scan(page_table, q16, p['wi'].reshape(db, H_IDX, 1), p['logf'].reshape(db, H_A, 1),
                                 caches['idx_k'], caches['logf_t'], layer, scan_pps, pps)
    q_f32 = p['qi_hi'].astype(F32) + p['qi_lo'].astype(F32)
    mb, mbn = _sample_select(scores.reshape(db, past), q_f32, jnp.tile(p['ki'], (1, H_IDX)), p['wi'],
                             min(TOPK_MAX, (past + 1) // 4))
    new = dict(ka=p['ka'].reshape(db, 1, -1), va=p['va'].reshape(db, 1, -1), kb=p['kb'].reshape(db, 1, -1),
               vb=p['vb'].reshape(db, 1, -1), mbn=mbn.reshape(db, 1, LANES))
    n_chunks = n_pages // pps
    oa_w, ob_w = _sample_attention(page_table, _block_diag_q(p['qa']), _block_diag_q(p['qb']), dec_c,
                                   mb.reshape(db, n_chunks, 1, pps * PAGE_SIZE), rbt, new,
                                   (caches['a_k'], caches['a_v'], caches['b_k'], caches['b_v']), layer, pps)
    y = _post_block(x2d, _diag_heads(oa_w).astype(BF16), _diag_heads(ob_w).astype(BF16), wpost, alpha)
    return y, (p['ka'], p['va'], p['logf'], p['kb'], p['vb'], p['ki'])


def kernel(x_prompt, x_sample, cache_a_k, cache_a_v, cache_a_logf, cache_b_k, cache_b_v, cache_idx_k,
           page_table, w_in, b_f, w_out, ln1_g, ln1_b, w_up, w_down, ln2_g, ln2_b, rel_bias):
    depth = w_in.shape[0]
    batch, seq, d_model = x_prompt.shape
    db, dec_seq, _ = x_sample.shape
    assert dec_seq == 1
    alpha = (2 * depth) ** 0.25
    natural = tuple(range(H_A))
    tq = min(ATTN_BLOCK, seq)

    kv_t = lambda c: c.transpose(0, 1, 3, 4, 2).reshape(c.shape[:2] + (-1, PAGE_SIZE))
    caches = dict(a_k=kv_t(cache_a_k), a_v=kv_t(cache_a_v), b_k=kv_t(cache_b_k), b_v=kv_t(cache_b_v),
                  idx_k=cache_idx_k.transpose(0, 1, 3, 2),
                  logf_t=cache_a_logf.astype(F32).transpose(0, 1, 3, 2))
    relb = _rel_bias_tiles(rel_bias[:, list(HEAD_PERM)], tq)
    rb_far = rel_bias[N_BUCKETS - 1, list(HEAD_PERM)]
    rbt = rel_bias.T

    xp = x_prompt.reshape(batch * seq, d_model)
    xs = x_sample.reshape(db, d_model)
    p_rows, s_rows = [], []
    p_kv = None
    for l in range(depth):
        bf_pad = _pad_cols(b_f[l].reshape(1, -1), LANES)
        post = lambda perm: _prep_post_weights(w_out[l], perm, perm, ln1_g[l], ln1_b[l], w_up[l], w_down[l],
                                               ln2_g[l], ln2_b[l], FF_CHUNK)
        xp, p_kv, rows = _prompt_layer(xp, batch, seq, _prep_proj_t_weights(w_in[l], HEAD_PERM),
                                       post(HEAD_PERM), bf_pad, relb, rb_far, alpha, p_kv)
        p_rows.append(rows)
        xs, rows = _sample_layer(xs, page_table, caches, l, _prep_proj_weights(w_in[l], natural),
                                 post(natural), bf_pad, rbt, alpha)
        s_rows.append(rows)

    stack = lambda rows, j: jnp.stack([r[j] for r in rows])
    sample = lambda j, tail: stack(s_rows, j).reshape((depth, db, dec_seq) + tail)
    kv = (KV_A, HEAD_DIM)
    prompt_kv = lambda a: a.reshape(depth, batch, KV_A, HEAD_DIM, seq).transpose(0, 1, 4, 2, 3)
    vat, vbt, kat, kbt = p_kv
    return (xp.reshape(batch, seq, d_model), xs.reshape(db, dec_seq, d_model),
            prompt_kv(kat), prompt_kv(vat), stack(p_rows, 0), prompt_kv(kbt), prompt_kv(vbt), stack(p_rows, 1),
            sample(0, kv), sample(1, kv), sample(2, (H_A,)), sample(3, kv), sample(4, kv), sample(5, (D_IDX,)))
```

```python
import functools
import math

import numpy as np
import jax
import jax.numpy as jnp
from jax import lax
from jax.experimental import pallas as pl
from jax.experimental.pallas import tpu as pltpu

HEAD_DIM = 64
H_A = 8
KV_A = 4
H_B = 8
KV_B = 4
GROUP = 2
H_IDX = 8
D_IDX = 64
TOPK_MAX = 256
N_BUCKETS = 32
MAX_DISTANCE = 128
PAGE_SIZE = 128
LN_EPS = 1e-5

LANES = 128
VMEM_LIMIT_BYTES = 56 * 1024 * 1024

NEG_BIG = -1e30
INT_MIN = -(2 ** 31)
HALF_BITS = 16
I16_MIN = -(2 ** 15)

BF16 = jnp.bfloat16
F32 = jnp.float32

HEAD_PERM = (0, 2, 1, 3, 4, 6, 5, 7)


def _cparams(semantics):
    return pltpu.CompilerParams(dimension_semantics=semantics,
                                vmem_limit_bytes=VMEM_LIMIT_BYTES)


def _dot(a, b):
    return jnp.dot(a, b, preferred_element_type=F32)


def _dot_nt(a, b):
    return lax.dot_general(a, b, (((1,), (1,)), ((), ())), preferred_element_type=F32)


def _split2(x):
    hi = x.astype(BF16)
    lo = (x - hi.astype(F32)).astype(BF16)
    return hi, lo


def _split3(x):
    h1 = x.astype(BF16)
    r1 = x - h1.astype(F32)
    h2 = r1.astype(BF16)
    h3 = (r1 - h2.astype(F32)).astype(BF16)
    return h1, h2, h3


def _dot3(xh, xl, wh, wl):
    return _dot(xh, wh) + _dot(xl, wh) + _dot(xh, wl)


def _log_sigmoid(x):
    return jnp.minimum(x, 0.0) - jnp.log1p(jnp.exp(-jnp.abs(x)))


def _layer_norm(x, g, b):
    mu = jnp.mean(x, axis=-1, keepdims=True)
    xc = x - mu
    var = jnp.mean(xc * xc, axis=-1, keepdims=True)
    return xc * lax.rsqrt(var + LN_EPS) * g + b


def _rel_bucket(dist):
    max_exact = N_BUCKETS // 2
    d = jnp.maximum(dist, 1).astype(F32)
    large = max_exact + (jnp.log(d / max_exact) / math.log(MAX_DISTANCE / max_exact)
                         * (N_BUCKETS - max_exact)).astype(jnp.int32)
    large = jnp.minimum(large, N_BUCKETS - 1)
    return jnp.where(dist < max_exact, dist, large)


def _order_key(score):
    bits = pltpu.bitcast(score, jnp.int32)
    key = bits ^ (lax.shift_right_arithmetic(bits, 31) & jnp.int32(0x7FFFFFFF))
    return jnp.where(bits == jnp.int32(INT_MIN), 0, key)


def _proj_kernel(x_ref, wbig_ref, wih_ref, wil_ref, bf_ref,
                 qa_ref, qb_ref, ka_ref, va_ref, kb_ref, vb_ref, qih_ref, qil_ref, ki_ref, kih_ref, kil_ref,
                 wi_ref, logf_ref):
    x = x_ref[...]
    xh, xl = _split2(x)
    nq = H_A * HEAD_DIM
    nk = KV_A * HEAD_DIM
    qa_ref[...] = _dot(xh, wbig_ref[:, 0:nq]).astype(BF16)
    qb_ref[...] = _dot(xh, wbig_ref[:, nq:2 * nq]).astype(BF16)
    o = 2 * nq
    ka_ref[...] = _dot(xh, wbig_ref[:, o:o + nk])
    va_ref[...] = _dot(xh, wbig_ref[:, o + nk:o + 2 * nk])
    kb_ref[...] = _dot(xh, wbig_ref[:, o + 2 * nk:o + 3 * nk])
    vb_ref[...] = _dot(xh, wbig_ref[:, o + 3 * nk:o + 4 * nk])

    ni = H_IDX * D_IDX
    qi = _dot3(xh, xl, wih_ref[:, 0:ni], wil_ref[:, 0:ni])
    qh, ql = _split2(qi)
    qih_ref[...] = qh
    qil_ref[...] = ql
    rest = _dot3(xh, xl, wih_ref[:, ni:ni + 2 * LANES], wil_ref[:, ni:ni + 2 * LANES])
    ki2 = rest[:, 0:LANES]
    ki_ref[...] = ki2[:, 0:D_IDX]
    kh, kl = _split2(ki2)
    kih_ref[...] = kh
    kil_ref[...] = kl

    small = rest[:, LANES:2 * LANES]
    wi_ref[...] = small[:, H_A:H_A + H_IDX] * (H_IDX ** -0.5)
    logf = _log_sigmoid(small + bf_ref[...])
    logf_ref[...] = logf[:, 0:H_A]


def _project(x2d, wts, bf_pad):
    m, d = x2d.shape
    tm = min(512, m)
    assert m % tm == 0
    nq, nk, ni = H_A * HEAD_DIM, KV_A * HEAD_DIM, H_IDX * D_IDX
    full = lambda a: pl.BlockSpec(a.shape, lambda i: (0,) * a.ndim)
    row = lambda n: pl.BlockSpec((tm, n), lambda i: (i, 0))
    inputs = [x2d, wts['big'], wts['idx_hi'], wts['idx_lo'], bf_pad]
    out_shapes = [((m, nq), BF16), ((m, nq), BF16), ((m, nk), F32), ((m, nk), F32), ((m, nk), F32),
                  ((m, nk), F32), ((m, ni), BF16), ((m, ni), BF16), ((m, D_IDX), F32),
                  ((m, LANES), BF16), ((m, LANES), BF16), ((m, H_IDX), F32), ((m, H_A), F32)]
    outs = pl.pallas_call(
        _proj_kernel,
        grid=(m // tm,),
        in_specs=[row(d)] + [full(a) for a in inputs[1:]],
        out_specs=[row(s[0][1]) for s in out_shapes],
        out_shape=[jax.ShapeDtypeStruct(*s) for s in out_shapes],
        compiler_params=_cparams(("arbitrary",)),
        name="proj",
    )(*inputs)
    names = ['qa', 'qb', 'ka', 'va', 'kb', 'vb', 'qi_hi', 'qi_lo', 'ki', 'ki2_hi', 'ki2_lo', 'wi', 'logf']
    return dict(zip(names, outs))


def _proj_t_kernel(*refs, tiles_per_seq, tk, n_prev):
    (x_ref, wt_ref, wn_ref, wnh_ref, wnl_ref, wth_ref, wtl_ref, bf_ref, tri_ref,
     p1_ref, p2_ref, p3_ref) = refs[:12]
    prev = refs[12:16] if n_prev else (None,) * 4
    (qat_ref, qbt_ref, vat_ref, vbt_ref, kat_ref, kbt_ref, vax_ref, vbx_ref,
     kaug_ref, kbn_ref, qith_ref, qitl_ref, ki_ref, kih_ref, wi_ref, logf_ref,
     carry_ref) = refs[12 + (4 if n_prev else 0):]

    def emit(out_ref, prev_ref, value):
        if n_prev:
            out_ref[0:n_prev] = prev_ref[...]
        out_ref[n_prev] = value

    x = x_ref[...]
    xh, xl = _split2(x)
    tm = x.shape[0]
    nq = H_A * HEAD_DIM
    nk = KV_A * HEAD_DIM

    qat_ref[...] = _dot_nt(wt_ref[0:nq, :], xh).astype(BF16)
    qbt_ref[...] = _dot_nt(wt_ref[nq:2 * nq, :], xh).astype(BF16)
    o = 2 * nq
    row = lax.broadcasted_iota(jnp.int32, (2 * HEAD_DIM, tm), 0)
    lo_rows = row < HEAD_DIM
    for t_ref, p_ref, x_out_ref, r0 in ((vat_ref, prev[0], vax_ref, o), (vbt_ref, prev[1], vbx_ref, o + nk)):
        vt = _dot_nt(wt_ref[r0:r0 + nk, :], xh)
        emit(t_ref, p_ref, vt)
        for m in range(KV_A // 2):
            blk = vt[m * 2 * HEAD_DIM:(m + 1) * 2 * HEAD_DIM, :]
            lo = jnp.where(lo_rows, blk, 1.0).astype(BF16)
            hi = jnp.where(lo_rows, 1.0, blk).astype(BF16)
            for t in range(tm // tk):
                cols = slice(t * tk, (t + 1) * tk)
                x_out_ref[t, m * 4 * HEAD_DIM:m * 4 * HEAD_DIM + 2 * HEAD_DIM, :] = lo[:, cols]
                x_out_ref[t, m * 4 * HEAD_DIM + 2 * HEAD_DIM:(m + 1) * 4 * HEAD_DIM, :] = hi[:, cols]
    emit(kat_ref, prev[2], _dot_nt(wt_ref[o + 2 * nk:o + 3 * nk, :], xh))
    emit(kbt_ref, prev[3], _dot_nt(wt_ref[o + 3 * nk:o + 4 * nk, :], xh))

    kn = _dot(xh, wn_ref[...])
    kbn_ref[...] = kn[:, 2 * nk:3 * nk].astype(BF16)

    qit = (_dot_nt(wth_ref[...], xh) + _dot_nt(wth_ref[...], xl)) + _dot_nt(wtl_ref[...], xh)
    qh, ql = _split2(qit)
    qith_ref[...] = qh
    qitl_ref[...] = ql

    rest = _dot3(xh, xl, wnh_ref[...], wnl_ref[...])
    ki2 = rest[:, 0:LANES]
    ki_ref[...] = ki2[:, 0:D_IDX]
    kih_ref[...] = ki2.astype(BF16)
    small = rest[:, LANES:2 * LANES]
    wi_ref[...] = small[:, H_A:H_A + H_IDX] * (H_IDX ** -0.5)
    logf = _log_sigmoid(small + bf_ref[...])
    logf_ref[...] = logf[:, 0:H_A]

    @pl.when(pl.program_id(0) % tiles_per_seq == 0)
    def _():
        carry_ref[...] = jnp.zeros_like(carry_ref)

    h1, h2, h3 = _split3(logf)
    tri = tri_ref[...]
    c = (_dot(tri, h1) + _dot(tri, h2)) + _dot(tri, h3) + carry_ref[...]
    carry_ref[...] = c[tm - 1:tm, :]
    c1, c2, c3 = _split3(c)
    aug = (_dot(c1, p1_ref[...]) + _dot(c2, p2_ref[...])) + _dot(c3, p3_ref[...])
    kaug_ref[...] = (kn[:, 0:2 * nk] + aug).astype(BF16)


def _decay_placement():
    mats = np.zeros((3, LANES, 2 * KV_A * HEAD_DIM), np.float32)
    for m in range(KV_A // 2):
        for v in range(2 * GROUP):
            head = HEAD_PERM[4 * m + v]
            hi_variant, g = v % 2, v // 2
            base = m * 4 * HEAD_DIM + (2 * HEAD_DIM if hi_variant else HEAD_DIM)
            for t in range(3):
                mats[t, head, base + 3 * g + t] = -1.0
    return [jnp.asarray(mats[t], BF16) for t in range(3)]


def _project_t(x2d, wts, bf_pad, prev, *, batch, seq, tk):
    n_prev = 0 if prev is None else prev[0].shape[0]
    m, d = x2d.shape
    tm = min(512, seq)
    assert seq % tm == 0 and tm % tk == 0
    tps = seq // tm
    nq, nk, ni = H_A * HEAD_DIM, KV_A * HEAD_DIM, H_IDX * D_IDX
    nkv = seq // tk
    full = lambda a: pl.BlockSpec(a.shape, lambda i: (0,) * a.ndim)
    row = lambda n: pl.BlockSpec((tm, n), lambda i: (i, 0))
    feat = lambda r: pl.BlockSpec((None, r, tm), lambda i: (i // tps, 0, i % tps))
    vx = pl.BlockSpec((None, tm // tk, 2 * nk, tk), lambda i: (i // tps, i % tps, 0, 0))
    tri = jnp.tril(jnp.ones((tm, tm), F32)).astype(BF16)
    inputs = [x2d, wts['t_big'], wts['n_keys'], wts['n_idx_hi'], wts['n_idx_lo'], wts['t_idx_hi'],
              wts['t_idx_lo'], bf_pad, tri] + _decay_placement()
    in_specs = [row(d)] + [full(a) for a in inputs[1:]]
    layers = lambda n: pl.BlockSpec((n, None, nk, tm), lambda i: (0, i // tps, 0, i % tps))
    if n_prev:
        inputs += list(prev)
        in_specs += [layers(n_prev)] * 4
    fshape = lambda r, dt: ((batch, r, seq), dt)
    stacked = (((n_prev + 1, batch, nk, seq), F32), layers(n_prev + 1))
    outs = [
        (fshape(nq, BF16), feat(nq)), (fshape(nq, BF16), feat(nq)),
        stacked, stacked, stacked, stacked,
        (((batch, nkv, 2 * nk, tk), BF16), vx), (((batch, nkv, 2 * nk, tk), BF16), vx),
        (((m, 2 * nk), BF16), row(2 * nk)), (((m, nk), BF16), row(nk)),
        (fshape(ni, BF16), feat(ni)), (fshape(ni, BF16), feat(ni)),
        (((m, D_IDX), F32), row(D_IDX)), (((m, LANES), BF16), row(LANES)),
        (((m, H_IDX), F32), row(H_IDX)), (((m, H_A), F32), row(H_A)),
    ]
    res = pl.pallas_call(
        functools.partial(_proj_t_kernel, tiles_per_seq=tps, tk=tk, n_prev=n_prev),
        grid=(m // tm,),
        in_specs=in_specs,
        out_specs=[o[1] for o in outs],
        out_shape=[jax.ShapeDtypeStruct(*o[0]) for o in outs],
        scratch_shapes=[pltpu.VMEM((1, LANES), F32)],
        compiler_params=_cparams(("arbitrary",)),
        name="proj_prompt",
    )(*inputs)
    names = ['qat', 'qbt', 'vat', 'vbt', 'kat', 'kbt', 'vax', 'vbx', 'kaug', 'kbn', 'qit_hi', 'qit_lo',
             'ki', 'ki2_hi', 'wi', 'logf']
    return dict(zip(names, res))


def _post_kernel(x_ref, oa_ref, ob_ref, woa_ref, wob_ref, g1_ref, b1_ref, wup_ref, wdn_ref,
                 g2_ref, b2_ref, y_ref, acc_ref, *, alpha, n_chunks, feature_major):
    x = x_ref[...]
    if feature_major:
        tn = lambda a, w: lax.dot_general(a, w, (((0,), (0,)), ((), ())), preferred_element_type=F32)
        att = tn(oa_ref[...], woa_ref[...]) + tn(ob_ref[...], wob_ref[...])
    else:
        att = _dot(oa_ref[...], woa_ref[...]) + _dot(ob_ref[...], wob_ref[...])
    x1 = _layer_norm(alpha * x + att, g1_ref[...], b1_ref[...])
    x1b = x1.astype(BF16)
    acc_ref[...] = jnp.zeros_like(acc_ref)

    def body(c, carry):
        cols = pl.ds(pl.multiple_of(c * FF_CHUNK, FF_CHUNK), FF_CHUNK)
        u = jnp.maximum(_dot(x1b, wup_ref[:, cols]), 0.0)
        acc_ref[...] += _dot((u * u).astype(BF16), wdn_ref[c])
        return carry

    lax.fori_loop(0, n_chunks, body, 0)
    y_ref[...] = _layer_norm(alpha * x1 + acc_ref[...], g2_ref[...], b2_ref[...])


def _post_block(x2d, oa, ob, wts, alpha, seq=None):
    m, d = x2d.shape
    tm = min(POST_ROWS, m if seq is None else seq)
    n_chunks = wts['down'].shape[0]
    full = lambda a: pl.BlockSpec(a.shape, lambda i: (0,) * a.ndim, pipeline_mode=pl.Buffered(1))
    row = lambda n: pl.BlockSpec((tm, n), lambda i: (i, 0))
    if seq is None:
        o_spec = lambda a: row(a.shape[1])
    else:
        tps = seq // tm
        o_spec = lambda a: pl.BlockSpec((None, a.shape[1], tm), lambda i: (i // tps, 0, i % tps))
    weights = [wts['out_a'], wts['out_b'], wts['ln1_g'], wts['ln1_b'], wts['up'], wts['down'],
               wts['ln2_g'], wts['ln2_b']]
    return pl.pallas_call(
        functools.partial(_post_kernel, alpha=alpha, n_chunks=n_chunks, feature_major=seq is not None),
        grid=(m // tm,),
        in_specs=[row(d), o_spec(oa), o_spec(ob)] + [full(w) for w in weights],
        out_specs=row(d),
        out_shape=jax.ShapeDtypeStruct((m, d), F32),
        scratch_shapes=[pltpu.VMEM((tm, d), F32)],
        compiler_params=_cparams(("arbitrary",)),
        name="post_block",
    )(x2d, oa, ob, *weights)


def _variant_rows(g, hi_half, ones_from=None):
    row = lax.broadcasted_iota(jnp.int32, g.shape, 0)
    keep = (row >= HEAD_DIM) if hi_half else (row < HEAD_DIM)
    fill = 0.0
    if ones_from is not None:
        fill = jnp.where((row >= ones_from) & (row < ones_from + 3), 1.0, 0.0)
    return jnp.where(keep, g, fill).astype(BF16)


def _softmax_chunk(s, causal_q0, m_ref, p_ref, acc_ref, cols, acols):
    if causal_q0 is not None:
        key = lax.broadcasted_iota(jnp.int32, s.shape, 0)
        qry = lax.broadcasted_iota(jnp.int32, s.shape, 1) + causal_q0
        s = jnp.where(key <= qry, s, NEG_BIG)
    m_old = m_ref[:, cols]
    m_new = jnp.maximum(m_old, jnp.max(s, axis=0, keepdims=True))
    m_ref[:, cols] = m_new
    p_ref[:, cols] = jnp.exp(s - m_new[0:1, :]).astype(BF16)
    acc_ref[:, acols] = acc_ref[:, acols] * jnp.exp(m_old - m_new)[0:1, :]


def _finish_pair(acc_lo, acc_hi, tq):
    row = lax.broadcasted_iota(jnp.int32, (2 * HEAD_DIM, tq), 0)
    lo_rows = row < HEAD_DIM
    out = []
    for g in range(GROUP):
        cols = slice(g * tq, (g + 1) * tq)
        a_lo, a_hi = acc_lo[:, cols], acc_hi[:, cols]
        o_lo = a_lo / a_lo[HEAD_DIM:HEAD_DIM + 1, :]
        o_hi = a_hi / a_hi[0:1, :]
        out.append(jnp.where(lo_rows, o_lo, o_hi))
    return jnp.concatenate(out, axis=0).astype(BF16)


def _fox_kernel(q_ref, k_ref, v_ref, o_ref, qt_ref, s_ref, p_ref, m_ref, acc_ref, *, tq):
    i = pl.program_id(2)
    q = q_ref[...].astype(F32)
    g0, g1 = q[0:2 * HEAD_DIM], q[2 * HEAD_DIM:4 * HEAD_DIM]
    qt_ref[0, :, 0:tq] = _variant_rows(g0, False, HEAD_DIM)
    qt_ref[0, :, tq:2 * tq] = _variant_rows(g1, False, HEAD_DIM + 3)
    qt_ref[1, :, 0:tq] = _variant_rows(g0, True, 0)
    qt_ref[1, :, tq:2 * tq] = _variant_rows(g1, True, 3)
    m_ref[...] = jnp.full_like(m_ref, NEG_BIG)
    acc_ref[...] = jnp.zeros_like(acc_ref)

    def process(blocks):
        for slot, (j, _) in enumerate(blocks):
            start = pl.multiple_of(j * tq, tq)
            for d in range(2):
                s_ref[slot, d] = _dot(k_ref[pl.ds(start, tq), d * LANES:(d + 1) * LANES], qt_ref[d])
        for slot, (j, causal) in enumerate(blocks):
            for d in range(2):
                for c in range(2 * tq // LANES):
                    cols = slice(c * LANES, (c + 1) * LANES)
                    q0 = (c * LANES) % tq if causal else None
                    _softmax_chunk(s_ref[slot, d, :, cols], q0, m_ref.at[d], p_ref.at[slot, d],
                                   acc_ref.at[d], cols, cols)
                acc_ref[d] += _dot(v_ref[j, d * LANES:(d + 1) * LANES, :], p_ref[slot, d])

    def pair_body(t, carry):
        process(((2 * t, False), (2 * t + 1, False)))
        return carry

    lax.fori_loop(0, i // 2, pair_body, 0)

    @pl.when(i % 2 == 1)
    def _():
        process(((i - 1, False), (i, True)))

    @pl.when(i % 2 == 0)
    def _():
        process(((i, True),))

    o_ref[...] = _finish_pair(acc_ref[0], acc_ref[1], tq)


def _fox_attention(qt, kaug, vx, batch, seq, tq):
    nq = seq // tq
    pair = 4 * HEAD_DIM
    return pl.pallas_call(
        functools.partial(_fox_kernel, tq=tq),
        grid=(batch, KV_A // 2, nq),
        in_specs=[
            pl.BlockSpec((None, pair, tq), lambda b, m, i: (b, m, i)),
            pl.BlockSpec((seq, pair), lambda b, m, i: (b, m)),
            pl.BlockSpec((None, nq, pair, tq), lambda b, m, i: (b, 0, m, 0)),
        ],
        out_specs=pl.BlockSpec((None, pair, tq), lambda b, m, i: (b, m, i)),
        out_shape=jax.ShapeDtypeStruct(qt.shape, BF16),
        scratch_shapes=[pltpu.VMEM((2, LANES, 2 * tq), BF16), pltpu.VMEM((2, 2, tq, 2 * tq), F32),
                        pltpu.VMEM((2, 2, tq, 2 * tq), BF16), pltpu.VMEM((2, 8, 2 * tq), F32),
                        pltpu.VMEM((2, LANES, 2 * tq), F32)],
        compiler_params=_cparams(("arbitrary", "arbitrary", "arbitrary")),
        name="fox_attention",
    )(qt, kaug, vx)


def _relb_kernel(rb_ref, o_ref, *, tq):
    delta = pl.program_id(0)
    h = pl.program_id(1)
    key = lax.broadcasted_iota(jnp.int32, (tq, tq), 0)
    qry = lax.broadcasted_iota(jnp.int32, (tq, tq), 1)
    bucket = _rel_bucket(jnp.maximum(delta * tq + qry - key, 0))
    acc = jnp.zeros((tq, tq), F32)
    for n in range(N_BUCKETS):
        acc = jnp.where(bucket == n, rb_ref[n, h], acc)
    o_ref[...] = acc


def _rel_bias_tiles(rb_perm, tq):
    return pl.pallas_call(
        functools.partial(_relb_kernel, tq=tq),
        grid=(2, H_B),
        in_specs=[pl.BlockSpec(memory_space=pltpu.SMEM)],
        out_specs=pl.BlockSpec((None, None, tq, tq), lambda d, h: (d, h, 0, 0)),
        out_shape=jax.ShapeDtypeStruct((2, H_B, tq, tq), F32),
        compiler_params=_cparams(("arbitrary", "arbitrary")),
        name="rel_bias_tiles",
    )(rb_perm)


def _strict_upper(n):
    r = lax.broadcasted_iota(jnp.int32, (n, n), 0)
    c = lax.broadcasted_iota(jnp.int32, (n, n), 1)
    return jnp.where(r < c, 1.0, 0.0).astype(BF16)


def _strict_lower(n):
    r = lax.broadcasted_iota(jnp.int32, (n, n), 0)
    c = lax.broadcasted_iota(jnp.int32, (n, n), 1)
    return jnp.where(c < r, 1.0, 0.0).astype(BF16)


KEY_ROWS = 64

DSA_VARIANT_ORDER = (0, 2, 1, 3)


def _dsa_kernel(qih_ref, qil_ref, kih_ref, wi_ref, qb_ref, k_ref, v_ref, relb_ref, rbfar_ref,
                o_ref, keys_ref, khi_ref, klo_ref, mb_ref, qi_ref, lg_ref, qt_ref, p_ref, m_ref, acc_ref,
                *, tq, topk):
    i = pl.program_id(1)
    nblk = i + 1
    nlc = tq // LANES

    for h in range(H_IDX):
        rows = slice((h // 2) * LANES, (h // 2 + 1) * LANES)
        cols = slice(h * tq, (h + 1) * tq)
        qi_ref[0:LANES, cols] = _variant_rows(qih_ref[rows, :].astype(F32), h % 2 == 1)
        qi_ref[LANES:2 * LANES, cols] = _variant_rows(qil_ref[rows, :].astype(F32), h % 2 == 1)

    def score_block(j, diag):
        start = pl.multiple_of(j * tq, tq)
        kh = kih_ref[pl.ds(start, tq), :]
        lg_ref[...] = _dot(jnp.concatenate([kh, kh], axis=1), qi_ref[...])
        for c in range(nlc):
            for r in range(tq // KEY_ROWS):
                rows = slice(r * KEY_ROWS, (r + 1) * KEY_ROWS)
                sc = jnp.zeros((KEY_ROWS, LANES), F32)
                for h in range(H_IDX):
                    lane0 = h * tq + c * LANES
                    sc = sc + wi_ref[h:h + 1, c * LANES:(c + 1) * LANES] * jnp.maximum(
                        lg_ref[rows, lane0:lane0 + LANES], 0.0)
                key = _order_key(sc)
                if diag:
                    kidx = lax.broadcasted_iota(jnp.int32, key.shape, 0) + r * KEY_ROWS
                    qidx = lax.broadcasted_iota(jnp.int32, key.shape, 1) + c * LANES
                    key = jnp.where(kidx <= qidx, key, INT_MIN)
                keys_ref[j, rows, c * LANES:(c + 1) * LANES] = key
                khi_ref[j, rows, c * LANES:(c + 1) * LANES] = lax.shift_right_arithmetic(
                    key, HALF_BITS).astype(jnp.int16)

    def score_body(j, carry):
        score_block(j, False)
        return carry

    lax.fori_loop(0, i, score_body, 0)
    score_block(i, True)

    needs_select = (i + 1) * tq > topk

    @pl.when(jnp.logical_not(needs_select))
    def _():
        def zero_body(j, carry):
            mb_ref[j] = jnp.zeros((tq, tq), F32)
            return carry
        lax.fori_loop(0, nblk, zero_body, 0)

    @pl.when(needs_select)
    def _():
        npairs = (nblk + 1) // 2

        @pl.when(nblk % 2 == 1)
        def _():
            khi_ref[nblk] = jnp.full((tq, tq), I16_MIN, jnp.int16)
            klo_ref[nblk] = jnp.full((tq, tq), I16_MIN, jnp.int16)

        def count16(ref, pred_fn):
            def body(t, acc):
                for j in (2 * t, 2 * t + 1):
                    hit = jnp.where(pred_fn(ref[j]), jnp.int16(1), jnp.int16(0))
                    for r in range(tq // 16):
                        acc = acc + hit[r * 16:(r + 1) * 16, :]
                return acc
            acc = lax.fori_loop(0, npairs, body, jnp.zeros((16, tq), jnp.int16))
            return jnp.sum(acc.astype(F32), axis=0, keepdims=True)

        def search16(ref, want):
            def bit_body(b, t):
                cand = t + lax.shift_left(jnp.int32(1), HALF_BITS - 1 - b)
                cand16 = cand.astype(jnp.int16)
                return jnp.where(count16(ref, lambda k: k >= cand16) >= want, cand, t)
            return lax.fori_loop(0, HALF_BITS, bit_body, jnp.full((1, tq), I16_MIN, jnp.int32))

        thr_hi = search16(khi_ref, topk)
        thr_hi16 = thr_hi.astype(jnp.int16)
        above = count16(khi_ref, lambda k: k > thr_hi16)

        def low_body(j, carry):
            kblk = keys_ref[j]
            in_band = lax.shift_right_arithmetic(kblk, HALF_BITS) == thr_hi
            low = (kblk & jnp.int32(2 ** HALF_BITS - 1)) + I16_MIN
            klo_ref[j] = jnp.where(in_band, low, I16_MIN).astype(jnp.int16)
            return carry

        lax.fori_loop(0, nblk, low_body, 0)
        thr_lo = search16(klo_ref, topk - above)
        thr_lo16 = thr_lo.astype(jnp.int16)
        thr = thr_hi * (2 ** HALF_BITS) + (thr_lo - I16_MIN)
        need = topk - (above + count16(klo_ref, lambda k: k > thr_lo16))
        n_eq = count16(klo_ref, lambda k: k == thr_lo16)
        tie_break = jnp.max(jnp.where(n_eq > need, 1.0, 0.0)) > 0.0

        @pl.when(jnp.logical_not(tie_break))
        def _():
            def plain_body(j, carry):
                mb_ref[j] = jnp.where(keys_ref[j] >= thr, 0.0, NEG_BIG)
                return carry
            lax.fori_loop(0, nblk, plain_body, 0)

        @pl.when(tie_break)
        def _():
            earlier = _strict_lower(tq)

            def mask_body(j, seen):
                kblk = keys_ref[j]
                eq = kblk == thr
                eqf = jnp.where(eq, 1.0, 0.0)
                rank = _dot(earlier, eqf.astype(BF16)) + seen
                mb_ref[j] = jnp.where(kblk > thr, 0.0,
                                      jnp.where(eq, jnp.where(rank < need, 0.0, NEG_BIG), NEG_BIG))
                return seen + jnp.sum(eqf, axis=0, keepdims=True)

            lax.fori_loop(0, nblk, mask_body, jnp.zeros((1, tq), F32))

    pair = 4 * HEAD_DIM
    for m in range(KV_B // 2):
        q = qb_ref[m * pair:(m + 1) * pair, :].astype(F32)
        g = (q[0:2 * HEAD_DIM], q[2 * HEAD_DIM:4 * HEAD_DIM])
        for n, v in enumerate(DSA_VARIANT_ORDER):
            qt_ref[:, n * tq:(n + 1) * tq] = _variant_rows(g[v // 2], v % 2 == 1)
        m_ref[...] = jnp.full_like(m_ref, NEG_BIG)
        acc_ref[...] = jnp.zeros_like(acc_ref)

        far = lambda slot, qcols: rbfar_ref[slot]
        near = lambda slot, qcols: relb_ref[1, slot, :, qcols]
        diag = lambda slot, qcols: relb_ref[0, slot, :, qcols]

        def process(blocks):
            for slot, (j, _) in enumerate(blocks):
                start = pl.multiple_of(j * tq, tq)
                lg_ref[:, slot * 4 * tq:(slot + 1) * 4 * tq] = _dot(
                    k_ref[pl.ds(start, tq), m * LANES:(m + 1) * LANES], qt_ref[...])
            for slot, (j, rel) in enumerate(blocks):
                for n, v in enumerate(DSA_VARIANT_ORDER):
                    for c in range(nlc):
                        qcols = slice(c * LANES, (c + 1) * LANES)
                        cols = slice(n * tq + c * LANES, n * tq + (c + 1) * LANES)
                        lcols = slice(slot * 4 * tq + cols.start, slot * 4 * tq + cols.stop)
                        s = lg_ref[:, lcols] + (mb_ref[j, :, qcols] + rel(4 * m + v, qcols))
                        acols = slice((n % 2) * tq + c * LANES, (n % 2) * tq + (c + 1) * LANES)
                        _softmax_chunk(s, c * LANES if rel is diag else None, m_ref, p_ref.at[slot],
                                       acc_ref.at[n // 2], cols, acols)
                for d in range(2):
                    vt = v_ref[j, m * pair + d * LANES:m * pair + (d + 1) * LANES, :]
                    acc_ref[d] += _dot(vt, p_ref[slot, :, d * 2 * tq:(d + 1) * 2 * tq])

        n_far = jnp.maximum(i - 1, 0)

        def far_body(t, carry):
            process(((2 * t, far), (2 * t + 1, far)))
            return carry

        lax.fori_loop(0, n_far // 2, far_body, 0)

        @pl.when(n_far % 2 == 1)
        def _():
            process(((i - 2, far),))

        @pl.when(i >= 1)
        def _():
            process(((i - 1, near), (i, diag)))

        @pl.when(i == 0)
        def _():
            process(((i, diag),))

        o_ref[m * pair:(m + 1) * pair, :] = _finish_pair(acc_ref[0], acc_ref[1], tq)


def _dsa_attention(p, wi_t, relb, rb_far, batch, seq, tq, topk):
    nq = seq // tq
    nf = H_B * HEAD_DIM
    qcol = lambda r: pl.BlockSpec((None, r, tq), lambda b, i: (b, 0, i))
    seqblk = lambda n: pl.BlockSpec((seq, n), lambda b, i: (b, 0))
    return pl.pallas_call(
        functools.partial(_dsa_kernel, tq=tq, topk=topk),
        grid=(batch, nq),
        in_specs=[qcol(H_IDX * D_IDX), qcol(H_IDX * D_IDX), seqblk(LANES), qcol(H_IDX),
                  qcol(nf), seqblk(KV_B * HEAD_DIM),
                  pl.BlockSpec((None, nq, nf, tq), lambda b, i: (b, 0, 0, 0)),
                  pl.BlockSpec(relb.shape, lambda b, i: (0, 0, 0, 0)),
                  pl.BlockSpec(memory_space=pltpu.SMEM)],
        out_specs=qcol(nf),
        out_shape=jax.ShapeDtypeStruct((batch, nf, seq), BF16),
        scratch_shapes=[pltpu.VMEM((nq, tq, tq), jnp.int32), pltpu.VMEM((nq, tq, tq), jnp.int16),
                        pltpu.VMEM((nq, tq, tq), jnp.int16), pltpu.VMEM((nq, tq, tq), F32),
                        pltpu.VMEM((2 * LANES, H_IDX * tq), BF16), pltpu.VMEM((tq, H_IDX * tq), F32),
                        pltpu.VMEM((LANES, 4 * tq), BF16), pltpu.VMEM((2, tq, 4 * tq), BF16),
                        pltpu.VMEM((8, 4 * tq), F32), pltpu.VMEM((2, LANES, 2 * tq), F32)],
        compiler_params=_cparams(("arbitrary", "arbitrary")),
        name="dsa_attention",
    )(p['qit_hi'], p['qit_lo'], p['ki2_hi'], wi_t, p['qbt'], p['kbn'], p['vbx'], relb, rb_far)


def _sscan_kernel(pt_ref, q16_ref, wi_ref, lnew_ref, idx_hbm, logf_hbm, sc_ref, dec_ref,
                  kbuf, fbuf, sem, carry_ref, *, pps, n_chunks, attn_pps, layer):
    b = pl.program_id(0)
    n_rows = pl.num_programs(0)

    def copies(row, step, slot):
        chunk = n_chunks - 1 - step
        out = []
        for r in range(pps):
            page = pt_ref[row, chunk * pps + r]
            out.append(pltpu.make_async_copy(idx_hbm.at[layer, page], kbuf.at[slot, r], sem.at[slot]))
            out.append(pltpu.make_async_copy(logf_hbm.at[layer, page], fbuf.at[slot, r], sem.at[slot]))
        return out

    def start(row, step, slot):
        for n, c in enumerate(copies(row, step, slot)):
            c.start(priority=n % 2)

    def wait(row, step, slot):
        for c in copies(row, step, slot):
            c.wait()

    @pl.when(b == 0)
    def _():
        start(0, 0, 0)

    carry_ref[...] = jnp.broadcast_to(lnew_ref[...], carry_ref.shape)
    q16 = q16_ref[...]
    qh = q16[0:H_IDX]
    wi = wi_ref[...]
    r_i = lax.broadcasted_iota(jnp.int32, (PAGE_SIZE, PAGE_SIZE), 0)
    c_i = lax.broadcasted_iota(jnp.int32, (PAGE_SIZE, PAGE_SIZE), 1)
    later = jnp.where(r_i > c_i, 1.0, 0.0).astype(BF16)
    per = pps // attn_pps

    def compute(step, slot):
        chunk = n_chunks - 1 - step
        x_all = jnp.concatenate([fbuf[slot, r] for r in range(pps)], axis=0)
        h1, h2, h3 = _split3(x_all)
        within = (_dot(h1, later) + _dot(h2, later)) + _dot(h3, later)
        totals = jnp.sum(x_all, axis=1, keepdims=True)
        carry = carry_ref[...]
        for r in reversed(range(pps)):
            rows = slice(r * H_A, (r + 1) * H_A)
            lanes = slice((r % attn_pps) * PAGE_SIZE, (r % attn_pps + 1) * PAGE_SIZE)
            dec_ref[chunk * per + r // attn_pps, :, lanes] = within[rows, :] + carry[:, 0:1]
            carry = carry + totals[rows, :]
        carry_ref[...] = carry
        k_all = jnp.concatenate([kbuf[slot, r] for r in range(pps)], axis=1)
        kh, kl = _split2(k_all)
        a = _dot(q16, kh)
        logits = (a[0:H_IDX] + a[H_IDX:2 * H_IDX]) + _dot(qh, kl)
        sc_ref[chunk] = jnp.sum(wi * jnp.maximum(logits, 0.0), axis=0, keepdims=True)

    def pair_body(sp, carry):
        s0 = 2 * sp
        start(b, s0 + 1, 1)
        wait(b, s0, 0)
        compute(s0, 0)

        @pl.when(s0 + 2 < n_chunks)
        def _():
            start(b, s0 + 2, 0)

        @pl.when(jnp.logical_and(s0 + 2 == n_chunks, b + 1 < n_rows))
        def _():
            start(b + 1, 0, 0)

        wait(b, s0 + 1, 1)
        compute(s0 + 1, 1)
        return carry

    lax.fori_loop(0, n_chunks // 2, pair_body, 0)


def _sample_scan(page_table, q16, wi_col, logf_new_col, cache_idx_k, logf_t_cache, layer, pps, attn_pps):
    db, n_pages = page_table.shape
    n_chunks = n_pages // pps
    assert n_chunks % 2 == 0 and n_chunks * pps == n_pages and pps % attn_pps == 0
    n_attn = n_pages // attn_pps
    per_b = lambda shape: pl.BlockSpec((None,) + shape, lambda b, pt: (b,) + (0,) * len(shape))
    hbm = pl.BlockSpec(memory_space=pl.ANY)
    grid_spec = pltpu.PrefetchScalarGridSpec(
        num_scalar_prefetch=1,
        grid=(db,),
        in_specs=[per_b((2 * H_IDX, D_IDX)), per_b((H_IDX, 1)), per_b((H_A, 1)), hbm, hbm],
        out_specs=[per_b((n_chunks, 1, pps * PAGE_SIZE)), per_b((n_attn, H_A, attn_pps * PAGE_SIZE))],
        scratch_shapes=[pltpu.VMEM((2, pps, D_IDX, PAGE_SIZE), F32), pltpu.VMEM((2, pps, H_A, PAGE_SIZE), F32),
                        pltpu.SemaphoreType.DMA((2,)), pltpu.VMEM((H_A, LANES), F32)],
    )
    return pl.pallas_call(
        functools.partial(_sscan_kernel, pps=pps, n_chunks=n_chunks, attn_pps=attn_pps, layer=layer),
        grid_spec=grid_spec,
        out_shape=[jax.ShapeDtypeStruct((db, n_chunks, 1, pps * PAGE_SIZE), F32),
                   jax.ShapeDtypeStruct((db, n_attn, H_A, attn_pps * PAGE_SIZE), F32)],
        compiler_params=_cparams(("arbitrary",)),
        name="sample_scan",
    )(page_table, q16, wi_col, logf_new_col, cache_idx_k, logf_t_cache)


def _sselect_kernel(sc_ref, q_ref, kt_ref, wi_ref, mb_ref, mbn_ref, keys_ref, *, topk, chunk):
    db, past = sc_ref.shape
    prod = q_ref[...] * kt_ref[...]
    lane = lax.broadcasted_iota(jnp.int32, prod.shape, 1)
    wi = wi_ref[...]
    sc_new = jnp.zeros((db, 1), F32)
    for h in range(H_IDX):
        seg = (lane >= h * D_IDX) & (lane < (h + 1) * D_IDX)
        logit = jnp.sum(jnp.where(seg, prod, 0.0), axis=1, keepdims=True)
        sc_new = sc_new + wi[:, h:h + 1] * jnp.maximum(logit, 0.0)
    key_new = _order_key(sc_new)
    keys_ref[...] = _order_key(sc_ref[...])

    def count(pred_fn):
        n = jnp.sum(jnp.where(pred_fn(keys_ref[...]), 1.0, 0.0), axis=1, keepdims=True)
        return n + jnp.where(pred_fn(key_new), 1.0, 0.0)

    def bit_body(b, t):
        cand = t + lax.shift_left(jnp.int32(1), 31 - b)
        return jnp.where(count(lambda k: k >= cand) >= topk, cand, t)

    thr = lax.fori_loop(0, 32, bit_body, jnp.full((db, 1), INT_MIN, jnp.int32))
    need = topk - count(lambda k: k > thr)
    sut = _strict_upper(chunk)
    seen = jnp.zeros((db, 1), F32)
    for c in range(past // chunk):
        cols = slice(c * chunk, (c + 1) * chunk)
        kblk = keys_ref[:, cols]
        eq = kblk == thr
        eqf = jnp.where(eq, 1.0, 0.0)
        rank = _dot(eqf.astype(BF16), sut) + seen
        mb_ref[:, cols] = jnp.where(kblk > thr, 0.0,
                                    jnp.where(eq, jnp.where(rank < need, 0.0, NEG_BIG), NEG_BIG))
        seen = seen + jnp.sum(eqf, axis=1, keepdims=True)
    sel_new = jnp.where(key_new > thr, 0.0,
                        jnp.where(key_new == thr, jnp.where(seen < need, 0.0, NEG_BIG), NEG_BIG))
    mbn_ref[...] = jnp.broadcast_to(sel_new, mbn_ref.shape)


def _sample_select(scores, q_f32, ki_tiled, wi, topk):
    db, past = scores.shape
    return pl.pallas_call(
        functools.partial(_sselect_kernel, topk=topk, chunk=2 * LANES),
        out_shape=[jax.ShapeDtypeStruct((db, past), F32), jax.ShapeDtypeStruct((db, LANES), F32)],
        scratch_shapes=[pltpu.VMEM((db, past), jnp.int32)],
        compiler_params=pltpu.CompilerParams(vmem_limit_bytes=VMEM_LIMIT_BYTES),
        name="sample_select",
    )(scores, q_f32, ki_tiled, wi)


def _sattn_kernel(pt_ref, qa_ref, qb_ref, dec_ref, mb_ref, rbt_ref, kan_ref, van_ref, kbn_ref, vbn_ref,
                  mbn_ref, ak_hbm, av_hbm, bk_hbm, bv_hbm, oa_ref, ob_ref,
                  buf, sem, ma_ref, la_ref, acca_ref, mb_m_ref, lb_ref, accb_ref, *, pps, past, layer):
    b = pl.program_id(0)
    n_rows = pl.num_programs(0)
    width = pps * PAGE_SIZE
    n_chunks = past // width
    caches = (ak_hbm, av_hbm, bk_hbm, bv_hbm)

    def page_copy(row, chunk, slot, t, r):
        page = pt_ref[row, chunk * pps + r]
        return pltpu.make_async_copy(caches[t].at[layer, page], buf.at[slot, t, r], sem.at[slot])

    def start_chunk(row, chunk, slot):
        for t in range(len(caches)):
            for r in range(pps):
                page_copy(row, chunk, slot, t, r).start(priority=r % 2)

    def wait_chunk(row, chunk, slot):
        for t in range(len(caches)):
            for r in range(pps):
                page_copy(row, chunk, slot, t, r).wait()

    @pl.when(b == 0)
    def _():
        start_chunk(0, 0, 0)

    for m_ref, l_ref, acc_ref in ((ma_ref, la_ref, acca_ref), (mb_m_ref, lb_ref, accb_ref)):
        m_ref[...] = jnp.full_like(m_ref, NEG_BIG)
        l_ref[...] = jnp.zeros_like(l_ref)
        acc_ref[...] = jnp.zeros_like(acc_ref)

    qa, qb = qa_ref[...], qb_ref[...]
    rbt = rbt_ref[...]

    def update(s, slot, t, m_ref, l_ref, acc_ref):
        m_old = m_ref[...]
        m_new = jnp.maximum(m_old, jnp.max(s, axis=1, keepdims=True))
        a = jnp.exp(m_old - m_new)
        p = jnp.exp(s - m_new)
        l_ref[...] = a * l_ref[...] + jnp.sum(p, axis=1, keepdims=True)
        pv = jnp.zeros(acc_ref.shape, F32)
        for r in range(pps):
            pv = pv + _dot_nt(p[:, r * PAGE_SIZE:(r + 1) * PAGE_SIZE].astype(BF16),
                              buf[slot, t, r].astype(BF16))
        acc_ref[...] = a * acc_ref[...] + pv
        m_ref[...] = m_new

    def compute(chunk, slot):
        s_a = jnp.concatenate([_dot(qa, buf[slot, 0, r].astype(BF16)) for r in range(pps)], axis=1)
        update(s_a + dec_ref[chunk], slot, 1, ma_ref, la_ref, acca_ref)
        pos = chunk * width + lax.broadcasted_iota(jnp.int32, (1, width), 1)
        bucket = _rel_bucket(past - pos)
        rel = jnp.zeros((H_B, width), F32)
        for n in range(N_BUCKETS):
            rel = jnp.where(bucket == n, rbt[:, n:n + 1], rel)
        s_b = jnp.concatenate([_dot(qb, buf[slot, 2, r].astype(BF16)) for r in range(pps)], axis=1)
        update(s_b + rel + mb_ref[chunk], slot, 3, mb_m_ref, lb_ref, accb_ref)

    def pair_body(cp, carry):
        c0 = 2 * cp
        start_chunk(b, c0 + 1, 1)
        wait_chunk(b, c0, 0)
        compute(c0, 0)

        @pl.when(c0 + 2 < n_chunks)
        def _():
            start_chunk(b, c0 + 2, 0)

        @pl.when(jnp.logical_and(c0 + 2 == n_chunks, b + 1 < n_rows))
        def _():
            start_chunk(b + 1, 0, 0)

        wait_chunk(b, c0 + 1, 1)
        compute(c0 + 1, 1)
        return carry

    lax.fori_loop(0, n_chunks // 2, pair_body, 0)

    def finish(q, kn, vn, extra, m_ref, l_ref, acc_ref, o_ref):
        s = jnp.sum(q.astype(F32) * kn, axis=1, keepdims=True) + extra
        m_old = m_ref[...]
        m_new = jnp.maximum(m_old, s)
        a = jnp.exp(m_old - m_new)
        p = jnp.exp(s - m_new)
        l = a * l_ref[...] + p
        o_ref[...] = (a * acc_ref[...] + p * vn) / l

    finish(qa, kan_ref[...], van_ref[...], 0.0, ma_ref, la_ref, acca_ref, oa_ref)
    finish(qb, kbn_ref[...], vbn_ref[...], rbt[:, 0:1] + mbn_ref[:, 0:1], mb_m_ref, lb_ref, accb_ref, ob_ref)


def _sample_attention(page_table, qa_blk, qb_blk, dec, mb, rbt, new, caches, layer, pps):
    db, n_pages = page_table.shape
    past = n_pages * PAGE_SIZE
    width = KV_A * HEAD_DIM
    n_chunks = n_pages // pps
    assert n_chunks % 2 == 0 and n_chunks * pps == n_pages
    per_b = lambda shape: pl.BlockSpec((None,) + shape, lambda b, pt: (b,) + (0,) * len(shape))
    hbm = pl.BlockSpec(memory_space=pl.ANY)
    in_specs = [per_b((H_A, width)), per_b((H_B, width)),
                per_b((n_chunks, H_A, pps * PAGE_SIZE)), per_b((n_chunks, 1, pps * PAGE_SIZE)),
                pl.BlockSpec(rbt.shape, lambda b, pt: (0, 0)),
                per_b((1, width)), per_b((1, width)), per_b((1, width)), per_b((1, width)),
                per_b((1, LANES)), hbm, hbm, hbm, hbm]
    grid_spec = pltpu.PrefetchScalarGridSpec(
        num_scalar_prefetch=1,
        grid=(db,),
        in_specs=in_specs,
        out_specs=[per_b((H_A, width)), per_b((H_B, width))],
        scratch_shapes=[pltpu.VMEM((2, len(caches), pps, width, PAGE_SIZE), F32), pltpu.SemaphoreType.DMA((2,)),
                        pltpu.VMEM((H_A, 1), F32), pltpu.VMEM((H_A, 1), F32), pltpu.VMEM((H_A, width), F32),
                        pltpu.VMEM((H_B, 1), F32), pltpu.VMEM((H_B, 1), F32), pltpu.VMEM((H_B, width), F32)],
    )
    return pl.pallas_call(
        functools.partial(_sattn_kernel, pps=pps, past=past, layer=layer),
        grid_spec=grid_spec,
        out_shape=[jax.ShapeDtypeStruct((db, H_A, width), F32), jax.ShapeDtypeStruct((db, H_B, width), F32)],
        compiler_params=_cparams(("arbitrary",)),
        name="sample_attention",
    )(page_table, qa_blk, qb_blk, dec, mb, rbt, new['ka'], new['va'], new['kb'], new['vb'], new['mbn'], *caches)


def _head_cols(w, perm):
    return jnp.concatenate([w[:, h * HEAD_DIM:(h + 1) * HEAD_DIM] for h in perm], axis=1)


def _pad_cols(w, n):
    return jnp.pad(w, ((0, 0), (0, n - w.shape[1])))


def _split_w_in(w_in_l):
    sizes = (H_A * HEAD_DIM, KV_A * HEAD_DIM, KV_A * HEAD_DIM, H_A, H_B * HEAD_DIM, KV_B * HEAD_DIM,
             KV_B * HEAD_DIM, H_IDX * D_IDX, D_IDX, H_IDX)
    offs = np.cumsum((0,) + sizes)
    return [w_in_l[:, offs[n]:offs[n + 1]] for n in range(len(sizes))]


def _prep_proj_weights(w_in_l, perm):
    qa, ka, va, fa, qb, kb, vb, qi, ki, wi = _split_w_in(w_in_l)
    scale = HEAD_DIM ** -0.5
    big = jnp.concatenate([_head_cols(qa, perm) * scale, _head_cols(qb, perm) * scale, ka, va, kb, vb],
                          axis=1).astype(BF16)
    small = _pad_cols(jnp.concatenate([fa, wi], axis=1), LANES)
    idx = jnp.concatenate([qi * (D_IDX ** -0.5), ki, ki, small], axis=1)
    idx_hi, idx_lo = _split2(idx)
    return dict(big=big, idx_hi=idx_hi, idx_lo=idx_lo)


def _prep_proj_t_weights(w_in_l, perm):
    qa, ka, va, fa, qb, kb, vb, qi, ki, wi = _split_w_in(w_in_l)
    d = w_in_l.shape[0]
    scale = HEAD_DIM ** -0.5
    t_big = jnp.concatenate([_head_cols(qa, perm) * scale, _head_cols(qb, perm) * scale, va, vb, ka, kb],
                            axis=1).T.astype(BF16)
    zero = jnp.zeros((d, HEAD_DIM), w_in_l.dtype)
    spread = []
    for m in range(KV_A // 2):
        spread += [ka[:, 2 * m * HEAD_DIM:(2 * m + 1) * HEAD_DIM], zero, zero,
                   ka[:, (2 * m + 1) * HEAD_DIM:(2 * m + 2) * HEAD_DIM]]
    n_keys = jnp.concatenate(spread + [kb], axis=1).astype(BF16)
    small = _pad_cols(jnp.concatenate([fa, wi], axis=1), LANES)
    n_idx_hi, n_idx_lo = _split2(jnp.concatenate([ki, ki, small], axis=1))
    t_idx_hi, t_idx_lo = _split2((qi * (D_IDX ** -0.5)).T)
    return dict(t_big=t_big, n_keys=n_keys, n_idx_hi=n_idx_hi, n_idx_lo=n_idx_lo,
                t_idx_hi=t_idx_hi, t_idx_lo=t_idx_lo)


def _prep_post_weights(w_out_l, perm_a, perm_b, ln1_g, ln1_b, w_up_l, w_down_l, ln2_g, ln2_b, ff_chunk):
    d_mix, d = w_out_l.shape
    na = H_A * HEAD_DIM
    rows = lambda w, perm: jnp.concatenate([w[h * HEAD_DIM:(h + 1) * HEAD_DIM] for h in perm], axis=0)
    d_ff = w_up_l.shape[1]
    nc = d_ff // ff_chunk
    row = lambda a: a.reshape(1, -1)
    return dict(out_a=rows(w_out_l[:na], perm_a).astype(BF16), out_b=rows(w_out_l[na:], perm_b).astype(BF16),
                ln1_g=row(ln1_g), ln1_b=row(ln1_b), ln2_g=row(ln2_g), ln2_b=row(ln2_b),
                up=w_up_l.astype(BF16),
                down=w_down_l.reshape(nc, ff_chunk, d).astype(BF16))


ATTN_BLOCK = 256
FF_CHUNK = 512
POST_ROWS = 1024
SCAN_PAGES_PER_STEP = 32
ATTN_PAGES_PER_STEP = 16


def _prompt_layer(x2d, batch, seq, wproj, wpost, bf_pad, relb, rb_far, alpha, prev_kv):
    tq = min(ATTN_BLOCK, seq)
    p = _project_t(x2d, wproj, bf_pad, prev_kv, batch=batch, seq=seq, tk=tq)
    oa_t = _fox_attention(p['qat'], p['kaug'], p['vax'], batch, seq, tq)
    wi_t = p['wi'].reshape(batch, seq, H_IDX).transpose(0, 2, 1)
    ob_t = _dsa_attention(p, wi_t, relb, rb_far, batch, seq, tq, min(TOPK_MAX, seq // 4))
    y = _post_block(x2d, oa_t, ob_t, wpost, alpha, seq=seq)
    return (y, (p['vat'], p['vbt'], p['kat'], p['kbt']),
            (p['logf'].reshape(batch, seq, H_A), p['ki'].reshape(batch, seq, D_IDX)))


def _block_diag_q(q):
    db = q.shape[0]
    qh = q.reshape(db, H_A, 1, HEAD_DIM)
    kv_of_head = jnp.arange(H_A) // GROUP
    onehot = (kv_of_head[:, None] == jnp.arange(KV_A)[None, :])[None, :, :, None]
    return jnp.where(onehot, qh, jnp.zeros((), q.dtype)).reshape(db, H_A, KV_A * HEAD_DIM)


def _diag_heads(o_wide):
    db = o_wide.shape[0]
    o = o_wide.reshape(db, H_A, KV_A, HEAD_DIM)
    return jnp.concatenate([o[:, h, h // GROUP, :] for h in range(H_A)], axis=1)


def _sample_layer(x2d, page_table, caches, layer, wproj, wpost, bf_pad, rbt, alpha):
    db = x2d.shape[0]
    n_pages = page_table.shape[1]
    past = n_pages * PAGE_SIZE
    p = _project(x2d, wproj, bf_pad)
    q16 = jnp.concatenate([p['qi_hi'].reshape(db, H_IDX, D_IDX), p['qi_lo'].reshape(db, H_IDX, D_IDX)], axis=1)
    pps = min(ATTN_PAGES_PER_STEP, n_pages // 2)
    scan_pps = min(SCAN_PAGES_PER_STEP, n_pages // 2)
    scores, dec_c = _sample_scan(page_table, q16, p['wi'].reshape(db, H_IDX, 1), p['logf'].reshape(db, H_A, 1),
                                 caches['idx_k'], caches['logf_t'], layer, scan_pps, pps)
    q_f32 = p['qi_hi'].astype(F32) + p['qi_lo'].astype(F32)
    mb, mbn = _sample_select(scores.reshape(db, past), q_f32, jnp.tile(p['ki'], (1, H_IDX)), p['wi'],
                             min(TOPK_MAX, (past + 1) // 4))
    new = dict(ka=p['ka'].reshape(db, 1, -1), va=p['va'].reshape(db, 1, -1), kb=p['kb'].reshape(db, 1, -1),
               vb=p['vb'].reshape(db, 1, -1), mbn=mbn.reshape(db, 1, LANES))
    n_chunks = n_pages // pps
    oa_w, ob_w = _sample_attention(page_table, _block_diag_q(p['qa']), _block_diag_q(p['qb']), dec_c,
                                   mb.reshape(db, n_chunks, 1, pps * PAGE_SIZE), rbt, new,
                                   (caches['a_k'], caches['a_v'], caches['b_k'], caches['b_v']), layer, pps)
    y = _post_block(x2d, _diag_heads(oa_w).astype(BF16), _diag_heads(ob_w).astype(BF16), wpost, alpha)
    return y, (p['ka'], p['va'], p['logf'], p['kb'], p['vb'], p['ki'])


def kernel(x_prompt, x_sample, cache_a_k, cache_a_v, cache_a_logf, cache_b_k, cache_b_v, cache_idx_k,
           page_table, w_in, b_f, w_out, ln1_g, ln1_b, w_up, w_down, ln2_g, ln2_b, rel_bias):
    depth = w_in.shape[0]
    batch, seq, d_model = x_prompt.shape
    db, dec_seq, _ = x_sample.shape
    assert dec_seq == 1
    alpha = (2 * depth) ** 0.25
    natural = tuple(range(H_A))
    tq = min(ATTN_BLOCK, seq)

    kv_t = lambda c: c.transpose(0, 1, 3, 4, 2).reshape(c.shape[:2] + (-1, PAGE_SIZE))
    caches = dict(a_k=kv_t(cache_a_k), a_v=kv_t(cache_a_v), b_k=kv_t(cache_b_k), b_v=kv_t(cache_b_v),
                  idx_k=cache_idx_k.transpose(0, 1, 3, 2),
                  logf_t=cache_a_logf.astype(F32).transpose(0, 1, 3, 2))
    rb_perm = jnp.stack([rel_bias[:, h] for h in HEAD_PERM], axis=1)
    relb = _rel_bias_tiles(rb_perm, tq)
    rb_far = rb_perm[N_BUCKETS - 1]
    rbt = rel_bias.T

    xp = x_prompt.reshape(batch * seq, d_model)
    xs = x_sample.reshape(db, d_model)
    p_rows, s_rows = [], []
    p_kv = None
    for l in range(depth):
        bf_pad = _pad_cols(b_f[l].reshape(1, -1), LANES)
        post = lambda perm: _prep_post_weights(w_out[l], perm, perm, ln1_g[l], ln1_b[l], w_up[l], w_down[l],
                                               ln2_g[l], ln2_b[l], FF_CHUNK)
        xp, p_kv, rows = _prompt_layer(xp, batch, seq, _prep_proj_t_weights(w_in[l], HEAD_PERM),
                                       post(HEAD_PERM), bf_pad, relb, rb_far, alpha, p_kv)
        p_rows.append(rows)
        xs, rows = _sample_layer(xs, page_table, caches, l, _prep_proj_weights(w_in[l], natural),
                                 post(natural), bf_pad, rbt, alpha)
        s_rows.append(rows)

    stack = lambda rows, j: jnp.stack([r[j] for r in rows])
    sample = lambda j, tail: stack(s_rows, j).reshape((depth, db, dec_seq) + tail)
    kv = (KV_A, HEAD_DIM)
    prompt_kv = lambda a: a.reshape(depth, batch, KV_A, HEAD_DIM, seq).transpose(0, 1, 4, 2, 3)
    vat, vbt, kat, kbt = p_kv
    return (xp.reshape(batch, seq, d_model), xs.reshape(db, dec_seq, d_model),
            prompt_kv(kat), prompt_kv(vat), stack(p_rows, 0), prompt_kv(kbt), prompt_kv(vbt), stack(p_rows, 1),
            sample(0, kv), sample(1, kv), sample(2, (H_A,)), sample(3, kv), sample(4, kv), sample(5, (D_IDX,)))
```

```python
import functools
import math

import numpy as np
import jax
import jax.numpy as jnp
from jax import lax
from jax.experimental import pallas as pl
from jax.experimental.pallas import tpu as pltpu

HEAD_DIM = 64
H_A = 8
KV_A = 4
H_B = 8
KV_B = 4
GROUP = 2
H_IDX = 8
D_IDX = 64
TOPK_MAX = 256
N_BUCKETS = 32
MAX_DISTANCE = 128
PAGE_SIZE = 128
LN_EPS = 1e-5

LANES = 128
VMEM_LIMIT_BYTES = 56 * 1024 * 1024

NEG_BIG = -1e30
INT_MIN = -(2 ** 31)
HALF_BITS = 16
I16_MIN = -(2 ** 15)

BF16 = jnp.bfloat16
F32 = jnp.float32

HEAD_PERM = (0, 2, 1, 3, 4, 6, 5, 7)


def _cparams(semantics):
    return pltpu.CompilerParams(dimension_semantics=semantics,
                                vmem_limit_bytes=VMEM_LIMIT_BYTES)


def _dot(a, b):
    return jnp.dot(a, b, preferred_element_type=F32)


def _dot_nt(a, b):
    return lax.dot_general(a, b, (((1,), (1,)), ((), ())), preferred_element_type=F32)


def _split2(x):
    hi = x.astype(BF16)
    lo = (x - hi.astype(F32)).astype(BF16)
    return hi, lo


def _split3(x):
    h1 = x.astype(BF16)
    r1 = x - h1.astype(F32)
    h2 = r1.astype(BF16)
    h3 = (r1 - h2.astype(F32)).astype(BF16)
    return h1, h2, h3


def _dot3(xh, xl, wh, wl):
    return _dot(xh, wh) + _dot(xl, wh) + _dot(xh, wl)


def _log_sigmoid(x):
    return jnp.minimum(x, 0.0) - jnp.log1p(jnp.exp(-jnp.abs(x)))


def _layer_norm(x, g, b):
    mu = jnp.mean(x, axis=-1, keepdims=True)
    xc = x - mu
    var = jnp.mean(xc * xc, axis=-1, keepdims=True)
    return xc * lax.rsqrt(var + LN_EPS) * g + b


def _rel_bucket(dist):
    max_exact = N_BUCKETS // 2
    d = jnp.maximum(dist, 1).astype(F32)
    large = max_exact + (jnp.log(d / max_exact) / math.log(MAX_DISTANCE / max_exact)
                         * (N_BUCKETS - max_exact)).astype(jnp.int32)
    large = jnp.minimum(large, N_BUCKETS - 1)
    return jnp.where(dist < max_exact, dist, large)


def _order_key(score):
    bits = pltpu.bitcast(score, jnp.int32)
    key = bits ^ (lax.shift_right_arithmetic(bits, 31) & jnp.int32(0x7FFFFFFF))
    return jnp.where(bits == jnp.int32(INT_MIN), 0, key)


def _proj_kernel(x_ref, wbig_ref, wih_ref, wil_ref, bf_ref,
                 qa_ref, qb_ref, ka_ref, va_ref, kb_ref, vb_ref, qih_ref, qil_ref, ki_ref, kih_ref, kil_ref,
                 wi_ref, logf_ref):
    x = x_ref[...]
    xh, xl = _split2(x)
    nq = H_A * HEAD_DIM
    nk = KV_A * HEAD_DIM
    qa_ref[...] = _dot(xh, wbig_ref[:, 0:nq]).astype(BF16)
    qb_ref[...] = _dot(xh, wbig_ref[:, nq:2 * nq]).astype(BF16)
    o = 2 * nq
    ka_ref[...] = _dot(xh, wbig_ref[:, o:o + nk])
    va_ref[...] = _dot(xh, wbig_ref[:, o + nk:o + 2 * nk])
    kb_ref[...] = _dot(xh, wbig_ref[:, o + 2 * nk:o + 3 * nk])
    vb_ref[...] = _dot(xh, wbig_ref[:, o + 3 * nk:o + 4 * nk])

    ni = H_IDX * D_IDX
    qi = _dot3(xh, xl, wih_ref[:, 0:ni], wil_ref[:, 0:ni])
    qh, ql = _split2(qi)
    qih_ref[...] = qh
    qil_ref[...] = ql
    rest = _dot3(xh, xl, wih_ref[:, ni:ni + 2 * LANES], wil_ref[:, ni:ni + 2 * LANES])
    ki2 = rest[:, 0:LANES]
    ki_ref[...] = ki2[:, 0:D_IDX]
    kh, kl = _split2(ki2)
    kih_ref[...] = kh
    kil_ref[...] = kl

    small = rest[:, LANES:2 * LANES]
    wi_ref[...] = small[:, H_A:H_A + H_IDX] * (H_IDX ** -0.5)
    logf = _log_sigmoid(small + bf_ref[...])
    logf_ref[...] = logf[:, 0:H_A]


def _project(x2d, wts, bf_pad):
    m, d = x2d.shape
    tm = min(512, m)
    assert m % tm == 0
    nq, nk, ni = H_A * HEAD_DIM, KV_A * HEAD_DIM, H_IDX * D_IDX
    full = lambda a: pl.BlockSpec(a.shape, lambda i: (0,) * a.ndim)
    row = lambda n: pl.BlockSpec((tm, n), lambda i: (i, 0))
    inputs = [x2d, wts['big'], wts['idx_hi'], wts['idx_lo'], bf_pad]
    out_shapes = [((m, nq), BF16), ((m, nq), BF16), ((m, nk), F32), ((m, nk), F32), ((m, nk), F32),
                  ((m, nk), F32), ((m, ni), BF16), ((m, ni), BF16), ((m, D_IDX), F32),
                  ((m, LANES), BF16), ((m, LANES), BF16), ((m, H_IDX), F32), ((m, H_A), F32)]
    outs = pl.pallas_call(
        _proj_kernel,
        grid=(m // tm,),
        in_specs=[row(d)] + [full(a) for a in inputs[1:]],
        out_specs=[row(s[0][1]) for s in out_shapes],
        out_shape=[jax.ShapeDtypeStruct(*s) for s in out_shapes],
        compiler_params=_cparams(("arbitrary",)),
        name="proj",
    )(*inputs)
    names = ['qa', 'qb', 'ka', 'va', 'kb', 'vb', 'qi_hi', 'qi_lo', 'ki', 'ki2_hi', 'ki2_lo', 'wi', 'logf']
    return dict(zip(names, outs))


def _proj_t_kernel(*refs, tiles_per_seq, tk, n_prev):
    (x_ref, wt_ref, wn_ref, wnh_ref, wnl_ref, wth_ref, wtl_ref, bf_ref, tri_ref,
     p1_ref, p2_ref, p3_ref) = refs[:12]
    prev = refs[12:16] if n_prev else (None,) * 4
    (qat_ref, qbt_ref, vat_ref, vbt_ref, kat_ref, kbt_ref, vax_ref, vbx_ref,
     kaug_ref, kbn_ref, qith_ref, qitl_ref, ki_ref, kih_ref, wi_ref, logf_ref,
     carry_ref) = refs[12 + (4 if n_prev else 0):]

    def emit(out_ref, prev_ref, value):
        if n_prev:
            out_ref[0:n_prev] = prev_ref[...]
        out_ref[n_prev] = value

    x = x_ref[...]
    xh, xl = _split2(x)
    tm = x.shape[0]
    nq = H_A * HEAD_DIM
    nk = KV_A * HEAD_DIM

    qat_ref[...] = _dot_nt(wt_ref[0:nq, :], xh).astype(BF16)
    qbt_ref[...] = _dot_nt(wt_ref[nq:2 * nq, :], xh).astype(BF16)
    o = 2 * nq
    row = lax.broadcasted_iota(jnp.int32, (2 * HEAD_DIM, tm), 0)
    lo_rows = row < HEAD_DIM
    for t_ref, p_ref, x_out_ref, r0 in ((vat_ref, prev[0], vax_ref, o), (vbt_ref, prev[1], vbx_ref, o + nk)):
        vt = _dot_nt(wt_ref[r0:r0 + nk, :], xh)
        emit(t_ref, p_ref, vt)
        for m in range(KV_A // 2):
            blk = vt[m * 2 * HEAD_DIM:(m + 1) * 2 * HEAD_DIM, :]
            lo = jnp.where(lo_rows, blk, 1.0).astype(BF16)
            hi = jnp.where(lo_rows, 1.0, blk).astype(BF16)
            for t in range(tm // tk):
                cols = slice(t * tk, (t + 1) * tk)
                x_out_ref[t, m * 4 * HEAD_DIM:m * 4 * HEAD_DIM + 2 * HEAD_DIM, :] = lo[:, cols]
                x_out_ref[t, m * 4 * HEAD_DIM + 2 * HEAD_DIM:(m + 1) * 4 * HEAD_DIM, :] = hi[:, cols]
    emit(kat_ref, prev[2], _dot_nt(wt_ref[o + 2 * nk:o + 3 * nk, :], xh))
    emit(kbt_ref, prev[3], _dot_nt(wt_ref[o + 3 * nk:o + 4 * nk, :], xh))

    kn = _dot(xh, wn_ref[...])
    kbn_ref[...] = kn[:, 2 * nk:3 * nk].astype(BF16)

    qit = (_dot_nt(wth_ref[...], xh) + _dot_nt(wth_ref[...], xl)) + _dot_nt(wtl_ref[...], xh)
    qh, ql = _split2(qit)
    qith_ref[...] = qh
    qitl_ref[...] = ql

    rest = _dot3(xh, xl, wnh_ref[...], wnl_ref[...])
    ki2 = rest[:, 0:LANES]
    ki_ref[...] = ki2[:, 0:D_IDX]
    kih_ref[...] = ki2.astype(BF16)
    small = rest[:, LANES:2 * LANES]
    wi_ref[...] = small[:, H_A:H_A + H_IDX] * (H_IDX ** -0.5)
    logf = _log_sigmoid(small + bf_ref[...])
    logf_ref[...] = logf[:, 0:H_A]

    @pl.when(pl.program_id(0) % tiles_per_seq == 0)
    def _():
        carry_ref[...] = jnp.zeros_like(carry_ref)

    h1, h2, h3 = _split3(logf)
    tri = tri_ref[...]
    c = (_dot(tri, h1) + _dot(tri, h2)) + _dot(tri, h3) + carry_ref[...]
    carry_ref[...] = c[tm - 1:tm, :]
    c1, c2, c3 = _split3(c)
    aug = (_dot(c1, p1_ref[...]) + _dot(c2, p2_ref[...])) + _dot(c3, p3_ref[...])
    kaug_ref[...] = (kn[:, 0:2 * nk] + aug).astype(BF16)


def _decay_placement():
    mats = np.zeros((3, LANES, 2 * KV_A * HEAD_DIM), np.float32)
    for m in range(KV_A // 2):
        for v in range(2 * GROUP):
            head = HEAD_PERM[4 * m + v]
            hi_variant, g = v % 2, v // 2
            base = m * 4 * HEAD_DIM + (2 * HEAD_DIM if hi_variant else HEAD_DIM)
            for t in range(3):
                mats[t, head, base + 3 * g + t] = -1.0
    return [jnp.asarray(mats[t], BF16) for t in range(3)]


def _project_t(x2d, wts, bf_pad, prev, *, batch, seq, tk):
    n_prev = 0 if prev is None else prev[0].shape[0]
    m, d = x2d.shape
    tm = min(512, seq)
    assert seq % tm == 0 and tm % tk == 0
    tps = seq // tm
    nq, nk, ni = H_A * HEAD_DIM, KV_A * HEAD_DIM, H_IDX * D_IDX
    nkv = seq // tk
    full = lambda a: pl.BlockSpec(a.shape, lambda i: (0,) * a.ndim)
    row = lambda n: pl.BlockSpec((tm, n), lambda i: (i, 0))
    feat = lambda r: pl.BlockSpec((None, r, tm), lambda i: (i // tps, 0, i % tps))
    vx = pl.BlockSpec((None, tm // tk, 2 * nk, tk), lambda i: (i // tps, i % tps, 0, 0))
    tri = jnp.tril(jnp.ones((tm, tm), F32)).astype(BF16)
    inputs = [x2d, wts['t_big'], wts['n_keys'], wts['n_idx_hi'], wts['n_idx_lo'], wts['t_idx_hi'],
              wts['t_idx_lo'], bf_pad, tri] + _decay_placement()
    in_specs = [row(d)] + [full(a) for a in inputs[1:]]
    layers = lambda n: pl.BlockSpec((n, None, nk, tm), lambda i: (0, i // tps, 0, i % tps))
    if n_prev:
        inputs += list(prev)
        in_specs += [layers(n_prev)] * 4
    fshape = lambda r, dt: ((batch, r, seq), dt)
    stacked = (((n_prev + 1, batch, nk, seq), F32), layers(n_prev + 1))
    outs = [
        (fshape(nq, BF16), feat(nq)), (fshape(nq, BF16), feat(nq)),
        stacked, stacked, stacked, stacked,
        (((batch, nkv, 2 * nk, tk), BF16), vx), (((batch, nkv, 2 * nk, tk), BF16), vx),
        (((m, 2 * nk), BF16), row(2 * nk)), (((m, nk), BF16), row(nk)),
        (fshape(ni, BF16), feat(ni)), (fshape(ni, BF16), feat(ni)),
        (((m, D_IDX), F32), row(D_IDX)), (((m, LANES), BF16), row(LANES)),
        (((m, H_IDX), F32), row(H_IDX)), (((m, H_A), F32), row(H_A)),
    ]
    res = pl.pallas_call(
        functools.partial(_proj_t_kernel, tiles_per_seq=tps, tk=tk, n_prev=n_prev),
        grid=(m // tm,),
        in_specs=in_specs,
        out_specs=[o[1] for o in outs],
        out_shape=[jax.ShapeDtypeStruct(*o[0]) for o in outs],
        scratch_shapes=[pltpu.VMEM((1, LANES), F32)],
        compiler_params=_cparams(("arbitrary",)),
        name="proj_prompt",
    )(*inputs)
    names = ['qat', 'qbt', 'vat', 'vbt', 'kat', 'kbt', 'vax', 'vbx', 'kaug', 'kbn', 'qit_hi', 'qit_lo',
             'ki', 'ki2_hi', 'wi', 'logf']
    return dict(zip(names, res))


def _post_kernel(x_ref, oa_ref, ob_ref, woa_ref, wob_ref, g1_ref, b1_ref, wup_ref, wdn_ref,
                 g2_ref, b2_ref, y_ref, acc_ref, *, alpha, n_chunks, feature_major):
    x = x_ref[...]
    if feature_major:
        tn = lambda a, w: lax.dot_general(a, w, (((0,), (0,)), ((), ())), preferred_element_type=F32)
        att = tn(oa_ref[...], woa_ref[...]) + tn(ob_ref[...], wob_ref[...])
    else:
        att = _dot(oa_ref[...], woa_ref[...]) + _dot(ob_ref[...], wob_ref[...])
    x1 = _layer_norm(alpha * x + att, g1_ref[...], b1_ref[...])
    x1b = x1.astype(BF16)
    acc_ref[...] = jnp.zeros_like(acc_ref)

    def body(c, carry):
        cols = pl.ds(pl.multiple_of(c * FF_CHUNK, FF_CHUNK), FF_CHUNK)
        u = jnp.maximum(_dot(x1b, wup_ref[:, cols]), 0.0)
        acc_ref[...] += _dot((u * u).astype(BF16), wdn_ref[c])
        return carry

    lax.fori_loop(0, n_chunks, body, 0)
    y_ref[...] = _layer_norm(alpha * x1 + acc_ref[...], g2_ref[...], b2_ref[...])


def _post_block(x2d, oa, ob, wts, alpha, seq=None):
    m, d = x2d.shape
    tm = min(POST_ROWS, m if seq is None else seq)
    n_chunks = wts['down'].shape[0]
    full = lambda a: pl.BlockSpec(a.shape, lambda i: (0,) * a.ndim, pipeline_mode=pl.Buffered(1))
    row = lambda n: pl.BlockSpec((tm, n), lambda i: (i, 0))
    if seq is None:
        o_spec = lambda a: row(a.shape[1])
    else:
        tps = seq // tm
        o_spec = lambda a: pl.BlockSpec((None, a.shape[1], tm), lambda i: (i // tps, 0, i % tps))
    weights = [wts['out_a'], wts['out_b'], wts['ln1_g'], wts['ln1_b'], wts['up'], wts['down'],
               wts['ln2_g'], wts['ln2_b']]
    return pl.pallas_call(
        functools.partial(_post_kernel, alpha=alpha, n_chunks=n_chunks, feature_major=seq is not None),
        grid=(m // tm,),
        in_specs=[row(d), o_spec(oa), o_spec(ob)] + [full(w) for w in weights],
        out_specs=row(d),
        out_shape=jax.ShapeDtypeStruct((m, d), F32),
        scratch_shapes=[pltpu.VMEM((tm, d), F32)],
        compiler_params=_cparams(("arbitrary",)),
        name="post_block",
    )(x2d, oa, ob, *weights)


def _variant_rows(g, hi_half, ones_from=None):
    row = lax.broadcasted_iota(jnp.int32, g.shape, 0)
    keep = (row >= HEAD_DIM) if hi_half else (row < HEAD_DIM)
    fill = 0.0
    if ones_from is not None:
        fill = jnp.where((row >= ones_from) & (row < ones_from + 3), 1.0, 0.0)
    return jnp.where(keep, g, fill).astype(BF16)


def _softmax_chunk(s, causal_q0, m_ref, p_ref, acc_ref, cols, acols):
    if causal_q0 is not None:
        key = lax.broadcasted_iota(jnp.int32, s.shape, 0)
        qry = lax.broadcasted_iota(jnp.int32, s.shape, 1) + causal_q0
        s = jnp.where(key <= qry, s, NEG_BIG)
    m_old = m_ref[:, cols]
    m_new = jnp.maximum(m_old, jnp.max(s, axis=0, keepdims=True))
    m_ref[:, cols] = m_new
    p_ref[:, cols] = jnp.exp(s - m_new[0:1, :]).astype(BF16)
    acc_ref[:, acols] = acc_ref[:, acols] * jnp.exp(m_old - m_new)[0:1, :]


def _finish_pair(acc_lo, acc_hi, tq):
    row = lax.broadcasted_iota(jnp.int32, (2 * HEAD_DIM, tq), 0)
    lo_rows = row < HEAD_DIM
    out = []
    for g in range(GROUP):
        cols = slice(g * tq, (g + 1) * tq)
        a_lo, a_hi = acc_lo[:, cols], acc_hi[:, cols]
        o_lo = a_lo / a_lo[HEAD_DIM:HEAD_DIM + 1, :]
        o_hi = a_hi / a_hi[0:1, :]
        out.append(jnp.where(lo_rows, o_lo, o_hi))
    return jnp.concatenate(out, axis=0).astype(BF16)


def _fox_kernel(q_ref, k_ref, v_ref, o_ref, qt_ref, s_ref, p_ref, m_ref, acc_ref, *, tq):
    i = pl.program_id(2)
    q = q_ref[...].astype(F32)
    g0, g1 = q[0:2 * HEAD_DIM], q[2 * HEAD_DIM:4 * HEAD_DIM]
    qt_ref[0, :, 0:tq] = _variant_rows(g0, False, HEAD_DIM)
    qt_ref[0, :, tq:2 * tq] = _variant_rows(g1, False, HEAD_DIM + 3)
    qt_ref[1, :, 0:tq] = _variant_rows(g0, True, 0)
    qt_ref[1, :, tq:2 * tq] = _variant_rows(g1, True, 3)
    m_ref[...] = jnp.full_like(m_ref, NEG_BIG)
    acc_ref[...] = jnp.zeros_like(acc_ref)

    def process(blocks):
        for slot, (j, _) in enumerate(blocks):
            start = pl.multiple_of(j * tq, tq)
            for d in range(2):
                s_ref[slot, d] = _dot(k_ref[pl.ds(start, tq), d * LANES:(d + 1) * LANES], qt_ref[d])
        for slot, (j, causal) in enumerate(blocks):
            for d in range(2):
                for c in range(2 * tq // LANES):
                    cols = slice(c * LANES, (c + 1) * LANES)
                    q0 = (c * LANES) % tq if causal else None
                    _softmax_chunk(s_ref[slot, d, :, cols], q0, m_ref.at[d], p_ref.at[slot, d],
                                   acc_ref.at[d], cols, cols)
                acc_ref[d] += _dot(v_ref[j, d * LANES:(d + 1) * LANES, :], p_ref[slot, d])

    def pair_body(t, carry):
        process(((2 * t, False), (2 * t + 1, False)))
        return carry

    lax.fori_loop(0, i // 2, pair_body, 0)

    @pl.when(i % 2 == 1)
    def _():
        process(((i - 1, False), (i, True)))

    @pl.when(i % 2 == 0)
    def _():
        process(((i, True),))

    o_ref[...] = _finish_pair(acc_ref[0], acc_ref[1], tq)


def _fox_attention(qt, kaug, vx, batch, seq, tq):
    nq = seq // tq
    pair = 4 * HEAD_DIM
    return pl.pallas_call(
        functools.partial(_fox_kernel, tq=tq),
        grid=(batch, KV_A // 2, nq),
        in_specs=[
            pl.BlockSpec((None, pair, tq), lambda b, m, i: (b, m, i)),
            pl.BlockSpec((seq, pair), lambda b, m, i: (b, m)),
            pl.BlockSpec((None, nq, pair, tq), lambda b, m, i: (b, 0, m, 0)),
        ],
        out_specs=pl.BlockSpec((None, pair, tq), lambda b, m, i: (b, m, i)),
        out_shape=jax.ShapeDtypeStruct(qt.shape, BF16),
        scratch_shapes=[pltpu.VMEM((2, LANES, 2 * tq), BF16), pltpu.VMEM((2, 2, tq, 2 * tq), F32),
                        pltpu.VMEM((2, 2, tq, 2 * tq), BF16), pltpu.VMEM((2, 8, 2 * tq), F32),
                        pltpu.VMEM((2, LANES, 2 * tq), F32)],
        compiler_params=_cparams(("arbitrary", "arbitrary", "arbitrary")),
        name="fox_attention",
    )(qt, kaug, vx)


def _relb_kernel(rb_ref, o_ref, *, tq):
    delta = pl.program_id(0)
    h = pl.program_id(1)
    key = lax.broadcasted_iota(jnp.int32, (tq, tq), 0)
    qry = lax.broadcasted_iota(jnp.int32, (tq, tq), 1)
    bucket = _rel_bucket(jnp.maximum(delta * tq + qry - key, 0))
    acc = jnp.zeros((tq, tq), F32)
    for n in range(N_BUCKETS):
        acc = jnp.where(bucket == n, rb_ref[n, h], acc)
    o_ref[...] = acc


def _rel_bias_tiles(rb_perm, tq):
    return pl.pallas_call(
        functools.partial(_relb_kernel, tq=tq),
        grid=(2, H_B),
        in_specs=[pl.BlockSpec(memory_space=pltpu.SMEM)],
        out_specs=pl.BlockSpec((None, None, tq, tq), lambda d, h: (d, h, 0, 0)),
        out_shape=jax.ShapeDtypeStruct((2, H_B, tq, tq), F32),
        compiler_params=_cparams(("arbitrary", "arbitrary")),
        name="rel_bias_tiles",
    )(rb_perm)


def _strict_upper(n):
    r = lax.broadcasted_iota(jnp.int32, (n, n), 0)
    c = lax.broadcasted_iota(jnp.int32, (n, n), 1)
    return jnp.where(r < c, 1.0, 0.0).astype(BF16)


def _strict_lower(n):
    r = lax.broadcasted_iota(jnp.int32, (n, n), 0)
    c = lax.broadcasted_iota(jnp.int32, (n, n), 1)
    return jnp.where(c < r, 1.0, 0.0).astype(BF16)


KEY_ROWS = 64

DSA_VARIANT_ORDER = (0, 2, 1, 3)


def _dsa_kernel(qih_ref, qil_ref, kih_ref, wi_ref, qb_ref, k_ref, v_ref, relb_ref, rbfar_ref,
                o_ref, keys_ref, khi_ref, klo_ref, mb_ref, qi_ref, lg_ref, qt_ref, p_ref, m_ref, acc_ref,
                *, tq, topk):
    i = pl.program_id(1)
    nblk = i + 1
    nlc = tq // LANES

    for h in range(H_IDX):
        rows = slice((h // 2) * LANES, (h // 2 + 1) * LANES)
        cols = slice(h * tq, (h + 1) * tq)
        qi_ref[0:LANES, cols] = _variant_rows(qih_ref[rows, :].astype(F32), h % 2 == 1)
        qi_ref[LANES:2 * LANES, cols] = _variant_rows(qil_ref[rows, :].astype(F32), h % 2 == 1)

    def score_block(j, diag):
        start = pl.multiple_of(j * tq, tq)
        kh = kih_ref[pl.ds(start, tq), :]
        lg_ref[...] = _dot(jnp.concatenate([kh, kh], axis=1), qi_ref[...])
        for c in range(nlc):
            for r in range(tq // KEY_ROWS):
                rows = slice(r * KEY_ROWS, (r + 1) * KEY_ROWS)
                sc = jnp.zeros((KEY_ROWS, LANES), F32)
                for h in range(H_IDX):
                    lane0 = h * tq + c * LANES
                    sc = sc + wi_ref[h:h + 1, c * LANES:(c + 1) * LANES] * jnp.maximum(
                        lg_ref[rows, lane0:lane0 + LANES], 0.0)
                key = _order_key(sc)
                if diag:
                    kidx = lax.broadcasted_iota(jnp.int32, key.shape, 0) + r * KEY_ROWS
                    qidx = lax.broadcasted_iota(jnp.int32, key.shape, 1) + c * LANES
                    key = jnp.where(kidx <= qidx, key, INT_MIN)
                keys_ref[j, rows, c * LANES:(c + 1) * LANES] = key
                khi_ref[j, rows, c * LANES:(c + 1) * LANES] = lax.shift_right_arithmetic(
                    key, HALF_BITS).astype(jnp.int16)

    def score_body(j, carry):
        score_block(j, False)
        return carry

    lax.fori_loop(0, i, score_body, 0)
    score_block(i, True)

    needs_select = (i + 1) * tq > topk

    @pl.when(jnp.logical_not(needs_select))
    def _():
        def zero_body(j, carry):
            mb_ref[j] = jnp.zeros((tq, tq), F32)
            return carry
        lax.fori_loop(0, nblk, zero_body, 0)

    @pl.when(needs_select)
    def _():
        npairs = (nblk + 1) // 2

        @pl.when(nblk % 2 == 1)
        def _():
            khi_ref[nblk] = jnp.full((tq, tq), I16_MIN, jnp.int16)
            klo_ref[nblk] = jnp.full((tq, tq), I16_MIN, jnp.int16)

        def count16(ref, pred_fn):
            def body(t, acc):
                for j in (2 * t, 2 * t + 1):
                    hit = jnp.where(pred_fn(ref[j]), jnp.int16(1), jnp.int16(0))
                    for r in range(tq // 16):
                        acc = acc + hit[r * 16:(r + 1) * 16, :]
                return acc
            acc = lax.fori_loop(0, npairs, body, jnp.zeros((16, tq), jnp.int16))
            return jnp.sum(acc.astype(F32), axis=0, keepdims=True)

        def search16(ref, want):
            def bit_body(b, t):
                cand = t + lax.shift_left(jnp.int32(1), HALF_BITS - 1 - b)
                cand16 = cand.astype(jnp.int16)
                return jnp.where(count16(ref, lambda k: k >= cand16) >= want, cand, t)
            return lax.fori_loop(0, HALF_BITS, bit_body, jnp.full((1, tq), I16_MIN, jnp.int32))

        thr_hi = search16(khi_ref, topk)
        thr_hi16 = thr_hi.astype(jnp.int16)
        above = count16(khi_ref, lambda k: k > thr_hi16)

        def low_body(j, carry):
            kblk = keys_ref[j]
            in_band = lax.shift_right_arithmetic(kblk, HALF_BITS) == thr_hi
            low = (kblk & jnp.int32(2 ** HALF_BITS - 1)) + I16_MIN
            klo_ref[j] = jnp.where(in_band, low, I16_MIN).astype(jnp.int16)
            return carry

        lax.fori_loop(0, nblk, low_body, 0)
        thr_lo = search16(klo_ref, topk - above)
        thr_lo16 = thr_lo.astype(jnp.int16)
        thr = thr_hi * (2 ** HALF_BITS) + (thr_lo - I16_MIN)
        need = topk - (above + count16(klo_ref, lambda k: k > thr_lo16))
        n_eq = count16(klo_ref, lambda k: k == thr_lo16)
        tie_break = jnp.max(jnp.where(n_eq > need, 1.0, 0.0)) > 0.0

        @pl.when(jnp.logical_not(tie_break))
        def _():
            def plain_body(j, carry):
                mb_ref[j] = jnp.where(keys_ref[j] >= thr, 0.0, NEG_BIG)
                return carry
            lax.fori_loop(0, nblk, plain_body, 0)

        @pl.when(tie_break)
        def _():
            earlier = _strict_lower(tq)

            def mask_body(j, seen):
                kblk = keys_ref[j]
                eq = kblk == thr
                eqf = jnp.where(eq, 1.0, 0.0)
                rank = _dot(earlier, eqf.astype(BF16)) + seen
                mb_ref[j] = jnp.where(kblk > thr, 0.0,
                                      jnp.where(eq, jnp.where(rank < need, 0.0, NEG_BIG), NEG_BIG))
                return seen + jnp.sum(eqf, axis=0, keepdims=True)

            lax.fori_loop(0, nblk, mask_body, jnp.zeros((1, tq), F32))

    pair = 4 * HEAD_DIM
    for m in range(KV_B // 2):
        q = qb_ref[m * pair:(m + 1) * pair, :].astype(F32)
        g = (q[0:2 * HEAD_DIM], q[2 * HEAD_DIM:4 * HEAD_DIM])
        for n, v in enumerate(DSA_VARIANT_ORDER):
            qt_ref[:, n * tq:(n + 1) * tq] = _variant_rows(g[v // 2], v % 2 == 1)
        m_ref[...] = jnp.full_like(m_ref, NEG_BIG)
        acc_ref[...] = jnp.zeros_like(acc_ref)

        far = lambda slot, qcols: rbfar_ref[slot]
        near = lambda slot, qcols: relb_ref[1, slot, :, qcols]
        diag = lambda slot, qcols: relb_ref[0, slot, :, qcols]

        def process(blocks):
            for slot, (j, _) in enumerate(blocks):
                start = pl.multiple_of(j * tq, tq)
                lg_ref[:, slot * 4 * tq:(slot + 1) * 4 * tq] = _dot(
                    k_ref[pl.ds(start, tq), m * LANES:(m + 1) * LANES], qt_ref[...])
            for slot, (j, rel) in enumerate(blocks):
                for n, v in enumerate(DSA_VARIANT_ORDER):
                    for c in range(nlc):
                        qcols = slice(c * LANES, (c + 1) * LANES)
                        cols = slice(n * tq + c * LANES, n * tq + (c + 1) * LANES)
                        lcols = slice(slot * 4 * tq + cols.start, slot * 4 * tq + cols.stop)
                        s = lg_ref[:, lcols] + (mb_ref[j, :, qcols] + rel(4 * m + v, qcols))
                        acols = slice((n % 2) * tq + c * LANES, (n % 2) * tq + (c + 1) * LANES)
                        _softmax_chunk(s, c * LANES if rel is diag else None, m_ref, p_ref.at[slot],
                                       acc_ref.at[n // 2], cols, acols)
                for d in range(2):
                    vt = v_ref[j, m * pair + d * LANES:m * pair + (d + 1) * LANES, :]
                    acc_ref[d] += _dot(vt, p_ref[slot, :, d * 2 * tq:(d + 1) * 2 * tq])

        n_far = jnp.maximum(i - 1, 0)

        def far_body(t, carry):
            process(((2 * t, far), (2 * t + 1, far)))
            return carry

        lax.fori_loop(0, n_far // 2, far_body, 0)

        @pl.when(n_far % 2 == 1)
        def _():
            process(((i - 2, far),))

        @pl.when(i >= 1)
        def _():
            process(((i - 1, near), (i, diag)))

        @pl.when(i == 0)
        def _():
            process(((i, diag),))

        o_ref[m * pair:(m + 1) * pair, :] = _finish_pair(acc_ref[0], acc_ref[1], tq)


def _dsa_attention(p, wi_t, relb, rb_far, batch, seq, tq, topk):
    nq = seq // tq
    nf = H_B * HEAD_DIM
    qcol = lambda r: pl.BlockSpec((None, r, tq), lambda b, i: (b, 0, i))
    seqblk = lambda n: pl.BlockSpec((seq, n), lambda b, i: (b, 0))
    return pl.pallas_call(
        functools.partial(_dsa_kernel, tq=tq, topk=topk),
        grid=(batch, nq),
        in_specs=[qcol(H_IDX * D_IDX), qcol(H_IDX * D_IDX), seqblk(LANES), qcol(H_IDX),
                  qcol(nf), seqblk(KV_B * HEAD_DIM),
                  pl.BlockSpec((None, nq, nf, tq), lambda b, i: (b, 0, 0, 0)),
                  pl.BlockSpec(relb.shape, lambda b, i: (0, 0, 0, 0)),
                  pl.BlockSpec(memory_space=pltpu.SMEM)],
        out_specs=qcol(nf),
        out_shape=jax.ShapeDtypeStruct((batch, nf, seq), BF16),
        scratch_shapes=[pltpu.VMEM((nq, tq, tq), jnp.int32), pltpu.VMEM((nq, tq, tq), jnp.int16),
                        pltpu.VMEM((nq, tq, tq), jnp.int16), pltpu.VMEM((nq, tq, tq), F32),
                        pltpu.VMEM((2 * LANES, H_IDX * tq), BF16), pltpu.VMEM((tq, H_IDX * tq), F32),
                        pltpu.VMEM((LANES, 4 * tq), BF16), pltpu.VMEM((2, tq, 4 * tq), BF16),
                        pltpu.VMEM((8, 4 * tq), F32), pltpu.VMEM((2, LANES, 2 * tq), F32)],
        compiler_params=_cparams(("arbitrary", "arbitrary")),
        name="dsa_attention",
    )(p['qit_hi'], p['qit_lo'], p['ki2_hi'], wi_t, p['qbt'], p['kbn'], p['vbx'], relb, rb_far)


def _sscan_kernel(pt_ref, q16_ref, wi_ref, lnew_ref, idx_hbm, logf_hbm, sc_ref, dec_ref,
                  kbuf, fbuf, sem, carry_ref, *, pps, n_chunks, attn_pps, layer):
    b = pl.program_id(0)
    n_rows = pl.num_programs(0)

    def copies(row, step, slot):
        chunk = n_chunks - 1 - step
        out = []
        for r in range(pps):
            page = pt_ref[row, chunk * pps + r]
            out.append(pltpu.make_async_copy(idx_hbm.at[layer, page], kbuf.at[slot, r], sem.at[slot]))
            out.append(pltpu.make_async_copy(logf_hbm.at[layer, page], fbuf.at[slot, r], sem.at[slot]))
        return out

    def start(row, step, slot):
        for n, c in enumerate(copies(row, step, slot)):
            c.start(priority=n % 2)

    def wait(row, step, slot):
        for c in copies(row, step, slot):
            c.wait()

    @pl.when(b == 0)
    def _():
        start(0, 0, 0)

    carry_ref[...] = jnp.broadcast_to(lnew_ref[...], carry_ref.shape)
    q16 = q16_ref[...]
    qh = q16[0:H_IDX]
    wi = wi_ref[...]
    r_i = lax.broadcasted_iota(jnp.int32, (PAGE_SIZE, PAGE_SIZE), 0)
    c_i = lax.broadcasted_iota(jnp.int32, (PAGE_SIZE, PAGE_SIZE), 1)
    later = jnp.where(r_i > c_i, 1.0, 0.0).astype(BF16)
    per = pps // attn_pps

    def compute(step, slot):
        chunk = n_chunks - 1 - step
        x_all = jnp.concatenate([fbuf[slot, r] for r in range(pps)], axis=0)
        h1, h2, h3 = _split3(x_all)
        within = (_dot(h1, later) + _dot(h2, later)) + _dot(h3, later)
        totals = jnp.sum(x_all, axis=1, keepdims=True)
        carry = carry_ref[...]
        for r in reversed(range(pps)):
            rows = slice(r * H_A, (r + 1) * H_A)
            lanes = slice((r % attn_pps) * PAGE_SIZE, (r % attn_pps + 1) * PAGE_SIZE)
            dec_ref[chunk * per + r // attn_pps, :, lanes] = within[rows, :] + carry[:, 0:1]
            carry = carry + totals[rows, :]
        carry_ref[...] = carry
        k_all = jnp.concatenate([kbuf[slot, r] for r in range(pps)], axis=1)
        kh, kl = _split2(k_all)
        a = _dot(q16, kh)
        logits = (a[0:H_IDX] + a[H_IDX:2 * H_IDX]) + _dot(qh, kl)
        sc_ref[chunk] = jnp.sum(wi * jnp.maximum(logits, 0.0), axis=0, keepdims=True)

    def pair_body(sp, carry):
        s0 = 2 * sp
        start(b, s0 + 1, 1)
        wait(b, s0, 0)
        compute(s0, 0)

        @pl.when(s0 + 2 < n_chunks)
        def _():
            start(b, s0 + 2, 0)

        @pl.when(jnp.logical_and(s0 + 2 == n_chunks, b + 1 < n_rows))
        def _():
            start(b + 1, 0, 0)

        wait(b, s0 + 1, 1)
        compute(s0 + 1, 1)
        return carry

    lax.fori_loop(0, n_chunks // 2, pair_body, 0)


def _sample_scan(page_table, q16, wi_col, logf_new_col, cache_idx_k, logf_t_cache, layer, pps, attn_pps):
    db, n_pages = page_table.shape
    n_chunks = n_pages // pps
    assert n_chunks % 2 == 0 and n_chunks * pps == n_pages and pps % attn_pps == 0
    n_attn = n_pages // attn_pps
    per_b = lambda shape: pl.BlockSpec((None,) + shape, lambda b, pt: (b,) + (0,) * len(shape))
    hbm = pl.BlockSpec(memory_space=pl.ANY)
    grid_spec = pltpu.PrefetchScalarGridSpec(
        num_scalar_prefetch=1,
        grid=(db,),
        in_specs=[per_b((2 * H_IDX, D_IDX)), per_b((H_IDX, 1)), per_b((H_A, 1)), hbm, hbm],
        out_specs=[per_b((n_chunks, 1, pps * PAGE_SIZE)), per_b((n_attn, H_A, attn_pps * PAGE_SIZE))],
        scratch_shapes=[pltpu.VMEM((2, pps, D_IDX, PAGE_SIZE), F32), pltpu.VMEM((2, pps, H_A, PAGE_SIZE), F32),
                        pltpu.SemaphoreType.DMA((2,)), pltpu.VMEM((H_A, LANES), F32)],
    )
    return pl.pallas_call(
        functools.partial(_sscan_kernel, pps=pps, n_chunks=n_chunks, attn_pps=attn_pps, layer=layer),
        grid_spec=grid_spec,
        out_shape=[jax.ShapeDtypeStruct((db, n_chunks, 1, pps * PAGE_SIZE), F32),
                   jax.ShapeDtypeStruct((db, n_attn, H_A, attn_pps * PAGE_SIZE), F32)],
        compiler_params=_cparams(("arbitrary",)),
        name="sample_scan",
    )(page_table, q16, wi_col, logf_new_col, cache_idx_k, logf_t_cache)


def _sselect_kernel(sc_ref, q_ref, kt_ref, wi_ref, mb_ref, mbn_ref, keys_ref, *, topk, chunk):
    db, past = sc_ref.shape
    prod = q_ref[...] * kt_ref[...]
    lane = lax.broadcasted_iota(jnp.int32, prod.shape, 1)
    wi = wi_ref[...]
    sc_new = jnp.zeros((db, 1), F32)
    for h in range(H_IDX):
        seg = (lane >= h * D_IDX) & (lane < (h + 1) * D_IDX)
        logit = jnp.sum(jnp.where(seg, prod, 0.0), axis=1, keepdims=True)
        sc_new = sc_new + wi[:, h:h + 1] * jnp.maximum(logit, 0.0)
    key_new = _order_key(sc_new)
    keys_ref[...] = _order_key(sc_ref[...])

    def count(pred_fn):
        n = jnp.sum(jnp.where(pred_fn(keys_ref[...]), 1.0, 0.0), axis=1, keepdims=True)
        return n + jnp.where(pred_fn(key_new), 1.0, 0.0)

    def bit_body(b, t):
        cand = t + lax.shift_left(jnp.int32(1), 31 - b)
        return jnp.where(count(lambda k: k >= cand) >= topk, cand, t)

    thr = lax.fori_loop(0, 32, bit_body, jnp.full((db, 1), INT_MIN, jnp.int32))
    need = topk - count(lambda k: k > thr)
    sut = _strict_upper(chunk)
    seen = jnp.zeros((db, 1), F32)
    for c in range(past // chunk):
        cols = slice(c * chunk, (c + 1) * chunk)
        kblk = keys_ref[:, cols]
        eq = kblk == thr
        eqf = jnp.where(eq, 1.0, 0.0)
        rank = _dot(eqf.astype(BF16), sut) + seen
        mb_ref[:, cols] = jnp.where(kblk > thr, 0.0,
                                    jnp.where(eq, jnp.where(rank < need, 0.0, NEG_BIG), NEG_BIG))
        seen = seen + jnp.sum(eqf, axis=1, keepdims=True)
    sel_new = jnp.where(key_new > thr, 0.0,
                        jnp.where(key_new == thr, jnp.where(seen < need, 0.0, NEG_BIG), NEG_BIG))
    mbn_ref[...] = jnp.broadcast_to(sel_new, mbn_ref.shape)


def _sample_select(scores, q_f32, ki_tiled, wi, topk):
    db, past = scores.shape
    return pl.pallas_call(
        functools.partial(_sselect_kernel, topk=topk, chunk=2 * LANES),
        out_shape=[jax.ShapeDtypeStruct((db, past), F32), jax.ShapeDtypeStruct((db, LANES), F32)],
        scratch_shapes=[pltpu.VMEM((db, past), jnp.int32)],
        compiler_params=pltpu.CompilerParams(vmem_limit_bytes=VMEM_LIMIT_BYTES),
        name="sample_select",
    )(scores, q_f32, ki_tiled, wi)


def _sattn_kernel(pt_ref, qa_ref, qb_ref, dec_ref, mb_ref, rbt_ref, kan_ref, van_ref, kbn_ref, vbn_ref,
                  mbn_ref, ak_hbm, av_hbm, bk_hbm, bv_hbm, oa_ref, ob_ref,
                  buf, sem, ma_ref, la_ref, acca_ref, mb_m_ref, lb_ref, accb_ref, *, pps, past, layer):
    b = pl.program_id(0)
    n_rows = pl.num_programs(0)
    width = pps * PAGE_SIZE
    n_chunks = past // width
    caches = (ak_hbm, av_hbm, bk_hbm, bv_hbm)

    def page_copy(row, chunk, slot, t, r):
        page = pt_ref[row, chunk * pps + r]
        return pltpu.make_async_copy(caches[t].at[layer, page], buf.at[slot, t, r], sem.at[slot])

    def start_chunk(row, chunk, slot):
        for t in range(len(caches)):
            for r in range(pps):
                page_copy(row, chunk, slot, t, r).start(priority=r % 2)

    def wait_chunk(row, chunk, slot):
        for t in range(len(caches)):
            for r in range(pps):
                page_copy(row, chunk, slot, t, r).wait()

    @pl.when(b == 0)
    def _():
        start_chunk(0, 0, 0)
        start_chunk(0, 1, 1)

    for m_ref, l_ref, acc_ref in ((ma_ref, la_ref, acca_ref), (mb_m_ref, lb_ref, accb_ref)):
        m_ref[...] = jnp.full_like(m_ref, NEG_BIG)
        l_ref[...] = jnp.zeros_like(l_ref)
        acc_ref[...] = jnp.zeros_like(acc_ref)

    qa, qb = qa_ref[...], qb_ref[...]
    rbt = rbt_ref[...]

    def update(s, slot, t, m_ref, l_ref, acc_ref):
        m_old = m_ref[...]
        m_new = jnp.maximum(m_old, jnp.max(s, axis=1, keepdims=True))
        a = jnp.exp(m_old - m_new)
        p = jnp.exp(s - m_new)
        l_ref[...] = a * l_ref[...] + jnp.sum(p, axis=1, keepdims=True)
        pv = jnp.zeros(acc_ref.shape, F32)
        for r in range(pps):
            pv = pv + _dot_nt(p[:, r * PAGE_SIZE:(r + 1) * PAGE_SIZE].astype(BF16),
                              buf[slot, t, r].astype(BF16))
        acc_ref[...] = a * acc_ref[...] + pv
        m_ref[...] = m_new

    def compute(chunk, slot):
        s_a = jnp.concatenate([_dot(qa, buf[slot, 0, r].astype(BF16)) for r in range(pps)], axis=1)
        update(s_a + dec_ref[chunk], slot, 1, ma_ref, la_ref, acca_ref)
        pos = chunk * width + lax.broadcasted_iota(jnp.int32, (1, width), 1)
        bucket = _rel_bucket(past - pos)
        rel = jnp.zeros((H_B, width), F32)
        for n in range(N_BUCKETS):
            rel = jnp.where(bucket == n, rbt[:, n:n + 1], rel)
        s_b = jnp.concatenate([_dot(qb, buf[slot, 2, r].astype(BF16)) for r in range(pps)], axis=1)
        update(s_b + rel + mb_ref[chunk], slot, 3, mb_m_ref, lb_ref, accb_ref)

    def chunk_body(c, carry):
        t = b * n_chunks + c
        ahead = c + 2
        row_a = b + ahead // n_chunks

        @pl.when(row_a < n_rows)
        def _():
            start_chunk(row_a, ahead % n_chunks, (t + 2) % RING_SLOTS)

        slot = t % RING_SLOTS
        wait_chunk(b, c, slot)
        compute(c, slot)
        return carry

    lax.fori_loop(0, n_chunks, chunk_body, 0)

    def finish(q, kn, vn, extra, m_ref, l_ref, acc_ref, o_ref):
        s = jnp.sum(q.astype(F32) * kn, axis=1, keepdims=True) + extra
        m_old = m_ref[...]
        m_new = jnp.maximum(m_old, s)
        a = jnp.exp(m_old - m_new)
        p = jnp.exp(s - m_new)
        l = a * l_ref[...] + p
        o_ref[...] = (a * acc_ref[...] + p * vn) / l

    finish(qa, kan_ref[...], van_ref[...], 0.0, ma_ref, la_ref, acca_ref, oa_ref)
    finish(qb, kbn_ref[...], vbn_ref[...], rbt[:, 0:1] + mbn_ref[:, 0:1], mb_m_ref, lb_ref, accb_ref, ob_ref)


def _sample_attention(page_table, qa_blk, qb_blk, dec, mb, rbt, new, caches, layer, pps):
    db, n_pages = page_table.shape
    past = n_pages * PAGE_SIZE
    width = KV_A * HEAD_DIM
    n_chunks = n_pages // pps
    assert n_chunks % 2 == 0 and n_chunks * pps == n_pages
    per_b = lambda shape: pl.BlockSpec((None,) + shape, lambda b, pt: (b,) + (0,) * len(shape))
    hbm = pl.BlockSpec(memory_space=pl.ANY)
    in_specs = [per_b((H_A, width)), per_b((H_B, width)),
                per_b((n_chunks, H_A, pps * PAGE_SIZE)), per_b((n_chunks, 1, pps * PAGE_SIZE)),
                pl.BlockSpec(rbt.shape, lambda b, pt: (0, 0)),
                per_b((1, width)), per_b((1, width)), per_b((1, width)), per_b((1, width)),
                per_b((1, LANES)), hbm, hbm, hbm, hbm]
    grid_spec = pltpu.PrefetchScalarGridSpec(
        num_scalar_prefetch=1,
        grid=(db,),
        in_specs=in_specs,
        out_specs=[per_b((H_A, width)), per_b((H_B, width))],
        scratch_shapes=[pltpu.VMEM((RING_SLOTS, len(caches), pps, width, PAGE_SIZE), F32),
                        pltpu.SemaphoreType.DMA((RING_SLOTS,)),
                        pltpu.VMEM((H_A, 1), F32), pltpu.VMEM((H_A, 1), F32), pltpu.VMEM((H_A, width), F32),
                        pltpu.VMEM((H_B, 1), F32), pltpu.VMEM((H_B, 1), F32), pltpu.VMEM((H_B, width), F32)],
    )
    return pl.pallas_call(
        functools.partial(_sattn_kernel, pps=pps, past=past, layer=layer),
        grid_spec=grid_spec,
        out_shape=[jax.ShapeDtypeStruct((db, H_A, width), F32), jax.ShapeDtypeStruct((db, H_B, width), F32)],
        compiler_params=_cparams(("arbitrary",)),
        name="sample_attention",
    )(page_table, qa_blk, qb_blk, dec, mb, rbt, new['ka'], new['va'], new['kb'], new['vb'], new['mbn'], *caches)


def _head_cols(w, perm):
    return jnp.concatenate([w[:, h * HEAD_DIM:(h + 1) * HEAD_DIM] for h in perm], axis=1)


def _pad_cols(w, n):
    return jnp.pad(w, ((0, 0), (0, n - w.shape[1])))


def _split_w_in(w_in_l):
    sizes = (H_A * HEAD_DIM, KV_A * HEAD_DIM, KV_A * HEAD_DIM, H_A, H_B * HEAD_DIM, KV_B * HEAD_DIM,
             KV_B * HEAD_DIM, H_IDX * D_IDX, D_IDX, H_IDX)
    offs = np.cumsum((0,) + sizes)
    return [w_in_l[:, offs[n]:offs[n + 1]] for n in range(len(sizes))]


def _prep_proj_weights(w_in_l, perm):
    qa, ka, va, fa, qb, kb, vb, qi, ki, wi = _split_w_in(w_in_l)
    scale = HEAD_DIM ** -0.5
    big = jnp.concatenate([_head_cols(qa, perm) * scale, _head_cols(qb, perm) * scale, ka, va, kb, vb],
                          axis=1).astype(BF16)
    small = _pad_cols(jnp.concatenate([fa, wi], axis=1), LANES)
    idx = jnp.concatenate([qi * (D_IDX ** -0.5), ki, ki, small], axis=1)
    idx_hi, idx_lo = _split2(idx)
    return dict(big=big, idx_hi=idx_hi, idx_lo=idx_lo)


def _prep_proj_t_weights(w_in_l, perm):
    qa, ka, va, fa, qb, kb, vb, qi, ki, wi = _split_w_in(w_in_l)
    d = w_in_l.shape[0]
    scale = HEAD_DIM ** -0.5
    t_big = jnp.concatenate([_head_cols(qa, perm) * scale, _head_cols(qb, perm) * scale, va, vb, ka, kb],
                            axis=1).T.astype(BF16)
    zero = jnp.zeros((d, HEAD_DIM), w_in_l.dtype)
    spread = []
    for m in range(KV_A // 2):
        spread += [ka[:, 2 * m * HEAD_DIM:(2 * m + 1) * HEAD_DIM], zero, zero,
                   ka[:, (2 * m + 1) * HEAD_DIM:(2 * m + 2) * HEAD_DIM]]
    n_keys = jnp.concatenate(spread + [kb], axis=1).astype(BF16)
    small = _pad_cols(jnp.concatenate([fa, wi], axis=1), LANES)
    n_idx_hi, n_idx_lo = _split2(jnp.concatenate([ki, ki, small], axis=1))
    t_idx_hi, t_idx_lo = _split2((qi * (D_IDX ** -0.5)).T)
    return dict(t_big=t_big, n_keys=n_keys, n_idx_hi=n_idx_hi, n_idx_lo=n_idx_lo,
                t_idx_hi=t_idx_hi, t_idx_lo=t_idx_lo)


def _prep_post_weights(w_out_l, perm_a, perm_b, ln1_g, ln1_b, w_up_l, w_down_l, ln2_g, ln2_b, ff_chunk):
    d_mix, d = w_out_l.shape
    na = H_A * HEAD_DIM
    rows = lambda w, perm: jnp.concatenate([w[h * HEAD_DIM:(h + 1) * HEAD_DIM] for h in perm], axis=0)
    d_ff = w_up_l.shape[1]
    nc = d_ff // ff_chunk
    row = lambda a: a.reshape(1, -1)
    return dict(out_a=rows(w_out_l[:na], perm_a).astype(BF16), out_b=rows(w_out_l[na:], perm_b).astype(BF16),
                ln1_g=row(ln1_g), ln1_b=row(ln1_b), ln2_g=row(ln2_g), ln2_b=row(ln2_b),
                up=w_up_l.astype(BF16),
                down=w_down_l.reshape(nc, ff_chunk, d).astype(BF16))


ATTN_BLOCK = 256
FF_CHUNK = 512
POST_ROWS = 1024
SCAN_PAGES_PER_STEP = 32
ATTN_PAGES_PER_STEP = 16
RING_SLOTS = 3


def _prompt_layer(x2d, batch, seq, wproj, wpost, bf_pad, relb, rb_far, alpha, prev_kv):
    tq = min(ATTN_BLOCK, seq)
    p = _project_t(x2d, wproj, bf_pad, prev_kv, batch=batch, seq=seq, tk=tq)
    oa_t = _fox_attention(p['qat'], p['kaug'], p['vax'], batch, seq, tq)
    wi_t = p['wi'].reshape(batch, seq, H_IDX).transpose(0, 2, 1)
    ob_t = _dsa_attention(p, wi_t, relb, rb_far, batch, seq, tq, min(TOPK_MAX, seq // 4))
    y = _post_block(x2d, oa_t, ob_t, wpost, alpha, seq=seq)
    return (y, (p['vat'], p['vbt'], p['kat'], p['kbt']),
            (p['logf'].reshape(batch, seq, H_A), p['ki'].reshape(batch, seq, D_IDX)))


def _block_diag_q(q):
    db = q.shape[0]
    qh = q.reshape(db, H_A, 1, HEAD_DIM)
    kv_of_head = jnp.arange(H_A) // GROUP
    onehot = (kv_of_head[:, None] == jnp.arange(KV_A)[None, :])[None, :, :, None]
    return jnp.where(onehot, qh, jnp.zeros((), q.dtype)).reshape(db, H_A, KV_A * HEAD_DIM)


def _diag_heads(o_wide):
    db = o_wide.shape[0]
    o = o_wide.reshape(db, H_A, KV_A, HEAD_DIM)
    return jnp.concatenate([o[:, h, h // GROUP, :] for h in range(H_A)], axis=1)


def _sample_layer(x2d, page_table, caches, layer, wproj, wpost, bf_pad, rbt, alpha):
    db = x2d.shape[0]
    n_pages = page_table.shape[1]
    past = n_pages * PAGE_SIZE
    p = _project(x2d, wproj, bf_pad)
    q16 = jnp.concatenate([p['qi_hi'].reshape(db, H_IDX, D_IDX), p['qi_lo'].reshape(db, H_IDX, D_IDX)], axis=1)
    pps = min(ATTN_PAGES_PER_STEP, n_pages // 2)
    scan_pps = min(SCAN_PAGES_PER_STEP, n_pages // 2)
    scores, dec_c = _sample_scan(page_table, q16, p['wi'].reshape(db, H_IDX, 1), p['logf'].reshape(db, H_A, 1),
                                 caches['idx_k'], caches['logf_t'], layer, scan_pps, pps)
    q_f32 = p['qi_hi'].astype(F32) + p['qi_lo'].astype(F32)
    mb, mbn = _sample_select(scores.reshape(db, past), q_f32, jnp.tile(p['ki'], (1, H_IDX)), p['wi'],
                             min(TOPK_MAX, (past + 1) // 4))
    new = dict(ka=p['ka'].reshape(db, 1, -1), va=p['va'].reshape(db, 1, -1), kb=p['kb'].reshape(db, 1, -1),
               vb=p['vb'].reshape(db, 1, -1), mbn=mbn.reshape(db, 1, LANES))
    n_chunks = n_pages // pps
    oa_w, ob_w = _sample_attention(page_table, _block_diag_q(p['qa']), _block_diag_q(p['qb']), dec_c,
                                   mb.reshape(db, n_chunks, 1, pps * PAGE_SIZE), rbt, new,
                                   (caches['a_k'], caches['a_v'], caches['b_k'], caches['b_v']), layer, pps)
    y = _post_block(x2d, _diag_heads(oa_w).astype(BF16), _diag_heads(ob_w).astype(BF16), wpost, alpha)
    return y, (p['ka'], p['va'], p['logf'], p['kb'], p['vb'], p['ki'])


def kernel(x_prompt, x_sample, cache_a_k, cache_a_v, cache_a_logf, cache_b_k, cache_b_v, cache_idx_k,
           page_table, w_in, b_f, w_out, ln1_g, ln1_b, w_up, w_down, ln2_g, ln2_b, rel_bias):
    depth = w_in.shape[0]
    batch, seq, d_model = x_prompt.shape
    db, dec_seq, _ = x_sample.shape
    assert dec_seq == 1
    alpha = (2 * depth) ** 0.25
    natural = tuple(range(H_A))
    tq = min(ATTN_BLOCK, seq)

    kv_t = lambda c: c.transpose(0, 1, 3, 4, 2).reshape(c.shape[:2] + (-1, PAGE_SIZE))
    caches = dict(a_k=kv_t(cache_a_k), a_v=kv_t(cache_a_v), b_k=kv_t(cache_b_k), b_v=kv_t(cache_b_v),
                  idx_k=cache_idx_k.transpose(0, 1, 3, 2),
                  logf_t=cache_a_logf.astype(F32).transpose(0, 1, 3, 2))
    rb_perm = jnp.stack([rel_bias[:, h] for h in HEAD_PERM], axis=1)
    relb = _rel_bias_tiles(rb_perm, tq)
    rb_far = rb_perm[N_BUCKETS - 1]
    rbt = rel_bias.T

    xp = x_prompt.reshape(batch * seq, d_model)
    xs = x_sample.reshape(db, d_model)
    p_rows, s_rows = [], []
    p_kv = None
    for l in range(depth):
        bf_pad = _pad_cols(b_f[l].reshape(1, -1), LANES)
        post = lambda perm: _prep_post_weights(w_out[l], perm, perm, ln1_g[l], ln1_b[l], w_up[l], w_down[l],
                                               ln2_g[l], ln2_b[l], FF_CHUNK)
        xp, p_kv, rows = _prompt_layer(xp, batch, seq, _prep_proj_t_weights(w_in[l], HEAD_PERM),
                                       post(HEAD_PERM), bf_pad, relb, rb_far, alpha, p_kv)
        p_rows.append(rows)
        xs, rows = _sample_layer(xs, page_table, caches, l, _prep_proj_weights(w_in[l], natural),
                                 post(natural), bf_pad, rbt, alpha)
        s_rows.append(rows)

    stack = lambda rows, j: jnp.stack([r[j] for r in rows])
    sample = lambda j, tail: stack(s_rows, j).reshape((depth, db, dec_seq) + tail)
    kv = (KV_A, HEAD_DIM)
    prompt_kv = lambda a: a.reshape(depth, batch, KV_A, HEAD_DIM, seq).transpose(0, 1, 4, 2, 3)
    vat, vbt, kat, kbt = p_kv
    return (xp.reshape(batch, seq, d_model), xs.reshape(db, dec_seq, d_model),
            prompt_kv(kat), prompt_kv(vat), stack(p_rows, 0), prompt_kv(kbt), prompt_kv(vbt), stack(p_rows, 1),
            sample(0, kv), sample(1, kv), sample(2, (H_A,)), sample(3, kv), sample(4, kv), sample(5, (D_IDX,)))
```
